```python
import math
import jax, jax.numpy as jnp
from jax import lax
import numpy as np

D_MODEL = 1024
BATCH = 16
SEQ = 4096
DEPTH = 1

HG_HEADS = 4
HG_HEAD_K = 128
HG_HEAD_V = 128
HG_K_DIM = HG_HEADS * HG_HEAD_K
HG_V_DIM = HG_HEADS * HG_HEAD_V
GDN_HEADS = 4
GDN_HEAD_K = 128
GDN_HEAD_V = 128
GDN_QK_DIM = GDN_HEADS * GDN_HEAD_K
GDN_V_DIM = GDN_HEADS * GDN_HEAD_V
GDN_QKV = 2 * GDN_QK_DIM + GDN_V_DIM
MIX_DIM = HG_V_DIM + GDN_V_DIM
CONV_K = 4
CHUNK = 64
IN_COLS = 2 * HG_K_DIM + 2 * HG_V_DIM + GDN_QKV + GDN_V_DIM + 2 * GDN_HEADS
N_GROUPS = 8
EXPERTS_PER_GROUP = 8
N_EXPERTS = N_GROUPS * EXPERTS_PER_GROUP
TOP_K = 2
EXPERT_FF = 256
MOE_BLOCK = 256
NORM_EPS = 1e-6

kernel_name = 'hybrid_hgrn2_gdn_hier_moe'


def rmsnorm(x, w):
    xf = x.astype(jnp.float32)
    y = xf * lax.rsqrt(jnp.mean(xf * xf, axis=-1, keepdims=True) + NORM_EPS)
    return (y * w.astype(jnp.float32)).astype(x.dtype)


def gated_head_rmsnorm(o, z, w, n_heads):
    b, s, _ = o.shape
    oh = o.astype(jnp.float32).reshape(b, s, n_heads, -1)
    y = oh * lax.rsqrt(jnp.mean(oh * oh, axis=-1, keepdims=True) + NORM_EPS) * w.astype(jnp.float32)
    zh = z.astype(jnp.float32).reshape(b, s, n_heads, -1)
    return (y * jax.nn.silu(zh)).reshape(b, s, -1)


def l2norm(t):
    return t * lax.rsqrt(jnp.sum(t * t, axis=-1, keepdims=True) + 1e-6)


def to_chunks(t):
    b, s, h = t.shape[:3]
    t = t.reshape((b, s // CHUNK, CHUNK, h) + t.shape[3:])
    return jnp.transpose(t, (1, 0, 3, 2) + tuple(range(4, t.ndim)))


def from_chunks(o):
    n, b, h, c, d = o.shape
    return jnp.transpose(o, (1, 0, 3, 2, 4)).reshape(b, n * c, h * d)


def causal_conv_silu(u, w):
    s = u.shape[1]
    up = jnp.pad(u, ((0, 0), (CONV_K - 1, 0), (0, 0)))
    y = up[:, 0:s] * w[0]
    for j in range(1, CONV_K):
        y = y + up[:, j:j + s] * w[j]
    return jax.nn.silu(y)


def hgrn2_mixer(q, f_logit, i, lb):
    f32 = jnp.float32
    b, s, _ = q.shape
    q = jax.nn.silu(q.astype(f32)) * HG_HEAD_K ** -0.5
    f = lb + (1.0 - lb) * jax.nn.sigmoid(f_logit.astype(f32))
    k = 1.0 - f
    log_f = jnp.log(f)
    qc = to_chunks(q.reshape(b, s, HG_HEADS, HG_HEAD_K))
    kc = to_chunks(k.reshape(b, s, HG_HEADS, HG_HEAD_K))
    lfc = to_chunks(log_f.reshape(b, s, HG_HEADS, HG_HEAD_K))
    vc = to_chunks(i.astype(f32).reshape(b, s, HG_HEADS, HG_HEAD_V))
    tri = jnp.tril(jnp.ones((CHUNK, CHUNK), dtype=bool))[:, :, None]

    def step(state, inp):
        qb, kb, vb, lfb = inp
        cum = jnp.cumsum(lfb, axis=2)
        diff = cum[:, :, :, None, :] - cum[:, :, None, :, :]
        dec = jnp.where(tri, jnp.exp(jnp.where(tri, diff, 0.0)), 0.0)
        att = jnp.einsum('bhtk,bhsk,bhtsk->bhts', qb, kb, dec)
        o = jnp.einsum('bhtk,bhkv->bhtv', qb * jnp.exp(cum), state) + jnp.einsum('bhts,bhsv->bhtv', att, vb)
        last = cum[:, :, -1:, :]
        state = jnp.exp(last)[:, :, 0, :, None] * state + jnp.einsum('bhsk,bhsv->bhkv', kb * jnp.exp(last - cum), vb)
        return state, o

    state0 = jnp.zeros((b, HG_HEADS, HG_HEAD_K, HG_HEAD_V), f32)
    _, o = lax.scan(step, state0, (qc, kc, vc, lfc))
    return from_chunks(o)


def gated_deltanet_mixer(q, k, v, a, b_logit, a_log, dt_bias):
    f32 = jnp.float32
    bsz, s, _ = q.shape
    q = l2norm(q.astype(f32).reshape(bsz, s, GDN_HEADS, GDN_HEAD_K)) * GDN_HEAD_K ** -0.5
    k = l2norm(k.astype(f32).reshape(bsz, s, GDN_HEADS, GDN_HEAD_K))
    v = v.astype(f32).reshape(bsz, s, GDN_HEADS, GDN_HEAD_V)
    beta = jax.nn.sigmoid(b_logit.astype(f32))
    g = -jnp.exp(a_log.astype(f32)) * jax.nn.softplus(a.astype(f32) + dt_bias.astype(f32))
    qc, kc, vc = to_chunks(q), to_chunks(k), to_chunks(v)
    bc = to_chunks(beta)
    gc = jnp.cumsum(to_chunks(g), axis=-1)
    incl = jnp.tril(jnp.ones((CHUNK, CHUNK), dtype=bool))
    strict = jnp.tril(jnp.ones((CHUNK, CHUNK), dtype=bool), k=-1)
    diff = gc[..., :, None] - gc[..., None, :]
    dec = jnp.where(incl, jnp.exp(jnp.where(incl, diff, 0.0)), 0.0)
    kk = jnp.einsum('nbhtk,nbhsk->nbhts', kc, kc)
    a_mat = jnp.where(strict, bc[..., :, None] * kk * dec, 0.0)
    rhs = jnp.concatenate([kc * (bc * jnp.exp(gc))[..., None], vc * bc[..., None]], axis=-1)
    sol = lax.linalg.triangular_solve(a_mat, rhs, left_side=True, lower=True, unit_diagonal=True)
    w_c, u_c = sol[..., :GDN_HEAD_K], sol[..., GDN_HEAD_K:]
    qk = jnp.einsum('nbhtk,nbhsk->nbhts', qc, kc) * dec
    qg = qc * jnp.exp(gc)[..., None]
    glast = gc[..., -1:]
    kdec = kc * jnp.exp(glast - gc)[..., None]
    gend = jnp.exp(glast[..., 0])

    def step(state, inp):
        qg_n, qk_n, w_n, u_n, kdec_n, gend_n = inp
        v_new = u_n - jnp.einsum('bhck,bhkv->bhcv', w_n, state)
        o = jnp.einsum('bhck,bhkv->bhcv', qg_n, state) + jnp.einsum('bhts,bhsv->bhtv', qk_n, v_new)
        state = gend_n[..., None, None] * state + jnp.einsum('bhck,bhcv->bhkv', kdec_n, v_new)
        return state, o

    state0 = jnp.zeros((bsz, GDN_HEADS, GDN_HEAD_K, GDN_HEAD_V), f32)
    _, o = lax.scan(step, state0, (qg, qk, w_c, u_c, kdec, gend))
    return from_chunks(o)


def hier_moe(xn, w_group, w_expert, w_gate, w_up, w_down):
    f32 = jnp.float32
    bsz, s, d = xn.shape
    t = bsz * s
    xt = xn.reshape(t, d)
    g_logits = (xt @ w_group).astype(f32)
    g_prob = jax.nn.softmax(g_logits, axis=-1)
    _, g_idx = lax.top_k(g_logits, 1)
    p_group = jnp.take_along_axis(g_prob, g_idx, axis=-1)
    e_logits = (xt @ w_expert).astype(f32).reshape(t, N_GROUPS, EXPERTS_PER_GROUP)
    e_sel = jnp.take_along_axis(e_logits, g_idx[:, :, None], axis=1)[:, 0]
    top_v, top_i = lax.top_k(e_sel, TOP_K)
    gate = (jax.nn.softmax(top_v, axis=-1) * p_group).reshape(-1)
    expert_id = (g_idx * EXPERTS_PER_GROUP + top_i).reshape(-1).astype(jnp.int32)
    tok = jnp.broadcast_to(jnp.arange(t, dtype=jnp.int32)[:, None], (t, TOP_K)).reshape(-1)
    n_assign = t * TOP_K
    n_blocks = n_assign // MOE_BLOCK + N_EXPERTS
    order = jnp.argsort(expert_id)
    se, stok, sgate = expert_id[order], tok[order], gate[order]
    counts = jax.ops.segment_sum(jnp.ones((n_assign,), jnp.int32), expert_id, num_segments=N_EXPERTS)
    starts = jnp.cumsum(counts) - counts
    padded = (counts + MOE_BLOCK - 1) // MOE_BLOCK * MOE_BLOCK
    pend = jnp.cumsum(padded)
    pstart = pend - padded
    dest = pstart[se] + jnp.arange(n_assign, dtype=jnp.int32) - starts[se]
    buf_tok = jnp.full((n_blocks * MOE_BLOCK,), t, jnp.int32).at[dest].set(stok)
    buf_gate = jnp.zeros((n_blocks * MOE_BLOCK,), f32).at[dest].set(sgate)
    block_start = jnp.arange(n_blocks, dtype=jnp.int32) * MOE_BLOCK
    block_expert = jnp.minimum(jnp.searchsorted(pend, block_start, side='right'), N_EXPERTS - 1).astype(jnp.int32)
    x_pad = jnp.concatenate([xt, jnp.zeros((1, d), xt.dtype)], axis=0)

    def block_fn(args):
        idx, gw, e = args
        xb = x_pad[idx]
        hb = jax.nn.silu(xb @ w_gate[e]) * (xb @ w_up[e])
        return (hb @ w_down[e]) * gw[:, None].astype(xb.dtype)

    yb = lax.map(block_fn, (buf_tok.reshape(n_blocks, MOE_BLOCK), buf_gate.reshape(n_blocks, MOE_BLOCK), block_expert))
    y = jax.ops.segment_sum(yb.reshape(-1, d), buf_tok, num_segments=t + 1)[:t]
    return y.reshape(bsz, s, d)


def setup_inputs(seed: int = 0) -> dict:
    key = jax.random.key(seed)
    ks = jax.random.split(key, 17)
    nrm = jax.random.normal
    dt = jnp.exp(jax.random.uniform(ks[7], (DEPTH, GDN_HEADS), minval=math.log(1e-3), maxval=math.log(1e-1)))
    return {
        'x': nrm(ks[0], (BATCH, SEQ, D_MODEL), jnp.float32),
        'norm_mix_w': 1.0 + 0.02 * nrm(ks[1], (DEPTH, D_MODEL), jnp.float32),
        'w_in': nrm(ks[2], (DEPTH, D_MODEL, IN_COLS), jnp.float32) * D_MODEL ** -0.5,
        'hgrn_lb_logits': 0.1 * nrm(ks[3], (DEPTH + 1, HG_K_DIM), jnp.float32),
        'hgrn_onorm_w': 1.0 + 0.02 * nrm(ks[4], (DEPTH, HG_HEAD_V), jnp.float32),
        'gdn_conv_w': nrm(ks[5], (DEPTH, CONV_K, GDN_QKV), jnp.float32) * CONV_K ** -0.5,
        'gdn_a_log': jnp.log(jax.random.uniform(ks[6], (DEPTH, GDN_HEADS), minval=1.0, maxval=16.0)),
        'gdn_dt_bias': dt + jnp.log(-jnp.expm1(-dt)),
        'gdn_onorm_w': 1.0 + 0.02 * nrm(ks[8], (DEPTH, GDN_HEAD_V), jnp.float32),
        'w_out': nrm(ks[9], (DEPTH, MIX_DIM, D_MODEL), jnp.float32) * MIX_DIM ** -0.5,
        'norm_moe_w': 1.0 + 0.02 * nrm(ks[10], (DEPTH, D_MODEL), jnp.float32),
        'router_group_w': nrm(ks[11], (DEPTH, D_MODEL, N_GROUPS), jnp.float32) * D_MODEL ** -0.5,
        'router_expert_w': nrm(ks[12], (DEPTH, D_MODEL, N_EXPERTS), jnp.float32) * D_MODEL ** -0.5,
        'expert_w_gate': nrm(ks[13], (DEPTH, N_EXPERTS, D_MODEL, EXPERT_FF), jnp.float32) * D_MODEL ** -0.5,
        'expert_w_up': nrm(ks[14], (DEPTH, N_EXPERTS, D_MODEL, EXPERT_FF), jnp.float32) * D_MODEL ** -0.5,
        'expert_w_down': nrm(ks[15], (DEPTH, N_EXPERTS, EXPERT_FF, D_MODEL), jnp.float32) * EXPERT_FF ** -0.5,
        'final_norm_w': 1.0 + 0.02 * nrm(ks[16], (D_MODEL,), jnp.float32),
    }


def reference(x, norm_mix_w, w_in, hgrn_lb_logits, hgrn_onorm_w, gdn_conv_w, gdn_a_log, gdn_dt_bias,
              gdn_onorm_w, w_out, norm_moe_w, router_group_w, router_expert_w, expert_w_gate,
              expert_w_up, expert_w_down, final_norm_w):
    sizes = (HG_K_DIM, HG_K_DIM, HG_V_DIM, HG_V_DIM, GDN_QKV, GDN_V_DIM, GDN_HEADS, GDN_HEADS)
    offsets = tuple(int(v) for v in np.cumsum(sizes)[:-1])
    lower_bounds = jnp.cumsum(jax.nn.softmax(hgrn_lb_logits.astype(jnp.float32), axis=0), axis=0)
    h = x
    for l in range(DEPTH):
        hn = rmsnorm(h, norm_mix_w[l])
        proj = hn @ w_in[l]
        hq, hf, hi, hg, gqkv, gz, ga, gb = jnp.split(proj, offsets, axis=-1)
        o_hg = hgrn2_mixer(hq, hf, hi, lower_bounds[l])
        o_hg = gated_head_rmsnorm(o_hg, hg, hgrn_onorm_w[l], HG_HEADS)
        qkv = causal_conv_silu(gqkv, gdn_conv_w[l])
        gq, gk, gv = jnp.split(qkv, (GDN_QK_DIM, 2 * GDN_QK_DIM), axis=-1)
        o_gd = gated_deltanet_mixer(gq, gk, gv, ga, gb, gdn_a_log[l], gdn_dt_bias[l])
        o_gd = gated_head_rmsnorm(o_gd, gz, gdn_onorm_w[l], GDN_HEADS)
        mix = jnp.concatenate([o_hg, o_gd], axis=-1).astype(h.dtype) @ w_out[l]
        h = h + mix
        h = h + hier_moe(rmsnorm(h, norm_moe_w[l]), router_group_w[l], router_expert_w[l],
                         expert_w_gate[l], expert_w_up[l], expert_w_down[l])
    return rmsnorm(h, final_norm_w)
```

```python
import functools

import numpy as np
import jax
import jax.numpy as jnp
from jax import lax
from jax.experimental import pallas as pl
from jax.experimental.pallas import tpu as pltpu

F32 = jnp.float32
BF16 = jnp.bfloat16
HI = lax.Precision.HIGHEST

D_MODEL = 1024
HEADS = 4
HEAD_DIM = 128
MIX_HALF = HEADS * HEAD_DIM
CHUNK = 64
STACK = HEADS * CHUNK
CONV_K = 4
N_GROUPS = 8
EXPERTS_PER_GROUP = 8
N_EXPERTS = N_GROUPS * EXPERTS_PER_GROUP
TOP_K = 2
EXPERT_FF = 256
NORM_EPS = 1e-6
LANES = 128
SLOT_BLOCK = 256
VMEM_LIMIT = 56 * 1024 * 1024


def _sigmoid(x):
    return 1.0 / (1.0 + jnp.exp(-x))


def _silu(x):
    return x * _sigmoid(x)


def _dot(a, b, precision=None):
    return jnp.dot(a, b, preferred_element_type=F32, precision=precision)


def _dot_nt(a, b, precision=None):
    return lax.dot_general(a, b, (((1,), (1,)), ((), ())), preferred_element_type=F32, precision=precision)


def _dot_tn(a, b, precision=None):
    return lax.dot_general(a, b, (((0,), (0,)), ((), ())), preferred_element_type=F32, precision=precision)


def _stack_heads(a):
    return jnp.concatenate([a[:, h * HEAD_DIM:(h + 1) * HEAD_DIM] for h in range(HEADS)], axis=0)


def _inproj_kernel(x_ref, nw_ref, wa_ref, wb_ref, wc_ref, oa_ref, ob_ref, oc_ref):
    x = x_ref[...]
    ms = jnp.mean(x * x, axis=-1, keepdims=True)
    hn = (x * lax.rsqrt(ms + NORM_EPS) * nw_ref[...]).astype(BF16)
    oa_ref[...] = _dot(hn, wa_ref[...])
    ob_ref[...] = _dot(hn, wb_ref[...])
    oc_ref[...] = _dot(hn, wc_ref[...])


def _inproj(x2, norm_w, wa, wb, wc, tm):
    t = x2.shape[0]
    na, nb, nc = wa.shape[1], wb.shape[1], wc.shape[1]
    const = lambda i: (0, 0)
    return pl.pallas_call(
        _inproj_kernel,
        grid=(t // tm,),
        in_specs=[
            pl.BlockSpec((tm, D_MODEL), lambda i: (i, 0)),
            pl.BlockSpec((1, D_MODEL), const),
            pl.BlockSpec((D_MODEL, na), const),
            pl.BlockSpec((D_MODEL, nb), const),
            pl.BlockSpec((D_MODEL, nc), const),
        ],
        out_specs=[
            pl.BlockSpec((tm, na), lambda i: (i, 0)),
            pl.BlockSpec((tm, nb), lambda i: (i, 0)),
            pl.BlockSpec((tm, nc), lambda i: (i, 0)),
        ],
        out_shape=[
            jax.ShapeDtypeStruct((t, na), F32),
            jax.ShapeDtypeStruct((t, nb), F32),
            jax.ShapeDtypeStruct((t, nc), F32),
        ],
        compiler_params=pltpu.CompilerParams(dimension_semantics=("arbitrary",), vmem_limit_bytes=VMEM_LIMIT),
        name="inproj",
    )(x2, norm_w, wa, wb, wc)


HGRN_LEVELS = (32, 16, 8, 4, 2, 1)
DIAG_CODE = len(HGRN_LEVELS)
NONE_CODE = DIAG_CODE + 1


def _hgrn_arg_matrix():
    t = np.arange(CHUNK)[:, None]
    u = np.arange(CHUNK)[None, :]
    mats = [u <= t, u > t]
    for b in HGRN_LEVELS:
        odd = (t // b) % 2 == 1
        start = (t // b) * b
        mats.append(np.where(odd, (u > start) & (u <= t), (u > t) & (u <= start + b)))
    return np.concatenate(mats, axis=0).astype(np.float32)


def _hgrn_level_codes():
    idx = np.arange(STACK)
    h, t = idx // CHUNK, idx % CHUNK
    same = h[:, None] == h[None, :]
    tt, ss = t[:, None], t[None, :]
    code = np.full((STACK, STACK), NONE_CODE, np.int32)
    code[same & (tt == ss)] = DIAG_CODE
    for l, b in enumerate(HGRN_LEVELS):
        sib = (tt // (2 * b) == ss // (2 * b)) & ((tt // b) % 2 == 1) & ((ss // b) % 2 == 0)
        code[same & sib] = l
    return code


def _hgrn_kernel(q_ref, f_ref, i_ref, g_ref, lbl_ref, onw_ref, marg_ref, code_ref, o_ref, st_ref, *, n_chunks):
    @pl.when(pl.program_id(1) == 0)
    def _():
        st_ref[...] = jnp.zeros_like(st_ref)

    lbl = lbl_ref[...]
    lmax = jnp.max(lbl, axis=0, keepdims=True)
    lexp = jnp.exp(lbl - lmax)
    lb = lexp[0:1, :] / jnp.sum(lexp, axis=0, keepdims=True)
    onw = onw_ref[...]
    marg = marg_ref[...]
    code = code_ref[...]
    trow = lax.broadcasted_iota(jnp.int32, (STACK, HEAD_DIM), 0) & (CHUNK - 1)

    for c in range(n_chunks):
        rows = slice(c * CHUNK, (c + 1) * CHUNK)
        q = _silu(q_ref[rows, :]) * (HEAD_DIM ** -0.5)
        fg = lb + (1.0 - lb) * _sigmoid(f_ref[rows, :])
        k = 1.0 - fg
        lf = jnp.log(fg)
        e = jnp.exp(_dot(marg, lf, precision=HI))
        qs, ks, vs = _stack_heads(q), _stack_heads(k), _stack_heads(i_ref[rows, :])
        cum_e = _stack_heads(e[0:CHUNK])
        suf_e = _stack_heads(e[CHUNK:2 * CHUNK])
        att = jnp.where(code == DIAG_CODE, _dot_nt(qs.astype(BF16), ks.astype(BF16)), 0.0)
        for l, b in enumerate(HGRN_LEVELS):
            el = _stack_heads(e[(2 + l) * CHUNK:(3 + l) * CHUNK])
            x = (jnp.where((trow & b) != 0, qs, ks) * el).astype(BF16)
            att = jnp.where(code == l, _dot_nt(x, x), att)
        o_intra = _dot(att.astype(BF16), vs.astype(BF16))
        qc = (qs * cum_e).astype(BF16)
        kd = (ks * suf_e).astype(BF16)
        for h in range(HEADS):
            hr = slice(h * CHUNK, (h + 1) * CHUNK)
            hc = slice(h * HEAD_DIM, (h + 1) * HEAD_DIM)
            s_t = st_ref[h]
            o = _dot_nt(qc[hr], s_t.astype(BF16)) + o_intra[hr]
            decay = cum_e[h * CHUNK + CHUNK - 1:h * CHUNK + CHUNK, :]
            st_ref[h] = decay * s_t + _dot_tn(vs[hr].astype(BF16), kd[hr])
            y = o * lax.rsqrt(jnp.mean(o * o, axis=-1, keepdims=True) + NORM_EPS) * onw
            o_ref[rows, hc] = (y * _silu(g_ref[rows, hc])).astype(o_ref.dtype)


def _hgrn(pa, lb_logits, onorm_w, batch, seq, n_chunks):
    rows = n_chunks * CHUNK
    steps = seq // rows
    col = lambda j: pl.BlockSpec((rows, MIX_HALF), lambda b, s, j=j: (b * steps + s, j))
    const = lambda b, s: (0, 0)
    marg = jnp.asarray(_hgrn_arg_matrix())
    code = jnp.asarray(_hgrn_level_codes())
    return pl.pallas_call(
        functools.partial(_hgrn_kernel, n_chunks=n_chunks),
        grid=(batch, steps),
        in_specs=[col(0), col(1), col(2), col(3),
                  pl.BlockSpec(lb_logits.shape, const),
                  pl.BlockSpec((1, HEAD_DIM), const),
                  pl.BlockSpec(marg.shape, const),
                  pl.BlockSpec(code.shape, const)],
        out_specs=pl.BlockSpec((rows, MIX_HALF), lambda b, s: (b * steps + s, 0)),
        out_shape=jax.ShapeDtypeStruct((batch * seq, MIX_HALF), BF16),
        scratch_shapes=[pltpu.VMEM((HEADS, HEAD_DIM, HEAD_DIM), F32)],
        compiler_params=pltpu.CompilerParams(dimension_semantics=("arbitrary", "arbitrary"),
                                             vmem_limit_bytes=VMEM_LIMIT),
        name="hgrn2",
    )(pa, pa, pa, pa, lb_logits, onorm_w, marg, code)


GDN_QKV = 3 * MIX_HALF
HIST = 8
C_NONE, C_DIAG, C_B16, C_B32, C_B64 = 0, 1, 2, 3, 4


def _gdn_codes():
    idx = np.arange(STACK)
    h, t = idx // CHUNK, idx % CHUNK
    same = h[:, None] == h[None, :]
    tt, ss = t[:, None], t[None, :]
    code = np.full((STACK, STACK), C_NONE, np.int32)
    low = same & (ss < tt)
    code[low] = C_B64
    code[low & (tt // 32 == ss // 32)] = C_B32
    code[low & (tt // 16 == ss // 16)] = C_B16
    code[same & (tt == ss)] = C_DIAG
    incl = (same & (ss <= tt)).astype(np.float32)
    after = (same & (tt > ss)).astype(np.float32)
    return code, incl, after


def _gdn_kernel(qkv_ref, z_ref, ab_ref, cw_ref, alog_ref, dtb_ref, onw_ref, code_ref, incl_ref, after_ref,
                o_ref, st_ref, ubuf, *, n_chunks):
    rows_blk = n_chunks * CHUNK

    @pl.when(pl.program_id(1) == 0)
    def _():
        st_ref[...] = jnp.zeros_like(st_ref)
        ubuf[0:HIST, :] = jnp.zeros((HIST, GDN_QKV), F32)

    ubuf[HIST:HIST + rows_blk, :] = qkv_ref[...]
    cw = cw_ref[...]
    conv = cw[CONV_K - 1:CONV_K, :] * ubuf[HIST:HIST + rows_blk, :]
    for j in range(1, CONV_K):
        conv = conv + cw[CONV_K - 1 - j:CONV_K - j, :] * ubuf[HIST - j:HIST - j + rows_blk, :]
    ubuf[0:HIST, :] = ubuf[rows_blk:rows_blk + HIST, :]
    qkv = _silu(conv)

    onw = onw_ref[...]
    code = code_ref[...]
    incl_m = incl_ref[...]
    after_m = after_ref[...]
    incl = code >= C_DIAG
    eye = (code == C_DIAG).astype(F32)

    for c in range(n_chunks):
        rows = slice(c * CHUNK, (c + 1) * CHUNK)
        qs = _stack_heads(qkv[rows, 0:MIX_HALF])
        ks = _stack_heads(qkv[rows, MIX_HALF:2 * MIX_HALF])
        vs = _stack_heads(qkv[rows, 2 * MIX_HALF:3 * MIX_HALF])
        qs = qs * lax.rsqrt(jnp.sum(qs * qs, axis=-1, keepdims=True) + 1e-6) * (HEAD_DIM ** -0.5)
        ks = ks * lax.rsqrt(jnp.sum(ks * ks, axis=-1, keepdims=True) + 1e-6)
        ab = ab_ref[rows, :]
        xa = ab + dtb_ref[...]
        softplus = jnp.maximum(xa, 0.0) + jnp.log(1.0 + jnp.exp(-jnp.abs(xa)))
        g_all = -jnp.exp(alog_ref[...]) * softplus
        beta_all = _sigmoid(ab)
        g_st = jnp.concatenate([g_all[:, h:h + 1] for h in range(HEADS)], axis=0)
        beta_st = jnp.concatenate([beta_all[:, HEADS + h:HEADS + h + 1] for h in range(HEADS)], axis=0)
        g_rhs = jnp.concatenate([jnp.broadcast_to(g_st, (STACK, STACK)) * after_m,
                                 jnp.broadcast_to(g_st, (STACK, HEAD_DIM))], axis=1)
        gg = _dot(incl_m, g_rhs, precision=HI)
        diff = gg[:, 0:STACK]
        gc = gg[:, STACK:]
        dec = jnp.where(incl, jnp.exp(jnp.where(incl, diff, 0.0)), 0.0)
        kb = ks.astype(BF16)
        kk = _dot_nt(kb, kb)
        qk = _dot_nt(qs.astype(BF16), kb)
        am = beta_st * kk * dec
        a16 = jnp.where(code == C_B16, am, 0.0)
        n32 = jnp.where(code == C_B32, am, 0.0)
        n64 = jnp.where(code == C_B64, am, 0.0)
        a2 = _dot(a16, a16, precision=HI)
        a4 = _dot(a2, a2, precision=HI)
        a8 = _dot(a4, a4, precision=HI)
        p = eye - a16
        p = p + _dot(p, a2, precision=HI)
        p = p + _dot(p, a4, precision=HI)
        p = p + _dot(p, a8, precision=HI)
        p = p - _dot(p, _dot(n32, p, precision=HI), precision=HI)
        p = p - _dot(p, _dot(n64, p, precision=HI), precision=HI)
        egc = jnp.exp(gc)
        rhs = jnp.concatenate([ks * (beta_st * egc), vs * beta_st], axis=1)
        wu = _dot(p, rhs, precision=HI)
        w_c = wu[:, 0:HEAD_DIM].astype(BF16)
        u_c = wu[:, HEAD_DIM:]
        qkm = jnp.where(incl, qk * dec, 0.0).astype(BF16)
        qg = (qs * egc).astype(BF16)
        v_new, o_inter, kdecs, gends = [], [], [], []
        for h in range(HEADS):
            hr = slice(h * CHUNK, (h + 1) * CHUNK)
            s_t = st_ref[h].astype(BF16)
            v_new.append(u_c[hr] - _dot_nt(w_c[hr], s_t))
            o_inter.append(_dot_nt(qg[hr], s_t))
            glast = gc[h * CHUNK + CHUNK - 1:h * CHUNK + CHUNK, :]
            kdecs.append((ks[hr] * jnp.exp(glast - gc[hr])).astype(BF16))
            gends.append(jnp.exp(glast))
        v_new_st = jnp.concatenate(v_new, axis=0).astype(BF16)
        o_st = jnp.concatenate(o_inter, axis=0) + _dot(qkm, v_new_st)
        for h in range(HEADS):
            hr = slice(h * CHUNK, (h + 1) * CHUNK)
            hc = slice(h * HEAD_DIM, (h + 1) * HEAD_DIM)
            st_ref[h] = gends[h] * st_ref[h] + _dot_tn(v_new_st[hr], kdecs[h])
            o = o_st[hr]
            y = o * lax.rsqrt(jnp.mean(o * o, axis=-1, keepdims=True) + NORM_EPS) * onw
            o_ref[rows, hc] = (y * _silu(z_ref[rows, hc])).astype(o_ref.dtype)


def _gdn(pb, pc, conv_w, alog_row, dtb_row, onorm_w, batch, seq, n_chunks):
    rows = n_chunks * CHUNK
    steps = seq // rows
    const = lambda b, s: (0, 0)
    code, incl, after = (jnp.asarray(a) for a in _gdn_codes())
    sq = pl.BlockSpec((STACK, STACK), const)
    row = pl.BlockSpec((1, LANES), const)
    return pl.pallas_call(
        functools.partial(_gdn_kernel, n_chunks=n_chunks),
        grid=(batch, steps),
        in_specs=[pl.BlockSpec((rows, GDN_QKV), lambda b, s: (b * steps + s, 0)),
                  pl.BlockSpec((rows, MIX_HALF), lambda b, s: (b * steps + s, GDN_QKV // MIX_HALF)),
                  pl.BlockSpec((rows, LANES), lambda b, s: (b * steps + s, 0)),
                  pl.BlockSpec((CONV_K, GDN_QKV), const),
                  row, row, row, sq, sq, sq],
        out_specs=pl.BlockSpec((rows, MIX_HALF), lambda b, s: (b * steps + s, 0)),
        out_shape=jax.ShapeDtypeStruct((batch * seq, MIX_HALF), BF16),
        scratch_shapes=[pltpu.VMEM((HEADS, HEAD_DIM, HEAD_DIM), F32),
                        pltpu.VMEM((HIST + rows, GDN_QKV), F32)],
        compiler_params=pltpu.CompilerParams(dimension_semantics=("arbitrary", "arbitrary"),
                                             vmem_limit_bytes=VMEM_LIMIT),
        name="gdn",
    )(pb, pb, pc, conv_w, alog_row, dtb_row, onorm_w, code, incl, after)


def _route_kernel(x_ref, ohg_ref, ogd_ref, wo_ref, nw_ref, wr_ref, ltri_ref,
                  h_ref, hn_ref, ri_ref, rg_ref, cnt_ref, cnt_scr):
    @pl.when(pl.program_id(0) == 0)
    def _():
        cnt_scr[...] = jnp.zeros_like(cnt_scr)

    mix = _dot(ohg_ref[...], wo_ref[0:MIX_HALF, :]) + _dot(ogd_ref[...], wo_ref[MIX_HALF:2 * MIX_HALF, :])
    h = x_ref[...] + mix
    h_ref[...] = h
    hn = h * lax.rsqrt(jnp.mean(h * h, axis=-1, keepdims=True) + NORM_EPS) * nw_ref[...]
    hn_ref[...] = hn
    logits = _dot(hn, wr_ref[...], precision=HI)
    tm = logits.shape[0]
    lane = lax.broadcasted_iota(jnp.int32, (tm, LANES), 1)
    neg = jnp.float32(-jnp.inf)
    big = jnp.int32(LANES)

    def first_max(vals):
        m = jnp.max(vals, axis=-1, keepdims=True)
        return m, jnp.min(jnp.where(vals == m, lane, big), axis=-1, keepdims=True)

    gl = jnp.where(lane < N_GROUPS, logits, neg)
    gmax, gidx = first_max(gl)
    p_group = 1.0 / jnp.sum(jnp.exp(gl - gmax), axis=-1, keepdims=True)
    lo = N_GROUPS + EXPERTS_PER_GROUP * gidx
    el = jnp.where((lane >= lo) & (lane < lo + EXPERTS_PER_GROUP), logits, neg)
    m1, i1 = first_max(el)
    m2, i2 = first_max(jnp.where(lane == i1, neg, el))
    r = jnp.exp(m2 - m1)
    gate1 = p_group / (1.0 + r)
    gate2 = p_group * r / (1.0 + r)
    hot1 = lane == i1
    hot2 = lane == i2
    onehot = jnp.where(hot1 | hot2, 1.0, 0.0)
    before = _dot(ltri_ref[...], onehot.astype(BF16)) + cnt_scr[...]
    rank1 = jnp.sum(jnp.where(hot1, before, 0.0), axis=-1, keepdims=True)
    rank2 = jnp.sum(jnp.where(hot2, before, 0.0), axis=-1, keepdims=True)
    cnt = cnt_scr[...] + jnp.sum(onehot, axis=0, keepdims=True)
    cnt_scr[...] = cnt
    cnt_ref[...] = cnt
    ri = jnp.where(lane == 0, i1 - N_GROUPS,
                   jnp.where(lane == 1, i2 - N_GROUPS,
                             jnp.where(lane == 2, rank1.astype(jnp.int32),
                                       jnp.where(lane == 3, rank2.astype(jnp.int32), 0))))
    ri_ref[...] = ri
    rg_ref[...] = jnp.where(lane == 0, gate1, jnp.where(lane == 1, gate2, 0.0))


def _route(x2, ohg, ogd, wo, norm_w, wr, tm):
    t = x2.shape[0]
    ltri = jnp.asarray(np.tril(np.ones((tm, tm), np.float32), -1), BF16)
    const = lambda i: (0, 0)
    tile = lambda n: pl.BlockSpec((tm, n), lambda i: (i, 0))
    return pl.pallas_call(
        _route_kernel,
        grid=(t // tm,),
        in_specs=[tile(D_MODEL), tile(MIX_HALF), tile(MIX_HALF),
                  pl.BlockSpec((D_MODEL, D_MODEL), const),
                  pl.BlockSpec((1, D_MODEL), const),
                  pl.BlockSpec((D_MODEL, LANES), const),
                  pl.BlockSpec((tm, tm), const)],
        out_specs=[tile(D_MODEL), tile(D_MODEL), tile(LANES), tile(LANES), pl.BlockSpec((1, LANES), const)],
        out_shape=[jax.ShapeDtypeStruct((t, D_MODEL), F32),
                   jax.ShapeDtypeStruct((t, D_MODEL), F32),
                   jax.ShapeDtypeStruct((t, LANES), jnp.int32),
                   jax.ShapeDtypeStruct((t, LANES), F32),
                   jax.ShapeDtypeStruct((1, LANES), F32)],
        scratch_shapes=[pltpu.VMEM((1, LANES), F32)],
        compiler_params=pltpu.CompilerParams(dimension_semantics=("arbitrary",), vmem_limit_bytes=VMEM_LIMIT),
        name="route",
    )(x2, ohg, ogd, wo, norm_w, wr, ltri)


def _row_copy(src_ref, src_row, dst_ref, dst_row, sem):
    return pltpu.make_async_copy(src_ref.at[pl.ds(src_row, 1)], dst_ref.at[pl.ds(dst_row, 1)], sem)


def _dispatch_kernel(starts_ref, idx_ref, hn_ref, xs_ref, sem, *, tm):
    def issue(t, carry):
        for k in range(TOP_K):
            dest = starts_ref[idx_ref[0, 0, 4 * t + k]] + idx_ref[0, 0, 4 * t + 2 + k]
            _row_copy(hn_ref, t, xs_ref, dest, sem).start()
        return carry

    lax.fori_loop(0, tm, issue, 0)

    def drain(t, carry):
        for k in range(TOP_K):
            _row_copy(hn_ref, 0, xs_ref, 0, sem).wait()
        return carry

    lax.fori_loop(0, tm, drain, 0)


def _dispatch(starts, idx3, hn, tm):
    t = hn.shape[0]
    return pl.pallas_call(
        functools.partial(_dispatch_kernel, tm=tm),
        grid_spec=pltpu.PrefetchScalarGridSpec(
            num_scalar_prefetch=1,
            grid=(t // tm,),
            in_specs=[pl.BlockSpec((1, 1, 4 * tm), lambda i, s: (i, 0, 0), memory_space=pltpu.SMEM),
                      pl.BlockSpec((tm, D_MODEL), lambda i, s: (i, 0))],
            out_specs=pl.BlockSpec(memory_space=pl.ANY),
            scratch_shapes=[pltpu.SemaphoreType.DMA(())],
        ),
        out_shape=jax.ShapeDtypeStruct((t * TOP_K, D_MODEL), F32),
        compiler_params=pltpu.CompilerParams(dimension_semantics=("arbitrary",), vmem_limit_bytes=VMEM_LIMIT),
        name="dispatch",
    )(starts, idx3, hn)


def _expert_kernel(blk_ref, exp_ref, lo_ref, hi_ref, xs_ref, wg_ref, wu_ref, wd_ref, ys_ref):
    p = pl.program_id(0)
    xb = xs_ref[...].astype(BF16)
    a = _dot(xb, wg_ref[0])
    b = _dot(xb, wu_ref[0])
    y = _dot((_silu(a) * b).astype(BF16), wd_ref[0])
    slot = blk_ref[p] * SLOT_BLOCK + lax.broadcasted_iota(jnp.int32, (SLOT_BLOCK, 1), 0)
    y = jnp.where((slot >= lo_ref[p]) & (slot < hi_ref[p]), y, 0.0)
    first = jnp.logical_or(p == 0, blk_ref[p] != blk_ref[jnp.maximum(p - 1, 0)])

    @pl.when(first)
    def _():
        ys_ref[...] = y

    @pl.when(jnp.logical_not(first))
    def _():
        ys_ref[...] += y


def _experts(blk, exp, lo, hi, xs, wg, wu, wd):
    n_slots = xs.shape[0]
    n_pairs = blk.shape[0]
    return pl.pallas_call(
        _expert_kernel,
        grid_spec=pltpu.PrefetchScalarGridSpec(
            num_scalar_prefetch=4,
            grid=(n_pairs,),
            in_specs=[pl.BlockSpec((SLOT_BLOCK, D_MODEL), lambda p, blk, exp, lo, hi: (blk[p], 0)),
                      pl.BlockSpec((1, D_MODEL, EXPERT_FF), lambda p, blk, exp, lo, hi: (exp[p], 0, 0)),
                      pl.BlockSpec((1, D_MODEL, EXPERT_FF), lambda p, blk, exp, lo, hi: (exp[p], 0, 0)),
                      pl.BlockSpec((1, EXPERT_FF, D_MODEL), lambda p, blk, exp, lo, hi: (exp[p], 0, 0))],
            out_specs=pl.BlockSpec((SLOT_BLOCK, D_MODEL), lambda p, blk, exp, lo, hi: (blk[p], 0)),
        ),
        out_shape=jax.ShapeDtypeStruct((n_slots, D_MODEL), F32),
        compiler_params=pltpu.CompilerParams(dimension_semantics=("arbitrary",), vmem_limit_bytes=VMEM_LIMIT),
        name="experts",
    )(blk, exp, lo, hi, xs, wg, wu, wd)


def _combine_kernel(starts_ref, idx_ref, h_ref, rg_ref, fw_ref, ys_ref, o_ref, buf, sem, *, tm):
    def issue(t, carry):
        for k in range(TOP_K):
            src = starts_ref[idx_ref[0, 0, 4 * t + k]] + idx_ref[0, 0, 4 * t + 2 + k]
            _row_copy(ys_ref, src, buf.at[k], t, sem).start()
        return carry

    lax.fori_loop(0, tm, issue, 0)

    def drain(t, carry):
        for k in range(TOP_K):
            _row_copy(ys_ref, 0, buf.at[k], 0, sem).wait()
        return carry

    lax.fori_loop(0, tm, drain, 0)
    rg = rg_ref[...]
    h = h_ref[...] + rg[:, 0:1] * buf[0] + rg[:, 1:2] * buf[1]
    o_ref[...] = h * lax.rsqrt(jnp.mean(h * h, axis=-1, keepdims=True) + NORM_EPS) * fw_ref[...]


def _combine(starts, idx3, h, rg, final_w, ys, tm):
    t = h.shape[0]
    return pl.pallas_call(
        functools.partial(_combine_kernel, tm=tm),
        grid_spec=pltpu.PrefetchScalarGridSpec(
            num_scalar_prefetch=1,
            grid=(t // tm,),
            in_specs=[pl.BlockSpec((1, 1, 4 * tm), lambda i, s: (i, 0, 0), memory_space=pltpu.SMEM),
                      pl.BlockSpec((tm, D_MODEL), lambda i, s: (i, 0)),
                      pl.BlockSpec((tm, LANES), lambda i, s: (i, 0)),
                      pl.BlockSpec((1, D_MODEL), lambda i, s: (0, 0)),
                      pl.BlockSpec(memory_space=pl.ANY)],
            out_specs=pl.BlockSpec((tm, D_MODEL), lambda i, s: (i, 0)),
            scratch_shapes=[pltpu.VMEM((TOP_K, tm, D_MODEL), F32), pltpu.SemaphoreType.DMA(())],
        ),
        out_shape=jax.ShapeDtypeStruct((t, D_MODEL), F32),
        compiler_params=pltpu.CompilerParams(dimension_semantics=("arbitrary",), vmem_limit_bytes=VMEM_LIMIT),
        name="combine",
    )(starts, idx3, h, rg, final_w, ys)


def _pair_schedule(counts, n_slots):
    n_blocks = n_slots // SLOT_BLOCK
    n_pairs = n_blocks + N_EXPERTS - 1
    ends = jnp.cumsum(counts)
    starts = ends - counts
    first = starts // SLOT_BLOCK
    last = jnp.maximum(ends - 1, starts) // SLOT_BLOCK
    per_expert = jnp.where(counts > 0, last - first + 1, 0)
    cum = jnp.cumsum(per_expert)
    p = jnp.arange(n_pairs, dtype=jnp.int32)
    e = jnp.minimum(jnp.searchsorted(cum, p, side="right"), N_EXPERTS - 1).astype(jnp.int32)
    valid = p < cum[-1]
    blk = jnp.where(valid, first[e] + p - (cum[e] - per_expert[e]), n_blocks - 1).astype(jnp.int32)
    lo = jnp.where(valid, starts[e], 0).astype(jnp.int32)
    hi = jnp.where(valid, ends[e], 0).astype(jnp.int32)
    return starts.astype(jnp.int32), blk, e, lo, hi


def _layer(x, norm_mix_w, w_in, lb_logits, hgrn_onorm_w, gdn_conv_w, gdn_a_log, gdn_dt_bias, gdn_onorm_w, w_out,
           norm_moe_w, router_group_w, router_expert_w, w_gate, w_up, w_down, final_w, *, tm, n_chunks):
    batch, seq, _ = x.shape
    t = batch * seq
    x2 = x.reshape(t, D_MODEL)
    n_hg = 4 * MIX_HALF
    n_gd = GDN_QKV + MIX_HALF
    wb16 = w_in.astype(BF16)
    wa = wb16[:, 0:n_hg]
    wb = wb16[:, n_hg:n_hg + n_gd]
    wc = jnp.pad(wb16[:, n_hg + n_gd:], ((0, 0), (0, LANES - 2 * HEADS)))
    pa, pb, pc = _inproj(x2, norm_mix_w.reshape(1, D_MODEL), wa, wb, wc, tm)

    o_hg = _hgrn(pa, lb_logits, hgrn_onorm_w.reshape(1, HEAD_DIM), batch, seq, n_chunks)
    alog_row = jnp.pad(gdn_a_log.reshape(1, HEADS), ((0, 0), (0, LANES - HEADS)))
    dtb_row = jnp.pad(gdn_dt_bias.reshape(1, HEADS), ((0, 0), (0, LANES - HEADS)))
    o_gd = _gdn(pb, pc, gdn_conv_w, alog_row, dtb_row, gdn_onorm_w.reshape(1, HEAD_DIM), batch, seq, n_chunks)

    wr = jnp.pad(jnp.concatenate([router_group_w, router_expert_w], axis=1),
                 ((0, 0), (0, LANES - N_GROUPS - N_EXPERTS)))
    h, hn, ri, rg, cnt = _route(x2, o_hg, o_gd, w_out.astype(BF16), norm_moe_w.reshape(1, D_MODEL), wr, tm)

    counts = cnt[0, N_GROUPS:N_GROUPS + N_EXPERTS].astype(jnp.int32)
    starts, blk, exp, lo, hi = _pair_schedule(counts, t * TOP_K)
    idx3 = ri[:, 0:4].reshape(t // tm, 1, 4 * tm)
    xs = _dispatch(starts, idx3, hn, tm)
    ys = _experts(blk, exp, lo, hi, xs, w_gate.astype(BF16), w_up.astype(BF16), w_down.astype(BF16))
    out = _combine(starts, idx3, h, rg, final_w.reshape(1, D_MODEL), ys, tm)
    return out.reshape(batch, seq, D_MODEL)


def kernel(x, norm_mix_w, w_in, hgrn_lb_logits, hgrn_onorm_w, gdn_conv_w, gdn_a_log, gdn_dt_bias, gdn_onorm_w, w_out, norm_moe_w, router_group_w, router_expert_w, expert_w_gate, expert_w_up, expert_w_down, final_norm_w):
    return _layer(x, norm_mix_w[0], w_in[0], hgrn_lb_logits, hgrn_onorm_w[0], gdn_conv_w[0], gdn_a_log[0],
                  gdn_dt_bias[0], gdn_onorm_w[0], w_out[0], norm_moe_w[0], router_group_w[0], router_expert_w[0],
                  expert_w_gate[0], expert_w_up[0], expert_w_down[0], final_norm_w, tm=256, n_chunks=2)
```

```python
import functools

import numpy as np
import jax
import jax.numpy as jnp
from jax import lax
from jax.experimental import pallas as pl
from jax.experimental.pallas import tpu as pltpu

F32 = jnp.float32
BF16 = jnp.bfloat16
HI = lax.Precision.HIGHEST

D_MODEL = 1024
HEADS = 4
HEAD_DIM = 128
MIX_HALF = HEADS * HEAD_DIM
CHUNK = 64
GROUP = 2
N_STACKS = HEADS // GROUP
STACK = GROUP * CHUNK
CONV_K = 4
N_GROUPS = 8
EXPERTS_PER_GROUP = 8
N_EXPERTS = N_GROUPS * EXPERTS_PER_GROUP
TOP_K = 2
EXPERT_FF = 256
NORM_EPS = 1e-6
LANES = 128
SLOT_BLOCK = 256
VMEM_LIMIT = 56 * 1024 * 1024


def _sigmoid(x):
    return 1.0 / (1.0 + jnp.exp(-x))


def _silu(x):
    return x * _sigmoid(x)


def _dot(a, b, precision=None):
    return jnp.dot(a, b, preferred_element_type=F32, precision=precision)


def _dot_nt(a, b, precision=None):
    return lax.dot_general(a, b, (((1,), (1,)), ((), ())), preferred_element_type=F32, precision=precision)


def _dot_tn(a, b, precision=None):
    return lax.dot_general(a, b, (((0,), (0,)), ((), ())), preferred_element_type=F32, precision=precision)


def _bdot(a, b):
    return _dot(a.astype(BF16), b.astype(BF16))


def _masked_sum(mask, x):
    hi = x.astype(BF16)
    r1 = x - hi.astype(F32)
    mid = r1.astype(BF16)
    lo = (r1 - mid.astype(F32)).astype(BF16)
    return (_dot(mask, lo) + _dot(mask, mid)) + _dot(mask, hi)


def _stack_heads(a, p):
    return jnp.concatenate([a[:, h * HEAD_DIM:(h + 1) * HEAD_DIM] for h in range(p * GROUP, (p + 1) * GROUP)],
                           axis=0)


def _each(f, *lists):
    return [f(*args) for args in zip(*lists)]


def _inproj_kernel(x_ref, nw_ref, wa_ref, wb_ref, wc_ref, oa_ref, ob_ref, oc_ref):
    x = x_ref[...]
    ms = jnp.mean(x * x, axis=-1, keepdims=True)
    hn = (x * lax.rsqrt(ms + NORM_EPS) * nw_ref[...]).astype(BF16)
    oa_ref[...] = _dot(hn, wa_ref[...])
    ob_ref[...] = _dot(hn, wb_ref[...])
    oc_ref[...] = _dot(hn, wc_ref[...])


def _inproj(x2, norm_w, wa, wb, wc, tm):
    t = x2.shape[0]
    na, nb, nc = wa.shape[1], wb.shape[1], wc.shape[1]
    const = lambda i: (0, 0)
    return pl.pallas_call(
        _inproj_kernel,
        grid=(t // tm,),
        in_specs=[
            pl.BlockSpec((tm, D_MODEL), lambda i: (i, 0)),
            pl.BlockSpec((1, D_MODEL), const),
            pl.BlockSpec((D_MODEL, na), const),
            pl.BlockSpec((D_MODEL, nb), const),
            pl.BlockSpec((D_MODEL, nc), const),
        ],
        out_specs=[
            pl.BlockSpec((tm, na), lambda i: (i, 0)),
            pl.BlockSpec((tm, nb), lambda i: (i, 0)),
            pl.BlockSpec((tm, nc), lambda i: (i, 0)),
        ],
        out_shape=[
            jax.ShapeDtypeStruct((t, na), F32),
            jax.ShapeDtypeStruct((t, nb), F32),
            jax.ShapeDtypeStruct((t, nc), F32),
        ],
        compiler_params=pltpu.CompilerParams(dimension_semantics=("arbitrary",), vmem_limit_bytes=VMEM_LIMIT),
        name="inproj",
    )(x2, norm_w, wa, wb, wc)


HGRN_LEVELS = (32, 16, 8, 4, 2, 1)
DIAG_CODE = len(HGRN_LEVELS)
NONE_CODE = DIAG_CODE + 1


def _hgrn_arg_matrix():
    t = np.arange(CHUNK)[:, None]
    u = np.arange(CHUNK)[None, :]
    mats = [u <= t, u > t]
    for b in HGRN_LEVELS:
        odd = (t // b) % 2 == 1
        start = (t // b) * b
        mats.append(np.where(odd, (u > start) & (u <= t), (u > t) & (u <= start + b)))
    return np.concatenate(mats, axis=0).astype(np.float32)


def _hgrn_level_codes():
    idx = np.arange(STACK)
    h, t = idx // CHUNK, idx % CHUNK
    same = h[:, None] == h[None, :]
    tt, ss = t[:, None], t[None, :]
    code = np.full((STACK, STACK), NONE_CODE, np.int32)
    code[same & (tt == ss)] = DIAG_CODE
    for l, b in enumerate(HGRN_LEVELS):
        sib = (tt // (2 * b) == ss // (2 * b)) & ((tt // b) % 2 == 1) & ((ss // b) % 2 == 0)
        code[same & sib] = l
    return code


def _hgrn_kernel(q_ref, f_ref, i_ref, g_ref, lbl_ref, onw_ref, marg_ref, code_ref, o_ref, st_ref, *, n_chunks):
    @pl.when(pl.program_id(1) == 0)
    def _():
        st_ref[...] = jnp.zeros_like(st_ref)

    lbl = lbl_ref[...]
    lmax = jnp.max(lbl, axis=0, keepdims=True)
    lexp = jnp.exp(lbl - lmax)
    lb = lexp[0:1, :] / jnp.sum(lexp, axis=0, keepdims=True)
    onw = onw_ref[...]
    marg = marg_ref[...]
    code = code_ref[...]
    trow = lax.broadcasted_iota(jnp.int32, (STACK, HEAD_DIM), 0) & (CHUNK - 1)

    q_blk = _silu(q_ref[...]) * (HEAD_DIM ** -0.5)
    fg_blk = lb + (1.0 - lb) * _sigmoid(f_ref[...])
    k_blk = 1.0 - fg_blk
    lf_blk = jnp.log(fg_blk)
    v_blk = i_ref[...]

    rows = [slice(c * CHUNK, (c + 1) * CHUNK) for c in range(n_chunks)]
    e_chunk = [jnp.exp(_masked_sum(marg, lf_blk[r])) for r in rows]
    units = [(c, p) for c in range(n_chunks) for p in range(N_STACKS)]
    qs = [_stack_heads(q_blk[rows[c]], p) for c, p in units]
    ks = [_stack_heads(k_blk[rows[c]], p) for c, p in units]
    vs = [_stack_heads(v_blk[rows[c]], p).astype(BF16) for c, p in units]
    e_part = lambda n: [_stack_heads(e_chunk[c][n * CHUNK:(n + 1) * CHUNK], p) for c, p in units]
    cum_e, suf_e = e_part(0), e_part(1)
    att = _each(lambda q, k: jnp.where(code == DIAG_CODE, _dot_nt(q.astype(BF16), k.astype(BF16)), 0.0), qs, ks)
    for l, b in enumerate(HGRN_LEVELS):
        x = _each(lambda q, k, el: (jnp.where((trow & b) != 0, q, k) * el).astype(BF16), qs, ks, e_part(2 + l))
        att = _each(lambda xi, a: jnp.where(code == l, _dot_nt(xi, xi), a), x, att)
    o_intra = _each(lambda a, v: _dot(a.astype(BF16), v), att, vs)
    qc = _each(lambda q, e: (q * e).astype(BF16), qs, cum_e)
    kd = _each(lambda k, e: (k * e).astype(BF16), ks, suf_e)

    state = [st_ref[h] for h in range(HEADS)]
    for u, (c, p) in enumerate(units):
        for i in range(GROUP):
            h = p * GROUP + i
            hr = slice(i * CHUNK, (i + 1) * CHUNK)
            hc = slice(h * HEAD_DIM, (h + 1) * HEAD_DIM)
            o = _dot_nt(qc[u][hr], state[h].astype(BF16)) + o_intra[u][hr]
            decay = cum_e[u][i * CHUNK + CHUNK - 1:i * CHUNK + CHUNK, :]
            state[h] = decay * state[h] + _dot_tn(vs[u][hr], kd[u][hr])
            y = o * lax.rsqrt(jnp.mean(o * o, axis=-1, keepdims=True) + NORM_EPS) * onw
            o_ref[rows[c], hc] = (y * _silu(g_ref[rows[c], hc])).astype(o_ref.dtype)
    for h in range(HEADS):
        st_ref[h] = state[h]


def _hgrn(pa, lb_logits, onorm_w, batch, seq, n_chunks):
    rows = n_chunks * CHUNK
    steps = seq // rows
    col = lambda j: pl.BlockSpec((rows, MIX_HALF), lambda b, s, j=j: (b * steps + s, j))
    const = lambda b, s: (0, 0)
    marg = jnp.asarray(_hgrn_arg_matrix(), BF16)
    code = jnp.asarray(_hgrn_level_codes())
    return pl.pallas_call(
        functools.partial(_hgrn_kernel, n_chunks=n_chunks),
        grid=(batch, steps),
        in_specs=[col(0), col(1), col(2), col(3),
                  pl.BlockSpec(lb_logits.shape, const),
                  pl.BlockSpec((1, HEAD_DIM), const),
                  pl.BlockSpec(marg.shape, const),
                  pl.BlockSpec(code.shape, const)],
        out_specs=pl.BlockSpec((rows, MIX_HALF), lambda b, s: (b * steps + s, 0)),
        out_shape=jax.ShapeDtypeStruct((batch * seq, MIX_HALF), BF16),
        scratch_shapes=[pltpu.VMEM((HEADS, HEAD_DIM, HEAD_DIM), F32)],
        compiler_params=pltpu.CompilerParams(dimension_semantics=("arbitrary", "arbitrary"),
                                             vmem_limit_bytes=VMEM_LIMIT),
        name="hgrn2",
    )(pa, pa, pa, pa, lb_logits, onorm_w, marg, code)


GDN_QKV = 3 * MIX_HALF
HIST = 8
C_NONE, C_DIAG, C_B16, C_B32, C_B64 = 0, 1, 2, 3, 4
MASKED_EXPONENT = -1e30


def _gdn_codes():
    idx = np.arange(STACK)
    h, t = idx // CHUNK, idx % CHUNK
    same = h[:, None] == h[None, :]
    tt, ss = t[:, None], t[None, :]
    code = np.full((STACK, STACK), C_NONE, np.int32)
    low = same & (ss < tt)
    code[low] = C_B64
    code[low & (tt // 32 == ss // 32)] = C_B32
    code[low & (tt // 16 == ss // 16)] = C_B16
    code[same & (tt == ss)] = C_DIAG
    incl = (same & (ss <= tt)).astype(np.float32)
    return code, incl


def _gdn_kernel(qkv_ref, z_ref, ab_ref, cw_ref, alog_ref, dtb_ref, onw_ref, code_ref, incl_ref,
                o_ref, st_ref, ubuf, *, n_chunks):
    rows_blk = n_chunks * CHUNK

    @pl.when(pl.program_id(1) == 0)
    def _():
        st_ref[...] = jnp.zeros_like(st_ref)
        ubuf[0:HIST, :] = jnp.zeros((HIST, GDN_QKV), F32)

    ubuf[HIST:HIST + rows_blk, :] = qkv_ref[...]
    cw = cw_ref[...]
    conv = cw[CONV_K - 1:CONV_K, :] * ubuf[HIST:HIST + rows_blk, :]
    for j in range(1, CONV_K):
        conv = conv + cw[CONV_K - 1 - j:CONV_K - j, :] * ubuf[HIST - j:HIST - j + rows_blk, :]
    ubuf[0:HIST, :] = ubuf[rows_blk:rows_blk + HIST, :]
    qkv = _silu(conv)

    onw = onw_ref[...]
    code = code_ref[...]
    incl_m = incl_ref[...]
    incl = code >= C_DIAG
    eye = (code == C_DIAG).astype(F32)

    rows = [slice(c * CHUNK, (c + 1) * CHUNK) for c in range(n_chunks)]
    units = [(c, p) for c in range(n_chunks) for p in range(N_STACKS)]

    def l2n(a):
        return a * lax.rsqrt(jnp.sum(a * a, axis=-1, keepdims=True) + 1e-6)

    qs = [l2n(_stack_heads(qkv[rows[c], 0:MIX_HALF], p)) * (HEAD_DIM ** -0.5) for c, p in units]
    ks = [l2n(_stack_heads(qkv[rows[c], MIX_HALF:2 * MIX_HALF], p)) for c, p in units]
    vs = [_stack_heads(qkv[rows[c], 2 * MIX_HALF:3 * MIX_HALF], p) for c, p in units]
    ab_blk = ab_ref[...]
    xa = ab_blk + dtb_ref[...]
    softplus = jnp.maximum(xa, 0.0) + jnp.log(1.0 + jnp.exp(-jnp.abs(xa)))
    g_all = -jnp.exp(alog_ref[...]) * softplus
    beta_all = _sigmoid(ab_blk)

    def head_cols(a, c, p, first_lane):
        return jnp.concatenate([a[rows[c], first_lane + h:first_lane + h + 1]
                                for h in range(p * GROUP, (p + 1) * GROUP)], axis=0)

    g_st = [head_cols(g_all, c, p, 0) for c, p in units]
    beta_st = [head_cols(beta_all, c, p, HEADS) for c, p in units]
    gc = [_masked_sum(incl_m, jnp.broadcast_to(g, (STACK, HEAD_DIM))) for g in g_st]
    dec = [jnp.exp(jnp.where(incl, g - jnp.transpose(g)[0:1, :], MASKED_EXPONENT)) for g in gc]
    kb = [k.astype(BF16) for k in ks]
    kk = _each(_dot_nt, kb, kb)
    qk = _each(_dot_nt, [q.astype(BF16) for q in qs], kb)
    am = _each(lambda b, k2, d: b * k2 * d, beta_st, kk, dec)
    a16 = [jnp.where(code == C_B16, a, 0.0) for a in am]
    n32 = [jnp.where(code == C_B32, a, 0.0) for a in am]
    n64 = [jnp.where(code == C_B64, a, 0.0) for a in am]
    a2 = _each(_bdot, a16, a16)
    a4 = _each(_bdot, a2, a2)
    a8 = _each(_bdot, a4, a4)
    pinv = [eye - a for a in a16]
    for apow in (a2, a4, a8):
        pinv = _each(lambda pi, ai: pi + _bdot(pi, ai), pinv, apow)
    for nlev in (n32, n64):
        t = _each(_bdot, nlev, pinv)
        pinv = _each(lambda pi, ti: pi - _bdot(pi, ti), pinv, t)
    egc = [jnp.exp(g) for g in gc]
    rhs = _each(lambda k, v, b, e: jnp.concatenate([k * (b * e), v * b], axis=1), ks, vs, beta_st, egc)
    wu = _each(_bdot, pinv, rhs)
    w_c = [a[:, 0:HEAD_DIM].astype(BF16) for a in wu]
    u_c = [a[:, HEAD_DIM:] for a in wu]
    qkm = _each(lambda a, d: (a * d).astype(BF16), qk, dec)
    qg = _each(lambda q, e: (q * e).astype(BF16), qs, egc)
    glast = [[g[i * CHUNK + CHUNK - 1:(i + 1) * CHUNK, :] for i in range(GROUP)] for g in gc]
    kdec = [[(k[i * CHUNK:(i + 1) * CHUNK] * jnp.exp(gl[i] - g[i * CHUNK:(i + 1) * CHUNK])).astype(BF16)
             for i in range(GROUP)] for k, g, gl in zip(ks, gc, glast)]
    gend = [[jnp.exp(gi) for gi in gl] for gl in glast]

    state = [st_ref[h] for h in range(HEADS)]
    for u, (c, p) in enumerate(units):
        heads = range(p * GROUP, (p + 1) * GROUP)
        hrs = [slice(i * CHUNK, (i + 1) * CHUNK) for i in range(GROUP)]
        s_b = [state[h].astype(BF16) for h in heads]
        v_new = jnp.concatenate([u_c[u][hr] - _dot_nt(w_c[u][hr], s) for hr, s in zip(hrs, s_b)], axis=0)
        v_new = v_new.astype(BF16)
        o_st = jnp.concatenate([_dot_nt(qg[u][hr], s) for hr, s in zip(hrs, s_b)], axis=0) + _dot(qkm[u], v_new)
        for i, h in enumerate(heads):
            hc = slice(h * HEAD_DIM, (h + 1) * HEAD_DIM)
            state[h] = gend[u][i] * state[h] + _dot_tn(v_new[hrs[i]], kdec[u][i])
            o = o_st[hrs[i]]
            y = o * lax.rsqrt(jnp.mean(o * o, axis=-1, keepdims=True) + NORM_EPS) * onw
            o_ref[rows[c], hc] = (y * _silu(z_ref[rows[c], hc])).astype(o_ref.dtype)
    for h in range(HEADS):
        st_ref[h] = state[h]


def _gdn(pb, pc, conv_w, alog_row, dtb_row, onorm_w, batch, seq, n_chunks):
    rows = n_chunks * CHUNK
    steps = seq // rows
    const = lambda b, s: (0, 0)
    code, incl = _gdn_codes()
    code, incl = jnp.asarray(code), jnp.asarray(incl, BF16)
    sq = pl.BlockSpec((STACK, STACK), const)
    row = pl.BlockSpec((1, LANES), const)
    return pl.pallas_call(
        functools.partial(_gdn_kernel, n_chunks=n_chunks),
        grid=(batch, steps),
        in_specs=[pl.BlockSpec((rows, GDN_QKV), lambda b, s: (b * steps + s, 0)),
                  pl.BlockSpec((rows, MIX_HALF), lambda b, s: (b * steps + s, GDN_QKV // MIX_HALF)),
                  pl.BlockSpec((rows, LANES), lambda b, s: (b * steps + s, 0)),
                  pl.BlockSpec((CONV_K, GDN_QKV), const),
                  row, row, row, sq, sq],
        out_specs=pl.BlockSpec((rows, MIX_HALF), lambda b, s: (b * steps + s, 0)),
        out_shape=jax.ShapeDtypeStruct((batch * seq, MIX_HALF), BF16),
        scratch_shapes=[pltpu.VMEM((HEADS, HEAD_DIM, HEAD_DIM), F32),
                        pltpu.VMEM((HIST + rows, GDN_QKV), F32)],
        compiler_params=pltpu.CompilerParams(dimension_semantics=("arbitrary", "arbitrary"),
                                             vmem_limit_bytes=VMEM_LIMIT),
        name="gdn",
    )(pb, pb, pc, conv_w, alog_row, dtb_row, onorm_w, code, incl)


def _route_kernel(x_ref, ohg_ref, ogd_ref, wo_ref, nw_ref, wr_ref, ltri_ref,
                  h_ref, hn_ref, ri_ref, rg_ref, cnt_ref, cnt_scr):
    @pl.when(pl.program_id(0) == 0)
    def _():
        cnt_scr[...] = jnp.zeros_like(cnt_scr)

    mix = _dot(ohg_ref[...], wo_ref[0:MIX_HALF, :]) + _dot(ogd_ref[...], wo_ref[MIX_HALF:2 * MIX_HALF, :])
    h = x_ref[...] + mix
    h_ref[...] = h
    hn = h * lax.rsqrt(jnp.mean(h * h, axis=-1, keepdims=True) + NORM_EPS) * nw_ref[...]
    hn_ref[...] = hn
    hn_hi = hn.astype(BF16)
    hn_lo = (hn - hn_hi.astype(F32)).astype(BF16)
    part = _dot(hn_hi, wr_ref[...])
    logits = (_dot(hn_lo, wr_ref[:, 0:LANES]) + part[:, LANES:]) + part[:, 0:LANES]
    tm = logits.shape[0]
    lane = lax.broadcasted_iota(jnp.int32, (tm, LANES), 1)
    neg = jnp.float32(-jnp.inf)
    big = jnp.int32(LANES)

    def first_max(vals):
        m = jnp.max(vals, axis=-1, keepdims=True)
        return m, jnp.min(jnp.where(vals == m, lane, big), axis=-1, keepdims=True)

    gl = jnp.where(lane < N_GROUPS, logits, neg)
    gmax, gidx = first_max(gl)
    p_group = 1.0 / jnp.sum(jnp.exp(gl - gmax), axis=-1, keepdims=True)
    lo = N_GROUPS + EXPERTS_PER_GROUP * gidx
    el = jnp.where((lane >= lo) & (lane < lo + EXPERTS_PER_GROUP), logits, neg)
    m1, i1 = first_max(el)
    m2, i2 = first_max(jnp.where(lane == i1, neg, el))
    r = jnp.exp(m2 - m1)
    gate1 = p_group / (1.0 + r)
    gate2 = p_group * r / (1.0 + r)
    hot1 = lane == i1
    hot2 = lane == i2
    onehot = jnp.where(hot1 | hot2, 1.0, 0.0)
    before = _dot(ltri_ref[...], onehot.astype(BF16)) + cnt_scr[...]
    rank1 = jnp.sum(jnp.where(hot1, before, 0.0), axis=-1, keepdims=True)
    rank2 = jnp.sum(jnp.where(hot2, before, 0.0), axis=-1, keepdims=True)
    cnt = cnt_scr[...] + jnp.sum(onehot, axis=0, keepdims=True)
    cnt_scr[...] = cnt
    cnt_ref[...] = cnt
    ri = jnp.where(lane == 0, i1 - N_GROUPS,
                   jnp.where(lane == 1, i2 - N_GROUPS,
                             jnp.where(lane == 2, rank1.astype(jnp.int32),
                                       jnp.where(lane == 3, rank2.astype(jnp.int32), 0))))
    ri_ref[...] = ri
    rg_ref[...] = jnp.where(lane == 0, gate1, jnp.where(lane == 1, gate2, 0.0))


def _route(x2, ohg, ogd, wo, norm_w, wr, tm):
    t = x2.shape[0]
    ltri = jnp.asarray(np.tril(np.ones((tm, tm), np.float32), -1), BF16)
    const = lambda i: (0, 0)
    tile = lambda n: pl.BlockSpec((tm, n), lambda i: (i, 0))
    return pl.pallas_call(
        _route_kernel,
        grid=(t // tm,),
        in_specs=[tile(D_MODEL), tile(MIX_HALF), tile(MIX_HALF),
                  pl.BlockSpec((D_MODEL, D_MODEL), const),
                  pl.BlockSpec((1, D_MODEL), const),
                  pl.BlockSpec((D_MODEL, 2 * LANES), const),
                  pl.BlockSpec((tm, tm), const)],
        out_specs=[tile(D_MODEL), tile(D_MODEL), tile(LANES), tile(LANES), pl.BlockSpec((1, LANES), const)],
        out_shape=[jax.ShapeDtypeStruct((t, D_MODEL), F32),
                   jax.ShapeDtypeStruct((t, D_MODEL), F32),
                   jax.ShapeDtypeStruct((t, LANES), jnp.int32),
                   jax.ShapeDtypeStruct((t, LANES), F32),
                   jax.ShapeDtypeStruct((1, LANES), F32)],
        scratch_shapes=[pltpu.VMEM((1, LANES), F32)],
        compiler_params=pltpu.CompilerParams(dimension_semantics=("arbitrary",), vmem_limit_bytes=VMEM_LIMIT),
        name="route",
    )(x2, ohg, ogd, wo, norm_w, wr, ltri)


ISSUE_UNROLL = 8


def _row_copy(src_ref, src_row, dst_ref, dst_row, sem):
    return pltpu.make_async_copy(src_ref.at[pl.ds(src_row, 1)], dst_ref.at[pl.ds(dst_row, 1)], sem)


def _dispatch_kernel(starts_ref, idx_ref, hn_ref, xs_ref, sem, *, tm):
    def issue(t, carry):
        for k in range(TOP_K):
            dest = starts_ref[idx_ref[0, 0, 4 * t + k]] + idx_ref[0, 0, 4 * t + 2 + k]
            _row_copy(hn_ref, t, xs_ref, dest, sem).start(priority=k % 2)
        return carry

    lax.fori_loop(0, tm, issue, 0, unroll=ISSUE_UNROLL)
    for k in range(TOP_K):
        pltpu.make_async_copy(hn_ref, xs_ref.at[pl.ds(0, tm)], sem).wait()


def _dispatch(starts, idx3, hn, tm):
    t = hn.shape[0]
    return pl.pallas_call(
        functools.partial(_dispatch_kernel, tm=tm),
        grid_spec=pltpu.PrefetchScalarGridSpec(
            num_scalar_prefetch=1,
            grid=(t // tm,),
            in_specs=[pl.BlockSpec((1, 1, 4 * tm), lambda i, s: (i, 0, 0), memory_space=pltpu.SMEM),
                      pl.BlockSpec((tm, D_MODEL), lambda i, s: (i, 0))],
            out_specs=pl.BlockSpec(memory_space=pl.ANY),
            scratch_shapes=[pltpu.SemaphoreType.DMA(())],
        ),
        out_shape=jax.ShapeDtypeStruct((t * TOP_K, D_MODEL), F32),
        compiler_params=pltpu.CompilerParams(dimension_semantics=("arbitrary",), vmem_limit_bytes=VMEM_LIMIT),
        name="dispatch",
    )(starts, idx3, hn)


def _expert_kernel(blk_ref, exp_ref, lo_ref, hi_ref, xs_ref, wg_ref, wu_ref, wd_ref, ys_ref):
    p = pl.program_id(0)
    xb = xs_ref[...].astype(BF16)
    a = _dot(xb, wg_ref[0])
    b = _dot(xb, wu_ref[0])
    y = _dot((_silu(a) * b).astype(BF16), wd_ref[0])
    slot = blk_ref[p] * SLOT_BLOCK + lax.broadcasted_iota(jnp.int32, (SLOT_BLOCK, 1), 0)
    y = jnp.where((slot >= lo_ref[p]) & (slot < hi_ref[p]), y, 0.0)
    first = jnp.logical_or(p == 0, blk_ref[p] != blk_ref[jnp.maximum(p - 1, 0)])

    @pl.when(first)
    def _():
        ys_ref[...] = y

    @pl.when(jnp.logical_not(first))
    def _():
        ys_ref[...] += y


def _experts(blk, exp, lo, hi, xs, wg, wu, wd):
    n_slots = xs.shape[0]
    n_pairs = blk.shape[0]
    return pl.pallas_call(
        _expert_kernel,
        grid_spec=pltpu.PrefetchScalarGridSpec(
            num_scalar_prefetch=4,
            grid=(n_pairs,),
            in_specs=[pl.BlockSpec((SLOT_BLOCK, D_MODEL), lambda p, blk, exp, lo, hi: (blk[p], 0)),
                      pl.BlockSpec((1, D_MODEL, EXPERT_FF), lambda p, blk, exp, lo, hi: (exp[p], 0, 0)),
                      pl.BlockSpec((1, D_MODEL, EXPERT_FF), lambda p, blk, exp, lo, hi: (exp[p], 0, 0)),
                      pl.BlockSpec((1, EXPERT_FF, D_MODEL), lambda p, blk, exp, lo, hi: (exp[p], 0, 0))],
            out_specs=pl.BlockSpec((SLOT_BLOCK, D_MODEL), lambda p, blk, exp, lo, hi: (blk[p], 0)),
        ),
        out_shape=jax.ShapeDtypeStruct((n_slots, D_MODEL), F32),
        compiler_params=pltpu.CompilerParams(dimension_semantics=("arbitrary",), vmem_limit_bytes=VMEM_LIMIT),
        name="experts",
    )(blk, exp, lo, hi, xs, wg, wu, wd)


def _combine_kernel(starts_ref, idx_ref, h_ref, rg_ref, fw_ref, ys_ref, o_ref, buf, sem, *, tm):
    def issue(t, carry):
        for k in range(TOP_K):
            src = starts_ref[idx_ref[0, 0, 4 * t + k]] + idx_ref[0, 0, 4 * t + 2 + k]
            _row_copy(ys_ref, src, buf.at[k], t, sem).start(priority=k % 2)
        return carry

    lax.fori_loop(0, tm, issue, 0, unroll=ISSUE_UNROLL)
    for k in range(TOP_K):
        pltpu.make_async_copy(ys_ref.at[pl.ds(0, tm)], buf.at[k], sem).wait()
    rg = rg_ref[...]
    h = h_ref[...] + rg[:, 0:1] * buf[0] + rg[:, 1:2] * buf[1]
    o_ref[...] = h * lax.rsqrt(jnp.mean(h * h, axis=-1, keepdims=True) + NORM_EPS) * fw_ref[...]


def _combine(starts, idx3, h, rg, final_w, ys, tm):
    t = h.shape[0]
    return pl.pallas_call(
        functools.partial(_combine_kernel, tm=tm),
        grid_spec=pltpu.PrefetchScalarGridSpec(
            num_scalar_prefetch=1,
            grid=(t // tm,),
            in_specs=[pl.BlockSpec((1, 1, 4 * tm), lambda i, s: (i, 0, 0), memory_space=pltpu.SMEM),
                      pl.BlockSpec((tm, D_MODEL), lambda i, s: (i, 0)),
                      pl.BlockSpec((tm, LANES), lambda i, s: (i, 0)),
                      pl.BlockSpec((1, D_MODEL), lambda i, s: (0, 0)),
                      pl.BlockSpec(memory_space=pl.ANY)],
            out_specs=pl.BlockSpec((tm, D_MODEL), lambda i, s: (i, 0)),
            scratch_shapes=[pltpu.VMEM((TOP_K, tm, D_MODEL), F32), pltpu.SemaphoreType.DMA(())],
        ),
        out_shape=jax.ShapeDtypeStruct((t, D_MODEL), F32),
        compiler_params=pltpu.CompilerParams(dimension_semantics=("arbitrary",), vmem_limit_bytes=VMEM_LIMIT),
        name="combine",
    )(starts, idx3, h, rg, final_w, ys)


def _pair_schedule(counts, n_slots):
    n_blocks = n_slots // SLOT_BLOCK
    n_pairs = n_blocks + N_EXPERTS - 1
    ends = jnp.cumsum(counts)
    starts = ends - counts
    first = starts // SLOT_BLOCK
    last = jnp.maximum(ends - 1, starts) // SLOT_BLOCK
    per_expert = jnp.where(counts > 0, last - first + 1, 0)
    cum = jnp.cumsum(per_expert)
    p = jnp.arange(n_pairs, dtype=jnp.int32)
    e = jnp.minimum(jnp.sum(cum[None, :] <= p[:, None], axis=1), N_EXPERTS - 1).astype(jnp.int32)
    valid = p < cum[-1]
    blk = jnp.where(valid, first[e] + p - (cum[e] - per_expert[e]), n_blocks - 1).astype(jnp.int32)
    lo = jnp.where(valid, starts[e], 0).astype(jnp.int32)
    hi = jnp.where(valid, ends[e], 0).astype(jnp.int32)
    return starts.astype(jnp.int32), blk, e, lo, hi


def _layer(x, norm_mix_w, w_in, lb_logits, hgrn_onorm_w, gdn_conv_w, gdn_a_log, gdn_dt_bias, gdn_onorm_w, w_out,
           norm_moe_w, router_group_w, router_expert_w, w_gate, w_up, w_down, final_w, *, tm, n_chunks):
    batch, seq, _ = x.shape
    t = batch * seq
    x2 = x.reshape(t, D_MODEL)
    n_hg = 4 * MIX_HALF
    n_gd = GDN_QKV + MIX_HALF
    wb16 = w_in.astype(BF16)
    wa = wb16[:, 0:n_hg]
    wb = wb16[:, n_hg:n_hg + n_gd]
    wc = jnp.pad(wb16[:, n_hg + n_gd:], ((0, 0), (0, LANES - 2 * HEADS)))
    pa, pb, pc = _inproj(x2, norm_mix_w.reshape(1, D_MODEL), wa, wb, wc, tm)

    o_hg = _hgrn(pa, lb_logits, hgrn_onorm_w.reshape(1, HEAD_DIM), batch, seq, n_chunks)
    alog_row = jnp.pad(gdn_a_log.reshape(1, HEADS), ((0, 0), (0, LANES - HEADS)))
    dtb_row = jnp.pad(gdn_dt_bias.reshape(1, HEADS), ((0, 0), (0, LANES - HEADS)))
    o_gd = _gdn(pb, pc, gdn_conv_w, alog_row, dtb_row, gdn_onorm_w.reshape(1, HEAD_DIM), batch, seq, n_chunks)

    wr = jnp.pad(jnp.concatenate([router_group_w, router_expert_w], axis=1),
                 ((0, 0), (0, LANES - N_GROUPS - N_EXPERTS)))
    wr_hi = wr.astype(BF16)
    wr = jnp.concatenate([wr_hi, (wr - wr_hi.astype(F32)).astype(BF16)], axis=1)
    h, hn, ri, rg, cnt = _route(x2, o_hg, o_gd, w_out.astype(BF16), norm_moe_w.reshape(1, D_MODEL), wr, tm)

    counts = cnt[0, N_GROUPS:N_GROUPS + N_EXPERTS].astype(jnp.int32)
    starts, blk, exp, lo, hi = _pair_schedule(counts, t * TOP_K)
    idx3 = ri[:, 0:4].reshape(t // tm, 1, 4 * tm)
    xs = _dispatch(starts, idx3, hn, tm)
    ys = _experts(blk, exp, lo, hi, xs, w_gate.astype(BF16), w_up.astype(BF16), w_down.astype(BF16))
    out = _combine(starts, idx3, h, rg, final_w.reshape(1, D_MODEL), ys, tm)
    return out.reshape(batch, seq, D_MODEL)


def kernel(x, norm_mix_w, w_in, hgrn_lb_logits, hgrn_onorm_w, gdn_conv_w, gdn_a_log, gdn_dt_bias, gdn_onorm_w, w_out, norm_moe_w, router_group_w, router_expert_w, expert_w_gate, expert_w_up, expert_w_down, final_norm_w):
    return _layer(x, norm_mix_w[0], w_in[0], hgrn_lb_logits, hgrn_onorm_w[0], gdn_conv_w[0], gdn_a_log[0],
                  gdn_dt_bias[0], gdn_onorm_w[0], w_out[0], norm_moe_w[0], router_group_w[0], router_expert_w[0],
                  expert_w_gate[0], expert_w_up[0], expert_w_down[0], final_norm_w, tm=256, n_chunks=4)
```

```python
import functools

import numpy as np
import jax
import jax.numpy as jnp
from jax import lax
from jax.experimental import pallas as pl
from jax.experimental.pallas import tpu as pltpu

F32 = jnp.float32
BF16 = jnp.bfloat16
HI = lax.Precision.HIGHEST

D_MODEL = 1024
HEADS = 4
HEAD_DIM = 128
MIX_HALF = HEADS * HEAD_DIM
CHUNK = 64
GROUP = 2
N_STACKS = HEADS // GROUP
STACK = GROUP * CHUNK
CONV_K = 4
N_GROUPS = 8
EXPERTS_PER_GROUP = 8
N_EXPERTS = N_GROUPS * EXPERTS_PER_GROUP
TOP_K = 2
EXPERT_FF = 256
NORM_EPS = 1e-6
LANES = 128
SLOT_BLOCK = 256
VMEM_LIMIT = 56 * 1024 * 1024


def _sigmoid(x):
    return 1.0 / (1.0 + jnp.exp(-x))


def _silu(x):
    return x * _sigmoid(x)


def _dot(a, b, precision=None):
    return jnp.dot(a, b, preferred_element_type=F32, precision=precision)


def _dot_nt(a, b, precision=None):
    return lax.dot_general(a, b, (((1,), (1,)), ((), ())), preferred_element_type=F32, precision=precision)


def _dot_tn(a, b, precision=None):
    return lax.dot_general(a, b, (((0,), (0,)), ((), ())), preferred_element_type=F32, precision=precision)


def _bdot(a, b):
    return _dot(a.astype(BF16), b.astype(BF16))


def _masked_sum(mask, x):
    hi = x.astype(BF16)
    r1 = x - hi.astype(F32)
    mid = r1.astype(BF16)
    lo = (r1 - mid.astype(F32)).astype(BF16)
    return (_dot(mask, lo) + _dot(mask, mid)) + _dot(mask, hi)


def _stack_heads(a, p):
    return jnp.concatenate([a[:, h * HEAD_DIM:(h + 1) * HEAD_DIM] for h in range(p * GROUP, (p + 1) * GROUP)],
                           axis=0)


def _each(f, *lists):
    return [f(*args) for args in zip(*lists)]


def _inproj_kernel(x_ref, nw_ref, wa_ref, wb_ref, wc_ref, oa_ref, ob_ref, oc_ref):
    x = x_ref[...]
    ms = jnp.mean(x * x, axis=-1, keepdims=True)
    hn = (x * lax.rsqrt(ms + NORM_EPS) * nw_ref[...]).astype(BF16)
    oa_ref[...] = _dot(hn, wa_ref[...])
    ob_ref[...] = _dot(hn, wb_ref[...])
    oc_ref[...] = _dot(hn, wc_ref[...])


def _inproj(x2, norm_w, wa, wb, wc, tm):
    t = x2.shape[0]
    na, nb, nc = wa.shape[1], wb.shape[1], wc.shape[1]
    const = lambda i: (0, 0)
    return pl.pallas_call(
        _inproj_kernel,
        grid=(t // tm,),
        in_specs=[
            pl.BlockSpec((tm, D_MODEL), lambda i: (i, 0)),
            pl.BlockSpec((1, D_MODEL), const),
            pl.BlockSpec((D_MODEL, na), const),
            pl.BlockSpec((D_MODEL, nb), const),
            pl.BlockSpec((D_MODEL, nc), const),
        ],
        out_specs=[
            pl.BlockSpec((tm, na), lambda i: (i, 0)),
            pl.BlockSpec((tm, nb), lambda i: (i, 0)),
            pl.BlockSpec((tm, nc), lambda i: (i, 0)),
        ],
        out_shape=[
            jax.ShapeDtypeStruct((t, na), F32),
            jax.ShapeDtypeStruct((t, nb), F32),
            jax.ShapeDtypeStruct((t, nc), F32),
        ],
        compiler_params=pltpu.CompilerParams(dimension_semantics=("arbitrary",), vmem_limit_bytes=VMEM_LIMIT),
        name="inproj",
    )(x2, norm_w, wa, wb, wc)


HGRN_LEVELS = (32, 16, 8, 4, 2, 1)
DIAG_CODE = len(HGRN_LEVELS)
NONE_CODE = DIAG_CODE + 1


def _hgrn_arg_matrix():
    t = np.arange(CHUNK)[:, None]
    u = np.arange(CHUNK)[None, :]
    mats = [u <= t, u > t]
    for b in HGRN_LEVELS:
        odd = (t // b) % 2 == 1
        start = (t // b) * b
        mats.append(np.where(odd, (u > start) & (u <= t), (u > t) & (u <= start + b)))
    return np.concatenate(mats, axis=0).astype(np.float32)


def _hgrn_level_codes():
    idx = np.arange(STACK)
    h, t = idx // CHUNK, idx % CHUNK
    same = h[:, None] == h[None, :]
    tt, ss = t[:, None], t[None, :]
    code = np.full((STACK, STACK), NONE_CODE, np.int32)
    code[same & (tt == ss)] = DIAG_CODE
    for l, b in enumerate(HGRN_LEVELS):
        sib = (tt // (2 * b) == ss // (2 * b)) & ((tt // b) % 2 == 1) & ((ss // b) % 2 == 0)
        code[same & sib] = l
    return code


def _hgrn_kernel(q_ref, f_ref, i_ref, g_ref, lbl_ref, onw_ref, marg_ref, code_ref, o_ref, st_ref, *, n_chunks):
    @pl.when(pl.program_id(1) == 0)
    def _():
        st_ref[...] = jnp.zeros_like(st_ref)

    lbl = lbl_ref[...]
    lmax = jnp.max(lbl, axis=0, keepdims=True)
    lexp = jnp.exp(lbl - lmax)
    lb = lexp[0:1, :] / jnp.sum(lexp, axis=0, keepdims=True)
    onw = onw_ref[...]
    marg = marg_ref[...]
    code = code_ref[...]
    trow = lax.broadcasted_iota(jnp.int32, (STACK, HEAD_DIM), 0) & (CHUNK - 1)

    q_blk = _silu(q_ref[...]) * (HEAD_DIM ** -0.5)
    fg_blk = lb + (1.0 - lb) * _sigmoid(f_ref[...])
    k_blk = 1.0 - fg_blk
    lf_blk = jnp.log(fg_blk)
    v_blk = i_ref[...]

    rows = [slice(c * CHUNK, (c + 1) * CHUNK) for c in range(n_chunks)]
    e_chunk = [jnp.exp(_masked_sum(marg, lf_blk[r])) for r in rows]
    units = [(c, p) for c in range(n_chunks) for p in range(N_STACKS)]
    qs = [_stack_heads(q_blk[rows[c]], p) for c, p in units]
    ks = [_stack_heads(k_blk[rows[c]], p) for c, p in units]
    vs = [_stack_heads(v_blk[rows[c]], p).astype(BF16) for c, p in units]
    e_part = lambda n: [_stack_heads(e_chunk[c][n * CHUNK:(n + 1) * CHUNK], p) for c, p in units]
    cum_e, suf_e = e_part(0), e_part(1)
    att = _each(lambda q, k: jnp.where(code == DIAG_CODE, _dot_nt(q.astype(BF16), k.astype(BF16)), 0.0), qs, ks)
    for l, b in enumerate(HGRN_LEVELS):
        x = _each(lambda q, k, el: (jnp.where((trow & b) != 0, q, k) * el).astype(BF16), qs, ks, e_part(2 + l))
        att = _each(lambda xi, a: jnp.where(code == l, _dot_nt(xi, xi), a), x, att)
    o_intra = _each(lambda a, v: _dot(a.astype(BF16), v), att, vs)
    qc = _each(lambda q, e: (q * e).astype(BF16), qs, cum_e)
    kd = _each(lambda k, e: (k * e).astype(BF16), ks, suf_e)

    state = [st_ref[h] for h in range(HEADS)]
    for u, (c, p) in enumerate(units):
        for i in range(GROUP):
            h = p * GROUP + i
            hr = slice(i * CHUNK, (i + 1) * CHUNK)
            hc = slice(h * HEAD_DIM, (h + 1) * HEAD_DIM)
            o = _dot_nt(qc[u][hr], state[h].astype(BF16)) + o_intra[u][hr]
            decay = cum_e[u][i * CHUNK + CHUNK - 1:i * CHUNK + CHUNK, :]
            state[h] = decay * state[h] + _dot_tn(vs[u][hr], kd[u][hr])
            y = o * lax.rsqrt(jnp.mean(o * o, axis=-1, keepdims=True) + NORM_EPS) * onw
            o_ref[rows[c], hc] = (y * _silu(g_ref[rows[c], hc])).astype(o_ref.dtype)
    for h in range(HEADS):
        st_ref[h] = state[h]


def _hgrn(pa, lb_logits, onorm_w, batch, seq, n_chunks):
    rows = n_chunks * CHUNK
    steps = seq // rows
    col = lambda j: pl.BlockSpec((rows, MIX_HALF), lambda b, s, j=j: (b * steps + s, j))
    const = lambda b, s: (0, 0)
    marg = jnp.asarray(_hgrn_arg_matrix(), BF16)
    code = jnp.asarray(_hgrn_level_codes())
    return pl.pallas_call(
        functools.partial(_hgrn_kernel, n_chunks=n_chunks),
        grid=(batch, steps),
        in_specs=[col(0), col(1), col(2), col(3),
                  pl.BlockSpec(lb_logits.shape, const),
                  pl.BlockSpec((1, HEAD_DIM), const),
                  pl.BlockSpec(marg.shape, const),
                  pl.BlockSpec(code.shape, const)],
        out_specs=pl.BlockSpec((rows, MIX_HALF), lambda b, s: (b * steps + s, 0)),
        out_shape=jax.ShapeDtypeStruct((batch * seq, MIX_HALF), BF16),
        scratch_shapes=[pltpu.VMEM((HEADS, HEAD_DIM, HEAD_DIM), F32)],
        compiler_params=pltpu.CompilerParams(dimension_semantics=("arbitrary", "arbitrary"),
                                             vmem_limit_bytes=VMEM_LIMIT),
        name="hgrn2",
    )(pa, pa, pa, pa, lb_logits, onorm_w, marg, code)


GDN_QKV = 3 * MIX_HALF
HIST = 8
C_NONE, C_DIAG, C_B16, C_B32, C_B64 = 0, 1, 2, 3, 4
MASKED_EXPONENT = -1e30


def _gdn_codes():
    idx = np.arange(STACK)
    h, t = idx // CHUNK, idx % CHUNK
    same = h[:, None] == h[None, :]
    tt, ss = t[:, None], t[None, :]
    code = np.full((STACK, STACK), C_NONE, np.int32)
    low = same & (ss < tt)
    code[low] = C_B64
    code[low & (tt // 32 == ss // 32)] = C_B32
    code[low & (tt // 16 == ss // 16)] = C_B16
    code[same & (tt == ss)] = C_DIAG
    incl = (same & (ss <= tt)).astype(np.float32)
    return code, incl


def _gdn_kernel(qkv_ref, z_ref, ab_ref, cw_ref, alog_ref, dtb_ref, onw_ref, code_ref, incl_ref,
                o_ref, st_ref, ubuf, *, n_chunks):
    rows_blk = n_chunks * CHUNK

    @pl.when(pl.program_id(1) == 0)
    def _():
        st_ref[...] = jnp.zeros_like(st_ref)
        ubuf[0:HIST, :] = jnp.zeros((HIST, GDN_QKV), F32)

    ubuf[HIST:HIST + rows_blk, :] = qkv_ref[...]
    cw = cw_ref[...]
    conv = cw[CONV_K - 1:CONV_K, :] * ubuf[HIST:HIST + rows_blk, :]
    for j in range(1, CONV_K):
        conv = conv + cw[CONV_K - 1 - j:CONV_K - j, :] * ubuf[HIST - j:HIST - j + rows_blk, :]
    ubuf[0:HIST, :] = ubuf[rows_blk:rows_blk + HIST, :]
    qkv = _silu(conv)

    onw = onw_ref[...]
    code = code_ref[...]
    incl_m = incl_ref[...]
    incl = code >= C_DIAG
    eye = (code == C_DIAG).astype(F32)

    rows = [slice(c * CHUNK, (c + 1) * CHUNK) for c in range(n_chunks)]
    units = [(c, p) for c in range(n_chunks) for p in range(N_STACKS)]

    def l2n(a):
        return a * lax.rsqrt(jnp.sum(a * a, axis=-1, keepdims=True) + 1e-6)

    qs = [l2n(_stack_heads(qkv[rows[c], 0:MIX_HALF], p)) * (HEAD_DIM ** -0.5) for c, p in units]
    ks = [l2n(_stack_heads(qkv[rows[c], MIX_HALF:2 * MIX_HALF], p)) for c, p in units]
    vs = [_stack_heads(qkv[rows[c], 2 * MIX_HALF:3 * MIX_HALF], p) for c, p in units]
    ab_blk = ab_ref[...]
    xa = ab_blk + dtb_ref[...]
    softplus = jnp.maximum(xa, 0.0) + jnp.log(1.0 + jnp.exp(-jnp.abs(xa)))
    g_all = -jnp.exp(alog_ref[...]) * softplus
    beta_all = _sigmoid(ab_blk)

    def head_cols(a, c, p, first_lane):
        return jnp.concatenate([a[rows[c], first_lane + h:first_lane + h + 1]
                                for h in range(p * GROUP, (p + 1) * GROUP)], axis=0)

    g_st = [head_cols(g_all, c, p, 0) for c, p in units]
    beta_st = [head_cols(beta_all, c, p, HEADS) for c, p in units]
    gc = [_masked_sum(incl_m, jnp.broadcast_to(g, (STACK, HEAD_DIM))) for g in g_st]
    dec = [jnp.exp(jnp.where(incl, g - jnp.transpose(g)[0:1, :], MASKED_EXPONENT)) for g in gc]
    kb = [k.astype(BF16) for k in ks]
    kk = _each(_dot_nt, kb, kb)
    qk = _each(_dot_nt, [q.astype(BF16) for q in qs], kb)
    am = _each(lambda b, k2, d: b * k2 * d, beta_st, kk, dec)
    a16 = [jnp.where(code == C_B16, a, 0.0) for a in am]
    n32 = [jnp.where(code == C_B32, a, 0.0) for a in am]
    n64 = [jnp.where(code == C_B64, a, 0.0) for a in am]
    a2 = _each(_bdot, a16, a16)
    a4 = _each(_bdot, a2, a2)
    a8 = _each(_bdot, a4, a4)
    pinv = [eye - a for a in a16]
    for apow in (a2, a4, a8):
        pinv = _each(lambda pi, ai: pi + _bdot(pi, ai), pinv, apow)
    for nlev in (n32, n64):
        t = _each(_bdot, nlev, pinv)
        pinv = _each(lambda pi, ti: pi - _bdot(pi, ti), pinv, t)
    egc = [jnp.exp(g) for g in gc]
    rhs = _each(lambda k, v, b, e: jnp.concatenate([k * (b * e), v * b], axis=1), ks, vs, beta_st, egc)
    wu = _each(_bdot, pinv, rhs)
    w_c = [a[:, 0:HEAD_DIM].astype(BF16) for a in wu]
    u_c = [a[:, HEAD_DIM:] for a in wu]
    qkm = _each(lambda a, d: (a * d).astype(BF16), qk, dec)
    qg = _each(lambda q, e: (q * e).astype(BF16), qs, egc)
    glast = [[g[i * CHUNK + CHUNK - 1:(i + 1) * CHUNK, :] for i in range(GROUP)] for g in gc]
    kdec = [[(k[i * CHUNK:(i + 1) * CHUNK] * jnp.exp(gl[i] - g[i * CHUNK:(i + 1) * CHUNK])).astype(BF16)
             for i in range(GROUP)] for k, g, gl in zip(ks, gc, glast)]
    gend = [[jnp.exp(gi) for gi in gl] for gl in glast]

    state = [st_ref[h] for h in range(HEADS)]
    for u, (c, p) in enumerate(units):
        heads = range(p * GROUP, (p + 1) * GROUP)
        hrs = [slice(i * CHUNK, (i + 1) * CHUNK) for i in range(GROUP)]
        s_b = [state[h].astype(BF16) for h in heads]
        v_new = jnp.concatenate([u_c[u][hr] - _dot_nt(w_c[u][hr], s) for hr, s in zip(hrs, s_b)], axis=0)
        v_new = v_new.astype(BF16)
        o_st = jnp.concatenate([_dot_nt(qg[u][hr], s) for hr, s in zip(hrs, s_b)], axis=0) + _dot(qkm[u], v_new)
        for i, h in enumerate(heads):
            hc = slice(h * HEAD_DIM, (h + 1) * HEAD_DIM)
            state[h] = gend[u][i] * state[h] + _dot_tn(v_new[hrs[i]], kdec[u][i])
            o = o_st[hrs[i]]
            y = o * lax.rsqrt(jnp.mean(o * o, axis=-1, keepdims=True) + NORM_EPS) * onw
            o_ref[rows[c], hc] = (y * _silu(z_ref[rows[c], hc])).astype(o_ref.dtype)
    for h in range(HEADS):
        st_ref[h] = state[h]


def _gdn(pb, pc, conv_w, alog_row, dtb_row, onorm_w, batch, seq, n_chunks):
    rows = n_chunks * CHUNK
    steps = seq // rows
    const = lambda b, s: (0, 0)
    code, incl = _gdn_codes()
    code, incl = jnp.asarray(code), jnp.asarray(incl, BF16)
    sq = pl.BlockSpec((STACK, STACK), const)
    row = pl.BlockSpec((1, LANES), const)
    return pl.pallas_call(
        functools.partial(_gdn_kernel, n_chunks=n_chunks),
        grid=(batch, steps),
        in_specs=[pl.BlockSpec((rows, GDN_QKV), lambda b, s: (b * steps + s, 0)),
                  pl.BlockSpec((rows, MIX_HALF), lambda b, s: (b * steps + s, GDN_QKV // MIX_HALF)),
                  pl.BlockSpec((rows, LANES), lambda b, s: (b * steps + s, 0)),
                  pl.BlockSpec((CONV_K, GDN_QKV), const),
                  row, row, row, sq, sq],
        out_specs=pl.BlockSpec((rows, MIX_HALF), lambda b, s: (b * steps + s, 0)),
        out_shape=jax.ShapeDtypeStruct((batch * seq, MIX_HALF), BF16),
        scratch_shapes=[pltpu.VMEM((HEADS, HEAD_DIM, HEAD_DIM), F32),
                        pltpu.VMEM((HIST + rows, GDN_QKV), F32)],
        compiler_params=pltpu.CompilerParams(dimension_semantics=("arbitrary", "arbitrary"),
                                             vmem_limit_bytes=VMEM_LIMIT),
        name="gdn",
    )(pb, pb, pc, conv_w, alog_row, dtb_row, onorm_w, code, incl)


def _route_kernel(x_ref, ohg_ref, ogd_ref, wo_ref, nw_ref, wr_ref, ltri_ref,
                  h_ref, hn_ref, ri_ref, rg_ref, cnt_ref, cnt_scr):
    @pl.when(pl.program_id(0) == 0)
    def _():
        cnt_scr[...] = jnp.zeros_like(cnt_scr)

    mix = _dot(ohg_ref[...], wo_ref[0:MIX_HALF, :]) + _dot(ogd_ref[...], wo_ref[MIX_HALF:2 * MIX_HALF, :])
    h = x_ref[...] + mix
    h_ref[...] = h
    hn = h * lax.rsqrt(jnp.mean(h * h, axis=-1, keepdims=True) + NORM_EPS) * nw_ref[...]
    _store_rows_as_tiles(hn_ref, hn)
    hn_hi = hn.astype(BF16)
    hn_lo = (hn - hn_hi.astype(F32)).astype(BF16)
    part = _dot(hn_hi, wr_ref[...])
    logits = (_dot(hn_lo, wr_ref[:, 0:LANES]) + part[:, LANES:]) + part[:, 0:LANES]
    tm = logits.shape[0]
    lane = lax.broadcasted_iota(jnp.int32, (tm, LANES), 1)
    neg = jnp.float32(-jnp.inf)
    big = jnp.int32(LANES)

    def first_max(vals):
        m = jnp.max(vals, axis=-1, keepdims=True)
        return m, jnp.min(jnp.where(vals == m, lane, big), axis=-1, keepdims=True)

    gl = jnp.where(lane < N_GROUPS, logits, neg)
    gmax, gidx = first_max(gl)
    p_group = 1.0 / jnp.sum(jnp.exp(gl - gmax), axis=-1, keepdims=True)
    lo = N_GROUPS + EXPERTS_PER_GROUP * gidx
    el = jnp.where((lane >= lo) & (lane < lo + EXPERTS_PER_GROUP), logits, neg)
    m1, i1 = first_max(el)
    m2, i2 = first_max(jnp.where(lane == i1, neg, el))
    r = jnp.exp(m2 - m1)
    gate1 = p_group / (1.0 + r)
    gate2 = p_group * r / (1.0 + r)
    hot1 = lane == i1
    hot2 = lane == i2
    onehot = jnp.where(hot1 | hot2, 1.0, 0.0)
    before = _dot(ltri_ref[...], onehot.astype(BF16)) + cnt_scr[...]
    rank1 = jnp.sum(jnp.where(hot1, before, 0.0), axis=-1, keepdims=True)
    rank2 = jnp.sum(jnp.where(hot2, before, 0.0), axis=-1, keepdims=True)
    cnt = cnt_scr[...] + jnp.sum(onehot, axis=0, keepdims=True)
    cnt_scr[...] = cnt
    cnt_ref[...] = cnt
    ri = jnp.where(lane == 0, i1 - N_GROUPS,
                   jnp.where(lane == 1, i2 - N_GROUPS,
                             jnp.where(lane == 2, rank1.astype(jnp.int32),
                                       jnp.where(lane == 3, rank2.astype(jnp.int32), 0))))
    ri_ref[...] = ri
    rg_ref[...] = jnp.where(lane == 0, gate1, jnp.where(lane == 1, gate2, 0.0))


def _route(x2, ohg, ogd, wo, norm_w, wr, tm):
    t = x2.shape[0]
    ltri = jnp.asarray(np.tril(np.ones((tm, tm), np.float32), -1), BF16)
    const = lambda i: (0, 0)
    tile = lambda n: pl.BlockSpec((tm, n), lambda i: (i, 0))
    return pl.pallas_call(
        _route_kernel,
        grid=(t // tm,),
        in_specs=[tile(D_MODEL), tile(MIX_HALF), tile(MIX_HALF),
                  pl.BlockSpec((D_MODEL, D_MODEL), const),
                  pl.BlockSpec((1, D_MODEL), const),
                  pl.BlockSpec((D_MODEL, 2 * LANES), const),
                  pl.BlockSpec((tm, tm), const)],
        out_specs=[tile(D_MODEL), pl.BlockSpec((tm * PIECES, LANES), lambda i: (i, 0)), tile(LANES), tile(LANES),
                   pl.BlockSpec((1, LANES), const)],
        out_shape=[jax.ShapeDtypeStruct((t, D_MODEL), F32),
                   jax.ShapeDtypeStruct((t * PIECES, LANES), F32),
                   jax.ShapeDtypeStruct((t, LANES), jnp.int32),
                   jax.ShapeDtypeStruct((t, LANES), F32),
                   jax.ShapeDtypeStruct((1, LANES), F32)],
        scratch_shapes=[pltpu.VMEM((1, LANES), F32)],
        compiler_params=pltpu.CompilerParams(dimension_semantics=("arbitrary",), vmem_limit_bytes=VMEM_LIMIT),
        name="route",
    )(x2, ohg, ogd, wo, norm_w, wr, ltri)


ISSUE_UNROLL = 8


PIECES = D_MODEL // LANES


def _store_rows_as_tiles(ref, x):
    n = x.shape[0]
    for s in range(PIECES):
        ref[pl.ds(s, n, stride=PIECES), :] = x[:, s * LANES:(s + 1) * LANES]


def _load_rows_from_tiles(ref, n):
    return jnp.concatenate([ref[pl.ds(s, n, stride=PIECES), :] for s in range(PIECES)], axis=1)


def _row_copy(src_ref, src_row, dst_ref, dst_row, sem):
    src = src_ref.at[pl.ds(pl.multiple_of(src_row * PIECES, PIECES), PIECES)]
    dst = dst_ref.at[pl.ds(pl.multiple_of(dst_row * PIECES, PIECES), PIECES)]
    return pltpu.make_async_copy(src, dst, sem)


def _dispatch_kernel(starts_ref, idx_ref, hn_ref, xs_ref, sem, *, tm):
    def issue(t, carry):
        for k in range(TOP_K):
            dest = starts_ref[idx_ref[0, 0, 4 * t + k]] + idx_ref[0, 0, 4 * t + 2 + k]
            _row_copy(hn_ref, t, xs_ref, dest, sem).start(priority=k % 2)
        return carry

    lax.fori_loop(0, tm, issue, 0, unroll=ISSUE_UNROLL)
    for k in range(TOP_K):
        pltpu.make_async_copy(hn_ref, xs_ref.at[pl.ds(0, tm * PIECES)], sem).wait()


def _dispatch(starts, idx3, hn, tm):
    t = hn.shape[0] // PIECES
    return pl.pallas_call(
        functools.partial(_dispatch_kernel, tm=tm),
        grid_spec=pltpu.PrefetchScalarGridSpec(
            num_scalar_prefetch=1,
            grid=(t // tm,),
            in_specs=[pl.BlockSpec((1, 1, 4 * tm), lambda i, s: (i, 0, 0), memory_space=pltpu.SMEM),
                      pl.BlockSpec((tm * PIECES, LANES), lambda i, s: (i, 0))],
            out_specs=pl.BlockSpec(memory_space=pl.ANY),
            scratch_shapes=[pltpu.SemaphoreType.DMA(())],
        ),
        out_shape=jax.ShapeDtypeStruct((t * TOP_K * PIECES, LANES), F32),
        compiler_params=pltpu.CompilerParams(dimension_semantics=("arbitrary",), vmem_limit_bytes=VMEM_LIMIT),
        name="dispatch",
    )(starts, idx3, hn)


def _expert_kernel(blk_ref, exp_ref, lo_ref, hi_ref, xs_ref, wg_ref, wu_ref, wd_ref, ys_ref, wgu_b, wd_b):
    p = pl.program_id(0)

    @pl.when(jnp.logical_or(p == 0, exp_ref[p] != exp_ref[jnp.maximum(p - 1, 0)]))
    def _():
        wgu_b[:, 0:EXPERT_FF] = wg_ref[0].astype(BF16)
        wgu_b[:, EXPERT_FF:2 * EXPERT_FF] = wu_ref[0].astype(BF16)
        wd_b[...] = wd_ref[0].astype(BF16)

    xb = _load_rows_from_tiles(xs_ref, SLOT_BLOCK).astype(BF16)
    ab = _dot(xb, wgu_b[...])
    hb = _silu(ab[:, 0:EXPERT_FF]) * ab[:, EXPERT_FF:2 * EXPERT_FF]
    y = _dot(hb.astype(BF16), wd_b[...])
    slot = blk_ref[p] * SLOT_BLOCK + lax.broadcasted_iota(jnp.int32, (SLOT_BLOCK, 1), 0)
    y = jnp.where((slot >= lo_ref[p]) & (slot < hi_ref[p]), y, 0.0)
    first = jnp.logical_or(p == 0, blk_ref[p] != blk_ref[jnp.maximum(p - 1, 0)])

    @pl.when(first)
    def _():
        _store_rows_as_tiles(ys_ref, y)

    @pl.when(jnp.logical_not(first))
    def _():
        _store_rows_as_tiles(ys_ref, _load_rows_from_tiles(ys_ref, SLOT_BLOCK) + y)


def _experts(blk, exp, lo, hi, xs, wg, wu, wd):
    n_pairs = blk.shape[0]
    return pl.pallas_call(
        _expert_kernel,
        grid_spec=pltpu.PrefetchScalarGridSpec(
            num_scalar_prefetch=4,
            grid=(n_pairs,),
            in_specs=[pl.BlockSpec((SLOT_BLOCK * PIECES, LANES), lambda p, blk, exp, lo, hi: (blk[p], 0)),
                      pl.BlockSpec((1, D_MODEL, EXPERT_FF), lambda p, blk, exp, lo, hi: (exp[p], 0, 0)),
                      pl.BlockSpec((1, D_MODEL, EXPERT_FF), lambda p, blk, exp, lo, hi: (exp[p], 0, 0)),
                      pl.BlockSpec((1, EXPERT_FF, D_MODEL), lambda p, blk, exp, lo, hi: (exp[p], 0, 0))],
            out_specs=pl.BlockSpec((SLOT_BLOCK * PIECES, LANES), lambda p, blk, exp, lo, hi: (blk[p], 0)),
            scratch_shapes=[pltpu.VMEM((D_MODEL, 2 * EXPERT_FF), BF16), pltpu.VMEM((EXPERT_FF, D_MODEL), BF16)],
        ),
        out_shape=jax.ShapeDtypeStruct(xs.shape, F32),
        compiler_params=pltpu.CompilerParams(dimension_semantics=("arbitrary",), vmem_limit_bytes=VMEM_LIMIT),
        name="experts",
    )(blk, exp, lo, hi, xs, wg, wu, wd)


def _combine_kernel(starts_ref, idx_ref, h_ref, rg_ref, fw_ref, ys_ref, o_ref, buf, sem, *, tm):
    def issue(t, carry):
        for k in range(TOP_K):
            src = starts_ref[idx_ref[0, 0, 4 * t + k]] + idx_ref[0, 0, 4 * t + 2 + k]
            _row_copy(ys_ref, src, buf.at[k], t, sem).start(priority=k % 2)
        return carry

    lax.fori_loop(0, tm, issue, 0, unroll=ISSUE_UNROLL)
    for k in range(TOP_K):
        pltpu.make_async_copy(ys_ref.at[pl.ds(0, tm * PIECES)], buf.at[k], sem).wait()
    rg = rg_ref[...]
    h = (h_ref[...] + rg[:, 0:1] * _load_rows_from_tiles(buf.at[0], tm)
         + rg[:, 1:2] * _load_rows_from_tiles(buf.at[1], tm))
    o_ref[...] = h * lax.rsqrt(jnp.mean(h * h, axis=-1, keepdims=True) + NORM_EPS) * fw_ref[...]


def _combine(starts, idx3, h, rg, final_w, ys, tm):
    t = h.shape[0]
    return pl.pallas_call(
        functools.partial(_combine_kernel, tm=tm),
        grid_spec=pltpu.PrefetchScalarGridSpec(
            num_scalar_prefetch=1,
            grid=(t // tm,),
            in_specs=[pl.BlockSpec((1, 1, 4 * tm), lambda i, s: (i, 0, 0), memory_space=pltpu.SMEM),
                      pl.BlockSpec((tm, D_MODEL), lambda i, s: (i, 0)),
                      pl.BlockSpec((tm, LANES), lambda i, s: (i, 0)),
                      pl.BlockSpec((1, D_MODEL), lambda i, s: (0, 0)),
                      pl.BlockSpec(memory_space=pl.ANY)],
            out_specs=pl.BlockSpec((tm, D_MODEL), lambda i, s: (i, 0)),
            scratch_shapes=[pltpu.VMEM((TOP_K, tm * PIECES, LANES), F32), pltpu.SemaphoreType.DMA(())],
        ),
        out_shape=jax.ShapeDtypeStruct((t, D_MODEL), F32),
        compiler_params=pltpu.CompilerParams(dimension_semantics=("arbitrary",), vmem_limit_bytes=VMEM_LIMIT),
        name="combine",
    )(starts, idx3, h, rg, final_w, ys)


def _pair_schedule(counts, n_slots):
    n_blocks = n_slots // SLOT_BLOCK
    n_pairs = n_blocks + N_EXPERTS - 1
    ends = jnp.cumsum(counts)
    starts = ends - counts
    first = starts // SLOT_BLOCK
    last = jnp.maximum(ends - 1, starts) // SLOT_BLOCK
    per_expert = jnp.where(counts > 0, last - first + 1, 0)
    cum = jnp.cumsum(per_expert)
    p = jnp.arange(n_pairs, dtype=jnp.int32)
    e = jnp.minimum(jnp.sum(cum[None, :] <= p[:, None], axis=1), N_EXPERTS - 1).astype(jnp.int32)
    valid = p < cum[-1]
    blk = jnp.where(valid, first[e] + p - (cum[e] - per_expert[e]), n_blocks - 1).astype(jnp.int32)
    lo = jnp.where(valid, starts[e], 0).astype(jnp.int32)
    hi = jnp.where(valid, ends[e], 0).astype(jnp.int32)
    return starts.astype(jnp.int32), blk, e, lo, hi


def _layer(x, norm_mix_w, w_in, lb_logits, hgrn_onorm_w, gdn_conv_w, gdn_a_log, gdn_dt_bias, gdn_onorm_w, w_out,
           norm_moe_w, router_group_w, router_expert_w, w_gate, w_up, w_down, final_w, *, tm, n_chunks):
    batch, seq, _ = x.shape
    t = batch * seq
    x2 = x.reshape(t, D_MODEL)
    n_hg = 4 * MIX_HALF
    n_gd = GDN_QKV + MIX_HALF
    wb16 = w_in.astype(BF16)
    wa = wb16[:, 0:n_hg]
    wb = wb16[:, n_hg:n_hg + n_gd]
    wc = jnp.pad(wb16[:, n_hg + n_gd:], ((0, 0), (0, LANES - 2 * HEADS)))
    pa, pb, pc = _inproj(x2, norm_mix_w.reshape(1, D_MODEL), wa, wb, wc, tm)

    o_hg = _hgrn(pa, lb_logits, hgrn_onorm_w.reshape(1, HEAD_DIM), batch, seq, n_chunks)
    alog_row = jnp.pad(gdn_a_log.reshape(1, HEADS), ((0, 0), (0, LANES - HEADS)))
    dtb_row = jnp.pad(gdn_dt_bias.reshape(1, HEADS), ((0, 0), (0, LANES - HEADS)))
    o_gd = _gdn(pb, pc, gdn_conv_w, alog_row, dtb_row, gdn_onorm_w.reshape(1, HEAD_DIM), batch, seq, n_chunks)

    wr = jnp.pad(jnp.concatenate([router_group_w, router_expert_w], axis=1),
                 ((0, 0), (0, LANES - N_GROUPS - N_EXPERTS)))
    wr_hi = wr.astype(BF16)
    wr = jnp.concatenate([wr_hi, (wr - wr_hi.astype(F32)).astype(BF16)], axis=1)
    h, hn, ri, rg, cnt = _route(x2, o_hg, o_gd, w_out.astype(BF16), norm_moe_w.reshape(1, D_MODEL), wr, tm)

    counts = cnt[0, N_GROUPS:N_GROUPS + N_EXPERTS].astype(jnp.int32)
    starts, blk, exp, lo, hi = _pair_schedule(counts, t * TOP_K)
    idx3 = ri[:, 0:4].reshape(t // tm, 1, 4 * tm)
    xs = _dispatch(starts, idx3, hn, tm)
    ys = _experts(blk, exp, lo, hi, xs, w_gate, w_up, w_down)
    out = _combine(starts, idx3, h, rg, final_w.reshape(1, D_MODEL), ys, tm)
    return out.reshape(batch, seq, D_MODEL)


def kernel(x, norm_mix_w, w_in, hgrn_lb_logits, hgrn_onorm_w, gdn_conv_w, gdn_a_log, gdn_dt_bias, gdn_onorm_w, w_out, norm_moe_w, router_group_w, router_expert_w, expert_w_gate, expert_w_up, expert_w_down, final_norm_w):
    return _layer(x, norm_mix_w[0], w_in[0], hgrn_lb_logits, hgrn_onorm_w[0], gdn_conv_w[0], gdn_a_log[0],
                  gdn_dt_bias[0], gdn_onorm_w[0], w_out[0], norm_moe_w[0], router_group_w[0], router_expert_w[0],
                  expert_w_gate[0], expert_w_up[0], expert_w_down[0], final_norm_w, tm=256, n_chunks=4)
```

```python
import functools

import numpy as np
import jax
import jax.numpy as jnp
from jax import lax
from jax.experimental import pallas as pl
from jax.experimental.pallas import tpu as pltpu

F32 = jnp.float32
BF16 = jnp.bfloat16
HI = lax.Precision.HIGHEST

D_MODEL = 1024
HEADS = 4
HEAD_DIM = 128
MIX_HALF = HEADS * HEAD_DIM
CHUNK = 64
GROUP = 2
N_STACKS = HEADS // GROUP
STACK = GROUP * CHUNK
CONV_K = 4
N_GROUPS = 8
EXPERTS_PER_GROUP = 8
N_EXPERTS = N_GROUPS * EXPERTS_PER_GROUP
TOP_K = 2
EXPERT_FF = 256
NORM_EPS = 1e-6
LANES = 128
SLOT_BLOCK = 256
VMEM_LIMIT = 56 * 1024 * 1024


def _sigmoid(x):
    return 1.0 / (1.0 + jnp.exp(-x))


def _silu(x):
    return x * _sigmoid(x)


def _dot(a, b, precision=None):
    return jnp.dot(a, b, preferred_element_type=F32, precision=precision)


def _dot_nt(a, b, precision=None):
    return lax.dot_general(a, b, (((1,), (1,)), ((), ())), preferred_element_type=F32, precision=precision)


def _dot_tn(a, b, precision=None):
    return lax.dot_general(a, b, (((0,), (0,)), ((), ())), preferred_element_type=F32, precision=precision)


def _bdot(a, b):
    return _dot(a.astype(BF16), b.astype(BF16))


def _masked_sum(mask3, x):
    hi = x.astype(BF16)
    r1 = x - hi.astype(F32)
    mid = r1.astype(BF16)
    lo = (r1 - mid.astype(F32)).astype(BF16)
    return _dot(mask3, jnp.concatenate([hi, mid, lo], axis=0))


def _triple(mask):
    return np.concatenate([mask, mask, mask], axis=1)


def _stack_heads(a, p):
    return jnp.concatenate([a[:, h * HEAD_DIM:(h + 1) * HEAD_DIM] for h in range(p * GROUP, (p + 1) * GROUP)],
                           axis=0)


def _each(f, *lists):
    return [f(*args) for args in zip(*lists)]


def _inproj_kernel(x_ref, nw_ref, wa_ref, wb_ref, wc_ref, oa_ref, ob_ref, oc_ref):
    x = x_ref[...]
    ms = jnp.mean(x * x, axis=-1, keepdims=True)
    hn = (x * lax.rsqrt(ms + NORM_EPS) * nw_ref[...]).astype(BF16)
    oa_ref[...] = _dot(hn, wa_ref[...])
    ob_ref[...] = _dot(hn, wb_ref[...])
    oc_ref[...] = _dot(hn, wc_ref[...])


def _inproj(x2, norm_w, wa, wb, wc, tm):
    t = x2.shape[0]
    na, nb, nc = wa.shape[1], wb.shape[1], wc.shape[1]
    const = lambda i: (0, 0)
    return pl.pallas_call(
        _inproj_kernel,
        grid=(t // tm,),
        in_specs=[
            pl.BlockSpec((tm, D_MODEL), lambda i: (i, 0)),
            pl.BlockSpec((1, D_MODEL), const),
            pl.BlockSpec((D_MODEL, na), const),
            pl.BlockSpec((D_MODEL, nb), const),
            pl.BlockSpec((D_MODEL, nc), const),
        ],
        out_specs=[
            pl.BlockSpec((tm, na), lambda i: (i, 0)),
            pl.BlockSpec((tm, nb), lambda i: (i, 0)),
            pl.BlockSpec((tm, nc), lambda i: (i, 0)),
        ],
        out_shape=[
            jax.ShapeDtypeStruct((t, na), F32),
            jax.ShapeDtypeStruct((t, nb), F32),
            jax.ShapeDtypeStruct((t, nc), F32),
        ],
        compiler_params=pltpu.CompilerParams(dimension_semantics=("arbitrary",), vmem_limit_bytes=VMEM_LIMIT),
        name="inproj",
    )(x2, norm_w, wa, wb, wc)


HGRN_LEVELS = (32, 16, 8, 4, 2, 1)
DIAG_CODE = len(HGRN_LEVELS)
NONE_CODE = DIAG_CODE + 1


def _hgrn_arg_matrix():
    t = np.arange(CHUNK)[:, None]
    u = np.arange(CHUNK)[None, :]
    mats = [u <= t, u > t]
    for b in HGRN_LEVELS:
        odd = (t // b) % 2 == 1
        start = (t // b) * b
        mats.append(np.where(odd, (u > start) & (u <= t), (u > t) & (u <= start + b)))
    return np.concatenate(mats, axis=0).astype(np.float32)


def _hgrn_level_codes():
    idx = np.arange(STACK)
    h, t = idx // CHUNK, idx % CHUNK
    same = h[:, None] == h[None, :]
    tt, ss = t[:, None], t[None, :]
    code = np.full((STACK, STACK), NONE_CODE, np.int32)
    code[same & (tt == ss)] = DIAG_CODE
    for l, b in enumerate(HGRN_LEVELS):
        sib = (tt // (2 * b) == ss // (2 * b)) & ((tt // b) % 2 == 1) & ((ss // b) % 2 == 0)
        code[same & sib] = l
    return code


def _hgrn_kernel(q_ref, f_ref, i_ref, g_ref, lbl_ref, onw_ref, marg_ref, code_ref, o_ref, st_ref, *, n_chunks):
    @pl.when(pl.program_id(1) == 0)
    def _():
        st_ref[...] = jnp.zeros_like(st_ref)

    lbl = lbl_ref[...]
    lmax = jnp.max(lbl, axis=0, keepdims=True)
    lexp = jnp.exp(lbl - lmax)
    lb = lexp[0:1, :] / jnp.sum(lexp, axis=0, keepdims=True)
    onw = onw_ref[...]
    marg = marg_ref[...]
    code = code_ref[...]
    trow = lax.broadcasted_iota(jnp.int32, (STACK, HEAD_DIM), 0) & (CHUNK - 1)

    q_blk = _silu(q_ref[...]) * (HEAD_DIM ** -0.5)
    fg_blk = lb + (1.0 - lb) * _sigmoid(f_ref[...])
    k_blk = 1.0 - fg_blk
    lf_blk = jnp.log(fg_blk)
    v_blk = i_ref[...]

    rows = [slice(c * CHUNK, (c + 1) * CHUNK) for c in range(n_chunks)]
    e_chunk = [jnp.exp(_masked_sum(marg, lf_blk[r])) for r in rows]
    units = [(c, p) for c in range(n_chunks) for p in range(N_STACKS)]
    qs = [_stack_heads(q_blk[rows[c]], p) for c, p in units]
    ks = [_stack_heads(k_blk[rows[c]], p) for c, p in units]
    vs = [_stack_heads(v_blk[rows[c]], p).astype(BF16) for c, p in units]
    e_part = lambda n: [_stack_heads(e_chunk[c][n * CHUNK:(n + 1) * CHUNK], p) for c, p in units]
    cum_e, suf_e = e_part(0), e_part(1)
    att = _each(lambda q, k: jnp.where(code == DIAG_CODE, _dot_nt(q.astype(BF16), k.astype(BF16)), 0.0), qs, ks)
    for l, b in enumerate(HGRN_LEVELS):
        x = _each(lambda q, k, el: (jnp.where((trow & b) != 0, q, k) * el).astype(BF16), qs, ks, e_part(2 + l))
        att = _each(lambda xi, a: jnp.where(code == l, _dot_nt(xi, xi), a), x, att)
    o_intra = _each(lambda a, v: _dot(a.astype(BF16), v), att, vs)
    qc = _each(lambda q, e: (q * e).astype(BF16), qs, cum_e)
    kd = _each(lambda k, e: (k * e).astype(BF16), ks, suf_e)

    state = [st_ref[h] for h in range(HEADS)]
    for u, (c, p) in enumerate(units):
        for i in range(GROUP):
            h = p * GROUP + i
            hr = slice(i * CHUNK, (i + 1) * CHUNK)
            hc = slice(h * HEAD_DIM, (h + 1) * HEAD_DIM)
            o = _dot_nt(qc[u][hr], state[h].astype(BF16)) + o_intra[u][hr]
            decay = cum_e[u][i * CHUNK + CHUNK - 1:i * CHUNK + CHUNK, :]
            state[h] = decay * state[h] + _dot_tn(vs[u][hr], kd[u][hr])
            y = o * lax.rsqrt(jnp.mean(o * o, axis=-1, keepdims=True) + NORM_EPS) * onw
            o_ref[rows[c], hc] = (y * _silu(g_ref[rows[c], hc])).astype(o_ref.dtype)
    for h in range(HEADS):
        st_ref[h] = state[h]


def _hgrn(pa, lb_logits, onorm_w, batch, seq, n_chunks):
    rows = n_chunks * CHUNK
    steps = seq // rows
    col = lambda j: pl.BlockSpec((rows, MIX_HALF), lambda b, s, j=j: (b * steps + s, j))
    const = lambda b, s: (0, 0)
    marg = jnp.asarray(_triple(_hgrn_arg_matrix()), BF16)
    code = jnp.asarray(_hgrn_level_codes())
    return pl.pallas_call(
        functools.partial(_hgrn_kernel, n_chunks=n_chunks),
        grid=(batch, steps),
        in_specs=[col(0), col(1), col(2), col(3),
                  pl.BlockSpec(lb_logits.shape, const),
                  pl.BlockSpec((1, HEAD_DIM), const),
                  pl.BlockSpec(marg.shape, const),
                  pl.BlockSpec(code.shape, const)],
        out_specs=pl.BlockSpec((rows, MIX_HALF), lambda b, s: (b * steps + s, 0)),
        out_shape=jax.ShapeDtypeStruct((batch * seq, MIX_HALF), BF16),
        scratch_shapes=[pltpu.VMEM((HEADS, HEAD_DIM, HEAD_DIM), F32)],
        compiler_params=pltpu.CompilerParams(dimension_semantics=("arbitrary", "arbitrary"),
                                             vmem_limit_bytes=VMEM_LIMIT),
        name="hgrn2",
    )(pa, pa, pa, pa, lb_logits, onorm_w, marg, code)


GDN_QKV = 3 * MIX_HALF
HIST = 8
C_NONE, C_DIAG, C_B16, C_B32, C_B64 = 0, 1, 2, 3, 4
MASKED_EXPONENT = -1e30


def _gdn_codes():
    idx = np.arange(STACK)
    h, t = idx // CHUNK, idx % CHUNK
    same = h[:, None] == h[None, :]
    tt, ss = t[:, None], t[None, :]
    code = np.full((STACK, STACK), C_NONE, np.int32)
    low = same & (ss < tt)
    code[low] = C_B64
    code[low & (tt // 32 == ss // 32)] = C_B32
    code[low & (tt // 16 == ss // 16)] = C_B16
    code[same & (tt == ss)] = C_DIAG
    incl = (same & (ss <= tt)).astype(np.float32)
    return code, incl


def _gdn_kernel(qkv_ref, z_ref, ab_ref, cw_ref, alog_ref, dtb_ref, onw_ref, code_ref, incl_ref,
                o_ref, st_ref, ubuf, *, n_chunks):
    rows_blk = n_chunks * CHUNK

    @pl.when(pl.program_id(1) == 0)
    def _():
        st_ref[...] = jnp.zeros_like(st_ref)
        ubuf[0:HIST, :] = jnp.zeros((HIST, GDN_QKV), F32)

    ubuf[HIST:HIST + rows_blk, :] = qkv_ref[...]
    cw = cw_ref[...]
    conv = cw[CONV_K - 1:CONV_K, :] * ubuf[HIST:HIST + rows_blk, :]
    for j in range(1, CONV_K):
        conv = conv + cw[CONV_K - 1 - j:CONV_K - j, :] * ubuf[HIST - j:HIST - j + rows_blk, :]
    ubuf[0:HIST, :] = ubuf[rows_blk:rows_blk + HIST, :]
    qkv = _silu(conv)

    onw = onw_ref[...]
    code = code_ref[...]
    incl_m = incl_ref[...]
    incl = code >= C_DIAG
    eye = (code == C_DIAG).astype(F32)

    rows = [slice(c * CHUNK, (c + 1) * CHUNK) for c in range(n_chunks)]
    units = [(c, p) for c in range(n_chunks) for p in range(N_STACKS)]

    def l2n(a):
        return a * lax.rsqrt(jnp.sum(a * a, axis=-1, keepdims=True) + 1e-6)

    qs = [l2n(_stack_heads(qkv[rows[c], 0:MIX_HALF], p)) * (HEAD_DIM ** -0.5) for c, p in units]
    ks = [l2n(_stack_heads(qkv[rows[c], MIX_HALF:2 * MIX_HALF], p)) for c, p in units]
    vs = [_stack_heads(qkv[rows[c], 2 * MIX_HALF:3 * MIX_HALF], p) for c, p in units]
    ab_blk = ab_ref[...]
    xa = ab_blk + dtb_ref[...]
    softplus = jnp.maximum(xa, 0.0) + jnp.log(1.0 + jnp.exp(-jnp.abs(xa)))
    g_all = -jnp.exp(alog_ref[...]) * softplus
    beta_all = _sigmoid(ab_blk)

    def head_cols(a, c, p, first_lane):
        return jnp.concatenate([a[rows[c], first_lane + h:first_lane + h + 1]
                                for h in range(p * GROUP, (p + 1) * GROUP)], axis=0)

    g_st = [head_cols(g_all, c, p, 0) for c, p in units]
    beta_st = [head_cols(beta_all, c, p, HEADS) for c, p in units]
    gc = [_masked_sum(incl_m, jnp.broadcast_to(g, (STACK, HEAD_DIM))) for g in g_st]
    dec = [jnp.exp(jnp.where(incl, g - jnp.transpose(g)[0:1, :], MASKED_EXPONENT)) for g in gc]
    kb = [k.astype(BF16) for k in ks]
    kk = _each(_dot_nt, kb, kb)
    qk = _each(_dot_nt, [q.astype(BF16) for q in qs], kb)
    am = _each(lambda b, k2, d: b * k2 * d, beta_st, kk, dec)
    a16 = [jnp.where(code == C_B16, a, 0.0) for a in am]
    n32 = [jnp.where(code == C_B32, a, 0.0) for a in am]
    n64 = [jnp.where(code == C_B64, a, 0.0) for a in am]
    a2 = _each(_bdot, a16, a16)
    a4 = _each(_bdot, a2, a2)
    a8 = _each(_bdot, a4, a4)
    pinv = [eye - a for a in a16]
    for apow in (a2, a4, a8):
        pinv = _each(lambda pi, ai: pi + _bdot(pi, ai), pinv, apow)
    for nlev in (n32, n64):
        t = _each(_bdot, nlev, pinv)
        pinv = _each(lambda pi, ti: pi - _bdot(pi, ti), pinv, t)
    egc = [jnp.exp(g) for g in gc]
    rhs = _each(lambda k, v, b, e: jnp.concatenate([k * (b * e), v * b], axis=1), ks, vs, beta_st, egc)
    wu = _each(_bdot, pinv, rhs)
    w_c = [a[:, 0:HEAD_DIM].astype(BF16) for a in wu]
    u_c = [a[:, HEAD_DIM:] for a in wu]
    qkm = _each(lambda a, d: (a * d).astype(BF16), qk, dec)
    qg = _each(lambda q, e: (q * e).astype(BF16), qs, egc)
    glast = [[g[i * CHUNK + CHUNK - 1:(i + 1) * CHUNK, :] for i in range(GROUP)] for g in gc]
    kdec = [[(k[i * CHUNK:(i + 1) * CHUNK] * jnp.exp(gl[i] - g[i * CHUNK:(i + 1) * CHUNK])).astype(BF16)
             for i in range(GROUP)] for k, g, gl in zip(ks, gc, glast)]
    gend = [[jnp.exp(gi) for gi in gl] for gl in glast]

    state = [st_ref[h] for h in range(HEADS)]
    for u, (c, p) in enumerate(units):
        heads = range(p * GROUP, (p + 1) * GROUP)
        hrs = [slice(i * CHUNK, (i + 1) * CHUNK) for i in range(GROUP)]
        s_b = [state[h].astype(BF16) for h in heads]
        v_new = jnp.concatenate([u_c[u][hr] - _dot_nt(w_c[u][hr], s) for hr, s in zip(hrs, s_b)], axis=0)
        v_new = v_new.astype(BF16)
        o_st = jnp.concatenate([_dot_nt(qg[u][hr], s) for hr, s in zip(hrs, s_b)], axis=0) + _dot(qkm[u], v_new)
        for i, h in enumerate(heads):
            hc = slice(h * HEAD_DIM, (h + 1) * HEAD_DIM)
            state[h] = gend[u][i] * state[h] + _dot_tn(v_new[hrs[i]], kdec[u][i])
            o = o_st[hrs[i]]
            y = o * lax.rsqrt(jnp.mean(o * o, axis=-1, keepdims=True) + NORM_EPS) * onw
            o_ref[rows[c], hc] = (y * _silu(z_ref[rows[c], hc])).astype(o_ref.dtype)
    for h in range(HEADS):
        st_ref[h] = state[h]


def _gdn(pb, pc, conv_w, alog_row, dtb_row, onorm_w, batch, seq, n_chunks):
    rows = n_chunks * CHUNK
    steps = seq // rows
    const = lambda b, s: (0, 0)
    code, incl = _gdn_codes()
    code, incl = jnp.asarray(code), jnp.asarray(_triple(incl), BF16)
    sq = pl.BlockSpec((STACK, STACK), const)
    row = pl.BlockSpec((1, LANES), const)
    return pl.pallas_call(
        functools.partial(_gdn_kernel, n_chunks=n_chunks),
        grid=(batch, steps),
        in_specs=[pl.BlockSpec((rows, GDN_QKV), lambda b, s: (b * steps + s, 0)),
                  pl.BlockSpec((rows, MIX_HALF), lambda b, s: (b * steps + s, GDN_QKV // MIX_HALF)),
                  pl.BlockSpec((rows, LANES), lambda b, s: (b * steps + s, 0)),
                  pl.BlockSpec((CONV_K, GDN_QKV), const),
                  row, row, row, sq, pl.BlockSpec((STACK, 3 * STACK), const)],
        out_specs=pl.BlockSpec((rows, MIX_HALF), lambda b, s: (b * steps + s, 0)),
        out_shape=jax.ShapeDtypeStruct((batch * seq, MIX_HALF), BF16),
        scratch_shapes=[pltpu.VMEM((HEADS, HEAD_DIM, HEAD_DIM), F32),
                        pltpu.VMEM((HIST + rows, GDN_QKV), F32)],
        compiler_params=pltpu.CompilerParams(dimension_semantics=("arbitrary", "arbitrary"),
                                             vmem_limit_bytes=VMEM_LIMIT),
        name="gdn",
    )(pb, pb, pc, conv_w, alog_row, dtb_row, onorm_w, code, incl)


def _route_kernel(x_ref, ohg_ref, ogd_ref, wo_ref, nw_ref, wr_ref, ltri_ref,
                  h_ref, hn_ref, ri_ref, rg_ref, cnt_ref, cnt_scr):
    @pl.when(pl.program_id(0) == 0)
    def _():
        cnt_scr[...] = jnp.zeros_like(cnt_scr)

    mix = _dot(ohg_ref[...], wo_ref[0:MIX_HALF, :]) + _dot(ogd_ref[...], wo_ref[MIX_HALF:2 * MIX_HALF, :])
    h = x_ref[...] + mix
    h_ref[...] = h
    hn = h * lax.rsqrt(jnp.mean(h * h, axis=-1, keepdims=True) + NORM_EPS) * nw_ref[...]
    _store_rows_as_tiles(hn_ref, hn)
    hn_hi = hn.astype(BF16)
    hn_lo = (hn - hn_hi.astype(F32)).astype(BF16)
    part = _dot(hn_hi, wr_ref[...])
    logits = (_dot(hn_lo, wr_ref[:, 0:LANES]) + part[:, LANES:]) + part[:, 0:LANES]
    tm = logits.shape[0]
    lane = lax.broadcasted_iota(jnp.int32, (tm, LANES), 1)
    neg = jnp.float32(-jnp.inf)
    big = jnp.int32(LANES)

    def first_max(vals):
        m = jnp.max(vals, axis=-1, keepdims=True)
        return m, jnp.min(jnp.where(vals == m, lane, big), axis=-1, keepdims=True)

    gl = jnp.where(lane < N_GROUPS, logits, neg)
    gmax, gidx = first_max(gl)
    p_group = 1.0 / jnp.sum(jnp.exp(gl - gmax), axis=-1, keepdims=True)
    lo = N_GROUPS + EXPERTS_PER_GROUP * gidx
    el = jnp.where((lane >= lo) & (lane < lo + EXPERTS_PER_GROUP), logits, neg)
    m1, i1 = first_max(el)
    m2, i2 = first_max(jnp.where(lane == i1, neg, el))
    r = jnp.exp(m2 - m1)
    gate1 = p_group / (1.0 + r)
    gate2 = p_group * r / (1.0 + r)
    hot1 = lane == i1
    hot2 = lane == i2
    onehot = jnp.where(hot1 | hot2, 1.0, 0.0)
    before = _dot(ltri_ref[...], onehot.astype(BF16)) + cnt_scr[...]
    rank1 = jnp.sum(jnp.where(hot1, before, 0.0), axis=-1, keepdims=True)
    rank2 = jnp.sum(jnp.where(hot2, before, 0.0), axis=-1, keepdims=True)
    cnt = cnt_scr[...] + jnp.sum(onehot, axis=0, keepdims=True)
    cnt_scr[...] = cnt
    cnt_ref[...] = cnt
    ri = jnp.where(lane == 0, i1 - N_GROUPS,
                   jnp.where(lane == 1, i2 - N_GROUPS,
                             jnp.where(lane == 2, rank1.astype(jnp.int32),
                                       jnp.where(lane == 3, rank2.astype(jnp.int32), 0))))
    ri_ref[...] = ri
    rg_ref[...] = jnp.where(lane == 0, gate1, jnp.where(lane == 1, gate2, 0.0))


def _route(x2, ohg, ogd, wo, norm_w, wr, tm):
    t = x2.shape[0]
    ltri = jnp.asarray(np.tril(np.ones((tm, tm), np.float32), -1), BF16)
    const = lambda i: (0, 0)
    tile = lambda n: pl.BlockSpec((tm, n), lambda i: (i, 0))
    return pl.pallas_call(
        _route_kernel,
        grid=(t // tm,),
        in_specs=[tile(D_MODEL), tile(MIX_HALF), tile(MIX_HALF),
                  pl.BlockSpec((D_MODEL, D_MODEL), const),
                  pl.BlockSpec((1, D_MODEL), const),
                  pl.BlockSpec((D_MODEL, 2 * LANES), const),
                  pl.BlockSpec((tm, tm), const)],
        out_specs=[tile(D_MODEL), pl.BlockSpec((tm * PIECES, LANES), lambda i: (i, 0)), tile(LANES), tile(LANES),
                   pl.BlockSpec((1, LANES), const)],
        out_shape=[jax.ShapeDtypeStruct((t, D_MODEL), F32),
                   jax.ShapeDtypeStruct((t * PIECES, LANES), F32),
                   jax.ShapeDtypeStruct((t, LANES), jnp.int32),
                   jax.ShapeDtypeStruct((t, LANES), F32),
                   jax.ShapeDtypeStruct((1, LANES), F32)],
        scratch_shapes=[pltpu.VMEM((1, LANES), F32)],
        compiler_params=pltpu.CompilerParams(dimension_semantics=("arbitrary",), vmem_limit_bytes=VMEM_LIMIT),
        name="route",
    )(x2, ohg, ogd, wo, norm_w, wr, ltri)


ISSUE_UNROLL = 8


PIECES = D_MODEL // LANES


def _store_rows_as_tiles(ref, x):
    n = x.shape[0]
    for s in range(PIECES):
        ref[pl.ds(s, n, stride=PIECES), :] = x[:, s * LANES:(s + 1) * LANES]


def _load_rows_from_tiles(ref, n):
    return jnp.concatenate([ref[pl.ds(s, n, stride=PIECES), :] for s in range(PIECES)], axis=1)


def _row_copy(src_ref, src_row, dst_ref, dst_row, sem):
    src = src_ref.at[pl.ds(pl.multiple_of(src_row * PIECES, PIECES), PIECES)]
    dst = dst_ref.at[pl.ds(pl.multiple_of(dst_row * PIECES, PIECES), PIECES)]
    return pltpu.make_async_copy(src, dst, sem)


def _dispatch_kernel(starts_ref, idx_ref, hn_ref, xs_ref, sem, *, tm):
    def issue(t, carry):
        for k in range(TOP_K):
            dest = starts_ref[idx_ref[0, 0, 4 * t + k]] + idx_ref[0, 0, 4 * t + 2 + k]
            _row_copy(hn_ref, t, xs_ref, dest, sem).start(priority=k % 2)
        return carry

    lax.fori_loop(0, tm, issue, 0, unroll=ISSUE_UNROLL)
    for k in range(TOP_K):
        pltpu.make_async_copy(hn_ref, xs_ref.at[pl.ds(0, tm * PIECES)], sem).wait()


def _dispatch(starts, idx3, hn, tm):
    t = hn.shape[0] // PIECES
    return pl.pallas_call(
        functools.partial(_dispatch_kernel, tm=tm),
        grid_spec=pltpu.PrefetchScalarGridSpec(
            num_scalar_prefetch=1,
            grid=(t // tm,),
            in_specs=[pl.BlockSpec((1, 1, 4 * tm), lambda i, s: (i, 0, 0), memory_space=pltpu.SMEM),
                      pl.BlockSpec((tm * PIECES, LANES), lambda i, s: (i, 0))],
            out_specs=pl.BlockSpec(memory_space=pl.ANY),
            scratch_shapes=[pltpu.SemaphoreType.DMA(())],
        ),
        out_shape=jax.ShapeDtypeStruct((t * TOP_K * PIECES, LANES), F32),
        compiler_params=pltpu.CompilerParams(dimension_semantics=("arbitrary",), vmem_limit_bytes=VMEM_LIMIT),
        name="dispatch",
    )(starts, idx3, hn)


def _expert_kernel(blk_ref, exp_ref, lo_ref, hi_ref, xs_ref, wg_ref, wu_ref, wd_ref, ys_ref, wgu_b, wd_b):
    p = pl.program_id(0)

    @pl.when(jnp.logical_or(p == 0, exp_ref[p] != exp_ref[jnp.maximum(p - 1, 0)]))
    def _():
        wgu_b[:, 0:EXPERT_FF] = wg_ref[0].astype(BF16)
        wgu_b[:, EXPERT_FF:2 * EXPERT_FF] = wu_ref[0].astype(BF16)
        wd_b[...] = wd_ref[0].astype(BF16)

    xb = _load_rows_from_tiles(xs_ref, SLOT_BLOCK).astype(BF16)
    ab = _dot(xb, wgu_b[...])
    hb = _silu(ab[:, 0:EXPERT_FF]) * ab[:, EXPERT_FF:2 * EXPERT_FF]
    y = _dot(hb.astype(BF16), wd_b[...])
    slot = blk_ref[p] * SLOT_BLOCK + lax.broadcasted_iota(jnp.int32, (SLOT_BLOCK, 1), 0)
    y = jnp.where((slot >= lo_ref[p]) & (slot < hi_ref[p]), y, 0.0)
    first = jnp.logical_or(p == 0, blk_ref[p] != blk_ref[jnp.maximum(p - 1, 0)])

    @pl.when(first)
    def _():
        _store_rows_as_tiles(ys_ref, y)

    @pl.when(jnp.logical_not(first))
    def _():
        _store_rows_as_tiles(ys_ref, _load_rows_from_tiles(ys_ref, SLOT_BLOCK) + y)


def _experts(blk, exp, lo, hi, xs, wg, wu, wd):
    n_pairs = blk.shape[0]
    return pl.pallas_call(
        _expert_kernel,
        grid_spec=pltpu.PrefetchScalarGridSpec(
            num_scalar_prefetch=4,
            grid=(n_pairs,),
            in_specs=[pl.BlockSpec((SLOT_BLOCK * PIECES, LANES), lambda p, blk, exp, lo, hi: (blk[p], 0)),
                      pl.BlockSpec((1, D_MODEL, EXPERT_FF), lambda p, blk, exp, lo, hi: (exp[p], 0, 0)),
                      pl.BlockSpec((1, D_MODEL, EXPERT_FF), lambda p, blk, exp, lo, hi: (exp[p], 0, 0)),
                      pl.BlockSpec((1, EXPERT_FF, D_MODEL), lambda p, blk, exp, lo, hi: (exp[p], 0, 0))],
            out_specs=pl.BlockSpec((SLOT_BLOCK * PIECES, LANES), lambda p, blk, exp, lo, hi: (blk[p], 0)),
            scratch_shapes=[pltpu.VMEM((D_MODEL, 2 * EXPERT_FF), BF16), pltpu.VMEM((EXPERT_FF, D_MODEL), BF16)],
        ),
        out_shape=jax.ShapeDtypeStruct(xs.shape, F32),
        compiler_params=pltpu.CompilerParams(dimension_semantics=("arbitrary",), vmem_limit_bytes=VMEM_LIMIT),
        name="experts",
    )(blk, exp, lo, hi, xs, wg, wu, wd)


def _combine_kernel(starts_ref, idx_ref, h_ref, rg_ref, fw_ref, ys_ref, o_ref, buf, sem, *, tm):
    def issue(t, carry):
        for k in range(TOP_K):
            src = starts_ref[idx_ref[0, 0, 4 * t + k]] + idx_ref[0, 0, 4 * t + 2 + k]
            _row_copy(ys_ref, src, buf.at[k], t, sem).start(priority=k % 2)
        return carry

    lax.fori_loop(0, tm, issue, 0, unroll=ISSUE_UNROLL)
    for k in range(TOP_K):
        pltpu.make_async_copy(ys_ref.at[pl.ds(0, tm * PIECES)], buf.at[k], sem).wait()
    rg = rg_ref[...]
    h = (h_ref[...] + rg[:, 0:1] * _load_rows_from_tiles(buf.at[0], tm)
         + rg[:, 1:2] * _load_rows_from_tiles(buf.at[1], tm))
    o_ref[...] = h * lax.rsqrt(jnp.mean(h * h, axis=-1, keepdims=True) + NORM_EPS) * fw_ref[...]


def _combine(starts, idx3, h, rg, final_w, ys, tm):
    t = h.shape[0]
    return pl.pallas_call(
        functools.partial(_combine_kernel, tm=tm),
        grid_spec=pltpu.PrefetchScalarGridSpec(
            num_scalar_prefetch=1,
            grid=(t // tm,),
            in_specs=[pl.BlockSpec((1, 1, 4 * tm), lambda i, s: (i, 0, 0), memory_space=pltpu.SMEM),
                      pl.BlockSpec((tm, D_MODEL), lambda i, s: (i, 0)),
                      pl.BlockSpec((tm, LANES), lambda i, s: (i, 0)),
                      pl.BlockSpec((1, D_MODEL), lambda i, s: (0, 0)),
                      pl.BlockSpec(memory_space=pl.ANY)],
            out_specs=pl.BlockSpec((tm, D_MODEL), lambda i, s: (i, 0)),
            scratch_shapes=[pltpu.VMEM((TOP_K, tm * PIECES, LANES), F32), pltpu.SemaphoreType.DMA(())],
        ),
        out_shape=jax.ShapeDtypeStruct((t, D_MODEL), F32),
        compiler_params=pltpu.CompilerParams(dimension_semantics=("arbitrary",), vmem_limit_bytes=VMEM_LIMIT),
        name="combine",
    )(starts, idx3, h, rg, final_w, ys)


def _pair_schedule(counts, n_slots):
    n_blocks = n_slots // SLOT_BLOCK
    n_pairs = n_blocks + N_EXPERTS - 1
    ends = jnp.cumsum(counts)
    starts = ends - counts
    first = starts // SLOT_BLOCK
    last = jnp.maximum(ends - 1, starts) // SLOT_BLOCK
    per_expert = jnp.where(counts > 0, last - first + 1, 0)
    cum = jnp.cumsum(per_expert)
    p = jnp.arange(n_pairs, dtype=jnp.int32)
    e = jnp.minimum(jnp.sum(cum[None, :] <= p[:, None], axis=1), N_EXPERTS - 1).astype(jnp.int32)
    valid = p < cum[-1]
    onehot = e[:, None] == jnp.arange(N_EXPERTS, dtype=jnp.int32)[None, :]
    pick = lambda table: jnp.sum(jnp.where(onehot, table[None, :], 0), axis=1)
    blk = jnp.where(valid, pick(first) + p - pick(cum - per_expert), n_blocks - 1).astype(jnp.int32)
    lo = jnp.where(valid, pick(starts), 0).astype(jnp.int32)
    hi = jnp.where(valid, pick(ends), 0).astype(jnp.int32)
    return starts.astype(jnp.int32), blk, e, lo, hi


def _layer(x, norm_mix_w, w_in, lb_logits, hgrn_onorm_w, gdn_conv_w, gdn_a_log, gdn_dt_bias, gdn_onorm_w, w_out,
           norm_moe_w, router_group_w, router_expert_w, w_gate, w_up, w_down, final_w, *, tm, n_chunks):
    batch, seq, _ = x.shape
    t = batch * seq
    x2 = x.reshape(t, D_MODEL)
    n_hg = 4 * MIX_HALF
    n_gd = GDN_QKV + MIX_HALF
    wb16 = w_in.astype(BF16)
    wa = wb16[:, 0:n_hg]
    wb = wb16[:, n_hg:n_hg + n_gd]
    wc = jnp.pad(wb16[:, n_hg + n_gd:], ((0, 0), (0, LANES - 2 * HEADS)))
    pa, pb, pc = _inproj(x2, norm_mix_w.reshape(1, D_MODEL), wa, wb, wc, tm)

    o_hg = _hgrn(pa, lb_logits, hgrn_onorm_w.reshape(1, HEAD_DIM), batch, seq, n_chunks)
    alog_row = jnp.pad(gdn_a_log.reshape(1, HEADS), ((0, 0), (0, LANES - HEADS)))
    dtb_row = jnp.pad(gdn_dt_bias.reshape(1, HEADS), ((0, 0), (0, LANES - HEADS)))
    o_gd = _gdn(pb, pc, gdn_conv_w, alog_row, dtb_row, gdn_onorm_w.reshape(1, HEAD_DIM), batch, seq, n_chunks)

    wr = jnp.pad(jnp.concatenate([router_group_w, router_expert_w], axis=1),
                 ((0, 0), (0, LANES - N_GROUPS - N_EXPERTS)))
    wr_hi = wr.astype(BF16)
    wr = jnp.concatenate([wr_hi, (wr - wr_hi.astype(F32)).astype(BF16)], axis=1)
    h, hn, ri, rg, cnt = _route(x2, o_hg, o_gd, w_out.astype(BF16), norm_moe_w.reshape(1, D_MODEL), wr, tm)

    counts = cnt[0, N_GROUPS:N_GROUPS + N_EXPERTS].astype(jnp.int32)
    starts, blk, exp, lo, hi = _pair_schedule(counts, t * TOP_K)
    idx3 = ri[:, 0:4].reshape(t // tm, 1, 4 * tm)
    xs = _dispatch(starts, idx3, hn, tm)
    ys = _experts(blk, exp, lo, hi, xs, w_gate, w_up, w_down)
    out = _combine(starts, idx3, h, rg, final_w.reshape(1, D_MODEL), ys, tm)
    return out.reshape(batch, seq, D_MODEL)


def kernel(x, norm_mix_w, w_in, hgrn_lb_logits, hgrn_onorm_w, gdn_conv_w, gdn_a_log, gdn_dt_bias, gdn_onorm_w, w_out, norm_moe_w, router_group_w, router_expert_w, expert_w_gate, expert_w_up, expert_w_down, final_norm_w):
    return _layer(x, norm_mix_w[0], w_in[0], hgrn_lb_logits, hgrn_onorm_w[0], gdn_conv_w[0], gdn_a_log[0],
                  gdn_dt_bias[0], gdn_onorm_w[0], w_out[0], norm_moe_w[0], router_group_w[0], router_expert_w[0],
                  expert_w_gate[0], expert_w_up[0], expert_w_down[0], final_norm_w, tm=256, n_chunks=8)
```

```python
import functools

import numpy as np
import jax
import jax.numpy as jnp
from jax import lax
from jax.experimental import pallas as pl
from jax.experimental.pallas import tpu as pltpu

F32 = jnp.float32
BF16 = jnp.bfloat16
HI = lax.Precision.HIGHEST

D_MODEL = 1024
HEADS = 4
HEAD_DIM = 128
MIX_HALF = HEADS * HEAD_DIM
CHUNK = 64
GROUP = 2
N_STACKS = HEADS // GROUP
STACK = GROUP * CHUNK
CONV_K = 4
N_GROUPS = 8
EXPERTS_PER_GROUP = 8
N_EXPERTS = N_GROUPS * EXPERTS_PER_GROUP
TOP_K = 2
EXPERT_FF = 256
NORM_EPS = 1e-6
LANES = 128
SLOT_BLOCK = 256
VMEM_LIMIT = 56 * 1024 * 1024


def _sigmoid(x):
    return 1.0 / (1.0 + jnp.exp(-x))


def _silu(x):
    return x * _sigmoid(x)


def _dot(a, b, precision=None):
    return jnp.dot(a, b, preferred_element_type=F32, precision=precision)


def _dot_nt(a, b, precision=None):
    return lax.dot_general(a, b, (((1,), (1,)), ((), ())), preferred_element_type=F32, precision=precision)


def _dot_tn(a, b, precision=None):
    return lax.dot_general(a, b, (((0,), (0,)), ((), ())), preferred_element_type=F32, precision=precision)


def _bdot(a, b):
    return _dot(a.astype(BF16), b.astype(BF16))


def _masked_sum(mask3, x):
    hi = x.astype(BF16)
    r1 = x - hi.astype(F32)
    mid = r1.astype(BF16)
    lo = (r1 - mid.astype(F32)).astype(BF16)
    return _dot(mask3, jnp.concatenate([hi, mid, lo], axis=0))


def _triple(mask):
    return np.concatenate([mask, mask, mask], axis=1)


def _stack_heads(a, p):
    return jnp.concatenate([a[:, h * HEAD_DIM:(h + 1) * HEAD_DIM] for h in range(p * GROUP, (p + 1) * GROUP)],
                           axis=0)


def _each(f, *lists):
    return [f(*args) for args in zip(*lists)]


def _inproj_kernel(x_ref, nw_ref, wa_ref, wb_ref, wc_ref, oa_ref, ob_ref, oc_ref):
    x = x_ref[...]
    ms = jnp.mean(x * x, axis=-1, keepdims=True)
    hn = (x * lax.rsqrt(ms + NORM_EPS) * nw_ref[...]).astype(BF16)
    oa_ref[...] = _dot(hn, wa_ref[...])
    ob_ref[...] = _dot(hn, wb_ref[...])
    oc_ref[...] = _dot(hn, wc_ref[...])


def _inproj(x2, norm_w, wa, wb, wc, tm):
    t = x2.shape[0]
    na, nb, nc = wa.shape[1], wb.shape[1], wc.shape[1]
    const = lambda i: (0, 0)
    return pl.pallas_call(
        _inproj_kernel,
        grid=(t // tm,),
        in_specs=[
            pl.BlockSpec((tm, D_MODEL), lambda i: (i, 0)),
            pl.BlockSpec((1, D_MODEL), const),
            pl.BlockSpec((D_MODEL, na), const),
            pl.BlockSpec((D_MODEL, nb), const),
            pl.BlockSpec((D_MODEL, nc), const),
        ],
        out_specs=[
            pl.BlockSpec((tm, na), lambda i: (i, 0)),
            pl.BlockSpec((tm, nb), lambda i: (i, 0)),
            pl.BlockSpec((tm, nc), lambda i: (i, 0)),
        ],
        out_shape=[
            jax.ShapeDtypeStruct((t, na), F32),
            jax.ShapeDtypeStruct((t, nb), F32),
            jax.ShapeDtypeStruct((t, nc), F32),
        ],
        compiler_params=pltpu.CompilerParams(dimension_semantics=("arbitrary",), vmem_limit_bytes=VMEM_LIMIT),
        name="inproj",
    )(x2, norm_w, wa, wb, wc)


HGRN_LEVELS = (32, 16, 8, 4, 2, 1)
DIAG_CODE = len(HGRN_LEVELS)
NONE_CODE = DIAG_CODE + 1


def _hgrn_arg_matrix():
    t = np.arange(CHUNK)[:, None]
    u = np.arange(CHUNK)[None, :]
    mats = [u <= t, u > t]
    for b in HGRN_LEVELS:
        odd = (t // b) % 2 == 1
        start = (t // b) * b
        mats.append(np.where(odd, (u > start) & (u <= t), (u > t) & (u <= start + b)))
    return np.concatenate(mats, axis=0).astype(np.float32)


def _hgrn_level_codes():
    idx = np.arange(STACK)
    h, t = idx // CHUNK, idx % CHUNK
    same = h[:, None] == h[None, :]
    tt, ss = t[:, None], t[None, :]
    code = np.full((STACK, STACK), NONE_CODE, np.int32)
    code[same & (tt == ss)] = DIAG_CODE
    for l, b in enumerate(HGRN_LEVELS):
        sib = (tt // (2 * b) == ss // (2 * b)) & ((tt // b) % 2 == 1) & ((ss // b) % 2 == 0)
        code[same & sib] = l
    return code


def _hgrn_kernel(q_ref, f_ref, i_ref, g_ref, lbl_ref, onw_ref, marg_ref, code_ref, o_ref, st_ref, *, n_chunks):
    @pl.when(pl.program_id(1) == 0)
    def _():
        st_ref[...] = jnp.zeros_like(st_ref)

    lbl = lbl_ref[...]
    lmax = jnp.max(lbl, axis=0, keepdims=True)
    lexp = jnp.exp(lbl - lmax)
    lb = lexp[0:1, :] / jnp.sum(lexp, axis=0, keepdims=True)
    onw = onw_ref[...]
    marg = marg_ref[...]
    code = code_ref[...]
    trow = lax.broadcasted_iota(jnp.int32, (STACK, HEAD_DIM), 0) & (CHUNK - 1)

    q_blk = _silu(q_ref[...]) * (HEAD_DIM ** -0.5)
    fg_blk = lb + (1.0 - lb) * _sigmoid(f_ref[...])
    k_blk = 1.0 - fg_blk
    lf_blk = jnp.log(fg_blk)
    v_blk = i_ref[...]

    rows = [slice(c * CHUNK, (c + 1) * CHUNK) for c in range(n_chunks)]
    e_chunk = [jnp.exp(_masked_sum(marg, lf_blk[r])) for r in rows]
    units = [(c, p) for c in range(n_chunks) for p in range(N_STACKS)]
    qs = [_stack_heads(q_blk[rows[c]], p) for c, p in units]
    ks = [_stack_heads(k_blk[rows[c]], p) for c, p in units]
    vs = [_stack_heads(v_blk[rows[c]], p).astype(BF16) for c, p in units]
    e_part = lambda n: [_stack_heads(e_chunk[c][n * CHUNK:(n + 1) * CHUNK], p) for c, p in units]
    cum_e, suf_e = e_part(0), e_part(1)
    att = _each(lambda q, k: jnp.where(code == DIAG_CODE, _dot_nt(q.astype(BF16), k.astype(BF16)), 0.0), qs, ks)
    for l, b in enumerate(HGRN_LEVELS):
        x = _each(lambda q, k, el: (jnp.where((trow & b) != 0, q, k) * el).astype(BF16), qs, ks, e_part(2 + l))
        att = _each(lambda xi, a: jnp.where(code == l, _dot_nt(xi, xi), a), x, att)
    o_intra = _each(lambda a, v: _dot(a.astype(BF16), v), att, vs)
    qc = _each(lambda q, e: (q * e).astype(BF16), qs, cum_e)
    kd = _each(lambda k, e: (k * e).astype(BF16), ks, suf_e)

    state = [st_ref[h] for h in range(HEADS)]
    for u, (c, p) in enumerate(units):
        for i in range(GROUP):
            h = p * GROUP + i
            hr = slice(i * CHUNK, (i + 1) * CHUNK)
            hc = slice(h * HEAD_DIM, (h + 1) * HEAD_DIM)
            o = _dot_nt(qc[u][hr], state[h].astype(BF16)) + o_intra[u][hr]
            decay = cum_e[u][i * CHUNK + CHUNK - 1:i * CHUNK + CHUNK, :]
            state[h] = decay * state[h] + _dot_tn(vs[u][hr], kd[u][hr])
            y = o * lax.rsqrt(jnp.mean(o * o, axis=-1, keepdims=True) + NORM_EPS) * onw
            o_ref[rows[c], hc] = (y * _silu(g_ref[rows[c], hc])).astype(o_ref.dtype)
    for h in range(HEADS):
        st_ref[h] = state[h]


def _hgrn(pa, lb_logits, onorm_w, batch, seq, n_chunks):
    rows = n_chunks * CHUNK
    steps = seq // rows
    col = lambda j: pl.BlockSpec((rows, MIX_HALF), lambda b, s, j=j: (b * steps + s, j))
    const = lambda b, s: (0, 0)
    marg = jnp.asarray(_triple(_hgrn_arg_matrix()), BF16)
    code = jnp.asarray(_hgrn_level_codes())
    return pl.pallas_call(
        functools.partial(_hgrn_kernel, n_chunks=n_chunks),
        grid=(batch, steps),
        in_specs=[col(0), col(1), col(2), col(3),
                  pl.BlockSpec(lb_logits.shape, const),
                  pl.BlockSpec((1, HEAD_DIM), const),
                  pl.BlockSpec(marg.shape, const),
                  pl.BlockSpec(code.shape, const)],
        out_specs=pl.BlockSpec((rows, MIX_HALF), lambda b, s: (b * steps + s, 0)),
        out_shape=jax.ShapeDtypeStruct((batch * seq, MIX_HALF), BF16),
        scratch_shapes=[pltpu.VMEM((HEADS, HEAD_DIM, HEAD_DIM), F32)],
        compiler_params=pltpu.CompilerParams(dimension_semantics=("arbitrary", "arbitrary"),
                                             vmem_limit_bytes=VMEM_LIMIT),
        name="hgrn2",
    )(pa, pa, pa, pa, lb_logits, onorm_w, marg, code)


GDN_QKV = 3 * MIX_HALF
HIST = 8
C_NONE, C_DIAG, C_B16, C_B32, C_B64 = 0, 1, 2, 3, 4
MASKED_EXPONENT = -1e30


def _gdn_codes():
    idx = np.arange(STACK)
    h, t = idx // CHUNK, idx % CHUNK
    same = h[:, None] == h[None, :]
    tt, ss = t[:, None], t[None, :]
    code = np.full((STACK, STACK), C_NONE, np.int32)
    low = same & (ss < tt)
    code[low] = C_B64
    code[low & (tt // 32 == ss // 32)] = C_B32
    code[low & (tt // 16 == ss // 16)] = C_B16
    code[same & (tt == ss)] = C_DIAG
    incl = (same & (ss <= tt)).astype(np.float32)
    return code, incl


def _gdn_kernel(qkv_ref, z_ref, ab_ref, cw_ref, alog_ref, dtb_ref, onw_ref, code_ref, incl_ref,
                o_ref, st_ref, ubuf, *, n_chunks):
    rows_blk = n_chunks * CHUNK

    @pl.when(pl.program_id(1) == 0)
    def _():
        st_ref[...] = jnp.zeros_like(st_ref)
        ubuf[0:HIST, :] = jnp.zeros((HIST, GDN_QKV), F32)

    ubuf[HIST:HIST + rows_blk, :] = qkv_ref[...]
    cw = cw_ref[...]
    conv = cw[CONV_K - 1:CONV_K, :] * ubuf[HIST:HIST + rows_blk, :]
    for j in range(1, CONV_K):
        conv = conv + cw[CONV_K - 1 - j:CONV_K - j, :] * ubuf[HIST - j:HIST - j + rows_blk, :]
    ubuf[0:HIST, :] = ubuf[rows_blk:rows_blk + HIST, :]
    qkv = _silu(conv)

    onw = onw_ref[...]
    code = code_ref[...]
    incl_m = incl_ref[...]
    incl = code >= C_DIAG
    eye = (code == C_DIAG).astype(F32)

    rows = [slice(c * CHUNK, (c + 1) * CHUNK) for c in range(n_chunks)]
    units = [(c, p) for c in range(n_chunks) for p in range(N_STACKS)]

    def l2n(a):
        return a * lax.rsqrt(jnp.sum(a * a, axis=-1, keepdims=True) + 1e-6)

    qs = [l2n(_stack_heads(qkv[rows[c], 0:MIX_HALF], p)) * (HEAD_DIM ** -0.5) for c, p in units]
    ks = [l2n(_stack_heads(qkv[rows[c], MIX_HALF:2 * MIX_HALF], p)) for c, p in units]
    vs = [_stack_heads(qkv[rows[c], 2 * MIX_HALF:3 * MIX_HALF], p) for c, p in units]
    ab_blk = ab_ref[...]
    xa = ab_blk + dtb_ref[...]
    softplus = jnp.maximum(xa, 0.0) + jnp.log(1.0 + jnp.exp(-jnp.abs(xa)))
    g_all = -jnp.exp(alog_ref[...]) * softplus
    beta_all = _sigmoid(ab_blk)

    def head_cols(a, c, p, first_lane):
        return jnp.concatenate([a[rows[c], first_lane + h:first_lane + h + 1]
                                for h in range(p * GROUP, (p + 1) * GROUP)], axis=0)

    g_st = [head_cols(g_all, c, p, 0) for c, p in units]
    beta_st = [head_cols(beta_all, c, p, HEADS) for c, p in units]
    gc = [_masked_sum(incl_m, jnp.broadcast_to(g, (STACK, HEAD_DIM))) for g in g_st]
    dec = [jnp.exp(jnp.where(incl, g - jnp.transpose(g)[0:1, :], MASKED_EXPONENT)) for g in gc]
    kb = [k.astype(BF16) for k in ks]
    kk = _each(_dot_nt, kb, kb)
    qk = _each(_dot_nt, [q.astype(BF16) for q in qs], kb)
    am = _each(lambda b, k2, d: b * k2 * d, beta_st, kk, dec)
    a16 = [jnp.where(code == C_B16, a, 0.0) for a in am]
    n32 = [jnp.where(code == C_B32, a, 0.0) for a in am]
    n64 = [jnp.where(code == C_B64, a, 0.0) for a in am]
    a2 = _each(_bdot, a16, a16)
    a4 = _each(_bdot, a2, a2)
    a8 = _each(_bdot, a4, a4)
    pinv = [eye - a for a in a16]
    for apow in (a2, a4, a8):
        pinv = _each(lambda pi, ai: pi + _bdot(pi, ai), pinv, apow)
    for nlev in (n32, n64):
        t = _each(_bdot, nlev, pinv)
        pinv = _each(lambda pi, ti: pi - _bdot(pi, ti), pinv, t)
    egc = [jnp.exp(g) for g in gc]
    rhs = _each(lambda k, v, b, e: jnp.concatenate([k * (b * e), v * b], axis=1), ks, vs, beta_st, egc)
    wu = _each(_bdot, pinv, rhs)
    w_c = [a[:, 0:HEAD_DIM].astype(BF16) for a in wu]
    u_c = [a[:, HEAD_DIM:] for a in wu]
    qkm = _each(lambda a, d: (a * d).astype(BF16), qk, dec)
    qg = _each(lambda q, e: (q * e).astype(BF16), qs, egc)
    glast = [[g[i * CHUNK + CHUNK - 1:(i + 1) * CHUNK, :] for i in range(GROUP)] for g in gc]
    kdec = [[(k[i * CHUNK:(i + 1) * CHUNK] * jnp.exp(gl[i] - g[i * CHUNK:(i + 1) * CHUNK])).astype(BF16)
             for i in range(GROUP)] for k, g, gl in zip(ks, gc, glast)]
    gend = [[jnp.exp(gi) for gi in gl] for gl in glast]

    state = [st_ref[h] for h in range(HEADS)]
    for u, (c, p) in enumerate(units):
        heads = range(p * GROUP, (p + 1) * GROUP)
        hrs = [slice(i * CHUNK, (i + 1) * CHUNK) for i in range(GROUP)]
        s_b = [state[h].astype(BF16) for h in heads]
        v_new = jnp.concatenate([u_c[u][hr] - _dot_nt(w_c[u][hr], s) for hr, s in zip(hrs, s_b)], axis=0)
        v_new = v_new.astype(BF16)
        o_st = jnp.concatenate([_dot_nt(qg[u][hr], s) for hr, s in zip(hrs, s_b)], axis=0) + _dot(qkm[u], v_new)
        for i, h in enumerate(heads):
            hc = slice(h * HEAD_DIM, (h + 1) * HEAD_DIM)
            state[h] = gend[u][i] * state[h] + _dot_tn(v_new[hrs[i]], kdec[u][i])
            o = o_st[hrs[i]]
            y = o * lax.rsqrt(jnp.mean(o * o, axis=-1, keepdims=True) + NORM_EPS) * onw
            o_ref[rows[c], hc] = (y * _silu(z_ref[rows[c], hc])).astype(o_ref.dtype)
    for h in range(HEADS):
        st_ref[h] = state[h]


def _gdn(pb, pc, conv_w, alog_row, dtb_row, onorm_w, batch, seq, n_chunks):
    rows = n_chunks * CHUNK
    steps = seq // rows
    const = lambda b, s: (0, 0)
    code, incl = _gdn_codes()
    code, incl = jnp.asarray(code), jnp.asarray(_triple(incl), BF16)
    sq = pl.BlockSpec((STACK, STACK), const)
    row = pl.BlockSpec((1, LANES), const)
    return pl.pallas_call(
        functools.partial(_gdn_kernel, n_chunks=n_chunks),
        grid=(batch, steps),
        in_specs=[pl.BlockSpec((rows, GDN_QKV), lambda b, s: (b * steps + s, 0)),
                  pl.BlockSpec((rows, MIX_HALF), lambda b, s: (b * steps + s, GDN_QKV // MIX_HALF)),
                  pl.BlockSpec((rows, LANES), lambda b, s: (b * steps + s, 0)),
                  pl.BlockSpec((CONV_K, GDN_QKV), const),
                  row, row, row, sq, pl.BlockSpec((STACK, 3 * STACK), const)],
        out_specs=pl.BlockSpec((rows, MIX_HALF), lambda b, s: (b * steps + s, 0)),
        out_shape=jax.ShapeDtypeStruct((batch * seq, MIX_HALF), BF16),
        scratch_shapes=[pltpu.VMEM((HEADS, HEAD_DIM, HEAD_DIM), F32),
                        pltpu.VMEM((HIST + rows, GDN_QKV), F32)],
        compiler_params=pltpu.CompilerParams(dimension_semantics=("arbitrary", "arbitrary"),
                                             vmem_limit_bytes=VMEM_LIMIT),
        name="gdn",
    )(pb, pb, pc, conv_w, alog_row, dtb_row, onorm_w, code, incl)


def _route_kernel(x_ref, ohg_ref, ogd_ref, wo_ref, nw_ref, wr_ref, ltri_ref,
                  h_ref, hn_ref, ri_ref, rg_ref, cnt_ref, cnt_scr):
    @pl.when(pl.program_id(0) == 0)
    def _():
        cnt_scr[...] = jnp.zeros_like(cnt_scr)

    mix = _dot(ohg_ref[...], wo_ref[0:MIX_HALF, :]) + _dot(ogd_ref[...], wo_ref[MIX_HALF:2 * MIX_HALF, :])
    h = x_ref[...] + mix
    h_ref[...] = h
    hn = h * lax.rsqrt(jnp.mean(h * h, axis=-1, keepdims=True) + NORM_EPS) * nw_ref[...]
    _store_rows_as_tiles(hn_ref, hn)
    hn_hi = hn.astype(BF16)
    hn_lo = (hn - hn_hi.astype(F32)).astype(BF16)
    part = _dot(hn_hi, wr_ref[...])
    logits = (_dot(hn_lo, wr_ref[:, 0:LANES]) + part[:, LANES:]) + part[:, 0:LANES]
    tm = logits.shape[0]
    lane = lax.broadcasted_iota(jnp.int32, (tm, LANES), 1)
    neg = jnp.float32(-jnp.inf)
    big = jnp.int32(LANES)

    def first_max(vals):
        m = jnp.max(vals, axis=-1, keepdims=True)
        return m, jnp.min(jnp.where(vals == m, lane, big), axis=-1, keepdims=True)

    gl = jnp.where(lane < N_GROUPS, logits, neg)
    gmax, gidx = first_max(gl)
    p_group = 1.0 / jnp.sum(jnp.exp(gl - gmax), axis=-1, keepdims=True)
    lo = N_GROUPS + EXPERTS_PER_GROUP * gidx
    el = jnp.where((lane >= lo) & (lane < lo + EXPERTS_PER_GROUP), logits, neg)
    m1, i1 = first_max(el)
    m2, i2 = first_max(jnp.where(lane == i1, neg, el))
    r = jnp.exp(m2 - m1)
    gate1 = p_group / (1.0 + r)
    gate2 = p_group * r / (1.0 + r)
    hot1 = lane == i1
    hot2 = lane == i2
    onehot = jnp.where(hot1 | hot2, 1.0, 0.0)
    before = _dot(ltri_ref[...], onehot.astype(BF16)) + cnt_scr[...]
    rank1 = jnp.sum(jnp.where(hot1, before, 0.0), axis=-1, keepdims=True)
    rank2 = jnp.sum(jnp.where(hot2, before, 0.0), axis=-1, keepdims=True)
    cnt = cnt_scr[...] + jnp.sum(onehot, axis=0, keepdims=True)
    cnt_scr[...] = cnt
    cnt_ref[...] = cnt
    ri = jnp.where(lane == 0, i1 - N_GROUPS,
                   jnp.where(lane == 1, i2 - N_GROUPS,
                             jnp.where(lane == 2, rank1.astype(jnp.int32),
                                       jnp.where(lane == 3, rank2.astype(jnp.int32), 0))))
    ri_ref[...] = ri
    rg_ref[...] = jnp.where(lane == 0, gate1, jnp.where(lane == 1, gate2, 0.0))


def _route(x2, ohg, ogd, wo, norm_w, wr, tm):
    t = x2.shape[0]
    ltri = jnp.asarray(np.tril(np.ones((tm, tm), np.float32), -1), BF16)
    const = lambda i: (0, 0)
    tile = lambda n: pl.BlockSpec((tm, n), lambda i: (i, 0))
    return pl.pallas_call(
        _route_kernel,
        grid=(t // tm,),
        in_specs=[tile(D_MODEL), tile(MIX_HALF), tile(MIX_HALF),
                  pl.BlockSpec((D_MODEL, D_MODEL), const),
                  pl.BlockSpec((1, D_MODEL), const),
                  pl.BlockSpec((D_MODEL, 2 * LANES), const),
                  pl.BlockSpec((tm, tm), const)],
        out_specs=[tile(D_MODEL), pl.BlockSpec((tm * PIECES, LANES), lambda i: (i, 0)), tile(LANES), tile(LANES),
                   pl.BlockSpec((1, LANES), const)],
        out_shape=[jax.ShapeDtypeStruct((t, D_MODEL), F32),
                   jax.ShapeDtypeStruct((t * PIECES, LANES), U32),
                   jax.ShapeDtypeStruct((t, LANES), jnp.int32),
                   jax.ShapeDtypeStruct((t, LANES), F32),
                   jax.ShapeDtypeStruct((1, LANES), F32)],
        scratch_shapes=[pltpu.VMEM((1, LANES), F32)],
        compiler_params=pltpu.CompilerParams(dimension_semantics=("arbitrary",), vmem_limit_bytes=VMEM_LIMIT),
        name="route",
    )(x2, ohg, ogd, wo, norm_w, wr, ltri)


ISSUE_UNROLL = 8


HALF = D_MODEL // 2
PIECES = HALF // LANES
U32 = jnp.uint32
HIGH_HALF = np.uint32(0xFFFF0000)


def _store_rows_as_tiles(ref, x):
    n = x.shape[0]
    bits = lambda a: lax.bitcast_convert_type(a.astype(BF16).astype(F32), U32)
    words = (bits(x[:, 0:HALF]) >> 16) | (bits(x[:, HALF:]) & HIGH_HALF)
    for s in range(PIECES):
        ref[pl.ds(s, n, stride=PIECES), :] = words[:, s * LANES:(s + 1) * LANES]


def _load_rows_from_tiles(ref, n):
    words = jnp.concatenate([ref[pl.ds(s, n, stride=PIECES), :] for s in range(PIECES)], axis=1)
    return (lax.bitcast_convert_type(words << 16, F32), lax.bitcast_convert_type(words & HIGH_HALF, F32))


def _row_copy(src_ref, src_row, dst_ref, dst_row, sem):
    src = src_ref.at[pl.ds(pl.multiple_of(src_row * PIECES, PIECES), PIECES)]
    dst = dst_ref.at[pl.ds(pl.multiple_of(dst_row * PIECES, PIECES), PIECES)]
    return pltpu.make_async_copy(src, dst, sem)


def _dispatch_kernel(starts_ref, idx_ref, hn_ref, xs_ref, sem, *, tm):
    def issue(t, carry):
        for k in range(TOP_K):
            dest = starts_ref[idx_ref[0, 0, 4 * t + k]] + idx_ref[0, 0, 4 * t + 2 + k]
            _row_copy(hn_ref, t, xs_ref, dest, sem).start(priority=k % 2)
        return carry

    lax.fori_loop(0, tm, issue, 0, unroll=ISSUE_UNROLL)
    for k in range(TOP_K):
        pltpu.make_async_copy(hn_ref, xs_ref.at[pl.ds(0, tm * PIECES)], sem).wait()


def _dispatch(starts, idx3, hn, tm):
    t = hn.shape[0] // PIECES
    return pl.pallas_call(
        functools.partial(_dispatch_kernel, tm=tm),
        grid_spec=pltpu.PrefetchScalarGridSpec(
            num_scalar_prefetch=1,
            grid=(t // tm,),
            in_specs=[pl.BlockSpec((1, 1, 4 * tm), lambda i, s: (i, 0, 0), memory_space=pltpu.SMEM),
                      pl.BlockSpec((tm * PIECES, LANES), lambda i, s: (i, 0))],
            out_specs=pl.BlockSpec(memory_space=pl.ANY),
            scratch_shapes=[pltpu.SemaphoreType.DMA(())],
        ),
        out_shape=jax.ShapeDtypeStruct((t * TOP_K * PIECES, LANES), U32),
        compiler_params=pltpu.CompilerParams(dimension_semantics=("arbitrary",), vmem_limit_bytes=VMEM_LIMIT),
        name="dispatch",
    )(starts, idx3, hn)


def _expert_kernel(blk_ref, exp_ref, lo_ref, hi_ref, xs_ref, wg_ref, wu_ref, wd_ref, ys_ref, wgu_b, wd_b, acc):
    p = pl.program_id(0)

    @pl.when(jnp.logical_or(p == 0, exp_ref[p] != exp_ref[jnp.maximum(p - 1, 0)]))
    def _():
        wgu_b[:, 0:EXPERT_FF] = wg_ref[0].astype(BF16)
        wgu_b[:, EXPERT_FF:2 * EXPERT_FF] = wu_ref[0].astype(BF16)
        wd_b[...] = wd_ref[0].astype(BF16)

    x_lo, x_hi = _load_rows_from_tiles(xs_ref, SLOT_BLOCK)
    ab = _dot(x_lo.astype(BF16), wgu_b[0:HALF, :]) + _dot(x_hi.astype(BF16), wgu_b[HALF:D_MODEL, :])
    hb = _silu(ab[:, 0:EXPERT_FF]) * ab[:, EXPERT_FF:2 * EXPERT_FF]
    y = _dot(hb.astype(BF16), wd_b[...])
    slot = blk_ref[p] * SLOT_BLOCK + lax.broadcasted_iota(jnp.int32, (SLOT_BLOCK, 1), 0)
    y = jnp.where((slot >= lo_ref[p]) & (slot < hi_ref[p]), y, 0.0)
    first = jnp.logical_or(p == 0, blk_ref[p] != blk_ref[jnp.maximum(p - 1, 0)])

    @pl.when(first)
    def _():
        acc[...] = y

    @pl.when(jnp.logical_not(first))
    def _():
        acc[...] += y

    _store_rows_as_tiles(ys_ref, acc[...])


def _experts(blk, exp, lo, hi, xs, wg, wu, wd):
    n_pairs = blk.shape[0]
    return pl.pallas_call(
        _expert_kernel,
        grid_spec=pltpu.PrefetchScalarGridSpec(
            num_scalar_prefetch=4,
            grid=(n_pairs,),
            in_specs=[pl.BlockSpec((SLOT_BLOCK * PIECES, LANES), lambda p, blk, exp, lo, hi: (blk[p], 0)),
                      pl.BlockSpec((1, D_MODEL, EXPERT_FF), lambda p, blk, exp, lo, hi: (exp[p], 0, 0)),
                      pl.BlockSpec((1, D_MODEL, EXPERT_FF), lambda p, blk, exp, lo, hi: (exp[p], 0, 0)),
                      pl.BlockSpec((1, EXPERT_FF, D_MODEL), lambda p, blk, exp, lo, hi: (exp[p], 0, 0))],
            out_specs=pl.BlockSpec((SLOT_BLOCK * PIECES, LANES), lambda p, blk, exp, lo, hi: (blk[p], 0)),
            scratch_shapes=[pltpu.VMEM((D_MODEL, 2 * EXPERT_FF), BF16), pltpu.VMEM((EXPERT_FF, D_MODEL), BF16),
                            pltpu.VMEM((SLOT_BLOCK, D_MODEL), F32)],
        ),
        out_shape=jax.ShapeDtypeStruct(xs.shape, U32),
        compiler_params=pltpu.CompilerParams(dimension_semantics=("arbitrary",), vmem_limit_bytes=VMEM_LIMIT),
        name="experts",
    )(blk, exp, lo, hi, xs, wg, wu, wd)


def _combine_kernel(starts_ref, idx_ref, h_ref, rg_ref, fw_ref, ys_ref, o_ref, buf, sem, *, tm):
    def issue(t, carry):
        for k in range(TOP_K):
            src = starts_ref[idx_ref[0, 0, 4 * t + k]] + idx_ref[0, 0, 4 * t + 2 + k]
            _row_copy(ys_ref, src, buf.at[k], t, sem).start(priority=k % 2)
        return carry

    lax.fori_loop(0, tm, issue, 0, unroll=ISSUE_UNROLL)
    for k in range(TOP_K):
        pltpu.make_async_copy(ys_ref.at[pl.ds(0, tm * PIECES)], buf.at[k], sem).wait()
    rg = rg_ref[...]
    y = [jnp.concatenate(_load_rows_from_tiles(buf.at[k], tm), axis=1) for k in range(TOP_K)]
    h = h_ref[...] + rg[:, 0:1] * y[0] + rg[:, 1:2] * y[1]
    o_ref[...] = h * lax.rsqrt(jnp.mean(h * h, axis=-1, keepdims=True) + NORM_EPS) * fw_ref[...]


def _combine(starts, idx3, h, rg, final_w, ys, tm):
    t = h.shape[0]
    return pl.pallas_call(
        functools.partial(_combine_kernel, tm=tm),
        grid_spec=pltpu.PrefetchScalarGridSpec(
            num_scalar_prefetch=1,
            grid=(t // tm,),
            in_specs=[pl.BlockSpec((1, 1, 4 * tm), lambda i, s: (i, 0, 0), memory_space=pltpu.SMEM),
                      pl.BlockSpec((tm, D_MODEL), lambda i, s: (i, 0)),
                      pl.BlockSpec((tm, LANES), lambda i, s: (i, 0)),
                      pl.BlockSpec((1, D_MODEL), lambda i, s: (0, 0)),
                      pl.BlockSpec(memory_space=pl.ANY)],
            out_specs=pl.BlockSpec((tm, D_MODEL), lambda i, s: (i, 0)),
            scratch_shapes=[pltpu.VMEM((TOP_K, tm * PIECES, LANES), U32), pltpu.SemaphoreType.DMA(())],
        ),
        out_shape=jax.ShapeDtypeStruct((t, D_MODEL), F32),
        compiler_params=pltpu.CompilerParams(dimension_semantics=("arbitrary",), vmem_limit_bytes=VMEM_LIMIT),
        name="combine",
    )(starts, idx3, h, rg, final_w, ys)


def _pair_schedule(counts, n_slots):
    n_blocks = n_slots // SLOT_BLOCK
    n_pairs = n_blocks + N_EXPERTS - 1
    ends = jnp.cumsum(counts)
    starts = ends - counts
    first = starts // SLOT_BLOCK
    last = jnp.maximum(ends - 1, starts) // SLOT_BLOCK
    per_expert = jnp.where(counts > 0, last - first + 1, 0)
    cum = jnp.cumsum(per_expert)
    p = jnp.arange(n_pairs, dtype=jnp.int32)
    e = jnp.minimum(jnp.sum(cum[None, :] <= p[:, None], axis=1), N_EXPERTS - 1).astype(jnp.int32)
    valid = p < cum[-1]
    onehot = e[:, None] == jnp.arange(N_EXPERTS, dtype=jnp.int32)[None, :]
    pick = lambda table: jnp.sum(jnp.where(onehot, table[None, :], 0), axis=1)
    blk = jnp.where(valid, pick(first) + p - pick(cum - per_expert), n_blocks - 1).astype(jnp.int32)
    lo = jnp.where(valid, pick(starts), 0).astype(jnp.int32)
    hi = jnp.where(valid, pick(ends), 0).astype(jnp.int32)
    return starts.astype(jnp.int32), blk, e, lo, hi


def _layer(x, norm_mix_w, w_in, lb_logits, hgrn_onorm_w, gdn_conv_w, gdn_a_log, gdn_dt_bias, gdn_onorm_w, w_out,
           norm_moe_w, router_group_w, router_expert_w, w_gate, w_up, w_down, final_w, *, tm, n_chunks):
    batch, seq, _ = x.shape
    t = batch * seq
    x2 = x.reshape(t, D_MODEL)
    n_hg = 4 * MIX_HALF
    n_gd = GDN_QKV + MIX_HALF
    wb16 = w_in.astype(BF16)
    wa = wb16[:, 0:n_hg]
    wb = wb16[:, n_hg:n_hg + n_gd]
    wc = jnp.pad(wb16[:, n_hg + n_gd:], ((0, 0), (0, LANES - 2 * HEADS)))
    pa, pb, pc = _inproj(x2, norm_mix_w.reshape(1, D_MODEL), wa, wb, wc, tm)

    o_hg = _hgrn(pa, lb_logits, hgrn_onorm_w.reshape(1, HEAD_DIM), batch, seq, n_chunks)
    alog_row = jnp.pad(gdn_a_log.reshape(1, HEADS), ((0, 0), (0, LANES - HEADS)))
    dtb_row = jnp.pad(gdn_dt_bias.reshape(1, HEADS), ((0, 0), (0, LANES - HEADS)))
    o_gd = _gdn(pb, pc, gdn_conv_w, alog_row, dtb_row, gdn_onorm_w.reshape(1, HEAD_DIM), batch, seq, n_chunks)

    wr = jnp.pad(jnp.concatenate([router_group_w, router_expert_w], axis=1),
                 ((0, 0), (0, LANES - N_GROUPS - N_EXPERTS)))
    wr_hi = wr.astype(BF16)
    wr = jnp.concatenate([wr_hi, (wr - wr_hi.astype(F32)).astype(BF16)], axis=1)
    h, hn, ri, rg, cnt = _route(x2, o_hg, o_gd, w_out.astype(BF16), norm_moe_w.reshape(1, D_MODEL), wr, 2 * tm)

    counts = cnt[0, N_GROUPS:N_GROUPS + N_EXPERTS].astype(jnp.int32)
    starts, blk, exp, lo, hi = _pair_schedule(counts, t * TOP_K)
    idx3 = ri[:, 0:4].reshape(t // tm, 1, 4 * tm)
    xs = _dispatch(starts, idx3, hn, tm)
    ys = _experts(blk, exp, lo, hi, xs, w_gate, w_up, w_down)
    out = _combine(starts, idx3, h, rg, final_w.reshape(1, D_MODEL), ys, tm)
    return out.reshape(batch, seq, D_MODEL)


def kernel(x, norm_mix_w, w_in, hgrn_lb_logits, hgrn_onorm_w, gdn_conv_w, gdn_a_log, gdn_dt_bias, gdn_onorm_w, w_out, norm_moe_w, router_group_w, router_expert_w, expert_w_gate, expert_w_up, expert_w_down, final_norm_w):
    return _layer(x, norm_mix_w[0], w_in[0], hgrn_lb_logits, hgrn_onorm_w[0], gdn_conv_w[0], gdn_a_log[0],
                  gdn_dt_bias[0], gdn_onorm_w[0], w_out[0], norm_moe_w[0], router_group_w[0], router_expert_w[0],
                  expert_w_gate[0], expert_w_up[0], expert_w_down[0], final_norm_w, tm=256, n_chunks=8)
```

```python
import functools

import numpy as np
import jax
import jax.numpy as jnp
from jax import lax
from jax.experimental import pallas as pl
from jax.experimental.pallas import tpu as pltpu

F32 = jnp.float32
BF16 = jnp.bfloat16
HI = lax.Precision.HIGHEST

D_MODEL = 1024
HEADS = 4
HEAD_DIM = 128
MIX_HALF = HEADS * HEAD_DIM
CHUNK = 64
GROUP = 2
N_STACKS = HEADS // GROUP
STACK = GROUP * CHUNK
CONV_K = 4
N_GROUPS = 8
EXPERTS_PER_GROUP = 8
N_EXPERTS = N_GROUPS * EXPERTS_PER_GROUP
TOP_K = 2
EXPERT_FF = 256
NORM_EPS = 1e-6
LANES = 128
SLOT_BLOCK = 256
VMEM_LIMIT = 56 * 1024 * 1024


def _sigmoid(x):
    return 1.0 / (1.0 + jnp.exp(-x))


def _silu(x):
    return x * _sigmoid(x)


def _dot(a, b, precision=None):
    return jnp.dot(a, b, preferred_element_type=F32, precision=precision)


def _dot_nt(a, b, precision=None):
    return lax.dot_general(a, b, (((1,), (1,)), ((), ())), preferred_element_type=F32, precision=precision)


def _dot_tn(a, b, precision=None):
    return lax.dot_general(a, b, (((0,), (0,)), ((), ())), preferred_element_type=F32, precision=precision)


def _bdot(a, b):
    return _dot(a.astype(BF16), b.astype(BF16))


def _masked_sum(mask3, x):
    hi = x.astype(BF16)
    r1 = x - hi.astype(F32)
    mid = r1.astype(BF16)
    lo = (r1 - mid.astype(F32)).astype(BF16)
    return _dot(mask3, jnp.concatenate([hi, mid, lo], axis=0))


def _triple(mask):
    return np.concatenate([mask, mask, mask], axis=1)


def _stack_heads(a, p):
    return jnp.concatenate([a[:, h * HEAD_DIM:(h + 1) * HEAD_DIM] for h in range(p * GROUP, (p + 1) * GROUP)],
                           axis=0)


def _each(f, *lists):
    return [f(*args) for args in zip(*lists)]


def _l2norm_heads(a):
    return jnp.concatenate(
        [a[:, h * HEAD_DIM:(h + 1) * HEAD_DIM]
         * lax.rsqrt(jnp.sum(jnp.square(a[:, h * HEAD_DIM:(h + 1) * HEAD_DIM]), axis=-1, keepdims=True) + 1e-6)
         for h in range(HEADS)], axis=1)


def _inproj_kernel(x_ref, nw_ref, wa_ref, wb_ref, wc_ref, cw_ref, oa_ref, ob_ref, oc_ref, ubuf, *, tiles_per_seq):
    tm = x_ref.shape[0]

    @pl.when(lax.rem(pl.program_id(0), tiles_per_seq) == 0)
    def _():
        ubuf[0:HIST, :] = jnp.zeros((HIST, GDN_QKV), F32)

    x = x_ref[...]
    ms = jnp.mean(x * x, axis=-1, keepdims=True)
    hn = (x * lax.rsqrt(ms + NORM_EPS) * nw_ref[...]).astype(BF16)
    cw = cw_ref[...]
    scale = (HEAD_DIM ** -0.5, 1.0, None)
    for g in range(GDN_QKV // MIX_HALF):
        cols = slice(g * MIX_HALF, (g + 1) * MIX_HALF)
        ubuf[HIST:HIST + tm, cols] = _dot(hn, wb_ref[:, cols])
        conv = cw[CONV_K - 1:CONV_K, cols] * ubuf[HIST:HIST + tm, cols]
        for j in range(1, CONV_K):
            conv = conv + cw[CONV_K - 1 - j:CONV_K - j, cols] * ubuf[HIST - j:HIST - j + tm, cols]
        ubuf[0:HIST, cols] = ubuf[tm:tm + HIST, cols]
        act = _silu(conv)
        ob_ref[:, cols] = act if scale[g] is None else _l2norm_heads(act) * scale[g]
        oa_ref[:, cols] = _dot(hn, wa_ref[:, cols])
    ob_ref[:, GDN_QKV:] = _dot(hn, wb_ref[:, GDN_QKV:])
    oa_ref[:, GDN_QKV:] = _dot(hn, wa_ref[:, GDN_QKV:])
    oc_ref[...] = _dot(hn, wc_ref[...])


def _inproj(x2, norm_w, wa, wb, wc, conv_w, tm, tiles_per_seq):
    t = x2.shape[0]
    na, nb, nc = wa.shape[1], wb.shape[1], wc.shape[1]
    const = lambda i: (0, 0)
    return pl.pallas_call(
        functools.partial(_inproj_kernel, tiles_per_seq=tiles_per_seq),
        grid=(t // tm,),
        in_specs=[
            pl.BlockSpec((tm, D_MODEL), lambda i: (i, 0)),
            pl.BlockSpec((1, D_MODEL), const),
            pl.BlockSpec((D_MODEL, na), const),
            pl.BlockSpec((D_MODEL, nb), const),
            pl.BlockSpec((D_MODEL, nc), const),
            pl.BlockSpec((CONV_K, GDN_QKV), const),
        ],
        out_specs=[
            pl.BlockSpec((tm, na), lambda i: (i, 0)),
            pl.BlockSpec((tm, nb), lambda i: (i, 0)),
            pl.BlockSpec((tm, nc), lambda i: (i, 0)),
        ],
        out_shape=[
            jax.ShapeDtypeStruct((t, na), F32),
            jax.ShapeDtypeStruct((t, nb), F32),
            jax.ShapeDtypeStruct((t, nc), F32),
        ],
        scratch_shapes=[pltpu.VMEM((HIST + tm, GDN_QKV), F32)],
        compiler_params=pltpu.CompilerParams(dimension_semantics=("arbitrary",), vmem_limit_bytes=VMEM_LIMIT),
        name="inproj",
    )(x2, norm_w, wa, wb, wc, conv_w)


HGRN_LEVELS = (32, 16, 8, 4, 2, 1)
DIAG_CODE = len(HGRN_LEVELS)
NONE_CODE = DIAG_CODE + 1


def _hgrn_arg_matrix():
    t = np.arange(CHUNK)[:, None]
    u = np.arange(CHUNK)[None, :]
    mats = [u <= t, u > t]
    for b in HGRN_LEVELS:
        odd = (t // b) % 2 == 1
        start = (t // b) * b
        mats.append(np.where(odd, (u > start) & (u <= t), (u > t) & (u <= start + b)))
    return np.concatenate(mats, axis=0).astype(np.float32)


def _hgrn_level_codes():
    idx = np.arange(STACK)
    h, t = idx // CHUNK, idx % CHUNK
    same = h[:, None] == h[None, :]
    tt, ss = t[:, None], t[None, :]
    code = np.full((STACK, STACK), NONE_CODE, np.int32)
    code[same & (tt == ss)] = DIAG_CODE
    for l, b in enumerate(HGRN_LEVELS):
        sib = (tt // (2 * b) == ss // (2 * b)) & ((tt // b) % 2 == 1) & ((ss // b) % 2 == 0)
        code[same & sib] = l
    return code


def _hgrn_kernel(q_ref, f_ref, i_ref, g_ref, lbl_ref, onw_ref, marg_ref, code_ref, o_ref, st_ref, *, n_chunks):
    @pl.when(pl.program_id(1) == 0)
    def _():
        st_ref[...] = jnp.zeros_like(st_ref)

    lbl = lbl_ref[...]
    lmax = jnp.max(lbl, axis=0, keepdims=True)
    lexp = jnp.exp(lbl - lmax)
    lb = lexp[0:1, :] / jnp.sum(lexp, axis=0, keepdims=True)
    onw = onw_ref[...]
    marg = marg_ref[...]
    code = code_ref[...]
    trow = lax.broadcasted_iota(jnp.int32, (STACK, HEAD_DIM), 0) & (CHUNK - 1)

    q_blk = _silu(q_ref[...]) * (HEAD_DIM ** -0.5)
    fg_blk = lb + (1.0 - lb) * _sigmoid(f_ref[...])
    k_blk = 1.0 - fg_blk
    lf_blk = jnp.log(fg_blk)
    v_blk = i_ref[...]

    rows = [slice(c * CHUNK, (c + 1) * CHUNK) for c in range(n_chunks)]
    e_chunk = [jnp.exp(_masked_sum(marg, lf_blk[r])) for r in rows]
    units = [(c, p) for c in range(n_chunks) for p in range(N_STACKS)]
    qs = [_stack_heads(q_blk[rows[c]], p) for c, p in units]
    ks = [_stack_heads(k_blk[rows[c]], p) for c, p in units]
    vs = [_stack_heads(v_blk[rows[c]], p).astype(BF16) for c, p in units]
    e_part = lambda n: [_stack_heads(e_chunk[c][n * CHUNK:(n + 1) * CHUNK], p) for c, p in units]
    cum_e, suf_e = e_part(0), e_part(1)
    att = _each(lambda q, k: jnp.where(code == DIAG_CODE, _dot_nt(q.astype(BF16), k.astype(BF16)), 0.0), qs, ks)
    for l, b in enumerate(HGRN_LEVELS):
        x = _each(lambda q, k, el: (jnp.where((trow & b) != 0, q, k) * el).astype(BF16), qs, ks, e_part(2 + l))
        att = _each(lambda xi, a: jnp.where(code == l, _dot_nt(xi, xi), a), x, att)
    o_intra = _each(lambda a, v: _dot(a.astype(BF16), v), att, vs)
    qc = _each(lambda q, e: (q * e).astype(BF16), qs, cum_e)
    kd = _each(lambda k, e: (k * e).astype(BF16), ks, suf_e)

    state = [st_ref[h] for h in range(HEADS)]
    for u, (c, p) in enumerate(units):
        for i in range(GROUP):
            h = p * GROUP + i
            hr = slice(i * CHUNK, (i + 1) * CHUNK)
            hc = slice(h * HEAD_DIM, (h + 1) * HEAD_DIM)
            o = _dot_nt(qc[u][hr], state[h].astype(BF16)) + o_intra[u][hr]
            decay = cum_e[u][i * CHUNK + CHUNK - 1:i * CHUNK + CHUNK, :]
            state[h] = decay * state[h] + _dot_tn(vs[u][hr], kd[u][hr])
            y = o * lax.rsqrt(jnp.mean(o * o, axis=-1, keepdims=True) + NORM_EPS) * onw
            o_ref[rows[c], hc] = (y * _silu(g_ref[rows[c], hc])).astype(o_ref.dtype)
    for h in range(HEADS):
        st_ref[h] = state[h]


def _hgrn(pa, lb_logits, onorm_w, batch, seq, n_chunks):
    rows = n_chunks * CHUNK
    steps = seq // rows
    col = lambda j: pl.BlockSpec((rows, MIX_HALF), lambda b, s, j=j: (b * steps + s, j))
    const = lambda b, s: (0, 0)
    marg = jnp.asarray(_triple(_hgrn_arg_matrix()), BF16)
    code = jnp.asarray(_hgrn_level_codes())
    return pl.pallas_call(
        functools.partial(_hgrn_kernel, n_chunks=n_chunks),
        grid=(batch, steps),
        in_specs=[col(0), col(1), col(2), col(3),
                  pl.BlockSpec(lb_logits.shape, const),
                  pl.BlockSpec((1, HEAD_DIM), const),
                  pl.BlockSpec(marg.shape, const),
                  pl.BlockSpec(code.shape, const)],
        out_specs=pl.BlockSpec((rows, MIX_HALF), lambda b, s: (b * steps + s, 0)),
        out_shape=jax.ShapeDtypeStruct((batch * seq, MIX_HALF), BF16),
        scratch_shapes=[pltpu.VMEM((HEADS, HEAD_DIM, HEAD_DIM), F32)],
        compiler_params=pltpu.CompilerParams(dimension_semantics=("arbitrary", "arbitrary"),
                                             vmem_limit_bytes=VMEM_LIMIT),
        name="hgrn2",
    )(pa, pa, pa, pa, lb_logits, onorm_w, marg, code)


GDN_QKV = 3 * MIX_HALF
HIST = 8
C_NONE, C_DIAG, C_B16, C_B32, C_B64 = 0, 1, 2, 3, 4
MASKED_EXPONENT = -1e30


def _gdn_codes():
    idx = np.arange(STACK)
    h, t = idx // CHUNK, idx % CHUNK
    same = h[:, None] == h[None, :]
    tt, ss = t[:, None], t[None, :]
    code = np.full((STACK, STACK), C_NONE, np.int32)
    low = same & (ss < tt)
    code[low] = C_B64
    code[low & (tt // 32 == ss // 32)] = C_B32
    code[low & (tt // 16 == ss // 16)] = C_B16
    code[same & (tt == ss)] = C_DIAG
    incl = np.tril(np.ones((CHUNK, CHUNK), np.float32))
    return code, incl


def _gdn_kernel(qkv_ref, z_ref, ab_ref, alog_ref, dtb_ref, onw_ref, code_ref, incl_ref,
                o_ref, st_ref, *, n_chunks):
    @pl.when(pl.program_id(1) == 0)
    def _():
        st_ref[...] = jnp.zeros_like(st_ref)

    onw = onw_ref[...]
    code = code_ref[...]
    incl_m = incl_ref[...]
    incl = code >= C_DIAG
    eye = (code == C_DIAG).astype(F32)

    rows = [slice(c * CHUNK, (c + 1) * CHUNK) for c in range(n_chunks)]
    units = [(c, p) for c in range(n_chunks) for p in range(N_STACKS)]

    qs = [_stack_heads(qkv_ref[rows[c], 0:MIX_HALF], p) for c, p in units]
    ks = [_stack_heads(qkv_ref[rows[c], MIX_HALF:2 * MIX_HALF], p) for c, p in units]
    vs = [_stack_heads(qkv_ref[rows[c], 2 * MIX_HALF:3 * MIX_HALF], p) for c, p in units]
    ab_blk = ab_ref[...]
    xa = ab_blk + dtb_ref[...]
    softplus = jnp.maximum(xa, 0.0) + jnp.log(1.0 + jnp.exp(-jnp.abs(xa)))
    g_all = -jnp.exp(alog_ref[...]) * softplus
    beta_all = _sigmoid(ab_blk)

    def head_cols(a, c, p, first_lane):
        return jnp.concatenate([a[c][:, first_lane + h:first_lane + h + 1]
                                for h in range(p * GROUP, (p + 1) * GROUP)], axis=0)

    g_cum = [_masked_sum(incl_m, g_all[r]) for r in rows]
    beta_chunk = [beta_all[r] for r in rows]
    beta_st = [head_cols(beta_chunk, c, p, HEADS) for c, p in units]
    gc = [jnp.broadcast_to(head_cols(g_cum, c, p, 0), (STACK, HEAD_DIM)) for c, p in units]
    dec = [jnp.exp(jnp.where(incl, g - jnp.transpose(g)[0:1, :], MASKED_EXPONENT)) for g in gc]
    kb = [k.astype(BF16) for k in ks]
    kq = _each(lambda k, q: _dot_nt(jnp.concatenate([k, q.astype(BF16)], axis=0), k), kb, qs)
    kk = [a[0:STACK] for a in kq]
    qk = [a[STACK:] for a in kq]
    am = _each(lambda b, k2, d: b * k2 * d, beta_st, kk, dec)
    a16 = [jnp.where(code == C_B16, a, 0.0) for a in am]
    n32 = [jnp.where(code == C_B32, a, 0.0) for a in am]
    n64 = [jnp.where(code == C_B64, a, 0.0) for a in am]
    apow = _each(_bdot, a16, a16)
    pinv = [eye - a for a in a16]
    for _ in range(2):
        both = _each(lambda ai, pi: _bdot(jnp.concatenate([ai, pi], axis=0), ai), apow, pinv)
        pinv = _each(lambda pi, b: pi + b[STACK:], pinv, both)
        apow = [b[0:STACK] for b in both]
    pinv = _each(lambda pi, ai: pi + _bdot(pi, ai), pinv, apow)
    for nlev in (n32, n64):
        t = _each(_bdot, nlev, pinv)
        pinv = _each(lambda pi, ti: pi - _bdot(pi, ti), pinv, t)
    egc = [jnp.exp(g) for g in gc]
    rhs = _each(lambda k, v, b, e: jnp.concatenate([k * (b * e), v * b], axis=1), ks, vs, beta_st, egc)
    wu = _each(_bdot, pinv, rhs)
    w_c = [a[:, 0:HEAD_DIM].astype(BF16) for a in wu]
    u_c = [a[:, HEAD_DIM:] for a in wu]
    qkm = _each(lambda a, d: (a * d).astype(BF16), qk, dec)
    qg = _each(lambda q, e: (q * e).astype(BF16), qs, egc)
    glast = [[g[i * CHUNK + CHUNK - 1:(i + 1) * CHUNK, :] for i in range(GROUP)] for g in gc]
    kdec = [[(k[i * CHUNK:(i + 1) * CHUNK] * jnp.exp(gl[i] - g[i * CHUNK:(i + 1) * CHUNK])).astype(BF16)
             for i in range(GROUP)] for k, g, gl in zip(ks, gc, glast)]
    gend = [[jnp.exp(gi) for gi in gl] for gl in glast]

    state = [st_ref[h] for h in range(HEADS)]
    for u, (c, p) in enumerate(units):
        heads = range(p * GROUP, (p + 1) * GROUP)
        hrs = [slice(i * CHUNK, (i + 1) * CHUNK) for i in range(GROUP)]
        ws = [_dot_nt(jnp.concatenate([w_c[u][hr], qg[u][hr]], axis=0), state[h].astype(BF16))
              for hr, h in zip(hrs, heads)]
        v_new = jnp.concatenate([u_c[u][hr] - a[0:CHUNK] for hr, a in zip(hrs, ws)], axis=0).astype(BF16)
        o_st = jnp.concatenate([a[CHUNK:] for a in ws], axis=0) + _dot(qkm[u], v_new)
        for i, h in enumerate(heads):
            hc = slice(h * HEAD_DIM, (h + 1) * HEAD_DIM)
            state[h] = gend[u][i] * state[h] + _dot_tn(v_new[hrs[i]], kdec[u][i])
            o = o_st[hrs[i]]
            y = o * lax.rsqrt(jnp.mean(o * o, axis=-1, keepdims=True) + NORM_EPS) * onw
            o_ref[rows[c], hc] = (y * _silu(z_ref[rows[c], hc])).astype(o_ref.dtype)
    for h in range(HEADS):
        st_ref[h] = state[h]


def _gdn(pb, pc, alog_row, dtb_row, onorm_w, batch, seq, n_chunks):
    rows = n_chunks * CHUNK
    steps = seq // rows
    const = lambda b, s: (0, 0)
    code, incl = _gdn_codes()
    code, incl = jnp.asarray(code), jnp.asarray(_triple(incl), BF16)
    sq = pl.BlockSpec((STACK, STACK), const)
    row = pl.BlockSpec((1, LANES), const)
    return pl.pallas_call(
        functools.partial(_gdn_kernel, n_chunks=n_chunks),
        grid=(batch, steps),
        in_specs=[pl.BlockSpec((rows, GDN_QKV), lambda b, s: (b * steps + s, 0)),
                  pl.BlockSpec((rows, MIX_HALF), lambda b, s: (b * steps + s, GDN_QKV // MIX_HALF)),
                  pl.BlockSpec((rows, LANES), lambda b, s: (b * steps + s, 0)),
                  row, row, row, sq, pl.BlockSpec((CHUNK, 3 * CHUNK), const)],
        out_specs=pl.BlockSpec((rows, MIX_HALF), lambda b, s: (b * steps + s, 0)),
        out_shape=jax.ShapeDtypeStruct((batch * seq, MIX_HALF), BF16),
        scratch_shapes=[pltpu.VMEM((HEADS, HEAD_DIM, HEAD_DIM), F32)],
        compiler_params=pltpu.CompilerParams(dimension_semantics=("arbitrary", "arbitrary"),
                                             vmem_limit_bytes=VMEM_LIMIT),
        name="gdn",
    )(pb, pb, pc, alog_row, dtb_row, onorm_w, code, incl)


def _route_kernel(x_ref, ohg_ref, ogd_ref, wo_ref, nw_ref, wr_ref, ltri_ref,
                  h_ref, hn_ref, ri_ref, rg_ref, cnt_ref, cnt_scr):
    @pl.when(pl.program_id(0) == 0)
    def _():
        cnt_scr[...] = jnp.zeros_like(cnt_scr)

    mix = _dot(ohg_ref[...], wo_ref[0:MIX_HALF, :]) + _dot(ogd_ref[...], wo_ref[MIX_HALF:2 * MIX_HALF, :])
    h = x_ref[...] + mix
    h_ref[...] = h
    hn = h * lax.rsqrt(jnp.mean(h * h, axis=-1, keepdims=True) + NORM_EPS) * nw_ref[...]
    _store_rows_as_tiles(hn_ref, hn)
    hn_hi = hn.astype(BF16)
    hn_lo = (hn - hn_hi.astype(F32)).astype(BF16)
    part = _dot(hn_hi, wr_ref[...])
    logits = (_dot(hn_lo, wr_ref[:, 0:LANES]) + part[:, LANES:]) + part[:, 0:LANES]
    tm = logits.shape[0]
    lane = lax.broadcasted_iota(jnp.int32, (tm, LANES), 1)
    neg = jnp.float32(-jnp.inf)
    big = jnp.int32(LANES)

    def first_max(vals):
        m = jnp.max(vals, axis=-1, keepdims=True)
        return m, jnp.min(jnp.where(vals == m, lane, big), axis=-1, keepdims=True)

    gl = jnp.where(lane < N_GROUPS, logits, neg)
    gmax, gidx = first_max(gl)
    p_group = 1.0 / jnp.sum(jnp.exp(gl - gmax), axis=-1, keepdims=True)
    lo = N_GROUPS + EXPERTS_PER_GROUP * gidx
    el = jnp.where((lane >= lo) & (lane < lo + EXPERTS_PER_GROUP), logits, neg)
    m1, i1 = first_max(el)
    m2, i2 = first_max(jnp.where(lane == i1, neg, el))
    r = jnp.exp(m2 - m1)
    gate1 = p_group / (1.0 + r)
    gate2 = p_group * r / (1.0 + r)
    hot1 = lane == i1
    hot2 = lane == i2
    onehot = jnp.where(hot1 | hot2, 1.0, 0.0)
    before = _dot(ltri_ref[...], onehot.astype(BF16)) + cnt_scr[...]
    rank1 = jnp.sum(jnp.where(hot1, before, 0.0), axis=-1, keepdims=True)
    rank2 = jnp.sum(jnp.where(hot2, before, 0.0), axis=-1, keepdims=True)
    cnt = cnt_scr[...] + jnp.sum(onehot, axis=0, keepdims=True)
    cnt_scr[...] = cnt
    cnt_ref[...] = cnt
    ri = jnp.where(lane == 0, i1 - N_GROUPS,
                   jnp.where(lane == 1, i2 - N_GROUPS,
                             jnp.where(lane == 2, rank1.astype(jnp.int32),
                                       jnp.where(lane == 3, rank2.astype(jnp.int32), 0))))
    ri_ref[...] = ri
    rg_ref[...] = jnp.where(lane == 0, gate1, jnp.where(lane == 1, gate2, 0.0))


def _route(x2, ohg, ogd, wo, norm_w, wr, tm):
    t = x2.shape[0]
    ltri = jnp.asarray(np.tril(np.ones((tm, tm), np.float32), -1), BF16)
    const = lambda i: (0, 0)
    tile = lambda n: pl.BlockSpec((tm, n), lambda i: (i, 0))
    return pl.pallas_call(
        _route_kernel,
        grid=(t // tm,),
        in_specs=[tile(D_MODEL), tile(MIX_HALF), tile(MIX_HALF),
                  pl.BlockSpec((D_MODEL, D_MODEL), const),
                  pl.BlockSpec((1, D_MODEL), const),
                  pl.BlockSpec((D_MODEL, 2 * LANES), const),
                  pl.BlockSpec((tm, tm), const)],
        out_specs=[tile(D_MODEL), pl.BlockSpec((tm * PIECES, LANES), lambda i: (i, 0)), tile(LANES), tile(LANES),
                   pl.BlockSpec((1, LANES), const)],
        out_shape=[jax.ShapeDtypeStruct((t, D_MODEL), F32),
                   jax.ShapeDtypeStruct((t * PIECES, LANES), U32),
                   jax.ShapeDtypeStruct((t, LANES), jnp.int32),
                   jax.ShapeDtypeStruct((t, LANES), F32),
                   jax.ShapeDtypeStruct((1, LANES), F32)],
        scratch_shapes=[pltpu.VMEM((1, LANES), F32)],
        compiler_params=pltpu.CompilerParams(dimension_semantics=("arbitrary",), vmem_limit_bytes=VMEM_LIMIT),
        name="route",
    )(x2, ohg, ogd, wo, norm_w, wr, ltri)


ISSUE_UNROLL = 8


HALF = D_MODEL // 2
PIECES = HALF // LANES
U32 = jnp.uint32
HIGH_HALF = np.uint32(0xFFFF0000)


def _store_rows_as_tiles(ref, x):
    n = x.shape[0]
    bits = lambda a: lax.bitcast_convert_type(a.astype(BF16).astype(F32), U32)
    words = (bits(x[:, 0:HALF]) >> 16) | (bits(x[:, HALF:]) & HIGH_HALF)
    for s in range(PIECES):
        ref[pl.ds(s, n, stride=PIECES), :] = words[:, s * LANES:(s + 1) * LANES]


def _load_rows_from_tiles(ref, n):
    words = jnp.concatenate([ref[pl.ds(s, n, stride=PIECES), :] for s in range(PIECES)], axis=1)
    return (lax.bitcast_convert_type(words << 16, F32), lax.bitcast_convert_type(words & HIGH_HALF, F32))


def _row_copy(src_ref, src_row, dst_ref, dst_row, sem):
    src = src_ref.at[pl.ds(pl.multiple_of(src_row * PIECES, PIECES), PIECES)]
    dst = dst_ref.at[pl.ds(pl.multiple_of(dst_row * PIECES, PIECES), PIECES)]
    return pltpu.make_async_copy(src, dst, sem)


def _dispatch_kernel(starts_ref, idx_ref, hn_ref, xs_ref, sem, *, tm):
    def issue(t, carry):
        for k in range(TOP_K):
            dest = starts_ref[idx_ref[0, 0, 4 * t + k]] + idx_ref[0, 0, 4 * t + 2 + k]
            _row_copy(hn_ref, t, xs_ref, dest, sem).start(priority=k % 2)
        return carry

    lax.fori_loop(0, tm, issue, 0, unroll=ISSUE_UNROLL)
    for k in range(TOP_K):
        pltpu.make_async_copy(hn_ref, xs_ref.at[pl.ds(0, tm * PIECES)], sem).wait()


def _dispatch(starts, idx3, hn, tm):
    t = hn.shape[0] // PIECES
    return pl.pallas_call(
        functools.partial(_dispatch_kernel, tm=tm),
        grid_spec=pltpu.PrefetchScalarGridSpec(
            num_scalar_prefetch=1,
            grid=(t // tm,),
            in_specs=[pl.BlockSpec((1, 1, 4 * tm), lambda i, s: (i, 0, 0), memory_space=pltpu.SMEM),
                      pl.BlockSpec((tm * PIECES, LANES), lambda i, s: (i, 0))],
            out_specs=pl.BlockSpec(memory_space=pl.ANY),
            scratch_shapes=[pltpu.SemaphoreType.DMA(())],
        ),
        out_shape=jax.ShapeDtypeStruct((t * TOP_K * PIECES, LANES), U32),
        compiler_params=pltpu.CompilerParams(dimension_semantics=("arbitrary",), vmem_limit_bytes=VMEM_LIMIT),
        name="dispatch",
    )(starts, idx3, hn)


EXPERT_SPLIT = 2


def _expert_kernel(blk_ref, exp_ref, lo_ref, hi_ref, xs_ref, wg_ref, wu_ref, wd_ref, ys_ref, wgu_b, wd_b, acc):
    p = pl.program_id(0)

    @pl.when(jnp.logical_or(p == 0, exp_ref[p] != exp_ref[jnp.maximum(p - 1, 0)]))
    def _():
        wgu_b[:, 0:EXPERT_FF] = wg_ref[0].astype(BF16)
        wgu_b[:, EXPERT_FF:2 * EXPERT_FF] = wu_ref[0].astype(BF16)
        wd_b[...] = wd_ref[0].astype(BF16)

    @pl.when(p == 0)
    def _():
        acc[...] = jnp.zeros_like(acc)

    first = jnp.logical_or(p == 0, blk_ref[p] != blk_ref[jnp.maximum(p - 1, 0)])
    sub = SLOT_BLOCK // EXPERT_SPLIT
    for r in range(EXPERT_SPLIT):
        x_lo, x_hi = _load_rows_from_tiles(xs_ref.at[pl.ds(r * sub * PIECES, sub * PIECES)], sub)
        ab = _dot(x_lo.astype(BF16), wgu_b[0:HALF, :]) + _dot(x_hi.astype(BF16), wgu_b[HALF:D_MODEL, :])
        hb = _silu(ab[:, 0:EXPERT_FF]) * ab[:, EXPERT_FF:2 * EXPERT_FF]
        y = _dot(hb.astype(BF16), wd_b[...])
        slot = blk_ref[p] * SLOT_BLOCK + r * sub + lax.broadcasted_iota(jnp.int32, (sub, 1), 0)
        y = jnp.where((slot >= lo_ref[p]) & (slot < hi_ref[p]), y, 0.0)
        total = y + jnp.where(first, 0.0, acc[pl.ds(r * sub, sub), :])
        acc[pl.ds(r * sub, sub), :] = total
        _store_rows_as_tiles(ys_ref.at[pl.ds(r * sub * PIECES, sub * PIECES)], total)


def _experts(blk, exp, lo, hi, xs, wg, wu, wd):
    n_pairs = blk.shape[0]
    return pl.pallas_call(
        _expert_kernel,
        grid_spec=pltpu.PrefetchScalarGridSpec(
            num_scalar_prefetch=4,
            grid=(n_pairs,),
            in_specs=[pl.BlockSpec((SLOT_BLOCK * PIECES, LANES), lambda p, blk, exp, lo, hi: (blk[p], 0)),
                      pl.BlockSpec((1, D_MODEL, EXPERT_FF), lambda p, blk, exp, lo, hi: (exp[p], 0, 0)),
                      pl.BlockSpec((1, D_MODEL, EXPERT_FF), lambda p, blk, exp, lo, hi: (exp[p], 0, 0)),
                      pl.BlockSpec((1, EXPERT_FF, D_MODEL), lambda p, blk, exp, lo, hi: (exp[p], 0, 0))],
            out_specs=pl.BlockSpec((SLOT_BLOCK * PIECES, LANES), lambda p, blk, exp, lo, hi: (blk[p], 0)),
            scratch_shapes=[pltpu.VMEM((D_MODEL, 2 * EXPERT_FF), BF16), pltpu.VMEM((EXPERT_FF, D_MODEL), BF16),
                            pltpu.VMEM((SLOT_BLOCK, D_MODEL), F32)],
        ),
        out_shape=jax.ShapeDtypeStruct(xs.shape, U32),
        compiler_params=pltpu.CompilerParams(dimension_semantics=("arbitrary",), vmem_limit_bytes=VMEM_LIMIT),
        name="experts",
    )(blk, exp, lo, hi, xs, wg, wu, wd)


def _combine_kernel(starts_ref, idx_ref, h_ref, rg_ref, fw_ref, ys_ref, o_ref, buf, sem, *, tm):
    def issue(t, carry):
        for k in range(TOP_K):
            src = starts_ref[idx_ref[0, 0, 4 * t + k]] + idx_ref[0, 0, 4 * t + 2 + k]
            _row_copy(ys_ref, src, buf.at[k], t, sem).start(priority=k % 2)
        return carry

    lax.fori_loop(0, tm, issue, 0, unroll=ISSUE_UNROLL)
    for k in range(TOP_K):
        pltpu.make_async_copy(ys_ref.at[pl.ds(0, tm * PIECES)], buf.at[k], sem).wait()
    rg = rg_ref[...]
    y = [jnp.concatenate(_load_rows_from_tiles(buf.at[k], tm), axis=1) for k in range(TOP_K)]
    h = h_ref[...] + rg[:, 0:1] * y[0] + rg[:, 1:2] * y[1]
    o_ref[...] = h * lax.rsqrt(jnp.mean(h * h, axis=-1, keepdims=True) + NORM_EPS) * fw_ref[...]


def _combine(starts, idx3, h, rg, final_w, ys, tm):
    t = h.shape[0]
    return pl.pallas_call(
        functools.partial(_combine_kernel, tm=tm),
        grid_spec=pltpu.PrefetchScalarGridSpec(
            num_scalar_prefetch=1,
            grid=(t // tm,),
            in_specs=[pl.BlockSpec((1, 1, 4 * tm), lambda i, s: (i, 0, 0), memory_space=pltpu.SMEM),
                      pl.BlockSpec((tm, D_MODEL), lambda i, s: (i, 0)),
                      pl.BlockSpec((tm, LANES), lambda i, s: (i, 0)),
                      pl.BlockSpec((1, D_MODEL), lambda i, s: (0, 0)),
                      pl.BlockSpec(memory_space=pl.ANY)],
            out_specs=pl.BlockSpec((tm, D_MODEL), lambda i, s: (i, 0)),
            scratch_shapes=[pltpu.VMEM((TOP_K, tm * PIECES, LANES), U32), pltpu.SemaphoreType.DMA(())],
        ),
        out_shape=jax.ShapeDtypeStruct((t, D_MODEL), F32),
        compiler_params=pltpu.CompilerParams(dimension_semantics=("arbitrary",), vmem_limit_bytes=VMEM_LIMIT),
        name="combine",
    )(starts, idx3, h, rg, final_w, ys)


def _pair_schedule(counts, n_slots):
    n_blocks = n_slots // SLOT_BLOCK
    n_pairs = n_blocks + N_EXPERTS - 1
    ends = jnp.cumsum(counts)
    starts = ends - counts
    first = starts // SLOT_BLOCK
    last = jnp.maximum(ends - 1, starts) // SLOT_BLOCK
    per_expert = jnp.where(counts > 0, last - first + 1, 0)
    cum = jnp.cumsum(per_expert)
    p = jnp.arange(n_pairs, dtype=jnp.int32)
    e = jnp.minimum(jnp.sum(cum[None, :] <= p[:, None], axis=1), N_EXPERTS - 1).astype(jnp.int32)
    valid = p < cum[-1]
    onehot = e[:, None] == jnp.arange(N_EXPERTS, dtype=jnp.int32)[None, :]
    pick = lambda table: jnp.sum(jnp.where(onehot, table[None, :], 0), axis=1)
    blk = jnp.where(valid, pick(first) + p - pick(cum - per_expert), n_blocks - 1).astype(jnp.int32)
    lo = jnp.where(valid, pick(starts), 0).astype(jnp.int32)
    hi = jnp.where(valid, pick(ends), 0).astype(jnp.int32)
    return starts.astype(jnp.int32), blk, e, lo, hi


def _layer(x, norm_mix_w, w_in, lb_logits, hgrn_onorm_w, gdn_conv_w, gdn_a_log, gdn_dt_bias, gdn_onorm_w, w_out,
           norm_moe_w, router_group_w, router_expert_w, w_gate, w_up, w_down, final_w, *, tm, n_chunks):
    batch, seq, _ = x.shape
    t = batch * seq
    x2 = x.reshape(t, D_MODEL)
    n_hg = 4 * MIX_HALF
    n_gd = GDN_QKV + MIX_HALF
    wb16 = w_in.astype(BF16)
    wa = wb16[:, 0:n_hg]
    wb = wb16[:, n_hg:n_hg + n_gd]
    wc = jnp.pad(wb16[:, n_hg + n_gd:], ((0, 0), (0, LANES - 2 * HEADS)))
    pa, pb, pc = _inproj(x2, norm_mix_w.reshape(1, D_MODEL), wa, wb, wc, gdn_conv_w, 2 * tm, seq // (2 * tm))

    o_hg = _hgrn(pa, lb_logits, hgrn_onorm_w.reshape(1, HEAD_DIM), batch, seq, n_chunks)
    alog_row = jnp.pad(gdn_a_log.reshape(1, HEADS), ((0, 0), (0, LANES - HEADS)))
    dtb_row = jnp.pad(gdn_dt_bias.reshape(1, HEADS), ((0, 0), (0, LANES - HEADS)))
    o_gd = _gdn(pb, pc, alog_row, dtb_row, gdn_onorm_w.reshape(1, HEAD_DIM), batch, seq, n_chunks)

    wr = jnp.pad(jnp.concatenate([router_group_w, router_expert_w], axis=1),
                 ((0, 0), (0, LANES - N_GROUPS - N_EXPERTS)))
    wr_hi = wr.astype(BF16)
    wr = jnp.concatenate([wr_hi, (wr - wr_hi.astype(F32)).astype(BF16)], axis=1)
    h, hn, ri, rg, cnt = _route(x2, o_hg, o_gd, w_out.astype(BF16), norm_moe_w.reshape(1, D_MODEL), wr, 2 * tm)

    counts = cnt[0, N_GROUPS:N_GROUPS + N_EXPERTS].astype(jnp.int32)
    starts, blk, exp, lo, hi = _pair_schedule(counts, t * TOP_K)
    tm_moe = 2 * tm
    idx3 = ri[:, 0:4].reshape(t // tm_moe, 1, 4 * tm_moe)
    xs = _dispatch(starts, idx3, hn, tm_moe)
    ys = _experts(blk, exp, lo, hi, xs, w_gate, w_up, w_down)
    out = _combine(starts, idx3, h, rg, final_w.reshape(1, D_MODEL), ys, tm_moe)
    return out.reshape(batch, seq, D_MODEL)


def kernel(x, norm_mix_w, w_in, hgrn_lb_logits, hgrn_onorm_w, gdn_conv_w, gdn_a_log, gdn_dt_bias, gdn_onorm_w, w_out, norm_moe_w, router_group_w, router_expert_w, expert_w_gate, expert_w_up, expert_w_down, final_norm_w):
    return _layer(x, norm_mix_w[0], w_in[0], hgrn_lb_logits, hgrn_onorm_w[0], gdn_conv_w[0], gdn_a_log[0],
                  gdn_dt_bias[0], gdn_onorm_w[0], w_out[0], norm_moe_w[0], router_group_w[0], router_expert_w[0],
                  expert_w_gate[0], expert_w_up[0], expert_w_down[0], final_norm_w, tm=256, n_chunks=8)
```

```python
import functools

import numpy as np
import jax
import jax.numpy as jnp
from jax import lax
from jax.experimental import pallas as pl
from jax.experimental.pallas import tpu as pltpu

F32 = jnp.float32
BF16 = jnp.bfloat16
HI = lax.Precision.HIGHEST

D_MODEL = 1024
HEADS = 4
HEAD_DIM = 128
MIX_HALF = HEADS * HEAD_DIM
CHUNK = 64
GROUP = 2
N_STACKS = HEADS // GROUP
STACK = GROUP * CHUNK
CONV_K = 4
N_GROUPS = 8
EXPERTS_PER_GROUP = 8
N_EXPERTS = N_GROUPS * EXPERTS_PER_GROUP
TOP_K = 2
EXPERT_FF = 256
NORM_EPS = 1e-6
LANES = 128
BF16_ROWS = 16
SLOT_BLOCK = 256
VMEM_LIMIT = 56 * 1024 * 1024


def _sigmoid(x):
    return 0.5 * jnp.tanh(0.5 * x) + 0.5


def _silu(x):
    return x * _sigmoid(x)


def _dot(a, b, precision=None):
    return jnp.dot(a, b, preferred_element_type=F32, precision=precision)


def _dot_nt(a, b, precision=None):
    return lax.dot_general(a, b, (((1,), (1,)), ((), ())), preferred_element_type=F32, precision=precision)


def _dot_tn(a, b, precision=None):
    return lax.dot_general(a, b, (((0,), (0,)), ((), ())), preferred_element_type=F32, precision=precision)


def _bdot(a, b):
    return _dot(a.astype(BF16), b.astype(BF16))


def _masked_sum(mask3, x):
    hi = x.astype(BF16)
    r1 = x - hi.astype(F32)
    mid = r1.astype(BF16)
    lo = (r1 - mid.astype(F32)).astype(BF16)
    return _dot(mask3, jnp.concatenate([hi, mid, lo], axis=0))


def _triple(mask):
    return np.concatenate([mask, mask, mask], axis=1)


def _stack_heads(a, p):
    return jnp.concatenate([a[:, h * HEAD_DIM:(h + 1) * HEAD_DIM] for h in range(p * GROUP, (p + 1) * GROUP)],
                           axis=0)


def _each(f, *lists):
    return [f(*args) for args in zip(*lists)]


def _l2norm_heads(a):
    return jnp.concatenate(
        [a[:, h * HEAD_DIM:(h + 1) * HEAD_DIM]
         * lax.rsqrt(jnp.sum(jnp.square(a[:, h * HEAD_DIM:(h + 1) * HEAD_DIM]), axis=-1, keepdims=True) + 1e-6)
         for h in range(HEADS)], axis=1)


def _inproj_kernel(x_ref, nw_ref, wa_ref, wb_ref, wc_ref, cw_ref, oa_ref, ob_ref, oc_ref, ubuf, *, tiles_per_seq):
    tm = x_ref.shape[0]

    @pl.when(lax.rem(pl.program_id(0), tiles_per_seq) == 0)
    def _():
        ubuf[0:HIST, :] = jnp.zeros((HIST, GDN_QKV), F32)

    x = x_ref[...]
    ms = jnp.mean(x * x, axis=-1, keepdims=True)
    hn = (x * lax.rsqrt(ms + NORM_EPS) * nw_ref[...]).astype(BF16)
    cw = cw_ref[...]
    scale = (HEAD_DIM ** -0.5, 1.0, None)
    for g in range(GDN_QKV // MIX_HALF):
        cols = slice(g * MIX_HALF, (g + 1) * MIX_HALF)
        ubuf[HIST:HIST + tm, cols] = _dot(hn, wb_ref[:, cols])
        conv = cw[CONV_K - 1:CONV_K, cols] * ubuf[HIST:HIST + tm, cols]
        for j in range(1, CONV_K):
            conv = conv + cw[CONV_K - 1 - j:CONV_K - j, cols] * ubuf[HIST - j:HIST - j + tm, cols]
        ubuf[0:HIST, cols] = ubuf[tm:tm + HIST, cols]
        act = _silu(conv)
        ob_ref[:, cols] = act if scale[g] is None else _l2norm_heads(act) * scale[g]
        oa_ref[:, cols] = _dot(hn, wa_ref[:, cols])
    ob_ref[:, GDN_QKV:] = _dot(hn, wb_ref[:, GDN_QKV:])
    oa_ref[:, GDN_QKV:] = _dot(hn, wa_ref[:, GDN_QKV:])
    oc_ref[...] = _dot(hn, wc_ref[...])


def _inproj(x2, norm_w, wa, wb, wc, conv_w, tm, tiles_per_seq):
    t = x2.shape[0]
    na, nb, nc = wa.shape[1], wb.shape[1], wc.shape[1]
    const = lambda i: (0, 0)
    return pl.pallas_call(
        functools.partial(_inproj_kernel, tiles_per_seq=tiles_per_seq),
        grid=(t // tm,),
        in_specs=[
            pl.BlockSpec((tm, D_MODEL), lambda i: (i, 0)),
            pl.BlockSpec((1, D_MODEL), const),
            pl.BlockSpec((D_MODEL, na), const),
            pl.BlockSpec((D_MODEL, nb), const),
            pl.BlockSpec((D_MODEL, nc), const),
            pl.BlockSpec((CONV_K, GDN_QKV), const),
        ],
        out_specs=[
            pl.BlockSpec((tm, na), lambda i: (i, 0)),
            pl.BlockSpec((tm, nb), lambda i: (i, 0)),
            pl.BlockSpec((tm, nc), lambda i: (i, 0)),
        ],
        out_shape=[
            jax.ShapeDtypeStruct((t, na), F32),
            jax.ShapeDtypeStruct((t, nb), F32),
            jax.ShapeDtypeStruct((t, nc), F32),
        ],
        scratch_shapes=[pltpu.VMEM((HIST + tm, GDN_QKV), F32)],
        compiler_params=pltpu.CompilerParams(dimension_semantics=("arbitrary",), vmem_limit_bytes=VMEM_LIMIT),
        name="inproj",
    )(x2, norm_w, wa, wb, wc, conv_w)


HGRN_LEVELS = (32, 16, 8, 4, 2, 1)
DIAG_CODE = len(HGRN_LEVELS)
NONE_CODE = DIAG_CODE + 1


def _hgrn_arg_matrix():
    t = np.arange(CHUNK)[:, None]
    u = np.arange(CHUNK)[None, :]
    mats = [u <= t]
    for b in HGRN_LEVELS:
        odd = (t // b) % 2 == 1
        start = (t // b) * b
        mats.append(np.where(odd, (u > start) & (u <= t), (u > t) & (u <= start + b)))
    return np.concatenate(mats, axis=0).astype(np.float32)


def _hgrn_level_codes():
    idx = np.arange(STACK)
    h, t = idx // CHUNK, idx % CHUNK
    same = h[:, None] == h[None, :]
    tt, ss = t[:, None], t[None, :]
    code = np.full((STACK, STACK), NONE_CODE, np.int32)
    code[same & (tt == ss)] = DIAG_CODE
    for l, b in enumerate(HGRN_LEVELS):
        sib = (tt // (2 * b) == ss // (2 * b)) & ((tt // b) % 2 == 1) & ((ss // b) % 2 == 0)
        code[same & sib] = l
    return code


def _hgrn_kernel(q_ref, f_ref, i_ref, g_ref, lbl_ref, onw_ref, marg_ref, code_ref, o_ref, st_ref, *, n_chunks):
    @pl.when(pl.program_id(1) == 0)
    def _():
        st_ref[...] = jnp.zeros_like(st_ref)

    lbl = lbl_ref[...]
    lmax = jnp.max(lbl, axis=0, keepdims=True)
    lexp = jnp.exp(lbl - lmax)
    lb = lexp[0:1, :] / jnp.sum(lexp, axis=0, keepdims=True)
    onw = onw_ref[...]
    marg = marg_ref[...]
    code = code_ref[...]
    trow = lax.broadcasted_iota(jnp.int32, (STACK, HEAD_DIM), 0) & (CHUNK - 1)

    q_blk = _silu(q_ref[...]) * (HEAD_DIM ** -0.5)
    fg_blk = lb + (1.0 - lb) * _sigmoid(f_ref[...])
    k_blk = 1.0 - fg_blk
    lf_blk = jnp.log(fg_blk)
    v_blk = i_ref[...]

    rows = [slice(c * CHUNK, (c + 1) * CHUNK) for c in range(n_chunks)]
    args = [_masked_sum(marg, lf_blk[r]) for r in rows]
    e_chunk = [jnp.exp(a) for a in args]
    suf_chunk = [jnp.exp(a[CHUNK - 1:CHUNK] - a[0:CHUNK]) for a in args]
    units = [(c, p) for c in range(n_chunks) for p in range(N_STACKS)]
    qs = [_stack_heads(q_blk[rows[c]], p) for c, p in units]
    ks = [_stack_heads(k_blk[rows[c]], p) for c, p in units]
    vs = [_stack_heads(v_blk[rows[c]], p).astype(BF16) for c, p in units]
    e_part = lambda n: [_stack_heads(e_chunk[c][n * CHUNK:(n + 1) * CHUNK], p) for c, p in units]
    cum_e = e_part(0)
    suf_e = [_stack_heads(suf_chunk[c], p) for c, p in units]
    att = _each(lambda q, k: jnp.where(code == DIAG_CODE, jnp.sum(q * k, axis=-1, keepdims=True), 0.0), qs, ks)

    def level_update(a, x, l, b):
        if b < BF16_ROWS:
            return jnp.where(code == l, _dot_nt(x, x), a)
        blocks = [slice(s0, s0 + b) for s0 in range(0, STACK, b)]
        r = _dot_nt(jnp.concatenate([x[bl] for bl in blocks[1::2]], axis=0), x)
        return jnp.concatenate(
            [a[bl] if n % 2 == 0 else jnp.where(code[bl] == l, r[(n // 2) * b:(n // 2 + 1) * b], a[bl])
             for n, bl in enumerate(blocks)], axis=0)

    for l, b in enumerate(HGRN_LEVELS):
        x = _each(lambda q, k, el: (jnp.where((trow & b) != 0, q, k) * el).astype(BF16), qs, ks, e_part(1 + l))
        att = _each(lambda xi, a: level_update(a, xi, l, b), x, att)
    o_intra = _each(lambda a, v: _dot(a.astype(BF16), v), att, vs)
    qc = _each(lambda q, e: (q * e).astype(BF16), qs, cum_e)
    kd = _each(lambda k, e: (k * e).astype(BF16), ks, suf_e)

    state = [st_ref[h] for h in range(HEADS)]
    for u, (c, p) in enumerate(units):
        for i in range(GROUP):
            h = p * GROUP + i
            hr = slice(i * CHUNK, (i + 1) * CHUNK)
            hc = slice(h * HEAD_DIM, (h + 1) * HEAD_DIM)
            o = _dot_nt(qc[u][hr], state[h].astype(BF16)) + o_intra[u][hr]
            decay = cum_e[u][i * CHUNK + CHUNK - 1:i * CHUNK + CHUNK, :]
            state[h] = decay * state[h] + _dot_tn(vs[u][hr], kd[u][hr])
            y = o * lax.rsqrt(jnp.mean(o * o, axis=-1, keepdims=True) + NORM_EPS) * onw
            o_ref[rows[c], hc] = (y * _silu(g_ref[rows[c], hc])).astype(o_ref.dtype)
    for h in range(HEADS):
        st_ref[h] = state[h]


def _hgrn(pa, lb_logits, onorm_w, batch, seq, n_chunks):
    rows = n_chunks * CHUNK
    steps = seq // rows
    col = lambda j: pl.BlockSpec((rows, MIX_HALF), lambda b, s, j=j: (b * steps + s, j))
    const = lambda b, s: (0, 0)
    marg = jnp.asarray(_triple(_hgrn_arg_matrix()), BF16)
    code = jnp.asarray(_hgrn_level_codes())
    return pl.pallas_call(
        functools.partial(_hgrn_kernel, n_chunks=n_chunks),
        grid=(batch, steps),
        in_specs=[col(0), col(1), col(2), col(3),
                  pl.BlockSpec(lb_logits.shape, const),
                  pl.BlockSpec((1, HEAD_DIM), const),
                  pl.BlockSpec(marg.shape, const),
                  pl.BlockSpec(code.shape, const)],
        out_specs=pl.BlockSpec((rows, MIX_HALF), lambda b, s: (b * steps + s, 0)),
        out_shape=jax.ShapeDtypeStruct((batch * seq, MIX_HALF), BF16),
        scratch_shapes=[pltpu.VMEM((HEADS, HEAD_DIM, HEAD_DIM), F32)],
        compiler_params=pltpu.CompilerParams(dimension_semantics=("arbitrary", "arbitrary"),
                                             vmem_limit_bytes=VMEM_LIMIT),
        name="hgrn2",
    )(pa, pa, pa, pa, lb_logits, onorm_w, marg, code)


GDN_QKV = 3 * MIX_HALF
HIST = 8
C_NONE, C_DIAG, C_B16, C_B32, C_B64 = 0, 1, 2, 3, 4
MASKED_EXPONENT = -1e30


def _gdn_codes():
    idx = np.arange(STACK)
    h, t = idx // CHUNK, idx % CHUNK
    same = h[:, None] == h[None, :]
    tt, ss = t[:, None], t[None, :]
    code = np.full((STACK, STACK), C_NONE, np.int32)
    low = same & (ss < tt)
    code[low] = C_B64
    code[low & (tt // 32 == ss // 32)] = C_B32
    code[low & (tt // 16 == ss // 16)] = C_B16
    code[same & (tt == ss)] = C_DIAG
    incl = np.tril(np.ones((CHUNK, CHUNK), np.float32))
    return code, incl


def _gdn_kernel(qkv_ref, z_ref, ab_ref, alog_ref, dtb_ref, onw_ref, code_ref, incl_ref,
                o_ref, st_ref, *, n_chunks):
    @pl.when(pl.program_id(1) == 0)
    def _():
        st_ref[...] = jnp.zeros_like(st_ref)

    onw = onw_ref[...]
    code = code_ref[...]
    incl_m = incl_ref[...]
    incl = code >= C_DIAG
    eye = (code == C_DIAG).astype(F32)

    rows = [slice(c * CHUNK, (c + 1) * CHUNK) for c in range(n_chunks)]
    units = [(c, p) for c in range(n_chunks) for p in range(N_STACKS)]

    qs = [_stack_heads(qkv_ref[rows[c], 0:MIX_HALF], p) for c, p in units]
    ks = [_stack_heads(qkv_ref[rows[c], MIX_HALF:2 * MIX_HALF], p) for c, p in units]
    vs = [_stack_heads(qkv_ref[rows[c], 2 * MIX_HALF:3 * MIX_HALF], p) for c, p in units]
    ab_blk = ab_ref[...]
    xa = ab_blk + dtb_ref[...]
    softplus = jnp.maximum(xa, 0.0) + jnp.log(1.0 + jnp.exp(-jnp.abs(xa)))
    g_all = -jnp.exp(alog_ref[...]) * softplus
    beta_all = _sigmoid(ab_blk)

    def head_cols(a, c, p, first_lane):
        return jnp.concatenate([a[c][:, first_lane + h:first_lane + h + 1]
                                for h in range(p * GROUP, (p + 1) * GROUP)], axis=0)

    g_cum = [_masked_sum(incl_m, g_all[r]) for r in rows]
    beta_chunk = [beta_all[r] for r in rows]
    beta_st = [head_cols(beta_chunk, c, p, HEADS) for c, p in units]
    gc = [jnp.broadcast_to(head_cols(g_cum, c, p, 0), (STACK, HEAD_DIM)) for c, p in units]
    dec = [jnp.exp(jnp.where(incl, g - jnp.transpose(g)[0:1, :], MASKED_EXPONENT)) for g in gc]
    kb = [k.astype(BF16) for k in ks]
    kq = _each(lambda k, q: _dot_nt(jnp.concatenate([k, q.astype(BF16)], axis=0), k), kb, qs)
    kk = [a[0:STACK] for a in kq]
    qk = [a[STACK:] for a in kq]
    am = _each(lambda b, k2, d: b * k2 * d, beta_st, kk, dec)
    a16 = [jnp.where(code == C_B16, a, 0.0) for a in am]
    n32 = [jnp.where(code == C_B32, a, 0.0) for a in am]
    n64 = [jnp.where(code == C_B64, a, 0.0) for a in am]
    apow = _each(_bdot, a16, a16)
    pinv = [eye - a for a in a16]
    for _ in range(2):
        both = _each(lambda ai, pi: _bdot(jnp.concatenate([ai, pi], axis=0), ai), apow, pinv)
        pinv = _each(lambda pi, b: pi + b[STACK:], pinv, both)
        apow = [b[0:STACK] for b in both]
    pinv = _each(lambda pi, ai: pi + _bdot(pi, ai), pinv, apow)
    for nlev in (n32, n64):
        t = _each(_bdot, nlev, pinv)
        pinv = _each(lambda pi, ti: pi - _bdot(pi, ti), pinv, t)
    egc = [jnp.exp(g) for g in gc]
    rhs = _each(lambda k, v, b, e: jnp.concatenate([k * (b * e), v * b], axis=1), ks, vs, beta_st, egc)
    wu = _each(_bdot, pinv, rhs)
    w_c = [a[:, 0:HEAD_DIM].astype(BF16) for a in wu]
    u_c = [a[:, HEAD_DIM:] for a in wu]
    qkm = _each(lambda a, d: (a * d).astype(BF16), qk, dec)
    qg = _each(lambda q, e: (q * e).astype(BF16), qs, egc)
    glast = [[g[i * CHUNK + CHUNK - 1:(i + 1) * CHUNK, :] for i in range(GROUP)] for g in gc]
    kdec = [[(k[i * CHUNK:(i + 1) * CHUNK] * jnp.exp(gl[i] - g[i * CHUNK:(i + 1) * CHUNK])).astype(BF16)
             for i in range(GROUP)] for k, g, gl in zip(ks, gc, glast)]
    gend = [[jnp.exp(gi) for gi in gl] for gl in glast]

    state = [st_ref[h] for h in range(HEADS)]
    for u, (c, p) in enumerate(units):
        heads = range(p * GROUP, (p + 1) * GROUP)
        hrs = [slice(i * CHUNK, (i + 1) * CHUNK) for i in range(GROUP)]
        ws = [_dot_nt(jnp.concatenate([w_c[u][hr], qg[u][hr]], axis=0), state[h].astype(BF16))
              for hr, h in zip(hrs, heads)]
        v_new = jnp.concatenate([u_c[u][hr] - a[0:CHUNK] for hr, a in zip(hrs, ws)], axis=0).astype(BF16)
        o_st = jnp.concatenate([a[CHUNK:] for a in ws], axis=0) + _dot(qkm[u], v_new)
        for i, h in enumerate(heads):
            hc = slice(h * HEAD_DIM, (h + 1) * HEAD_DIM)
            state[h] = gend[u][i] * state[h] + _dot_tn(v_new[hrs[i]], kdec[u][i])
            o = o_st[hrs[i]]
            y = o * lax.rsqrt(jnp.mean(o * o, axis=-1, keepdims=True) + NORM_EPS) * onw
            o_ref[rows[c], hc] = (y * _silu(z_ref[rows[c], hc])).astype(o_ref.dtype)
    for h in range(HEADS):
        st_ref[h] = state[h]


def _gdn(pb, pc, alog_row, dtb_row, onorm_w, batch, seq, n_chunks):
    rows = n_chunks * CHUNK
    steps = seq // rows
    const = lambda b, s: (0, 0)
    code, incl = _gdn_codes()
    code, incl = jnp.asarray(code), jnp.asarray(_triple(incl), BF16)
    sq = pl.BlockSpec((STACK, STACK), const)
    row = pl.BlockSpec((1, LANES), const)
    return pl.pallas_call(
        functools.partial(_gdn_kernel, n_chunks=n_chunks),
        grid=(batch, steps),
        in_specs=[pl.BlockSpec((rows, GDN_QKV), lambda b, s: (b * steps + s, 0)),
                  pl.BlockSpec((rows, MIX_HALF), lambda b, s: (b * steps + s, GDN_QKV // MIX_HALF)),
                  pl.BlockSpec((rows, LANES), lambda b, s: (b * steps + s, 0)),
                  row, row, row, sq, pl.BlockSpec((CHUNK, 3 * CHUNK), const)],
        out_specs=pl.BlockSpec((rows, MIX_HALF), lambda b, s: (b * steps + s, 0)),
        out_shape=jax.ShapeDtypeStruct((batch * seq, MIX_HALF), BF16),
        scratch_shapes=[pltpu.VMEM((HEADS, HEAD_DIM, HEAD_DIM), F32)],
        compiler_params=pltpu.CompilerParams(dimension_semantics=("arbitrary", "arbitrary"),
                                             vmem_limit_bytes=VMEM_LIMIT),
        name="gdn",
    )(pb, pb, pc, alog_row, dtb_row, onorm_w, code, incl)


def _route_kernel(x_ref, ohg_ref, ogd_ref, wo_ref, nw_ref, wr_ref, ltri_ref,
                  h_ref, hn_ref, ri_ref, rg_ref, cnt_ref, cnt_scr):
    @pl.when(pl.program_id(0) == 0)
    def _():
        cnt_scr[...] = jnp.zeros_like(cnt_scr)

    mix = _dot(ohg_ref[...], wo_ref[0:MIX_HALF, :]) + _dot(ogd_ref[...], wo_ref[MIX_HALF:2 * MIX_HALF, :])
    h = x_ref[...] + mix
    h_ref[...] = h
    hn = h * lax.rsqrt(jnp.mean(h * h, axis=-1, keepdims=True) + NORM_EPS) * nw_ref[...]
    _store_rows_as_tiles(hn_ref, hn)
    hn_hi = hn.astype(BF16)
    hn_lo = (hn - hn_hi.astype(F32)).astype(BF16)
    part = _dot(hn_hi, wr_ref[...])
    logits = (_dot(hn_lo, wr_ref[:, 0:LANES]) + part[:, LANES:]) + part[:, 0:LANES]
    tm = logits.shape[0]
    lane = lax.broadcasted_iota(jnp.int32, (tm, LANES), 1)
    neg = jnp.float32(-jnp.inf)
    big = jnp.int32(LANES)

    def first_max(vals):
        m = jnp.max(vals, axis=-1, keepdims=True)
        return m, jnp.min(jnp.where(vals == m, lane, big), axis=-1, keepdims=True)

    gl = jnp.where(lane < N_GROUPS, logits, neg)
    gmax, gidx = first_max(gl)
    p_group = 1.0 / jnp.sum(jnp.exp(gl - gmax), axis=-1, keepdims=True)
    lo = N_GROUPS + EXPERTS_PER_GROUP * gidx
    el = jnp.where((lane >= lo) & (lane < lo + EXPERTS_PER_GROUP), logits, neg)
    m1, i1 = first_max(el)
    m2, i2 = first_max(jnp.where(lane == i1, neg, el))
    r = jnp.exp(m2 - m1)
    gate1 = p_group / (1.0 + r)
    gate2 = p_group * r / (1.0 + r)
    hot1 = lane == i1
    hot2 = lane == i2
    onehot = jnp.where(hot1 | hot2, 1.0, 0.0)
    before = _dot(ltri_ref[...], onehot.astype(BF16)) + cnt_scr[...]
    rank1 = jnp.sum(jnp.where(hot1, before, 0.0), axis=-1, keepdims=True)
    rank2 = jnp.sum(jnp.where(hot2, before, 0.0), axis=-1, keepdims=True)
    cnt = cnt_scr[...] + jnp.sum(onehot, axis=0, keepdims=True)
    cnt_scr[...] = cnt
    cnt_ref[...] = cnt
    ri = jnp.where(lane == 0, i1 - N_GROUPS,
                   jnp.where(lane == 1, i2 - N_GROUPS,
                             jnp.where(lane == 2, rank1.astype(jnp.int32),
                                       jnp.where(lane == 3, rank2.astype(jnp.int32), 0))))
    ri_ref[...] = ri
    rg_ref[...] = jnp.where(lane == 0, gate1, jnp.where(lane == 1, gate2, 0.0))


def _route(x2, ohg, ogd, wo, norm_w, wr, tm):
    t = x2.shape[0]
    ltri = jnp.asarray(np.tril(np.ones((tm, tm), np.float32), -1), BF16)
    const = lambda i: (0, 0)
    tile = lambda n: pl.BlockSpec((tm, n), lambda i: (i, 0))
    return pl.pallas_call(
        _route_kernel,
        grid=(t // tm,),
        in_specs=[tile(D_MODEL), tile(MIX_HALF), tile(MIX_HALF),
                  pl.BlockSpec((D_MODEL, D_MODEL), const),
                  pl.BlockSpec((1, D_MODEL), const),
                  pl.BlockSpec((D_MODEL, 2 * LANES), const),
                  pl.BlockSpec((tm, tm), const)],
        out_specs=[tile(D_MODEL), pl.BlockSpec((tm * PIECES, LANES), lambda i: (i, 0)), tile(LANES), tile(LANES),
                   pl.BlockSpec((1, LANES), const)],
        out_shape=[jax.ShapeDtypeStruct((t, D_MODEL), F32),
                   jax.ShapeDtypeStruct((t * PIECES, LANES), U32),
                   jax.ShapeDtypeStruct((t, LANES), jnp.int32),
                   jax.ShapeDtypeStruct((t, LANES), F32),
                   jax.ShapeDtypeStruct((1, LANES), F32)],
        scratch_shapes=[pltpu.VMEM((1, LANES), F32)],
        compiler_params=pltpu.CompilerParams(dimension_semantics=("arbitrary",), vmem_limit_bytes=VMEM_LIMIT),
        name="route",
    )(x2, ohg, ogd, wo, norm_w, wr, ltri)


ISSUE_UNROLL = 8


HALF = D_MODEL // 2
PIECES = HALF // LANES
U32 = jnp.uint32
HIGH_HALF = np.uint32(0xFFFF0000)


def _store_rows_as_tiles(ref, x):
    n = x.shape[0]
    bits = lambda a: lax.bitcast_convert_type(a.astype(BF16).astype(F32), U32)
    words = (bits(x[:, 0:HALF]) >> 16) | (bits(x[:, HALF:]) & HIGH_HALF)
    for s in range(PIECES):
        ref[pl.ds(s, n, stride=PIECES), :] = words[:, s * LANES:(s + 1) * LANES]


def _load_rows_from_tiles(ref, n):
    words = jnp.concatenate([ref[pl.ds(s, n, stride=PIECES), :] for s in range(PIECES)], axis=1)
    return (lax.bitcast_convert_type(words << 16, F32), lax.bitcast_convert_type(words & HIGH_HALF, F32))


def _row_copy(src_ref, src_row, dst_ref, dst_row, sem):
    src = src_ref.at[pl.ds(pl.multiple_of(src_row * PIECES, PIECES), PIECES)]
    dst = dst_ref.at[pl.ds(pl.multiple_of(dst_row * PIECES, PIECES), PIECES)]
    return pltpu.make_async_copy(src, dst, sem)


def _dispatch_kernel(starts_ref, idx_ref, hn_ref, xs_ref, sem, *, tm):
    def issue(t, carry):
        for k in range(TOP_K):
            dest = starts_ref[idx_ref[0, 0, 4 * t + k]] + idx_ref[0, 0, 4 * t + 2 + k]
            _row_copy(hn_ref, t, xs_ref, dest, sem).start(priority=k % 2)
        return carry

    lax.fori_loop(0, tm, issue, 0, unroll=ISSUE_UNROLL)
    for k in range(TOP_K):
        pltpu.make_async_copy(hn_ref, xs_ref.at[pl.ds(0, tm * PIECES)], sem).wait()


def _dispatch(starts, idx3, hn, tm):
    t = hn.shape[0] // PIECES
    return pl.pallas_call(
        functools.partial(_dispatch_kernel, tm=tm),
        grid_spec=pltpu.PrefetchScalarGridSpec(
            num_scalar_prefetch=1,
            grid=(t // tm,),
            in_specs=[pl.BlockSpec((1, 1, 4 * tm), lambda i, s: (i, 0, 0), memory_space=pltpu.SMEM),
                      pl.BlockSpec((tm * PIECES, LANES), lambda i, s: (i, 0))],
            out_specs=pl.BlockSpec(memory_space=pl.ANY),
            scratch_shapes=[pltpu.SemaphoreType.DMA(())],
        ),
        out_shape=jax.ShapeDtypeStruct((t * TOP_K * PIECES, LANES), U32),
        compiler_params=pltpu.CompilerParams(dimension_semantics=("arbitrary",), vmem_limit_bytes=VMEM_LIMIT),
        name="dispatch",
    )(starts, idx3, hn)


EXPERT_SPLIT = 1


def _expert_kernel(blk_ref, exp_ref, lo_ref, hi_ref, xs_ref, wg_ref, wu_ref, wd_ref, ys_ref, wgu_b, wd_b, acc):
    p = pl.program_id(0)

    @pl.when(jnp.logical_or(p == 0, exp_ref[p] != exp_ref[jnp.maximum(p - 1, 0)]))
    def _():
        wgu_b[:, 0:EXPERT_FF] = wg_ref[0].astype(BF16)
        wgu_b[:, EXPERT_FF:2 * EXPERT_FF] = wu_ref[0].astype(BF16)
        wd_b[...] = wd_ref[0].astype(BF16)

    @pl.when(p == 0)
    def _():
        acc[...] = jnp.zeros_like(acc)

    first = jnp.logical_or(p == 0, blk_ref[p] != blk_ref[jnp.maximum(p - 1, 0)])
    sub = SLOT_BLOCK // EXPERT_SPLIT
    for r in range(EXPERT_SPLIT):
        x_lo, x_hi = _load_rows_from_tiles(xs_ref.at[pl.ds(r * sub * PIECES, sub * PIECES)], sub)
        ab = _dot(x_lo.astype(BF16), wgu_b[0:HALF, :]) + _dot(x_hi.astype(BF16), wgu_b[HALF:D_MODEL, :])
        hb = _silu(ab[:, 0:EXPERT_FF]) * ab[:, EXPERT_FF:2 * EXPERT_FF]
        y = _dot(hb.astype(BF16), wd_b[...])
        slot = blk_ref[p] * SLOT_BLOCK + r * sub + lax.broadcasted_iota(jnp.int32, (sub, 1), 0)
        y = jnp.where((slot >= lo_ref[p]) & (slot < hi_ref[p]), y, 0.0)
        total = y + jnp.where(first, 0.0, acc[pl.ds(r * sub, sub), :])
        acc[pl.ds(r * sub, sub), :] = total
        _store_rows_as_tiles(ys_ref.at[pl.ds(r * sub * PIECES, sub * PIECES)], total)


def _experts(blk, exp, lo, hi, xs, wg, wu, wd):
    n_pairs = blk.shape[0]
    return pl.pallas_call(
        _expert_kernel,
        grid_spec=pltpu.PrefetchScalarGridSpec(
            num_scalar_prefetch=4,
            grid=(n_pairs,),
            in_specs=[pl.BlockSpec((SLOT_BLOCK * PIECES, LANES), lambda p, blk, exp, lo, hi: (blk[p], 0)),
                      pl.BlockSpec((1, D_MODEL, EXPERT_FF), lambda p, blk, exp, lo, hi: (exp[p], 0, 0)),
                      pl.BlockSpec((1, D_MODEL, EXPERT_FF), lambda p, blk, exp, lo, hi: (exp[p], 0, 0)),
                      pl.BlockSpec((1, EXPERT_FF, D_MODEL), lambda p, blk, exp, lo, hi: (exp[p], 0, 0))],
            out_specs=pl.BlockSpec((SLOT_BLOCK * PIECES, LANES), lambda p, blk, exp, lo, hi: (blk[p], 0)),
            scratch_shapes=[pltpu.VMEM((D_MODEL, 2 * EXPERT_FF), BF16), pltpu.VMEM((EXPERT_FF, D_MODEL), BF16),
                            pltpu.VMEM((SLOT_BLOCK, D_MODEL), F32)],
        ),
        out_shape=jax.ShapeDtypeStruct(xs.shape, U32),
        compiler_params=pltpu.CompilerParams(dimension_semantics=("arbitrary",), vmem_limit_bytes=VMEM_LIMIT),
        name="experts",
    )(blk, exp, lo, hi, xs, wg, wu, wd)


def _combine_kernel(starts_ref, idx_ref, h_ref, rg_ref, fw_ref, ys_ref, o_ref, buf, sem, *, tm):
    def issue(t, carry):
        for k in range(TOP_K):
            src = starts_ref[idx_ref[0, 0, 4 * t + k]] + idx_ref[0, 0, 4 * t + 2 + k]
            _row_copy(ys_ref, src, buf.at[k], t, sem).start(priority=k % 2)
        return carry

    lax.fori_loop(0, tm, issue, 0, unroll=ISSUE_UNROLL)
    for k in range(TOP_K):
        pltpu.make_async_copy(ys_ref.at[pl.ds(0, tm * PIECES)], buf.at[k], sem).wait()
    rg = rg_ref[...]
    y = [jnp.concatenate(_load_rows_from_tiles(buf.at[k], tm), axis=1) for k in range(TOP_K)]
    h = h_ref[...] + rg[:, 0:1] * y[0] + rg[:, 1:2] * y[1]
    o_ref[...] = h * lax.rsqrt(jnp.mean(h * h, axis=-1, keepdims=True) + NORM_EPS) * fw_ref[...]


def _combine(starts, idx3, h, rg, final_w, ys, tm):
    t = h.shape[0]
    return pl.pallas_call(
        functools.partial(_combine_kernel, tm=tm),
        grid_spec=pltpu.PrefetchScalarGridSpec(
            num_scalar_prefetch=1,
            grid=(t // tm,),
            in_specs=[pl.BlockSpec((1, 1, 4 * tm), lambda i, s: (i, 0, 0), memory_space=pltpu.SMEM),
                      pl.BlockSpec((tm, D_MODEL), lambda i, s: (i, 0)),
                      pl.BlockSpec((tm, LANES), lambda i, s: (i, 0)),
                      pl.BlockSpec((1, D_MODEL), lambda i, s: (0, 0)),
                      pl.BlockSpec(memory_space=pl.ANY)],
            out_specs=pl.BlockSpec((tm, D_MODEL), lambda i, s: (i, 0)),
            scratch_shapes=[pltpu.VMEM((TOP_K, tm * PIECES, LANES), U32), pltpu.SemaphoreType.DMA(())],
        ),
        out_shape=jax.ShapeDtypeStruct((t, D_MODEL), F32),
        compiler_params=pltpu.CompilerParams(dimension_semantics=("arbitrary",), vmem_limit_bytes=VMEM_LIMIT),
        name="combine",
    )(starts, idx3, h, rg, final_w, ys)


def _pair_schedule(counts, n_slots):
    n_blocks = n_slots // SLOT_BLOCK
    n_pairs = n_blocks + N_EXPERTS - 1
    ends = jnp.cumsum(counts)
    starts = ends - counts
    first = starts // SLOT_BLOCK
    last = jnp.maximum(ends - 1, starts) // SLOT_BLOCK
    per_expert = jnp.where(counts > 0, last - first + 1, 0)
    cum = jnp.cumsum(per_expert)
    p = jnp.arange(n_pairs, dtype=jnp.int32)
    e = jnp.minimum(jnp.sum(cum[None, :] <= p[:, None], axis=1), N_EXPERTS - 1).astype(jnp.int32)
    valid = p < cum[-1]
    onehot = e[:, None] == jnp.arange(N_EXPERTS, dtype=jnp.int32)[None, :]
    pick = lambda table: jnp.sum(jnp.where(onehot, table[None, :], 0), axis=1)
    blk = jnp.where(valid, pick(first) + p - pick(cum - per_expert), n_blocks - 1).astype(jnp.int32)
    lo = jnp.where(valid, pick(starts), 0).astype(jnp.int32)
    hi = jnp.where(valid, pick(ends), 0).astype(jnp.int32)
    return starts.astype(jnp.int32), blk, e, lo, hi


def _layer(x, norm_mix_w, w_in, lb_logits, hgrn_onorm_w, gdn_conv_w, gdn_a_log, gdn_dt_bias, gdn_onorm_w, w_out,
           norm_moe_w, router_group_w, router_expert_w, w_gate, w_up, w_down, final_w, *, tm, n_chunks):
    batch, seq, _ = x.shape
    t = batch * seq
    x2 = x.reshape(t, D_MODEL)
    n_hg = 4 * MIX_HALF
    n_gd = GDN_QKV + MIX_HALF
    wb16 = w_in.astype(BF16)
    wa = wb16[:, 0:n_hg]
    wb = wb16[:, n_hg:n_hg + n_gd]
    wc = jnp.pad(wb16[:, n_hg + n_gd:], ((0, 0), (0, LANES - 2 * HEADS)))
    pa, pb, pc = _inproj(x2, norm_mix_w.reshape(1, D_MODEL), wa, wb, wc, gdn_conv_w, 2 * tm, seq // (2 * tm))

    o_hg = _hgrn(pa, lb_logits, hgrn_onorm_w.reshape(1, HEAD_DIM), batch, seq, n_chunks)
    alog_row = jnp.pad(gdn_a_log.reshape(1, HEADS), ((0, 0), (0, LANES - HEADS)))
    dtb_row = jnp.pad(gdn_dt_bias.reshape(1, HEADS), ((0, 0), (0, LANES - HEADS)))
    o_gd = _gdn(pb, pc, alog_row, dtb_row, gdn_onorm_w.reshape(1, HEAD_DIM), batch, seq, n_chunks)

    wr = jnp.pad(jnp.concatenate([router_group_w, router_expert_w], axis=1),
                 ((0, 0), (0, LANES - N_GROUPS - N_EXPERTS)))
    wr_hi = wr.astype(BF16)
    wr = jnp.concatenate([wr_hi, (wr - wr_hi.astype(F32)).astype(BF16)], axis=1)
    h, hn, ri, rg, cnt = _route(x2, o_hg, o_gd, w_out.astype(BF16), norm_moe_w.reshape(1, D_MODEL), wr, 2 * tm)

    counts = cnt[0, N_GROUPS:N_GROUPS + N_EXPERTS].astype(jnp.int32)
    starts, blk, exp, lo, hi = _pair_schedule(counts, t * TOP_K)
    tm_moe = 4 * tm
    idx3 = ri[:, 0:4].reshape(t // tm_moe, 1, 4 * tm_moe)
    xs = _dispatch(starts, idx3, hn, tm_moe)
    ys = _experts(blk, exp, lo, hi, xs, w_gate, w_up, w_down)
    out = _combine(starts, idx3, h, rg, final_w.reshape(1, D_MODEL), ys, tm_moe)
    return out.reshape(batch, seq, D_MODEL)


def kernel(x, norm_mix_w, w_in, hgrn_lb_logits, hgrn_onorm_w, gdn_conv_w, gdn_a_log, gdn_dt_bias, gdn_onorm_w, w_out, norm_moe_w, router_group_w, router_expert_w, expert_w_gate, expert_w_up, expert_w_down, final_norm_w):
    return _layer(x, norm_mix_w[0], w_in[0], hgrn_lb_logits, hgrn_onorm_w[0], gdn_conv_w[0], gdn_a_log[0],
                  gdn_dt_bias[0], gdn_onorm_w[0], w_out[0], norm_moe_w[0], router_group_w[0], router_expert_w[0],
                  expert_w_gate[0], expert_w_up[0], expert_w_down[0], final_norm_w, tm=256, n_chunks=8)
```

```python
import functools

import numpy as np
import jax
import jax.numpy as jnp
from jax import lax
from jax.experimental import pallas as pl
from jax.experimental.pallas import tpu as pltpu

F32 = jnp.float32
BF16 = jnp.bfloat16
HI = lax.Precision.HIGHEST

D_MODEL = 1024
HEADS = 4
HEAD_DIM = 128
MIX_HALF = HEADS * HEAD_DIM
CHUNK = 64
GROUP = 2
N_STACKS = HEADS // GROUP
STACK = GROUP * CHUNK
CONV_K = 4
N_GROUPS = 8
EXPERTS_PER_GROUP = 8
N_EXPERTS = N_GROUPS * EXPERTS_PER_GROUP
TOP_K = 2
EXPERT_FF = 256
NORM_EPS = 1e-6
LANES = 128
BF16_ROWS = 16
SLOT_BLOCK = 512
VMEM_LIMIT = 56 * 1024 * 1024


def _sigmoid(x):
    return 0.5 * jnp.tanh(0.5 * x) + 0.5


def _silu(x):
    return x * _sigmoid(x)


def _dot(a, b, precision=None):
    return jnp.dot(a, b, preferred_element_type=F32, precision=precision)


def _dot_nt(a, b, precision=None):
    return lax.dot_general(a, b, (((1,), (1,)), ((), ())), preferred_element_type=F32, precision=precision)


def _dot_tn(a, b, precision=None):
    return lax.dot_general(a, b, (((0,), (0,)), ((), ())), preferred_element_type=F32, precision=precision)


def _bdot(a, b):
    return _dot(a.astype(BF16), b.astype(BF16))


def _masked_sum(mask3, x):
    hi = x.astype(BF16)
    r1 = x - hi.astype(F32)
    mid = r1.astype(BF16)
    lo = (r1 - mid.astype(F32)).astype(BF16)
    return _dot(mask3, jnp.concatenate([hi, mid, lo], axis=0))


def _triple(mask):
    return np.concatenate([mask, mask, mask], axis=1)


def _stack_heads(a, p):
    return jnp.concatenate([a[:, h * HEAD_DIM:(h + 1) * HEAD_DIM] for h in range(p * GROUP, (p + 1) * GROUP)],
                           axis=0)


def _each(f, *lists):
    return [f(*args) for args in zip(*lists)]


INPROJ_GROUP = 256


def _l2norm_heads(a):
    return jnp.concatenate(
        [a[:, h * HEAD_DIM:(h + 1) * HEAD_DIM]
         * lax.rsqrt(jnp.sum(jnp.square(a[:, h * HEAD_DIM:(h + 1) * HEAD_DIM]), axis=-1, keepdims=True) + 1e-6)
         for h in range(a.shape[1] // HEAD_DIM)], axis=1)


def _inproj_kernel(x_ref, nw_ref, wa_ref, wb_ref, wc_ref, cw_ref, oa_ref, ob_ref, oc_ref, ubuf, *, tiles_per_seq):
    tm = x_ref.shape[0]

    @pl.when(lax.rem(pl.program_id(0), tiles_per_seq) == 0)
    def _():
        ubuf[0:HIST, :] = jnp.zeros((HIST, GDN_QKV), F32)

    x = x_ref[...]
    ms = jnp.mean(x * x, axis=-1, keepdims=True)
    hn = (x * lax.rsqrt(ms + NORM_EPS) * nw_ref[...]).astype(BF16)
    cw = cw_ref[...]
    width = INPROJ_GROUP
    group = lambda g: slice(g * width, (g + 1) * width)

    def project(w_ref, o_ref, g, row0=0):
        o_ref[row0:row0 + tm, group(g)] = _dot(hn, w_ref[:, group(g)])

    def conv_group(g):
        cols = group(g)
        conv = cw[CONV_K - 1:CONV_K, cols] * ubuf[HIST:HIST + tm, cols]
        for j in range(1, CONV_K):
            conv = conv + cw[CONV_K - 1 - j:CONV_K - j, cols] * ubuf[HIST - j:HIST - j + tm, cols]
        ubuf[0:HIST, cols] = ubuf[tm:tm + HIST, cols]
        act = _silu(conv)
        if g * width < MIX_HALF:
            act = _l2norm_heads(act) * (HEAD_DIM ** -0.5)
        elif g * width < 2 * MIX_HALF:
            act = _l2norm_heads(act)
        ob_ref[:, cols] = act

    n_conv = GDN_QKV // width
    n_a = wa_ref.shape[1] // width
    others = [functools.partial(project, wa_ref, oa_ref, g) for g in range(n_a)]
    others += [functools.partial(project, wb_ref, ob_ref, g) for g in range(n_conv, wb_ref.shape[1] // width)]
    project(wb_ref, ubuf, 0, HIST)
    for g in range(n_conv):
        if g + 1 < n_conv:
            project(wb_ref, ubuf, g + 1, HIST)
        conv_group(g)
        others.pop(0)()
    for task in others:
        task()
    oc_ref[...] = _dot(hn, wc_ref[...])


def _inproj(x2, norm_w, wa, wb, wc, conv_w, tm, tiles_per_seq):
    t = x2.shape[0]
    na, nb, nc = wa.shape[1], wb.shape[1], wc.shape[1]
    const = lambda i: (0, 0)
    return pl.pallas_call(
        functools.partial(_inproj_kernel, tiles_per_seq=tiles_per_seq),
        grid=(t // tm,),
        in_specs=[
            pl.BlockSpec((tm, D_MODEL), lambda i: (i, 0)),
            pl.BlockSpec((1, D_MODEL), const),
            pl.BlockSpec((D_MODEL, na), const),
            pl.BlockSpec((D_MODEL, nb), const),
            pl.BlockSpec((D_MODEL, nc), const),
            pl.BlockSpec((CONV_K, GDN_QKV), const),
        ],
        out_specs=[
            pl.BlockSpec((tm, na), lambda i: (i, 0)),
            pl.BlockSpec((tm, nb), lambda i: (i, 0)),
            pl.BlockSpec((tm, nc), lambda i: (i, 0)),
        ],
        out_shape=[
            jax.ShapeDtypeStruct((t, na), F32),
            jax.ShapeDtypeStruct((t, nb), F32),
            jax.ShapeDtypeStruct((t, nc), F32),
        ],
        scratch_shapes=[pltpu.VMEM((HIST + tm, GDN_QKV), F32)],
        compiler_params=pltpu.CompilerParams(dimension_semantics=("arbitrary",), vmem_limit_bytes=VMEM_LIMIT),
        name="inproj",
    )(x2, norm_w, wa, wb, wc, conv_w)


HGRN_LEVELS = (32, 16, 8, 4, 2, 1)
DIAG_CODE = len(HGRN_LEVELS)
NONE_CODE = DIAG_CODE + 1


def _hgrn_arg_matrix():
    t = np.arange(CHUNK)[:, None]
    u = np.arange(CHUNK)[None, :]
    mats = [u <= t]
    for b in HGRN_LEVELS:
        odd = (t // b) % 2 == 1
        start = (t // b) * b
        mats.append(np.where(odd, (u > start) & (u <= t), (u > t) & (u <= start + b)))
    return np.concatenate(mats, axis=0).astype(np.float32)


def _hgrn_level_codes():
    idx = np.arange(STACK)
    h, t = idx // CHUNK, idx % CHUNK
    same = h[:, None] == h[None, :]
    tt, ss = t[:, None], t[None, :]
    code = np.full((STACK, STACK), NONE_CODE, np.int32)
    code[same & (tt == ss)] = DIAG_CODE
    for l, b in enumerate(HGRN_LEVELS):
        sib = (tt // (2 * b) == ss // (2 * b)) & ((tt // b) % 2 == 1) & ((ss // b) % 2 == 0)
        code[same & sib] = l
    return code


def _hgrn_kernel(q_ref, f_ref, i_ref, g_ref, lbl_ref, onw_ref, marg_ref, code_ref, o_ref, st_ref, *, n_chunks):
    @pl.when(pl.program_id(1) == 0)
    def _():
        st_ref[...] = jnp.zeros_like(st_ref)

    lbl = lbl_ref[...]
    lmax = jnp.max(lbl, axis=0, keepdims=True)
    lexp = jnp.exp(lbl - lmax)
    lb = lexp[0:1, :] / jnp.sum(lexp, axis=0, keepdims=True)
    onw = onw_ref[...]
    marg = marg_ref[...]
    code = code_ref[...]
    trow = lax.broadcasted_iota(jnp.int32, (STACK, HEAD_DIM), 0) & (CHUNK - 1)

    q_blk = _silu(q_ref[...]) * (HEAD_DIM ** -0.5)
    fg_blk = lb + (1.0 - lb) * _sigmoid(f_ref[...])
    k_blk = 1.0 - fg_blk
    lf_blk = jnp.log(fg_blk)
    v_blk = i_ref[...]

    rows = [slice(c * CHUNK, (c + 1) * CHUNK) for c in range(n_chunks)]
    args = [_masked_sum(marg, lf_blk[r]) for r in rows]
    e_chunk = [jnp.exp(a) for a in args]
    suf_chunk = [jnp.exp(a[CHUNK - 1:CHUNK] - a[0:CHUNK]) for a in args]
    units = [(c, p) for c in range(n_chunks) for p in range(N_STACKS)]
    qs = [_stack_heads(q_blk[rows[c]], p) for c, p in units]
    ks = [_stack_heads(k_blk[rows[c]], p) for c, p in units]
    vs = [_stack_heads(v_blk[rows[c]], p).astype(BF16) for c, p in units]
    e_part = lambda n: [_stack_heads(e_chunk[c][n * CHUNK:(n + 1) * CHUNK], p) for c, p in units]
    cum_e = e_part(0)
    suf_e = [_stack_heads(suf_chunk[c], p) for c, p in units]
    att = _each(lambda q, k: jnp.where(code == DIAG_CODE, jnp.sum(q * k, axis=-1, keepdims=True), 0.0), qs, ks)

    def level_update(a, x, l, b):
        if b < BF16_ROWS:
            return jnp.where(code == l, _dot_nt(x, x), a)
        blocks = [slice(s0, s0 + b) for s0 in range(0, STACK, b)]
        r = _dot_nt(jnp.concatenate([x[bl] for bl in blocks[1::2]], axis=0), x)
        return jnp.concatenate(
            [a[bl] if n % 2 == 0 else jnp.where(code[bl] == l, r[(n // 2) * b:(n // 2 + 1) * b], a[bl])
             for n, bl in enumerate(blocks)], axis=0)

    for l, b in enumerate(HGRN_LEVELS):
        x = _each(lambda q, k, el: (jnp.where((trow & b) != 0, q, k) * el).astype(BF16), qs, ks, e_part(1 + l))
        att = _each(lambda xi, a: level_update(a, xi, l, b), x, att)
    o_intra = _each(lambda a, v: _dot(a.astype(BF16), v), att, vs)
    qc = _each(lambda q, e: (q * e).astype(BF16), qs, cum_e)
    kd = _each(lambda k, e: (k * e).astype(BF16), ks, suf_e)

    state = [st_ref[h] for h in range(HEADS)]
    for u, (c, p) in enumerate(units):
        for i in range(GROUP):
            h = p * GROUP + i
            hr = slice(i * CHUNK, (i + 1) * CHUNK)
            hc = slice(h * HEAD_DIM, (h + 1) * HEAD_DIM)
            o = _dot_nt(qc[u][hr], state[h].astype(BF16)) + o_intra[u][hr]
            decay = cum_e[u][i * CHUNK + CHUNK - 1:i * CHUNK + CHUNK, :]
            state[h] = decay * state[h] + _dot_tn(vs[u][hr], kd[u][hr])
            y = o * lax.rsqrt(jnp.mean(o * o, axis=-1, keepdims=True) + NORM_EPS) * onw
            o_ref[rows[c], hc] = (y * _silu(g_ref[rows[c], hc])).astype(o_ref.dtype)
    for h in range(HEADS):
        st_ref[h] = state[h]


def _hgrn(pa, lb_logits, onorm_w, batch, seq, n_chunks):
    rows = n_chunks * CHUNK
    steps = seq // rows
    col = lambda j: pl.BlockSpec((rows, MIX_HALF), lambda b, s, j=j: (b * steps + s, j))
    const = lambda b, s: (0, 0)
    marg = jnp.asarray(_triple(_hgrn_arg_matrix()), BF16)
    code = jnp.asarray(_hgrn_level_codes())
    return pl.pallas_call(
        functools.partial(_hgrn_kernel, n_chunks=n_chunks),
        grid=(batch, steps),
        in_specs=[col(0), col(1), col(2), col(3),
                  pl.BlockSpec(lb_logits.shape, const),
                  pl.BlockSpec((1, HEAD_DIM), const),
                  pl.BlockSpec(marg.shape, const),
                  pl.BlockSpec(code.shape, const)],
        out_specs=pl.BlockSpec((rows, MIX_HALF), lambda b, s: (b * steps + s, 0)),
        out_shape=jax.ShapeDtypeStruct((batch * seq, MIX_HALF), BF16),
        scratch_shapes=[pltpu.VMEM((HEADS, HEAD_DIM, HEAD_DIM), F32)],
        compiler_params=pltpu.CompilerParams(dimension_semantics=("arbitrary", "arbitrary"),
                                             vmem_limit_bytes=VMEM_LIMIT),
        name="hgrn2",
    )(pa, pa, pa, pa, lb_logits, onorm_w, marg, code)


GDN_QKV = 3 * MIX_HALF
HIST = 8
C_NONE, C_DIAG, C_B16, C_B32, C_B64 = 0, 1, 2, 3, 4
MASKED_EXPONENT = -1e30


def _gdn_codes():
    idx = np.arange(STACK)
    h, t = idx // CHUNK, idx % CHUNK
    same = h[:, None] == h[None, :]
    tt, ss = t[:, None], t[None, :]
    code = np.full((STACK, STACK), C_NONE, np.int32)
    low = same & (ss < tt)
    code[low] = C_B64
    code[low & (tt // 32 == ss // 32)] = C_B32
    code[low & (tt // 16 == ss // 16)] = C_B16
    code[same & (tt == ss)] = C_DIAG
    incl = np.tril(np.ones((CHUNK, CHUNK), np.float32))
    return code, incl


def _gdn_kernel(qkv_ref, z_ref, ab_ref, alog_ref, dtb_ref, onw_ref, code_ref, incl_ref,
                o_ref, st_ref, *, n_chunks):
    @pl.when(pl.program_id(1) == 0)
    def _():
        st_ref[...] = jnp.zeros_like(st_ref)

    onw = onw_ref[...]
    code = code_ref[...]
    incl_m = incl_ref[...]
    incl = code >= C_DIAG
    eye = (code == C_DIAG).astype(F32)

    rows = [slice(c * CHUNK, (c + 1) * CHUNK) for c in range(n_chunks)]
    units = [(c, p) for c in range(n_chunks) for p in range(N_STACKS)]

    qs = [_stack_heads(qkv_ref[rows[c], 0:MIX_HALF], p) for c, p in units]
    ks = [_stack_heads(qkv_ref[rows[c], MIX_HALF:2 * MIX_HALF], p) for c, p in units]
    vs = [_stack_heads(qkv_ref[rows[c], 2 * MIX_HALF:3 * MIX_HALF], p) for c, p in units]
    ab_blk = ab_ref[...]
    xa = ab_blk + dtb_ref[...]
    softplus = jnp.maximum(xa, 0.0) + jnp.log(1.0 + jnp.exp(-jnp.abs(xa)))
    g_all = -jnp.exp(alog_ref[...]) * softplus
    beta_all = _sigmoid(ab_blk)

    def head_cols(a, c, p, first_lane):
        return jnp.concatenate([a[c][:, first_lane + h:first_lane + h + 1]
                                for h in range(p * GROUP, (p + 1) * GROUP)], axis=0)

    g_cum = [_masked_sum(incl_m, g_all[r]) for r in rows]
    beta_chunk = [beta_all[r] for r in rows]
    beta_st = [head_cols(beta_chunk, c, p, HEADS) for c, p in units]
    gc = [jnp.broadcast_to(head_cols(g_cum, c, p, 0), (STACK, HEAD_DIM)) for c, p in units]
    dec = [jnp.exp(jnp.where(incl, g - jnp.transpose(g)[0:1, :], MASKED_EXPONENT)) for g in gc]
    kb = [k.astype(BF16) for k in ks]
    kq = _each(lambda k, q: _dot_nt(jnp.concatenate([k, q.astype(BF16)], axis=0), k), kb, qs)
    kk = [a[0:STACK] for a in kq]
    qk = [a[STACK:] for a in kq]
    am = _each(lambda b, k2, d: b * k2 * d, beta_st, kk, dec)
    a16 = [jnp.where(code == C_B16, a, 0.0) for a in am]
    n32 = [jnp.where(code == C_B32, a, 0.0) for a in am]
    n64 = [jnp.where(code == C_B64, a, 0.0) for a in am]
    apow = _each(_bdot, a16, a16)
    pinv = [eye - a for a in a16]
    for _ in range(2):
        both = _each(lambda ai, pi: _bdot(jnp.concatenate([ai, pi], axis=0), ai), apow, pinv)
        pinv = _each(lambda pi, b: pi + b[STACK:], pinv, both)
        apow = [b[0:STACK] for b in both]
    pinv = _each(lambda pi, ai: pi + _bdot(pi, ai), pinv, apow)
    for nlev in (n32, n64):
        t = _each(_bdot, nlev, pinv)
        pinv = _each(lambda pi, ti: pi - _bdot(pi, ti), pinv, t)
    egc = [jnp.exp(g) for g in gc]
    rhs = _each(lambda k, v, b, e: jnp.concatenate([k * (b * e), v * b], axis=1), ks, vs, beta_st, egc)
    wu = _each(_bdot, pinv, rhs)
    w_c = [a[:, 0:HEAD_DIM].astype(BF16) for a in wu]
    u_c = [a[:, HEAD_DIM:] for a in wu]
    qkm = _each(lambda a, d: (a * d).astype(BF16), qk, dec)
    qg = _each(lambda q, e: (q * e).astype(BF16), qs, egc)
    glast = [[g[i * CHUNK + CHUNK - 1:(i + 1) * CHUNK, :] for i in range(GROUP)] for g in gc]
    kdec = [[(k[i * CHUNK:(i + 1) * CHUNK] * jnp.exp(gl[i] - g[i * CHUNK:(i + 1) * CHUNK])).astype(BF16)
             for i in range(GROUP)] for k, g, gl in zip(ks, gc, glast)]
    gend = [[jnp.exp(gi) for gi in gl] for gl in glast]

    state = [st_ref[h] for h in range(HEADS)]
    for u, (c, p) in enumerate(units):
        heads = range(p * GROUP, (p + 1) * GROUP)
        hrs = [slice(i * CHUNK, (i + 1) * CHUNK) for i in range(GROUP)]
        ws = [_dot_nt(jnp.concatenate([w_c[u][hr], qg[u][hr]], axis=0), state[h].astype(BF16))
              for hr, h in zip(hrs, heads)]
        v_new = jnp.concatenate([u_c[u][hr] - a[0:CHUNK] for hr, a in zip(hrs, ws)], axis=0).astype(BF16)
        o_st = jnp.concatenate([a[CHUNK:] for a in ws], axis=0) + _dot(qkm[u], v_new)
        for i, h in enumerate(heads):
            hc = slice(h * HEAD_DIM, (h + 1) * HEAD_DIM)
            state[h] = gend[u][i] * state[h] + _dot_tn(v_new[hrs[i]], kdec[u][i])
            o = o_st[hrs[i]]
            y = o * lax.rsqrt(jnp.mean(o * o, axis=-1, keepdims=True) + NORM_EPS) * onw
            o_ref[rows[c], hc] = (y * _silu(z_ref[rows[c], hc])).astype(o_ref.dtype)
    for h in range(HEADS):
        st_ref[h] = state[h]


def _gdn(pb, pc, alog_row, dtb_row, onorm_w, batch, seq, n_chunks):
    rows = n_chunks * CHUNK
    steps = seq // rows
    const = lambda b, s: (0, 0)
    code, incl = _gdn_codes()
    code, incl = jnp.asarray(code), jnp.asarray(_triple(incl), BF16)
    sq = pl.BlockSpec((STACK, STACK), const)
    row = pl.BlockSpec((1, LANES), const)
    return pl.pallas_call(
        functools.partial(_gdn_kernel, n_chunks=n_chunks),
        grid=(batch, steps),
        in_specs=[pl.BlockSpec((rows, GDN_QKV), lambda b, s: (b * steps + s, 0)),
                  pl.BlockSpec((rows, MIX_HALF), lambda b, s: (b * steps + s, GDN_QKV // MIX_HALF)),
                  pl.BlockSpec((rows, LANES), lambda b, s: (b * steps + s, 0)),
                  row, row, row, sq, pl.BlockSpec((CHUNK, 3 * CHUNK), const)],
        out_specs=pl.BlockSpec((rows, MIX_HALF), lambda b, s: (b * steps + s, 0)),
        out_shape=jax.ShapeDtypeStruct((batch * seq, MIX_HALF), BF16),
        scratch_shapes=[pltpu.VMEM((HEADS, HEAD_DIM, HEAD_DIM), F32)],
        compiler_params=pltpu.CompilerParams(dimension_semantics=("arbitrary", "arbitrary"),
                                             vmem_limit_bytes=VMEM_LIMIT),
        name="gdn",
    )(pb, pb, pc, alog_row, dtb_row, onorm_w, code, incl)


def _route_kernel(x_ref, ohg_ref, ogd_ref, wo_ref, nw_ref, wr_ref, ltri_ref,
                  h_ref, hn_ref, ri_ref, rg_ref, cnt_ref, cnt_scr):
    @pl.when(pl.program_id(0) == 0)
    def _():
        cnt_scr[...] = jnp.zeros_like(cnt_scr)

    mix = _dot(ohg_ref[...], wo_ref[0:MIX_HALF, :]) + _dot(ogd_ref[...], wo_ref[MIX_HALF:2 * MIX_HALF, :])
    h = x_ref[...] + mix
    h_ref[...] = h
    hn = h * lax.rsqrt(jnp.mean(h * h, axis=-1, keepdims=True) + NORM_EPS) * nw_ref[...]
    _store_rows_as_tiles(hn_ref, hn)
    hn_hi = hn.astype(BF16)
    hn_lo = (hn - hn_hi.astype(F32)).astype(BF16)
    part = _dot(hn_hi, wr_ref[...])
    logits = (_dot(hn_lo, wr_ref[:, 0:LANES]) + part[:, LANES:]) + part[:, 0:LANES]
    tm = logits.shape[0]
    lane = lax.broadcasted_iota(jnp.int32, (tm, LANES), 1)
    neg = jnp.float32(-jnp.inf)
    big = jnp.int32(LANES)

    def first_max(vals):
        m = jnp.max(vals, axis=-1, keepdims=True)
        return m, jnp.min(jnp.where(vals == m, lane, big), axis=-1, keepdims=True)

    gl = jnp.where(lane < N_GROUPS, logits, neg)
    gmax, gidx = first_max(gl)
    p_group = 1.0 / jnp.sum(jnp.exp(gl - gmax), axis=-1, keepdims=True)
    lo = N_GROUPS + EXPERTS_PER_GROUP * gidx
    el = jnp.where((lane >= lo) & (lane < lo + EXPERTS_PER_GROUP), logits, neg)
    m1, i1 = first_max(el)
    m2, i2 = first_max(jnp.where(lane == i1, neg, el))
    r = jnp.exp(m2 - m1)
    gate1 = p_group / (1.0 + r)
    gate2 = p_group * r / (1.0 + r)
    hot1 = lane == i1
    hot2 = lane == i2
    onehot = jnp.where(hot1 | hot2, 1.0, 0.0)
    before = _dot(ltri_ref[...], onehot.astype(BF16)) + cnt_scr[...]
    rank1 = jnp.sum(jnp.where(hot1, before, 0.0), axis=-1, keepdims=True)
    rank2 = jnp.sum(jnp.where(hot2, before, 0.0), axis=-1, keepdims=True)
    cnt = cnt_scr[...] + jnp.sum(onehot, axis=0, keepdims=True)
    cnt_scr[...] = cnt
    cnt_ref[...] = cnt
    ri = jnp.where(lane == 0, i1 - N_GROUPS,
                   jnp.where(lane == 1, i2 - N_GROUPS,
                             jnp.where(lane == 2, rank1.astype(jnp.int32),
                                       jnp.where(lane == 3, rank2.astype(jnp.int32), 0))))
    ri_ref[...] = ri
    rg_ref[...] = jnp.where(lane == 0, gate1, jnp.where(lane == 1, gate2, 0.0))


def _route(x2, ohg, ogd, wo, norm_w, wr, tm):
    t = x2.shape[0]
    ltri = jnp.asarray(np.tril(np.ones((tm, tm), np.float32), -1), BF16)
    const = lambda i: (0, 0)
    tile = lambda n: pl.BlockSpec((tm, n), lambda i: (i, 0))
    return pl.pallas_call(
        _route_kernel,
        grid=(t // tm,),
        in_specs=[tile(D_MODEL), tile(MIX_HALF), tile(MIX_HALF),
                  pl.BlockSpec((D_MODEL, D_MODEL), const),
                  pl.BlockSpec((1, D_MODEL), const),
                  pl.BlockSpec((D_MODEL, 2 * LANES), const),
                  pl.BlockSpec((tm, tm), const)],
        out_specs=[tile(D_MODEL), pl.BlockSpec((tm * PIECES, LANES), lambda i: (i, 0)), tile(LANES), tile(LANES),
                   pl.BlockSpec((1, LANES), const)],
        out_shape=[jax.ShapeDtypeStruct((t, D_MODEL), F32),
                   jax.ShapeDtypeStruct((t * PIECES, LANES), U32),
                   jax.ShapeDtypeStruct((t, LANES), jnp.int32),
                   jax.ShapeDtypeStruct((t, LANES), F32),
                   jax.ShapeDtypeStruct((1, LANES), F32)],
        scratch_shapes=[pltpu.VMEM((1, LANES), F32)],
        compiler_params=pltpu.CompilerParams(dimension_semantics=("arbitrary",), vmem_limit_bytes=VMEM_LIMIT),
        name="route",
    )(x2, ohg, ogd, wo, norm_w, wr, ltri)


ISSUE_UNROLL = 8


HALF = D_MODEL // 2
PIECES = HALF // LANES
U32 = jnp.uint32
HIGH_HALF = np.uint32(0xFFFF0000)


def _store_rows_as_tiles(ref, x):
    n = x.shape[0]
    bits = lambda a: lax.bitcast_convert_type(a.astype(BF16).astype(F32), U32)
    words = (bits(x[:, 0:HALF]) >> 16) | (bits(x[:, HALF:]) & HIGH_HALF)
    for s in range(PIECES):
        ref[pl.ds(s, n, stride=PIECES), :] = words[:, s * LANES:(s + 1) * LANES]


def _load_rows_from_tiles(ref, n):
    words = jnp.concatenate([ref[pl.ds(s, n, stride=PIECES), :] for s in range(PIECES)], axis=1)
    return (lax.bitcast_convert_type(words << 16, F32), lax.bitcast_convert_type(words & HIGH_HALF, F32))


def _row_copy(src_ref, src_row, dst_ref, dst_row, sem):
    src = src_ref.at[pl.ds(pl.multiple_of(src_row * PIECES, PIECES), PIECES)]
    dst = dst_ref.at[pl.ds(pl.multiple_of(dst_row * PIECES, PIECES), PIECES)]
    return pltpu.make_async_copy(src, dst, sem)


def _dispatch_kernel(starts_ref, idx_ref, hn_ref, xs_ref, sem, *, tm):
    def issue(t, carry):
        for k in range(TOP_K):
            dest = starts_ref[idx_ref[0, 0, 4 * t + k]] + idx_ref[0, 0, 4 * t + 2 + k]
            _row_copy(hn_ref, t, xs_ref, dest, sem).start(priority=k % 2)
        return carry

    lax.fori_loop(0, tm, issue, 0, unroll=ISSUE_UNROLL)
    for k in range(TOP_K):
        pltpu.make_async_copy(hn_ref, xs_ref.at[pl.ds(0, tm * PIECES)], sem).wait()


def _dispatch(starts, idx3, hn, tm):
    t = hn.shape[0] // PIECES
    return pl.pallas_call(
        functools.partial(_dispatch_kernel, tm=tm),
        grid_spec=pltpu.PrefetchScalarGridSpec(
            num_scalar_prefetch=1,
            grid=(t // tm,),
            in_specs=[pl.BlockSpec((1, 1, 4 * tm), lambda i, s: (i, 0, 0), memory_space=pltpu.SMEM),
                      pl.BlockSpec((tm * PIECES, LANES), lambda i, s: (i, 0))],
            out_specs=pl.BlockSpec(memory_space=pl.ANY),
            scratch_shapes=[pltpu.SemaphoreType.DMA(())],
        ),
        out_shape=jax.ShapeDtypeStruct((t * TOP_K * PIECES, LANES), U32),
        compiler_params=pltpu.CompilerParams(dimension_semantics=("arbitrary",), vmem_limit_bytes=VMEM_LIMIT),
        name="dispatch",
    )(starts, idx3, hn)


EXPERT_SPLIT = 1


def _expert_kernel(blk_ref, exp_ref, lo_ref, hi_ref, xs_ref, wg_ref, wu_ref, wd_ref, ys_ref, wgu_b, wd_b, acc):
    p = pl.program_id(0)

    @pl.when(jnp.logical_or(p == 0, exp_ref[p] != exp_ref[jnp.maximum(p - 1, 0)]))
    def _():
        wgu_b[:, 0:EXPERT_FF] = wg_ref[0].astype(BF16)
        wgu_b[:, EXPERT_FF:2 * EXPERT_FF] = wu_ref[0].astype(BF16)
        wd_b[...] = wd_ref[0].astype(BF16)

    @pl.when(p == 0)
    def _():
        acc[...] = jnp.zeros_like(acc)

    first = jnp.logical_or(p == 0, blk_ref[p] != blk_ref[jnp.maximum(p - 1, 0)])
    sub = SLOT_BLOCK // EXPERT_SPLIT
    for r in range(EXPERT_SPLIT):
        x_lo, x_hi = _load_rows_from_tiles(xs_ref.at[pl.ds(r * sub * PIECES, sub * PIECES)], sub)
        ab = _dot(x_lo.astype(BF16), wgu_b[0:HALF, :]) + _dot(x_hi.astype(BF16), wgu_b[HALF:D_MODEL, :])
        hb = _silu(ab[:, 0:EXPERT_FF]) * ab[:, EXPERT_FF:2 * EXPERT_FF]
        y = _dot(hb.astype(BF16), wd_b[...])
        slot = blk_ref[p] * SLOT_BLOCK + r * sub + lax.broadcasted_iota(jnp.int32, (sub, 1), 0)
        y = jnp.where((slot >= lo_ref[p]) & (slot < hi_ref[p]), y, 0.0)
        total = y + jnp.where(first, 0.0, acc[pl.ds(r * sub, sub), :])
        acc[pl.ds(r * sub, sub), :] = total
        _store_rows_as_tiles(ys_ref.at[pl.ds(r * sub * PIECES, sub * PIECES)], total)


def _experts(blk, exp, lo, hi, xs, wg, wu, wd):
    n_pairs = blk.shape[0]
    return pl.pallas_call(
        _expert_kernel,
        grid_spec=pltpu.PrefetchScalarGridSpec(
            num_scalar_prefetch=4,
            grid=(n_pairs,),
            in_specs=[pl.BlockSpec((SLOT_BLOCK * PIECES, LANES), lambda p, blk, exp, lo, hi: (blk[p], 0)),
                      pl.BlockSpec((1, D_MODEL, EXPERT_FF), lambda p, blk, exp, lo, hi: (exp[p], 0, 0)),
                      pl.BlockSpec((1, D_MODEL, EXPERT_FF), lambda p, blk, exp, lo, hi: (exp[p], 0, 0)),
                      pl.BlockSpec((1, EXPERT_FF, D_MODEL), lambda p, blk, exp, lo, hi: (exp[p], 0, 0))],
            out_specs=pl.BlockSpec((SLOT_BLOCK * PIECES, LANES), lambda p, blk, exp, lo, hi: (blk[p], 0)),
            scratch_shapes=[pltpu.VMEM((D_MODEL, 2 * EXPERT_FF), BF16), pltpu.VMEM((EXPERT_FF, D_MODEL), BF16),
                            pltpu.VMEM((SLOT_BLOCK, D_MODEL), F32)],
        ),
        out_shape=jax.ShapeDtypeStruct(xs.shape, U32),
        compiler_params=pltpu.CompilerParams(dimension_semantics=("arbitrary",), vmem_limit_bytes=VMEM_LIMIT),
        name="experts",
    )(blk, exp, lo, hi, xs, wg, wu, wd)


def _combine_kernel(starts_ref, idx_ref, h_ref, rg_ref, fw_ref, ys_ref, o_ref, buf, sem, *, tm):
    def issue(t, carry):
        for k in range(TOP_K):
            src = starts_ref[idx_ref[0, 0, 4 * t + k]] + idx_ref[0, 0, 4 * t + 2 + k]
            _row_copy(ys_ref, src, buf.at[k], t, sem).start(priority=k % 2)
        return carry

    lax.fori_loop(0, tm, issue, 0, unroll=ISSUE_UNROLL)
    for k in range(TOP_K):
        pltpu.make_async_copy(ys_ref.at[pl.ds(0, tm * PIECES)], buf.at[k], sem).wait()
    rg = rg_ref[...]
    y = [jnp.concatenate(_load_rows_from_tiles(buf.at[k], tm), axis=1) for k in range(TOP_K)]
    h = h_ref[...] + rg[:, 0:1] * y[0] + rg[:, 1:2] * y[1]
    o_ref[...] = h * lax.rsqrt(jnp.mean(h * h, axis=-1, keepdims=True) + NORM_EPS) * fw_ref[...]


def _combine(starts, idx3, h, rg, final_w, ys, tm):
    t = h.shape[0]
    return pl.pallas_call(
        functools.partial(_combine_kernel, tm=tm),
        grid_spec=pltpu.PrefetchScalarGridSpec(
            num_scalar_prefetch=1,
            grid=(t // tm,),
            in_specs=[pl.BlockSpec((1, 1, 4 * tm), lambda i, s: (i, 0, 0), memory_space=pltpu.SMEM),
                      pl.BlockSpec((tm, D_MODEL), lambda i, s: (i, 0)),
                      pl.BlockSpec((tm, LANES), lambda i, s: (i, 0)),
                      pl.BlockSpec((1, D_MODEL), lambda i, s: (0, 0)),
                      pl.BlockSpec(memory_space=pl.ANY)],
            out_specs=pl.BlockSpec((tm, D_MODEL), lambda i, s: (i, 0)),
            scratch_shapes=[pltpu.VMEM((TOP_K, tm * PIECES, LANES), U32), pltpu.SemaphoreType.DMA(())],
        ),
        out_shape=jax.ShapeDtypeStruct((t, D_MODEL), F32),
        compiler_params=pltpu.CompilerParams(dimension_semantics=("arbitrary",), vmem_limit_bytes=VMEM_LIMIT),
        name="combine",
    )(starts, idx3, h, rg, final_w, ys)


def _pair_schedule(counts, n_slots):
    n_blocks = n_slots // SLOT_BLOCK
    n_pairs = n_blocks + N_EXPERTS - 1
    ends = jnp.cumsum(counts)
    starts = ends - counts
    first = starts // SLOT_BLOCK
    last = jnp.maximum(ends - 1, starts) // SLOT_BLOCK
    per_expert = jnp.where(counts > 0, last - first + 1, 0)
    cum = jnp.cumsum(per_expert)
    p = jnp.arange(n_pairs, dtype=jnp.int32)
    e = jnp.minimum(jnp.sum(cum[None, :] <= p[:, None], axis=1), N_EXPERTS - 1).astype(jnp.int32)
    valid = p < cum[-1]
    onehot = e[:, None] == jnp.arange(N_EXPERTS, dtype=jnp.int32)[None, :]
    pick = lambda table: jnp.sum(jnp.where(onehot, table[None, :], 0), axis=1)
    blk = jnp.where(valid, pick(first) + p - pick(cum - per_expert), n_blocks - 1).astype(jnp.int32)
    lo = jnp.where(valid, pick(starts), 0).astype(jnp.int32)
    hi = jnp.where(valid, pick(ends), 0).astype(jnp.int32)
    return starts.astype(jnp.int32), blk, e, lo, hi


def _layer(x, norm_mix_w, w_in, lb_logits, hgrn_onorm_w, gdn_conv_w, gdn_a_log, gdn_dt_bias, gdn_onorm_w, w_out,
           norm_moe_w, router_group_w, router_expert_w, w_gate, w_up, w_down, final_w, *, tm, n_chunks):
    batch, seq, _ = x.shape
    t = batch * seq
    x2 = x.reshape(t, D_MODEL)
    n_hg = 4 * MIX_HALF
    n_gd = GDN_QKV + MIX_HALF
    wb16 = w_in.astype(BF16)
    wa = wb16[:, 0:n_hg]
    wb = wb16[:, n_hg:n_hg + n_gd]
    wc = jnp.pad(wb16[:, n_hg + n_gd:], ((0, 0), (0, LANES - 2 * HEADS)))
    pa, pb, pc = _inproj(x2, norm_mix_w.reshape(1, D_MODEL), wa, wb, wc, gdn_conv_w, 2 * tm, seq // (2 * tm))

    o_hg = _hgrn(pa, lb_logits, hgrn_onorm_w.reshape(1, HEAD_DIM), batch, seq, n_chunks)
    alog_row = jnp.pad(gdn_a_log.reshape(1, HEADS), ((0, 0), (0, LANES - HEADS)))
    dtb_row = jnp.pad(gdn_dt_bias.reshape(1, HEADS), ((0, 0), (0, LANES - HEADS)))
    o_gd = _gdn(pb, pc, alog_row, dtb_row, gdn_onorm_w.reshape(1, HEAD_DIM), batch, seq, n_chunks)

    wr = jnp.pad(jnp.concatenate([router_group_w, router_expert_w], axis=1),
                 ((0, 0), (0, LANES - N_GROUPS - N_EXPERTS)))
    wr_hi = wr.astype(BF16)
    wr = jnp.concatenate([wr_hi, (wr - wr_hi.astype(F32)).astype(BF16)], axis=1)
    h, hn, ri, rg, cnt = _route(x2, o_hg, o_gd, w_out.astype(BF16), norm_moe_w.reshape(1, D_MODEL), wr, 2 * tm)

    counts = cnt[0, N_GROUPS:N_GROUPS + N_EXPERTS].astype(jnp.int32)
    starts, blk, exp, lo, hi = _pair_schedule(counts, t * TOP_K)
    tm_moe = 4 * tm
    idx3 = ri[:, 0:4].reshape(t // tm_moe, 1, 4 * tm_moe)
    xs = _dispatch(starts, idx3, hn, tm_moe)
    ys = _experts(blk, exp, lo, hi, xs, w_gate, w_up, w_down)
    out = _combine(starts, idx3, h, rg, final_w.reshape(1, D_MODEL), ys, tm_moe)
    return out.reshape(batch, seq, D_MODEL)


def kernel(x, norm_mix_w, w_in, hgrn_lb_logits, hgrn_onorm_w, gdn_conv_w, gdn_a_log, gdn_dt_bias, gdn_onorm_w, w_out, norm_moe_w, router_group_w, router_expert_w, expert_w_gate, expert_w_up, expert_w_down, final_norm_w):
    return _layer(x, norm_mix_w[0], w_in[0], hgrn_lb_logits, hgrn_onorm_w[0], gdn_conv_w[0], gdn_a_log[0],
                  gdn_dt_bias[0], gdn_onorm_w[0], w_out[0], norm_moe_w[0], router_group_w[0], router_expert_w[0],
                  expert_w_gate[0], expert_w_up[0], expert_w_down[0], final_norm_w, tm=256, n_chunks=8)
```

```python
import functools

import numpy as np
import jax
import jax.numpy as jnp
from jax import lax
from jax.experimental import pallas as pl
from jax.experimental.pallas import tpu as pltpu
from jax.experimental.pallas import tpu_sc as plsc

F32 = jnp.float32
BF16 = jnp.bfloat16
HI = lax.Precision.HIGHEST

D_MODEL = 1024
HEADS = 4
HEAD_DIM = 128
MIX_HALF = HEADS * HEAD_DIM
CHUNK = 64
GROUP = 2
N_STACKS = HEADS // GROUP
STACK = GROUP * CHUNK
CONV_K = 4
N_GROUPS = 8
EXPERTS_PER_GROUP = 8
N_EXPERTS = N_GROUPS * EXPERTS_PER_GROUP
TOP_K = 2
EXPERT_FF = 256
NORM_EPS = 1e-6
LANES = 128
BF16_ROWS = 16
SLOT_BLOCK = 512
VMEM_LIMIT = 56 * 1024 * 1024


def _sigmoid(x):
    return 0.5 * jnp.tanh(0.5 * x) + 0.5


def _silu(x):
    return x * _sigmoid(x)


def _dot(a, b, precision=None):
    return jnp.dot(a, b, preferred_element_type=F32, precision=precision)


def _dot_nt(a, b, precision=None):
    return lax.dot_general(a, b, (((1,), (1,)), ((), ())), preferred_element_type=F32, precision=precision)


def _dot_tn(a, b, precision=None):
    return lax.dot_general(a, b, (((0,), (0,)), ((), ())), preferred_element_type=F32, precision=precision)


def _bdot(a, b):
    return _dot(a.astype(BF16), b.astype(BF16))


def _masked_sum(mask3, x):
    hi = x.astype(BF16)
    r1 = x - hi.astype(F32)
    mid = r1.astype(BF16)
    lo = (r1 - mid.astype(F32)).astype(BF16)
    return _dot(mask3, jnp.concatenate([hi, mid, lo], axis=0))


def _triple(mask):
    return np.concatenate([mask, mask, mask], axis=1)


def _stack_heads(a, p):
    return jnp.concatenate([a[:, h * HEAD_DIM:(h + 1) * HEAD_DIM] for h in range(p * GROUP, (p + 1) * GROUP)],
                           axis=0)


def _each(f, *lists):
    return [f(*args) for args in zip(*lists)]


INPROJ_GROUP = 256


def _l2norm_heads(a):
    return jnp.concatenate(
        [a[:, h * HEAD_DIM:(h + 1) * HEAD_DIM]
         * lax.rsqrt(jnp.sum(jnp.square(a[:, h * HEAD_DIM:(h + 1) * HEAD_DIM]), axis=-1, keepdims=True) + 1e-6)
         for h in range(a.shape[1] // HEAD_DIM)], axis=1)


def _inproj_kernel(x_ref, nw_ref, wa_ref, wb_ref, wc_ref, cw_ref, oa_ref, ob_ref, oc_ref, ubuf, *, tiles_per_seq):
    tm = x_ref.shape[0]

    @pl.when(lax.rem(pl.program_id(0), tiles_per_seq) == 0)
    def _():
        ubuf[0:HIST, :] = jnp.zeros((HIST, GDN_QKV), F32)

    x = x_ref[...]
    ms = jnp.mean(x * x, axis=-1, keepdims=True)
    hn = (x * lax.rsqrt(ms + NORM_EPS) * nw_ref[...]).astype(BF16)
    cw = cw_ref[...]
    width = INPROJ_GROUP
    group = lambda g: slice(g * width, (g + 1) * width)

    def project(w_ref, o_ref, g, row0=0):
        o_ref[row0:row0 + tm, group(g)] = _dot(hn, w_ref[:, group(g)])

    def conv_group(g):
        cols = group(g)
        conv = cw[CONV_K - 1:CONV_K, cols] * ubuf[HIST:HIST + tm, cols]
        for j in range(1, CONV_K):
            conv = conv + cw[CONV_K - 1 - j:CONV_K - j, cols] * ubuf[HIST - j:HIST - j + tm, cols]
        ubuf[0:HIST, cols] = ubuf[tm:tm + HIST, cols]
        act = _silu(conv)
        if g * width < MIX_HALF:
            act = _l2norm_heads(act) * (HEAD_DIM ** -0.5)
        elif g * width < 2 * MIX_HALF:
            act = _l2norm_heads(act)
        ob_ref[:, cols] = act

    n_conv = GDN_QKV // width
    n_a = wa_ref.shape[1] // width
    others = [functools.partial(project, wa_ref, oa_ref, g) for g in range(n_a)]
    others += [functools.partial(project, wb_ref, ob_ref, g) for g in range(n_conv, wb_ref.shape[1] // width)]
    project(wb_ref, ubuf, 0, HIST)
    for g in range(n_conv):
        if g + 1 < n_conv:
            project(wb_ref, ubuf, g + 1, HIST)
        conv_group(g)
        others.pop(0)()
    for task in others:
        task()
    oc_ref[...] = _dot(hn, wc_ref[...])


def _inproj(x2, norm_w, wa, wb, wc, conv_w, tm, tiles_per_seq):
    t = x2.shape[0]
    na, nb, nc = wa.shape[1], wb.shape[1], wc.shape[1]
    const = lambda i: (0, 0)
    return pl.pallas_call(
        functools.partial(_inproj_kernel, tiles_per_seq=tiles_per_seq),
        grid=(t // tm,),
        in_specs=[
            pl.BlockSpec((tm, D_MODEL), lambda i: (i, 0)),
            pl.BlockSpec((1, D_MODEL), const),
            pl.BlockSpec((D_MODEL, na), const),
            pl.BlockSpec((D_MODEL, nb), const),
            pl.BlockSpec((D_MODEL, nc), const),
            pl.BlockSpec((CONV_K, GDN_QKV), const),
        ],
        out_specs=[
            pl.BlockSpec((tm, na), lambda i: (i, 0)),
            pl.BlockSpec((tm, nb), lambda i: (i, 0)),
            pl.BlockSpec((tm, nc), lambda i: (i, 0)),
        ],
        out_shape=[
            jax.ShapeDtypeStruct((t, na), F32),
            jax.ShapeDtypeStruct((t, nb), F32),
            jax.ShapeDtypeStruct((t, nc), F32),
        ],
        scratch_shapes=[pltpu.VMEM((HIST + tm, GDN_QKV), F32)],
        compiler_params=pltpu.CompilerParams(dimension_semantics=("arbitrary",), vmem_limit_bytes=VMEM_LIMIT),
        name="inproj",
    )(x2, norm_w, wa, wb, wc, conv_w)


HGRN_LEVELS = (32, 16, 8, 4, 2, 1)
DIAG_CODE = len(HGRN_LEVELS)
NONE_CODE = DIAG_CODE + 1


def _hgrn_arg_matrix():
    t = np.arange(CHUNK)[:, None]
    u = np.arange(CHUNK)[None, :]
    mats = [u <= t]
    for b in HGRN_LEVELS:
        odd = (t // b) % 2 == 1
        start = (t // b) * b
        mats.append(np.where(odd, (u > start) & (u <= t), (u > t) & (u <= start + b)))
    return np.concatenate(mats, axis=0).astype(np.float32)


def _hgrn_level_codes():
    idx = np.arange(STACK)
    h, t = idx // CHUNK, idx % CHUNK
    same = h[:, None] == h[None, :]
    tt, ss = t[:, None], t[None, :]
    code = np.full((STACK, STACK), NONE_CODE, np.int32)
    code[same & (tt == ss)] = DIAG_CODE
    for l, b in enumerate(HGRN_LEVELS):
        sib = (tt // (2 * b) == ss // (2 * b)) & ((tt // b) % 2 == 1) & ((ss // b) % 2 == 0)
        code[same & sib] = l
    return code


def _hgrn_kernel(q_ref, f_ref, i_ref, g_ref, lbl_ref, onw_ref, marg_ref, code_ref, o_ref, st_ref, *, n_chunks):
    @pl.when(pl.program_id(1) == 0)
    def _():
        st_ref[...] = jnp.zeros_like(st_ref)

    lbl = lbl_ref[...]
    lmax = jnp.max(lbl, axis=0, keepdims=True)
    lexp = jnp.exp(lbl - lmax)
    lb = lexp[0:1, :] / jnp.sum(lexp, axis=0, keepdims=True)
    onw = onw_ref[...]
    marg = marg_ref[...]
    code = code_ref[...]
    trow = lax.broadcasted_iota(jnp.int32, (STACK, HEAD_DIM), 0) & (CHUNK - 1)

    q_blk = _silu(q_ref[...]) * (HEAD_DIM ** -0.5)
    fg_blk = lb + (1.0 - lb) * _sigmoid(f_ref[...])
    k_blk = 1.0 - fg_blk
    lf_blk = jnp.log(fg_blk)
    v_blk = i_ref[...]

    rows = [slice(c * CHUNK, (c + 1) * CHUNK) for c in range(n_chunks)]
    args = [_masked_sum(marg, lf_blk[r]) for r in rows]
    e_chunk = [jnp.exp(a) for a in args]
    suf_chunk = [jnp.exp(a[CHUNK - 1:CHUNK] - a[0:CHUNK]) for a in args]
    units = [(c, p) for c in range(n_chunks) for p in range(N_STACKS)]
    qs = [_stack_heads(q_blk[rows[c]], p) for c, p in units]
    ks = [_stack_heads(k_blk[rows[c]], p) for c, p in units]
    vs = [_stack_heads(v_blk[rows[c]], p).astype(BF16) for c, p in units]
    e_part = lambda n: [_stack_heads(e_chunk[c][n * CHUNK:(n + 1) * CHUNK], p) for c, p in units]
    cum_e = e_part(0)
    suf_e = [_stack_heads(suf_chunk[c], p) for c, p in units]
    att = _each(lambda q, k: jnp.where(code == DIAG_CODE, jnp.sum(q * k, axis=-1, keepdims=True), 0.0), qs, ks)

    def level_update(a, x, l, b):
        if b < BF16_ROWS:
            return jnp.where(code == l, _dot_nt(x, x), a)
        blocks = [slice(s0, s0 + b) for s0 in range(0, STACK, b)]
        r = _dot_nt(jnp.concatenate([x[bl] for bl in blocks[1::2]], axis=0), x)
        return jnp.concatenate(
            [a[bl] if n % 2 == 0 else jnp.where(code[bl] == l, r[(n // 2) * b:(n // 2 + 1) * b], a[bl])
             for n, bl in enumerate(blocks)], axis=0)

    for l, b in enumerate(HGRN_LEVELS):
        x = _each(lambda q, k, el: (jnp.where((trow & b) != 0, q, k) * el).astype(BF16), qs, ks, e_part(1 + l))
        att = _each(lambda xi, a: level_update(a, xi, l, b), x, att)
    o_intra = _each(lambda a, v: _dot(a.astype(BF16), v), att, vs)
    qc = _each(lambda q, e: (q * e).astype(BF16), qs, cum_e)
    kd = _each(lambda k, e: (k * e).astype(BF16), ks, suf_e)

    state = [st_ref[h] for h in range(HEADS)]
    for u, (c, p) in enumerate(units):
        for i in range(GROUP):
            h = p * GROUP + i
            hr = slice(i * CHUNK, (i + 1) * CHUNK)
            hc = slice(h * HEAD_DIM, (h + 1) * HEAD_DIM)
            o = _dot_nt(qc[u][hr], state[h].astype(BF16)) + o_intra[u][hr]
            decay = cum_e[u][i * CHUNK + CHUNK - 1:i * CHUNK + CHUNK, :]
            state[h] = decay * state[h] + _dot_tn(vs[u][hr], kd[u][hr])
            y = o * lax.rsqrt(jnp.mean(o * o, axis=-1, keepdims=True) + NORM_EPS) * onw
            o_ref[rows[c], hc] = (y * _silu(g_ref[rows[c], hc])).astype(o_ref.dtype)
    for h in range(HEADS):
        st_ref[h] = state[h]


def _hgrn(pa, lb_logits, onorm_w, batch, seq, n_chunks):
    rows = n_chunks * CHUNK
    steps = seq // rows
    col = lambda j: pl.BlockSpec((rows, MIX_HALF), lambda b, s, j=j: (b * steps + s, j))
    const = lambda b, s: (0, 0)
    marg = jnp.asarray(_triple(_hgrn_arg_matrix()), BF16)
    code = jnp.asarray(_hgrn_level_codes())
    return pl.pallas_call(
        functools.partial(_hgrn_kernel, n_chunks=n_chunks),
        grid=(batch, steps),
        in_specs=[col(0), col(1), col(2), col(3),
                  pl.BlockSpec(lb_logits.shape, const),
                  pl.BlockSpec((1, HEAD_DIM), const),
                  pl.BlockSpec(marg.shape, const),
                  pl.BlockSpec(code.shape, const)],
        out_specs=pl.BlockSpec((rows, MIX_HALF), lambda b, s: (b * steps + s, 0)),
        out_shape=jax.ShapeDtypeStruct((batch * seq, MIX_HALF), BF16),
        scratch_shapes=[pltpu.VMEM((HEADS, HEAD_DIM, HEAD_DIM), F32)],
        compiler_params=pltpu.CompilerParams(dimension_semantics=("arbitrary", "arbitrary"),
                                             vmem_limit_bytes=VMEM_LIMIT),
        name="hgrn2",
    )(pa, pa, pa, pa, lb_logits, onorm_w, marg, code)


GDN_QKV = 3 * MIX_HALF
HIST = 8
C_NONE, C_DIAG, C_B16, C_B32, C_B64 = 0, 1, 2, 3, 4
MASKED_EXPONENT = -1e30


def _gdn_codes():
    idx = np.arange(STACK)
    h, t = idx // CHUNK, idx % CHUNK
    same = h[:, None] == h[None, :]
    tt, ss = t[:, None], t[None, :]
    code = np.full((STACK, STACK), C_NONE, np.int32)
    low = same & (ss < tt)
    code[low] = C_B64
    code[low & (tt // 32 == ss // 32)] = C_B32
    code[low & (tt // 16 == ss // 16)] = C_B16
    code[same & (tt == ss)] = C_DIAG
    incl = np.tril(np.ones((CHUNK, CHUNK), np.float32))
    return code, incl


def _gdn_kernel(qkv_ref, z_ref, ab_ref, alog_ref, dtb_ref, onw_ref, code_ref, incl_ref,
                o_ref, st_ref, *, n_chunks):
    @pl.when(pl.program_id(1) == 0)
    def _():
        st_ref[...] = jnp.zeros_like(st_ref)

    onw = onw_ref[...]
    code = code_ref[...]
    incl_m = incl_ref[...]
    incl = code >= C_DIAG
    eye = (code == C_DIAG).astype(F32)

    rows = [slice(c * CHUNK, (c + 1) * CHUNK) for c in range(n_chunks)]
    units = [(c, p) for c in range(n_chunks) for p in range(N_STACKS)]

    qs = [_stack_heads(qkv_ref[rows[c], 0:MIX_HALF], p) for c, p in units]
    ks = [_stack_heads(qkv_ref[rows[c], MIX_HALF:2 * MIX_HALF], p) for c, p in units]
    vs = [_stack_heads(qkv_ref[rows[c], 2 * MIX_HALF:3 * MIX_HALF], p) for c, p in units]
    ab_blk = ab_ref[...]
    xa = ab_blk + dtb_ref[...]
    softplus = jnp.maximum(xa, 0.0) + jnp.log(1.0 + jnp.exp(-jnp.abs(xa)))
    g_all = -jnp.exp(alog_ref[...]) * softplus
    beta_all = _sigmoid(ab_blk)

    def head_cols(a, c, p, first_lane):
        return jnp.concatenate([a[c][:, first_lane + h:first_lane + h + 1]
                                for h in range(p * GROUP, (p + 1) * GROUP)], axis=0)

    g_cum = [_masked_sum(incl_m, g_all[r]) for r in rows]
    beta_chunk = [beta_all[r] for r in rows]
    beta_st = [head_cols(beta_chunk, c, p, HEADS) for c, p in units]
    gc = [jnp.broadcast_to(head_cols(g_cum, c, p, 0), (STACK, HEAD_DIM)) for c, p in units]
    dec = [jnp.exp(jnp.where(incl, g - jnp.transpose(g)[0:1, :], MASKED_EXPONENT)) for g in gc]
    kb = [k.astype(BF16) for k in ks]
    kq = _each(lambda k, q: _dot_nt(jnp.concatenate([k, q.astype(BF16)], axis=0), k), kb, qs)
    kk = [a[0:STACK] for a in kq]
    qk = [a[STACK:] for a in kq]
    am = _each(lambda b, k2, d: b * k2 * d, beta_st, kk, dec)
    a16 = [jnp.where(code == C_B16, a, 0.0) for a in am]
    n32 = [jnp.where(code == C_B32, a, 0.0) for a in am]
    n64 = [jnp.where(code == C_B64, a, 0.0) for a in am]
    apow = _each(_bdot, a16, a16)
    pinv = [eye - a for a in a16]
    for _ in range(2):
        both = _each(lambda ai, pi: _bdot(jnp.concatenate([ai, pi], axis=0), ai), apow, pinv)
        pinv = _each(lambda pi, b: pi + b[STACK:], pinv, both)
        apow = [b[0:STACK] for b in both]
    pinv = _each(lambda pi, ai: pi + _bdot(pi, ai), pinv, apow)
    for nlev in (n32, n64):
        t = _each(_bdot, nlev, pinv)
        pinv = _each(lambda pi, ti: pi - _bdot(pi, ti), pinv, t)
    egc = [jnp.exp(g) for g in gc]
    rhs = _each(lambda k, v, b, e: jnp.concatenate([k * (b * e), v * b], axis=1), ks, vs, beta_st, egc)
    wu = _each(_bdot, pinv, rhs)
    w_c = [a[:, 0:HEAD_DIM].astype(BF16) for a in wu]
    u_c = [a[:, HEAD_DIM:] for a in wu]
    qkm = _each(lambda a, d: (a * d).astype(BF16), qk, dec)
    qg = _each(lambda q, e: (q * e).astype(BF16), qs, egc)
    glast = [[g[i * CHUNK + CHUNK - 1:(i + 1) * CHUNK, :] for i in range(GROUP)] for g in gc]
    kdec = [[(k[i * CHUNK:(i + 1) * CHUNK] * jnp.exp(gl[i] - g[i * CHUNK:(i + 1) * CHUNK])).astype(BF16)
             for i in range(GROUP)] for k, g, gl in zip(ks, gc, glast)]
    gend = [[jnp.exp(gi) for gi in gl] for gl in glast]

    state = [st_ref[h] for h in range(HEADS)]
    for u, (c, p) in enumerate(units):
        heads = range(p * GROUP, (p + 1) * GROUP)
        hrs = [slice(i * CHUNK, (i + 1) * CHUNK) for i in range(GROUP)]
        ws = [_dot_nt(jnp.concatenate([w_c[u][hr], qg[u][hr]], axis=0), state[h].astype(BF16))
              for hr, h in zip(hrs, heads)]
        v_new = jnp.concatenate([u_c[u][hr] - a[0:CHUNK] for hr, a in zip(hrs, ws)], axis=0).astype(BF16)
        o_st = jnp.concatenate([a[CHUNK:] for a in ws], axis=0) + _dot(qkm[u], v_new)
        for i, h in enumerate(heads):
            hc = slice(h * HEAD_DIM, (h + 1) * HEAD_DIM)
            state[h] = gend[u][i] * state[h] + _dot_tn(v_new[hrs[i]], kdec[u][i])
            o = o_st[hrs[i]]
            y = o * lax.rsqrt(jnp.mean(o * o, axis=-1, keepdims=True) + NORM_EPS) * onw
            o_ref[rows[c], hc] = (y * _silu(z_ref[rows[c], hc])).astype(o_ref.dtype)
    for h in range(HEADS):
        st_ref[h] = state[h]


def _gdn(pb, pc, alog_row, dtb_row, onorm_w, batch, seq, n_chunks):
    rows = n_chunks * CHUNK
    steps = seq // rows
    const = lambda b, s: (0, 0)
    code, incl = _gdn_codes()
    code, incl = jnp.asarray(code), jnp.asarray(_triple(incl), BF16)
    sq = pl.BlockSpec((STACK, STACK), const)
    row = pl.BlockSpec((1, LANES), const)
    return pl.pallas_call(
        functools.partial(_gdn_kernel, n_chunks=n_chunks),
        grid=(batch, steps),
        in_specs=[pl.BlockSpec((rows, GDN_QKV), lambda b, s: (b * steps + s, 0)),
                  pl.BlockSpec((rows, MIX_HALF), lambda b, s: (b * steps + s, GDN_QKV // MIX_HALF)),
                  pl.BlockSpec((rows, LANES), lambda b, s: (b * steps + s, 0)),
                  row, row, row, sq, pl.BlockSpec((CHUNK, 3 * CHUNK), const)],
        out_specs=pl.BlockSpec((rows, MIX_HALF), lambda b, s: (b * steps + s, 0)),
        out_shape=jax.ShapeDtypeStruct((batch * seq, MIX_HALF), BF16),
        scratch_shapes=[pltpu.VMEM((HEADS, HEAD_DIM, HEAD_DIM), F32)],
        compiler_params=pltpu.CompilerParams(dimension_semantics=("arbitrary", "arbitrary"),
                                             vmem_limit_bytes=VMEM_LIMIT),
        name="gdn",
    )(pb, pb, pc, alog_row, dtb_row, onorm_w, code, incl)


def _route_kernel(x_ref, ohg_ref, ogd_ref, wo_ref, nw_ref, wr_ref, ltri_ref,
                  h_ref, hn_ref, ri_ref, rg_ref, cnt_ref, cnt_scr):
    @pl.when(pl.program_id(0) == 0)
    def _():
        cnt_scr[...] = jnp.zeros_like(cnt_scr)

    mix = _dot(ohg_ref[...], wo_ref[0:MIX_HALF, :]) + _dot(ogd_ref[...], wo_ref[MIX_HALF:2 * MIX_HALF, :])
    h = x_ref[...] + mix
    h_ref[...] = h
    hn = h * lax.rsqrt(jnp.mean(h * h, axis=-1, keepdims=True) + NORM_EPS) * nw_ref[...]
    _store_rows_as_tiles(hn_ref, hn)
    hn_hi = hn.astype(BF16)
    hn_lo = (hn - hn_hi.astype(F32)).astype(BF16)
    part = _dot(hn_hi, wr_ref[...])
    logits = (_dot(hn_lo, wr_ref[:, 0:LANES]) + part[:, LANES:]) + part[:, 0:LANES]
    tm = logits.shape[0]
    lane = lax.broadcasted_iota(jnp.int32, (tm, LANES), 1)
    neg = jnp.float32(-jnp.inf)
    big = jnp.int32(LANES)

    def first_max(vals):
        m = jnp.max(vals, axis=-1, keepdims=True)
        return m, jnp.min(jnp.where(vals == m, lane, big), axis=-1, keepdims=True)

    gl = jnp.where(lane < N_GROUPS, logits, neg)
    gmax, gidx = first_max(gl)
    p_group = 1.0 / jnp.sum(jnp.exp(gl - gmax), axis=-1, keepdims=True)
    lo = N_GROUPS + EXPERTS_PER_GROUP * gidx
    el = jnp.where((lane >= lo) & (lane < lo + EXPERTS_PER_GROUP), logits, neg)
    m1, i1 = first_max(el)
    m2, i2 = first_max(jnp.where(lane == i1, neg, el))
    r = jnp.exp(m2 - m1)
    gate1 = p_group / (1.0 + r)
    gate2 = p_group * r / (1.0 + r)
    hot1 = lane == i1
    hot2 = lane == i2
    onehot = jnp.where(hot1 | hot2, 1.0, 0.0)
    before = _dot(ltri_ref[...], onehot.astype(BF16)) + cnt_scr[...]
    rank1 = jnp.sum(jnp.where(hot1, before, 0.0), axis=-1, keepdims=True)
    rank2 = jnp.sum(jnp.where(hot2, before, 0.0), axis=-1, keepdims=True)
    cnt = cnt_scr[...] + jnp.sum(onehot, axis=0, keepdims=True)
    cnt_scr[...] = cnt
    cnt_ref[...] = cnt
    ri = jnp.where(lane == 0, i1 - N_GROUPS,
                   jnp.where(lane == 1, i2 - N_GROUPS,
                             jnp.where(lane == 2, rank1.astype(jnp.int32),
                                       jnp.where(lane == 3, rank2.astype(jnp.int32), 0))))
    ri_ref[...] = ri
    rg_ref[...] = jnp.where(lane == 0, gate1, jnp.where(lane == 1, gate2, 0.0))


def _route(x2, ohg, ogd, wo, norm_w, wr, tm):
    t = x2.shape[0]
    ltri = jnp.asarray(np.tril(np.ones((tm, tm), np.float32), -1), BF16)
    const = lambda i: (0, 0)
    tile = lambda n: pl.BlockSpec((tm, n), lambda i: (i, 0))
    return pl.pallas_call(
        _route_kernel,
        grid=(t // tm,),
        in_specs=[tile(D_MODEL), tile(MIX_HALF), tile(MIX_HALF),
                  pl.BlockSpec((D_MODEL, D_MODEL), const),
                  pl.BlockSpec((1, D_MODEL), const),
                  pl.BlockSpec((D_MODEL, 2 * LANES), const),
                  pl.BlockSpec((tm, tm), const)],
        out_specs=[tile(D_MODEL), pl.BlockSpec((tm * PIECES, LANES), lambda i: (i, 0)), tile(LANES), tile(LANES),
                   pl.BlockSpec((1, LANES), const)],
        out_shape=[jax.ShapeDtypeStruct((t, D_MODEL), F32),
                   jax.ShapeDtypeStruct((t * PIECES, LANES), U32),
                   jax.ShapeDtypeStruct((t, LANES), jnp.int32),
                   jax.ShapeDtypeStruct((t, LANES), F32),
                   jax.ShapeDtypeStruct((1, LANES), F32)],
        scratch_shapes=[pltpu.VMEM((1, LANES), F32)],
        compiler_params=pltpu.CompilerParams(dimension_semantics=("arbitrary",), vmem_limit_bytes=VMEM_LIMIT),
        name="route",
    )(x2, ohg, ogd, wo, norm_w, wr, ltri)


ISSUE_UNROLL = 8


HALF = D_MODEL // 2
PIECES = HALF // LANES
U32 = jnp.uint32
HIGH_HALF = np.uint32(0xFFFF0000)


def _pack_words(x):
    bits = lambda a: lax.bitcast_convert_type(a.astype(BF16).astype(F32), U32)
    return (bits(x[:, 0:HALF]) >> 16) | (bits(x[:, HALF:]) & HIGH_HALF)


def _unpack_words(words):
    return (lax.bitcast_convert_type(words << 16, F32), lax.bitcast_convert_type(words & HIGH_HALF, F32))


def _store_rows_as_tiles(ref, x):
    n = x.shape[0]
    words = _pack_words(x)
    for s in range(PIECES):
        ref[pl.ds(s, n, stride=PIECES), :] = words[:, s * LANES:(s + 1) * LANES]


def _load_rows_from_tiles(ref, n):
    return _unpack_words(jnp.concatenate([ref[pl.ds(s, n, stride=PIECES), :] for s in range(PIECES)], axis=1))


SC_WINDOW = 64


def _sc_gather_rows(table, idx):
    info = plsc.get_sparse_core_info()
    n_workers = info.num_cores * info.num_subcores
    n_rows, width = idx.shape[0], table.shape[1]
    per_worker = n_rows // n_workers
    assert per_worker * n_workers == n_rows and per_worker % SC_WINDOW == 0

    @functools.partial(
        pl.kernel,
        mesh=plsc.VectorSubcoreMesh(core_axis_name="c", subcore_axis_name="s"),
        out_type=jax.ShapeDtypeStruct((n_rows, width), table.dtype),
        scratch_types=[pltpu.VMEM((SC_WINDOW,), jnp.int32), pltpu.VMEM((SC_WINDOW, width), table.dtype),
                       pltpu.SemaphoreType.DMA],
    )
    def gather(table_hbm, idx_hbm, out_hbm, idx_v, rows_v, sem):
        worker = lax.axis_index("s") * info.num_cores + lax.axis_index("c")

        @pl.loop(0, per_worker // SC_WINDOW)
        def _(j):
            base = pl.multiple_of(worker * per_worker + j * SC_WINDOW, SC_WINDOW)
            pltpu.sync_copy(idx_hbm.at[pl.ds(base, SC_WINDOW)], idx_v)
            pltpu.async_copy(table_hbm.at[idx_v], rows_v, sem).wait()
            pltpu.sync_copy(rows_v, out_hbm.at[pl.ds(base, SC_WINDOW)])

    return gather(table, idx)


def _row_copy(src_ref, src_row, dst_ref, dst_row, sem):
    src = src_ref.at[pl.ds(pl.multiple_of(src_row * PIECES, PIECES), PIECES)]
    dst = dst_ref.at[pl.ds(pl.multiple_of(dst_row * PIECES, PIECES), PIECES)]
    return pltpu.make_async_copy(src, dst, sem)


def _dispatch_kernel(starts_ref, idx_ref, hn_ref, xs_ref, sem, *, tm):
    def issue(t, carry):
        for k in range(TOP_K):
            dest = starts_ref[idx_ref[0, 0, 4 * t + k]] + idx_ref[0, 0, 4 * t + 2 + k]
            _row_copy(hn_ref, t, xs_ref, dest, sem).start(priority=k % 2)
        return carry

    lax.fori_loop(0, tm, issue, 0, unroll=ISSUE_UNROLL)
    for k in range(TOP_K):
        pltpu.make_async_copy(hn_ref, xs_ref.at[pl.ds(0, tm * PIECES)], sem).wait()


def _dispatch(starts, idx3, hn, tm):
    t = hn.shape[0] // PIECES
    return pl.pallas_call(
        functools.partial(_dispatch_kernel, tm=tm),
        grid_spec=pltpu.PrefetchScalarGridSpec(
            num_scalar_prefetch=1,
            grid=(t // tm,),
            in_specs=[pl.BlockSpec((1, 1, 4 * tm), lambda i, s: (i, 0, 0), memory_space=pltpu.SMEM),
                      pl.BlockSpec((tm * PIECES, LANES), lambda i, s: (i, 0))],
            out_specs=pl.BlockSpec(memory_space=pl.ANY),
            scratch_shapes=[pltpu.SemaphoreType.DMA(())],
        ),
        out_shape=jax.ShapeDtypeStruct((t * TOP_K * PIECES, LANES), U32),
        compiler_params=pltpu.CompilerParams(dimension_semantics=("arbitrary",), vmem_limit_bytes=VMEM_LIMIT),
        name="dispatch",
    )(starts, idx3, hn)


EXPERT_SPLIT = 1


def _expert_kernel(blk_ref, exp_ref, lo_ref, hi_ref, xs_ref, wg_ref, wu_ref, wd_ref, ys_ref, wgu_b, wd_b, acc):
    p = pl.program_id(0)

    @pl.when(jnp.logical_or(p == 0, exp_ref[p] != exp_ref[jnp.maximum(p - 1, 0)]))
    def _():
        wgu_b[:, 0:EXPERT_FF] = wg_ref[0].astype(BF16)
        wgu_b[:, EXPERT_FF:2 * EXPERT_FF] = wu_ref[0].astype(BF16)
        wd_b[...] = wd_ref[0].astype(BF16)

    @pl.when(p == 0)
    def _():
        acc[...] = jnp.zeros_like(acc)

    first = jnp.logical_or(p == 0, blk_ref[p] != blk_ref[jnp.maximum(p - 1, 0)])
    sub = SLOT_BLOCK // EXPERT_SPLIT
    for r in range(EXPERT_SPLIT):
        x_lo, x_hi = _load_rows_from_tiles(xs_ref.at[pl.ds(r * sub * PIECES, sub * PIECES)], sub)
        ab = _dot(x_lo.astype(BF16), wgu_b[0:HALF, :]) + _dot(x_hi.astype(BF16), wgu_b[HALF:D_MODEL, :])
        hb = _silu(ab[:, 0:EXPERT_FF]) * ab[:, EXPERT_FF:2 * EXPERT_FF]
        y = _dot(hb.astype(BF16), wd_b[...])
        slot = blk_ref[p] * SLOT_BLOCK + r * sub + lax.broadcasted_iota(jnp.int32, (sub, 1), 0)
        y = jnp.where((slot >= lo_ref[p]) & (slot < hi_ref[p]), y, 0.0)
        total = y + jnp.where(first, 0.0, acc[pl.ds(r * sub, sub), :])
        acc[pl.ds(r * sub, sub), :] = total
        ys_ref[pl.ds(r * sub, sub), :] = _pack_words(total)


def _experts(blk, exp, lo, hi, xs, wg, wu, wd):
    n_pairs = blk.shape[0]
    return pl.pallas_call(
        _expert_kernel,
        grid_spec=pltpu.PrefetchScalarGridSpec(
            num_scalar_prefetch=4,
            grid=(n_pairs,),
            in_specs=[pl.BlockSpec((SLOT_BLOCK * PIECES, LANES), lambda p, blk, exp, lo, hi: (blk[p], 0)),
                      pl.BlockSpec((1, D_MODEL, EXPERT_FF), lambda p, blk, exp, lo, hi: (exp[p], 0, 0)),
                      pl.BlockSpec((1, D_MODEL, EXPERT_FF), lambda p, blk, exp, lo, hi: (exp[p], 0, 0)),
                      pl.BlockSpec((1, EXPERT_FF, D_MODEL), lambda p, blk, exp, lo, hi: (exp[p], 0, 0))],
            out_specs=pl.BlockSpec((SLOT_BLOCK, HALF), lambda p, blk, exp, lo, hi: (blk[p], 0)),
            scratch_shapes=[pltpu.VMEM((D_MODEL, 2 * EXPERT_FF), BF16), pltpu.VMEM((EXPERT_FF, D_MODEL), BF16),
                            pltpu.VMEM((SLOT_BLOCK, D_MODEL), F32)],
        ),
        out_shape=jax.ShapeDtypeStruct((xs.shape[0] // PIECES, HALF), U32),
        compiler_params=pltpu.CompilerParams(dimension_semantics=("arbitrary",), vmem_limit_bytes=VMEM_LIMIT),
        name="experts",
    )(blk, exp, lo, hi, xs, wg, wu, wd)


def _combine_kernel(h_ref, rg_ref, fw_ref, y0_ref, y1_ref, o_ref):
    rg = rg_ref[...]
    y = [jnp.concatenate(_unpack_words(ref[...]), axis=1) for ref in (y0_ref, y1_ref)]
    h = h_ref[...] + rg[:, 0:1] * y[0] + rg[:, 1:2] * y[1]
    o_ref[...] = h * lax.rsqrt(jnp.mean(h * h, axis=-1, keepdims=True) + NORM_EPS) * fw_ref[...]


def _combine(h, rg, final_w, y2, tm):
    t = h.shape[0]
    n_tiles = t // tm
    return pl.pallas_call(
        _combine_kernel,
        grid=(n_tiles,),
        in_specs=[pl.BlockSpec((tm, D_MODEL), lambda i: (i, 0)),
                  pl.BlockSpec((tm, LANES), lambda i: (i, 0)),
                  pl.BlockSpec((1, D_MODEL), lambda i: (0, 0)),
                  pl.BlockSpec((tm, HALF), lambda i: (i, 0)),
                  pl.BlockSpec((tm, HALF), lambda i: (n_tiles + i, 0))],
        out_specs=pl.BlockSpec((tm, D_MODEL), lambda i: (i, 0)),
        out_shape=jax.ShapeDtypeStruct((t, D_MODEL), F32),
        compiler_params=pltpu.CompilerParams(dimension_semantics=("arbitrary",), vmem_limit_bytes=VMEM_LIMIT),
        name="combine",
    )(h, rg, final_w, y2, y2)


def _pair_schedule(counts, n_slots):
    n_blocks = n_slots // SLOT_BLOCK
    n_pairs = n_blocks + N_EXPERTS - 1
    ends = jnp.cumsum(counts)
    starts = ends - counts
    first = starts // SLOT_BLOCK
    last = jnp.maximum(ends - 1, starts) // SLOT_BLOCK
    per_expert = jnp.where(counts > 0, last - first + 1, 0)
    cum = jnp.cumsum(per_expert)
    p = jnp.arange(n_pairs, dtype=jnp.int32)
    e = jnp.minimum(jnp.sum(cum[None, :] <= p[:, None], axis=1), N_EXPERTS - 1).astype(jnp.int32)
    valid = p < cum[-1]
    onehot = e[:, None] == jnp.arange(N_EXPERTS, dtype=jnp.int32)[None, :]
    pick = lambda table: jnp.sum(jnp.where(onehot, table[None, :], 0), axis=1)
    blk = jnp.where(valid, pick(first) + p - pick(cum - per_expert), n_blocks - 1).astype(jnp.int32)
    lo = jnp.where(valid, pick(starts), 0).astype(jnp.int32)
    hi = jnp.where(valid, pick(ends), 0).astype(jnp.int32)
    return starts.astype(jnp.int32), blk, e, lo, hi


def _layer(x, norm_mix_w, w_in, lb_logits, hgrn_onorm_w, gdn_conv_w, gdn_a_log, gdn_dt_bias, gdn_onorm_w, w_out,
           norm_moe_w, router_group_w, router_expert_w, w_gate, w_up, w_down, final_w, *, tm, n_chunks):
    batch, seq, _ = x.shape
    t = batch * seq
    x2 = x.reshape(t, D_MODEL)
    n_hg = 4 * MIX_HALF
    n_gd = GDN_QKV + MIX_HALF
    wb16 = w_in.astype(BF16)
    wa = wb16[:, 0:n_hg]
    wb = wb16[:, n_hg:n_hg + n_gd]
    wc = jnp.pad(wb16[:, n_hg + n_gd:], ((0, 0), (0, LANES - 2 * HEADS)))
    pa, pb, pc = _inproj(x2, norm_mix_w.reshape(1, D_MODEL), wa, wb, wc, gdn_conv_w, 2 * tm, seq // (2 * tm))

    o_hg = _hgrn(pa, lb_logits, hgrn_onorm_w.reshape(1, HEAD_DIM), batch, seq, n_chunks)
    alog_row = jnp.pad(gdn_a_log.reshape(1, HEADS), ((0, 0), (0, LANES - HEADS)))
    dtb_row = jnp.pad(gdn_dt_bias.reshape(1, HEADS), ((0, 0), (0, LANES - HEADS)))
    o_gd = _gdn(pb, pc, alog_row, dtb_row, gdn_onorm_w.reshape(1, HEAD_DIM), batch, seq, n_chunks)

    wr = jnp.pad(jnp.concatenate([router_group_w, router_expert_w], axis=1),
                 ((0, 0), (0, LANES - N_GROUPS - N_EXPERTS)))
    wr_hi = wr.astype(BF16)
    wr = jnp.concatenate([wr_hi, (wr - wr_hi.astype(F32)).astype(BF16)], axis=1)
    h, hn, ri, rg, cnt = _route(x2, o_hg, o_gd, w_out.astype(BF16), norm_moe_w.reshape(1, D_MODEL), wr, 2 * tm)

    counts = cnt[0, N_GROUPS:N_GROUPS + N_EXPERTS].astype(jnp.int32)
    starts, blk, exp, lo, hi = _pair_schedule(counts, t * TOP_K)
    tm_moe = 4 * tm
    idx3 = ri[:, 0:4].reshape(t // tm_moe, 1, 4 * tm_moe)
    xs = _dispatch(starts, idx3, hn, tm_moe)
    ys = _experts(blk, exp, lo, hi, xs, w_gate, w_up, w_down)
    experts_kt = jnp.transpose(ri[:, 0:TOP_K])
    onehot = experts_kt[:, :, None] == jnp.arange(N_EXPERTS, dtype=jnp.int32)
    slot_kt = jnp.sum(jnp.where(onehot, starts, 0), axis=-1) + jnp.transpose(ri[:, TOP_K:2 * TOP_K])
    y2 = _sc_gather_rows(ys, slot_kt.reshape(TOP_K * t))
    out = _combine(h, rg, final_w.reshape(1, D_MODEL), y2, 2 * tm)
    return out.reshape(batch, seq, D_MODEL)


def kernel(x, norm_mix_w, w_in, hgrn_lb_logits, hgrn_onorm_w, gdn_conv_w, gdn_a_log, gdn_dt_bias, gdn_onorm_w, w_out, norm_moe_w, router_group_w, router_expert_w, expert_w_gate, expert_w_up, expert_w_down, final_norm_w):
    return _layer(x, norm_mix_w[0], w_in[0], hgrn_lb_logits, hgrn_onorm_w[0], gdn_conv_w[0], gdn_a_log[0],
                  gdn_dt_bias[0], gdn_onorm_w[0], w_out[0], norm_moe_w[0], router_group_w[0], router_expert_w[0],
                  expert_w_gate[0], expert_w_up[0], expert_w_down[0], final_norm_w, tm=256, n_chunks=8)
```

```python
import functools

import numpy as np
import jax
import jax.numpy as jnp
from jax import lax
from jax.experimental import pallas as pl
from jax.experimental.pallas import tpu as pltpu
from jax.experimental.pallas import tpu_sc as plsc

F32 = jnp.float32
BF16 = jnp.bfloat16
HI = lax.Precision.HIGHEST

D_MODEL = 1024
HEADS = 4
HEAD_DIM = 128
MIX_HALF = HEADS * HEAD_DIM
CHUNK = 64
GROUP = 2
N_STACKS = HEADS // GROUP
STACK = GROUP * CHUNK
CONV_K = 4
N_GROUPS = 8
EXPERTS_PER_GROUP = 8
N_EXPERTS = N_GROUPS * EXPERTS_PER_GROUP
TOP_K = 2
EXPERT_FF = 256
NORM_EPS = 1e-6
LANES = 128
BF16_ROWS = 16
SLOT_BLOCK = 512
VMEM_LIMIT = 56 * 1024 * 1024


def _sigmoid(x):
    return 0.5 * jnp.tanh(0.5 * x) + 0.5


def _silu(x):
    return x * _sigmoid(x)


def _dot(a, b, precision=None):
    return jnp.dot(a, b, preferred_element_type=F32, precision=precision)


def _dot_nt(a, b, precision=None):
    return lax.dot_general(a, b, (((1,), (1,)), ((), ())), preferred_element_type=F32, precision=precision)


def _dot_tn(a, b, precision=None):
    return lax.dot_general(a, b, (((0,), (0,)), ((), ())), preferred_element_type=F32, precision=precision)


def _bdot(a, b):
    return _dot(a.astype(BF16), b.astype(BF16))


def _masked_sum(mask3, x):
    hi = x.astype(BF16)
    r1 = x - hi.astype(F32)
    mid = r1.astype(BF16)
    lo = (r1 - mid.astype(F32)).astype(BF16)
    return _dot(mask3, jnp.concatenate([hi, mid, lo], axis=0))


def _triple(mask):
    return np.concatenate([mask, mask, mask], axis=1)


def _stack_heads(a, p):
    return jnp.concatenate([a[:, h * HEAD_DIM:(h + 1) * HEAD_DIM] for h in range(p * GROUP, (p + 1) * GROUP)],
                           axis=0)


def _each(f, *lists):
    return [f(*args) for args in zip(*lists)]


INPROJ_GROUP = 256


def _l2norm_heads(a):
    return jnp.concatenate(
        [a[:, h * HEAD_DIM:(h + 1) * HEAD_DIM]
         * lax.rsqrt(jnp.sum(jnp.square(a[:, h * HEAD_DIM:(h + 1) * HEAD_DIM]), axis=-1, keepdims=True) + 1e-6)
         for h in range(a.shape[1] // HEAD_DIM)], axis=1)


def _inproj_kernel(x_ref, nw_ref, wa_ref, wb_ref, wc_ref, cw_ref, oa_ref, ob_ref, oc_ref, ubuf, *, tiles_per_seq):
    tm = x_ref.shape[0]

    @pl.when(lax.rem(pl.program_id(0), tiles_per_seq) == 0)
    def _():
        ubuf[0:HIST, :] = jnp.zeros((HIST, GDN_QKV), F32)

    x = x_ref[...]
    ms = jnp.mean(x * x, axis=-1, keepdims=True)
    hn = (x * lax.rsqrt(ms + NORM_EPS) * nw_ref[...]).astype(BF16)
    cw = cw_ref[...]
    width = INPROJ_GROUP
    group = lambda g: slice(g * width, (g + 1) * width)

    def project(w_ref, o_ref, g, row0=0):
        o_ref[row0:row0 + tm, group(g)] = _dot(hn, w_ref[:, group(g)])

    def conv_group(g):
        cols = group(g)
        conv = cw[CONV_K - 1:CONV_K, cols] * ubuf[HIST:HIST + tm, cols]
        for j in range(1, CONV_K):
            conv = conv + cw[CONV_K - 1 - j:CONV_K - j, cols] * ubuf[HIST - j:HIST - j + tm, cols]
        ubuf[0:HIST, cols] = ubuf[tm:tm + HIST, cols]
        act = _silu(conv)
        if g * width < MIX_HALF:
            act = _l2norm_heads(act) * (HEAD_DIM ** -0.5)
        elif g * width < 2 * MIX_HALF:
            act = _l2norm_heads(act)
        ob_ref[:, cols] = act

    n_conv = GDN_QKV // width
    n_a = wa_ref.shape[1] // width
    others = [functools.partial(project, wa_ref, oa_ref, g) for g in range(n_a)]
    others += [functools.partial(project, wb_ref, ob_ref, g) for g in range(n_conv, wb_ref.shape[1] // width)]
    project(wb_ref, ubuf, 0, HIST)
    for g in range(n_conv):
        if g + 1 < n_conv:
            project(wb_ref, ubuf, g + 1, HIST)
        conv_group(g)
        others.pop(0)()
    for task in others:
        task()
    oc_ref[...] = _dot(hn, wc_ref[...])


def _inproj(x2, norm_w, wa, wb, wc, conv_w, tm, tiles_per_seq):
    t = x2.shape[0]
    na, nb, nc = wa.shape[1], wb.shape[1], wc.shape[1]
    const = lambda i: (0, 0)
    return pl.pallas_call(
        functools.partial(_inproj_kernel, tiles_per_seq=tiles_per_seq),
        grid=(t // tm,),
        in_specs=[
            pl.BlockSpec((tm, D_MODEL), lambda i: (i, 0)),
            pl.BlockSpec((1, D_MODEL), const),
            pl.BlockSpec((D_MODEL, na), const),
            pl.BlockSpec((D_MODEL, nb), const),
            pl.BlockSpec((D_MODEL, nc), const),
            pl.BlockSpec((CONV_K, GDN_QKV), const),
        ],
        out_specs=[
            pl.BlockSpec((tm, na), lambda i: (i, 0)),
            pl.BlockSpec((tm, nb), lambda i: (i, 0)),
            pl.BlockSpec((tm, nc), lambda i: (i, 0)),
        ],
        out_shape=[
            jax.ShapeDtypeStruct((t, na), F32),
            jax.ShapeDtypeStruct((t, nb), F32),
            jax.ShapeDtypeStruct((t, nc), F32),
        ],
        scratch_shapes=[pltpu.VMEM((HIST + tm, GDN_QKV), F32)],
        compiler_params=pltpu.CompilerParams(dimension_semantics=("arbitrary",), vmem_limit_bytes=VMEM_LIMIT),
        name="inproj",
    )(x2, norm_w, wa, wb, wc, conv_w)


HGRN_LEVELS = (32, 16, 8, 4, 2, 1)
DIAG_CODE = len(HGRN_LEVELS)
NONE_CODE = DIAG_CODE + 1


def _hgrn_arg_matrix():
    t = np.arange(CHUNK)[:, None]
    u = np.arange(CHUNK)[None, :]
    mats = [u <= t]
    for b in HGRN_LEVELS:
        odd = (t // b) % 2 == 1
        start = (t // b) * b
        mats.append(np.where(odd, (u > start) & (u <= t), (u > t) & (u <= start + b)))
    return np.concatenate(mats, axis=0).astype(np.float32)


def _hgrn_level_codes():
    idx = np.arange(STACK)
    h, t = idx // CHUNK, idx % CHUNK
    same = h[:, None] == h[None, :]
    tt, ss = t[:, None], t[None, :]
    code = np.full((STACK, STACK), NONE_CODE, np.int32)
    code[same & (tt == ss)] = DIAG_CODE
    for l, b in enumerate(HGRN_LEVELS):
        sib = (tt // (2 * b) == ss // (2 * b)) & ((tt // b) % 2 == 1) & ((ss // b) % 2 == 0)
        code[same & sib] = l
    return code


def _hgrn_kernel(q_ref, f_ref, i_ref, g_ref, lbl_ref, onw_ref, marg_ref, code_ref, o_ref, st_ref, *, n_chunks):
    @pl.when(pl.program_id(1) == 0)
    def _():
        st_ref[...] = jnp.zeros_like(st_ref)

    lbl = lbl_ref[...]
    lmax = jnp.max(lbl, axis=0, keepdims=True)
    lexp = jnp.exp(lbl - lmax)
    lb = lexp[0:1, :] / jnp.sum(lexp, axis=0, keepdims=True)
    onw = onw_ref[...]
    marg = marg_ref[...]
    code = code_ref[...]
    trow = lax.broadcasted_iota(jnp.int32, (STACK, HEAD_DIM), 0) & (CHUNK - 1)

    q_blk = _silu(q_ref[...]) * (HEAD_DIM ** -0.5)
    fg_blk = lb + (1.0 - lb) * _sigmoid(f_ref[...])
    k_blk = 1.0 - fg_blk
    lf_blk = jnp.log(fg_blk)
    v_blk = i_ref[...]

    rows = [slice(c * CHUNK, (c + 1) * CHUNK) for c in range(n_chunks)]
    args = [_masked_sum(marg, lf_blk[r]) for r in rows]
    e_chunk = [jnp.exp(a) for a in args]
    suf_chunk = [jnp.exp(a[CHUNK - 1:CHUNK] - a[0:CHUNK]) for a in args]
    units = [(c, p) for c in range(n_chunks) for p in range(N_STACKS)]
    qs = [_stack_heads(q_blk[rows[c]], p) for c, p in units]
    ks = [_stack_heads(k_blk[rows[c]], p) for c, p in units]
    vs = [_stack_heads(v_blk[rows[c]], p).astype(BF16) for c, p in units]
    e_part = lambda n: [_stack_heads(e_chunk[c][n * CHUNK:(n + 1) * CHUNK], p) for c, p in units]
    cum_e = e_part(0)
    suf_e = [_stack_heads(suf_chunk[c], p) for c, p in units]
    att = _each(lambda q, k: jnp.where(code == DIAG_CODE, jnp.sum(q * k, axis=-1, keepdims=True), 0.0), qs, ks)

    def level_update(a, x, l, b):
        if b < BF16_ROWS:
            return jnp.where(code == l, _dot_nt(x, x), a)
        blocks = [slice(s0, s0 + b) for s0 in range(0, STACK, b)]
        r = _dot_nt(jnp.concatenate([x[bl] for bl in blocks[1::2]], axis=0), x)
        return jnp.concatenate(
            [a[bl] if n % 2 == 0 else jnp.where(code[bl] == l, r[(n // 2) * b:(n // 2 + 1) * b], a[bl])
             for n, bl in enumerate(blocks)], axis=0)

    for l, b in enumerate(HGRN_LEVELS):
        x = _each(lambda q, k, el: (jnp.where((trow & b) != 0, q, k) * el).astype(BF16), qs, ks, e_part(1 + l))
        att = _each(lambda xi, a: level_update(a, xi, l, b), x, att)
    o_intra = _each(lambda a, v: _dot(a.astype(BF16), v), att, vs)
    qc = _each(lambda q, e: (q * e).astype(BF16), qs, cum_e)
    kd = _each(lambda k, e: (k * e).astype(BF16), ks, suf_e)

    state = [st_ref[h] for h in range(HEADS)]
    for u, (c, p) in enumerate(units):
        for i in range(GROUP):
            h = p * GROUP + i
            hr = slice(i * CHUNK, (i + 1) * CHUNK)
            hc = slice(h * HEAD_DIM, (h + 1) * HEAD_DIM)
            o = _dot_nt(qc[u][hr], state[h].astype(BF16)) + o_intra[u][hr]
            decay = cum_e[u][i * CHUNK + CHUNK - 1:i * CHUNK + CHUNK, :]
            state[h] = decay * state[h] + _dot_tn(vs[u][hr], kd[u][hr])
            y = o * lax.rsqrt(jnp.mean(o * o, axis=-1, keepdims=True) + NORM_EPS) * onw
            o_ref[rows[c], hc] = (y * _silu(g_ref[rows[c], hc])).astype(o_ref.dtype)
    for h in range(HEADS):
        st_ref[h] = state[h]


def _hgrn(pa, lb_logits, onorm_w, batch, seq, n_chunks):
    rows = n_chunks * CHUNK
    steps = seq // rows
    col = lambda j: pl.BlockSpec((rows, MIX_HALF), lambda b, s, j=j: (b * steps + s, j))
    const = lambda b, s: (0, 0)
    marg = jnp.asarray(_triple(_hgrn_arg_matrix()), BF16)
    code = jnp.asarray(_hgrn_level_codes())
    return pl.pallas_call(
        functools.partial(_hgrn_kernel, n_chunks=n_chunks),
        grid=(batch, steps),
        in_specs=[col(0), col(1), col(2), col(3),
                  pl.BlockSpec(lb_logits.shape, const),
                  pl.BlockSpec((1, HEAD_DIM), const),
                  pl.BlockSpec(marg.shape, const),
                  pl.BlockSpec(code.shape, const)],
        out_specs=pl.BlockSpec((rows, MIX_HALF), lambda b, s: (b * steps + s, 0)),
        out_shape=jax.ShapeDtypeStruct((batch * seq, MIX_HALF), BF16),
        scratch_shapes=[pltpu.VMEM((HEADS, HEAD_DIM, HEAD_DIM), F32)],
        compiler_params=pltpu.CompilerParams(dimension_semantics=("arbitrary", "arbitrary"),
                                             vmem_limit_bytes=VMEM_LIMIT),
        name="hgrn2",
    )(pa, pa, pa, pa, lb_logits, onorm_w, marg, code)


GDN_QKV = 3 * MIX_HALF
HIST = 8
C_NONE, C_DIAG, C_B16, C_B32, C_B64 = 0, 1, 2, 3, 4
MASKED_EXPONENT = -1e30


def _gdn_codes():
    idx = np.arange(STACK)
    h, t = idx // CHUNK, idx % CHUNK
    same = h[:, None] == h[None, :]
    tt, ss = t[:, None], t[None, :]
    code = np.full((STACK, STACK), C_NONE, np.int32)
    low = same & (ss < tt)
    code[low] = C_B64
    code[low & (tt // 32 == ss // 32)] = C_B32
    code[low & (tt // 16 == ss // 16)] = C_B16
    code[same & (tt == ss)] = C_DIAG
    incl = np.tril(np.ones((CHUNK, CHUNK), np.float32))
    return code, incl


def _gdn_kernel(qkv_ref, z_ref, ab_ref, alog_ref, dtb_ref, onw_ref, code_ref, incl_ref,
                o_ref, st_ref, *, n_chunks):
    @pl.when(pl.program_id(1) == 0)
    def _():
        st_ref[...] = jnp.zeros_like(st_ref)

    onw = onw_ref[...]
    code = code_ref[...]
    incl_m = incl_ref[...]
    incl = code >= C_DIAG
    eye = (code == C_DIAG).astype(F32)

    rows = [slice(c * CHUNK, (c + 1) * CHUNK) for c in range(n_chunks)]
    units = [(c, p) for c in range(n_chunks) for p in range(N_STACKS)]

    qs = [_stack_heads(qkv_ref[rows[c], 0:MIX_HALF], p) for c, p in units]
    ks = [_stack_heads(qkv_ref[rows[c], MIX_HALF:2 * MIX_HALF], p) for c, p in units]
    vs = [_stack_heads(qkv_ref[rows[c], 2 * MIX_HALF:3 * MIX_HALF], p) for c, p in units]
    ab_blk = ab_ref[...]
    xa = ab_blk + dtb_ref[...]
    softplus = jnp.maximum(xa, 0.0) + jnp.log(1.0 + jnp.exp(-jnp.abs(xa)))
    g_all = -jnp.exp(alog_ref[...]) * softplus
    beta_all = _sigmoid(ab_blk)

    def head_cols(a, c, p, first_lane):
        return jnp.concatenate([a[c][:, first_lane + h:first_lane + h + 1]
                                for h in range(p * GROUP, (p + 1) * GROUP)], axis=0)

    g_cum = [_masked_sum(incl_m, g_all[r]) for r in rows]
    beta_chunk = [beta_all[r] for r in rows]
    beta_st = [head_cols(beta_chunk, c, p, HEADS) for c, p in units]
    gc = [jnp.broadcast_to(head_cols(g_cum, c, p, 0), (STACK, HEAD_DIM)) for c, p in units]
    dec = [jnp.exp(jnp.where(incl, g - jnp.transpose(g)[0:1, :], MASKED_EXPONENT)) for g in gc]
    kb = [k.astype(BF16) for k in ks]
    kq = _each(lambda k, q: _dot_nt(jnp.concatenate([k, q.astype(BF16)], axis=0), k), kb, qs)
    kk = [a[0:STACK] for a in kq]
    qk = [a[STACK:] for a in kq]
    am = _each(lambda b, k2, d: b * k2 * d, beta_st, kk, dec)
    a16 = [jnp.where(code == C_B16, a, 0.0) for a in am]
    n32 = [jnp.where(code == C_B32, a, 0.0) for a in am]
    n64 = [jnp.where(code == C_B64, a, 0.0) for a in am]
    apow = _each(_bdot, a16, a16)
    pinv = [eye - a for a in a16]
    for _ in range(2):
        both = _each(lambda ai, pi: _bdot(jnp.concatenate([ai, pi], axis=0), ai), apow, pinv)
        pinv = _each(lambda pi, b: pi + b[STACK:], pinv, both)
        apow = [b[0:STACK] for b in both]
    pinv = _each(lambda pi, ai: pi + _bdot(pi, ai), pinv, apow)
    for nlev in (n32, n64):
        t = _each(_bdot, nlev, pinv)
        pinv = _each(lambda pi, ti: pi - _bdot(pi, ti), pinv, t)
    egc = [jnp.exp(g) for g in gc]
    rhs = _each(lambda k, v, b, e: jnp.concatenate([k * (b * e), v * b], axis=1), ks, vs, beta_st, egc)
    wu = _each(_bdot, pinv, rhs)
    w_c = [a[:, 0:HEAD_DIM].astype(BF16) for a in wu]
    u_c = [a[:, HEAD_DIM:] for a in wu]
    qkm = _each(lambda a, d: (a * d).astype(BF16), qk, dec)
    qg = _each(lambda q, e: (q * e).astype(BF16), qs, egc)
    glast = [[g[i * CHUNK + CHUNK - 1:(i + 1) * CHUNK, :] for i in range(GROUP)] for g in gc]
    kdec = [[(k[i * CHUNK:(i + 1) * CHUNK] * jnp.exp(gl[i] - g[i * CHUNK:(i + 1) * CHUNK])).astype(BF16)
             for i in range(GROUP)] for k, g, gl in zip(ks, gc, glast)]
    gend = [[jnp.exp(gi) for gi in gl] for gl in glast]

    state = [st_ref[h] for h in range(HEADS)]
    for u, (c, p) in enumerate(units):
        heads = range(p * GROUP, (p + 1) * GROUP)
        hrs = [slice(i * CHUNK, (i + 1) * CHUNK) for i in range(GROUP)]
        ws = [_dot_nt(jnp.concatenate([w_c[u][hr], qg[u][hr]], axis=0), state[h].astype(BF16))
              for hr, h in zip(hrs, heads)]
        v_new = jnp.concatenate([u_c[u][hr] - a[0:CHUNK] for hr, a in zip(hrs, ws)], axis=0).astype(BF16)
        o_st = jnp.concatenate([a[CHUNK:] for a in ws], axis=0) + _dot(qkm[u], v_new)
        for i, h in enumerate(heads):
            hc = slice(h * HEAD_DIM, (h + 1) * HEAD_DIM)
            state[h] = gend[u][i] * state[h] + _dot_tn(v_new[hrs[i]], kdec[u][i])
            o = o_st[hrs[i]]
            y = o * lax.rsqrt(jnp.mean(o * o, axis=-1, keepdims=True) + NORM_EPS) * onw
            o_ref[rows[c], hc] = (y * _silu(z_ref[rows[c], hc])).astype(o_ref.dtype)
    for h in range(HEADS):
        st_ref[h] = state[h]


def _gdn(pb, pc, alog_row, dtb_row, onorm_w, batch, seq, n_chunks):
    rows = n_chunks * CHUNK
    steps = seq // rows
    const = lambda b, s: (0, 0)
    code, incl = _gdn_codes()
    code, incl = jnp.asarray(code), jnp.asarray(_triple(incl), BF16)
    sq = pl.BlockSpec((STACK, STACK), const)
    row = pl.BlockSpec((1, LANES), const)
    return pl.pallas_call(
        functools.partial(_gdn_kernel, n_chunks=n_chunks),
        grid=(batch, steps),
        in_specs=[pl.BlockSpec((rows, GDN_QKV), lambda b, s: (b * steps + s, 0)),
                  pl.BlockSpec((rows, MIX_HALF), lambda b, s: (b * steps + s, GDN_QKV // MIX_HALF)),
                  pl.BlockSpec((rows, LANES), lambda b, s: (b * steps + s, 0)),
                  row, row, row, sq, pl.BlockSpec((CHUNK, 3 * CHUNK), const)],
        out_specs=pl.BlockSpec((rows, MIX_HALF), lambda b, s: (b * steps + s, 0)),
        out_shape=jax.ShapeDtypeStruct((batch * seq, MIX_HALF), BF16),
        scratch_shapes=[pltpu.VMEM((HEADS, HEAD_DIM, HEAD_DIM), F32)],
        compiler_params=pltpu.CompilerParams(dimension_semantics=("arbitrary", "arbitrary"),
                                             vmem_limit_bytes=VMEM_LIMIT),
        name="gdn",
    )(pb, pb, pc, alog_row, dtb_row, onorm_w, code, incl)


def _route_kernel(x_ref, ohg_ref, ogd_ref, wo_ref, nw_ref, wr_ref, ltri_ref,
                  h_ref, hn_ref, ri_ref, rg_ref, cnt_ref, cnt_scr):
    @pl.when(pl.program_id(0) == 0)
    def _():
        cnt_scr[...] = jnp.zeros_like(cnt_scr)

    mix = _dot(ohg_ref[...], wo_ref[0:MIX_HALF, :]) + _dot(ogd_ref[...], wo_ref[MIX_HALF:2 * MIX_HALF, :])
    h = x_ref[...] + mix
    h_ref[...] = h
    hn = h * lax.rsqrt(jnp.mean(h * h, axis=-1, keepdims=True) + NORM_EPS) * nw_ref[...]
    hn_ref[...] = _pack_words(hn)
    hn_hi = hn.astype(BF16)
    hn_lo = (hn - hn_hi.astype(F32)).astype(BF16)
    part = _dot(hn_hi, wr_ref[...])
    logits = (_dot(hn_lo, wr_ref[:, 0:LANES]) + part[:, LANES:]) + part[:, 0:LANES]
    tm = logits.shape[0]
    lane = lax.broadcasted_iota(jnp.int32, (tm, LANES), 1)
    neg = jnp.float32(-jnp.inf)
    big = jnp.int32(LANES)

    def first_max(vals):
        m = jnp.max(vals, axis=-1, keepdims=True)
        return m, jnp.min(jnp.where(vals == m, lane, big), axis=-1, keepdims=True)

    gl = jnp.where(lane < N_GROUPS, logits, neg)
    gmax, gidx = first_max(gl)
    p_group = 1.0 / jnp.sum(jnp.exp(gl - gmax), axis=-1, keepdims=True)
    lo = N_GROUPS + EXPERTS_PER_GROUP * gidx
    el = jnp.where((lane >= lo) & (lane < lo + EXPERTS_PER_GROUP), logits, neg)
    m1, i1 = first_max(el)
    m2, i2 = first_max(jnp.where(lane == i1, neg, el))
    r = jnp.exp(m2 - m1)
    gate1 = p_group / (1.0 + r)
    gate2 = p_group * r / (1.0 + r)
    hot1 = lane == i1
    hot2 = lane == i2
    onehot = jnp.where(hot1 | hot2, 1.0, 0.0)
    before = _dot(ltri_ref[...], onehot.astype(BF16)) + cnt_scr[...]
    rank1 = jnp.sum(jnp.where(hot1, before, 0.0), axis=-1, keepdims=True)
    rank2 = jnp.sum(jnp.where(hot2, before, 0.0), axis=-1, keepdims=True)
    cnt = cnt_scr[...] + jnp.sum(onehot, axis=0, keepdims=True)
    cnt_scr[...] = cnt
    cnt_ref[...] = cnt
    ri = jnp.where(lane == 0, i1 - N_GROUPS,
                   jnp.where(lane == 1, i2 - N_GROUPS,
                             jnp.where(lane == 2, rank1.astype(jnp.int32),
                                       jnp.where(lane == 3, rank2.astype(jnp.int32), 0))))
    ri_ref[...] = ri
    rg_ref[...] = jnp.where(lane == 0, gate1, jnp.where(lane == 1, gate2, 0.0))


def _route(x2, ohg, ogd, wo, norm_w, wr, tm):
    t = x2.shape[0]
    ltri = jnp.asarray(np.tril(np.ones((tm, tm), np.float32), -1), BF16)
    const = lambda i: (0, 0)
    tile = lambda n: pl.BlockSpec((tm, n), lambda i: (i, 0))
    return pl.pallas_call(
        _route_kernel,
        grid=(t // tm,),
        in_specs=[tile(D_MODEL), tile(MIX_HALF), tile(MIX_HALF),
                  pl.BlockSpec((D_MODEL, D_MODEL), const),
                  pl.BlockSpec((1, D_MODEL), const),
                  pl.BlockSpec((D_MODEL, 2 * LANES), const),
                  pl.BlockSpec((tm, tm), const)],
        out_specs=[tile(D_MODEL), tile(HALF), tile(LANES), tile(LANES),
                   pl.BlockSpec((1, LANES), const)],
        out_shape=[jax.ShapeDtypeStruct((t, D_MODEL), F32),
                   jax.ShapeDtypeStruct((t, HALF), U32),
                   jax.ShapeDtypeStruct((t, LANES), jnp.int32),
                   jax.ShapeDtypeStruct((t, LANES), F32),
                   jax.ShapeDtypeStruct((1, LANES), F32)],
        scratch_shapes=[pltpu.VMEM((1, LANES), F32)],
        compiler_params=pltpu.CompilerParams(dimension_semantics=("arbitrary",), vmem_limit_bytes=VMEM_LIMIT),
        name="route",
    )(x2, ohg, ogd, wo, norm_w, wr, ltri)


ISSUE_UNROLL = 8


HALF = D_MODEL // 2
PIECES = HALF // LANES
U32 = jnp.uint32
HIGH_HALF = np.uint32(0xFFFF0000)


def _pack_words(x):
    bits = lambda a: lax.bitcast_convert_type(a.astype(BF16).astype(F32), U32)
    return (bits(x[:, 0:HALF]) >> 16) | (bits(x[:, HALF:]) & HIGH_HALF)


def _unpack_words(words):
    return (lax.bitcast_convert_type(words << 16, F32), lax.bitcast_convert_type(words & HIGH_HALF, F32))


def _store_rows_as_tiles(ref, x):
    n = x.shape[0]
    words = _pack_words(x)
    for s in range(PIECES):
        ref[pl.ds(s, n, stride=PIECES), :] = words[:, s * LANES:(s + 1) * LANES]


def _load_rows_from_tiles(ref, n):
    return _unpack_words(jnp.concatenate([ref[pl.ds(s, n, stride=PIECES), :] for s in range(PIECES)], axis=1))


SC_WINDOW = 64


def _sc_gather_rows(table, idx):
    info = plsc.get_sparse_core_info()
    n_workers = info.num_cores * info.num_subcores
    n_rows, width = idx.shape[0], table.shape[1]
    per_worker = n_rows // n_workers
    assert per_worker * n_workers == n_rows and per_worker % SC_WINDOW == 0

    @functools.partial(
        pl.kernel,
        mesh=plsc.VectorSubcoreMesh(core_axis_name="c", subcore_axis_name="s"),
        out_type=jax.ShapeDtypeStruct((n_rows, width), table.dtype),
        scratch_types=[pltpu.VMEM((SC_WINDOW,), jnp.int32), pltpu.VMEM((SC_WINDOW, width), table.dtype),
                       pltpu.SemaphoreType.DMA],
    )
    def gather(table_hbm, idx_hbm, out_hbm, idx_v, rows_v, sem):
        worker = lax.axis_index("s") * info.num_cores + lax.axis_index("c")

        @pl.loop(0, per_worker // SC_WINDOW)
        def _(j):
            base = pl.multiple_of(worker * per_worker + j * SC_WINDOW, SC_WINDOW)
            pltpu.sync_copy(idx_hbm.at[pl.ds(base, SC_WINDOW)], idx_v)
            pltpu.async_copy(table_hbm.at[idx_v], rows_v, sem).wait()
            pltpu.sync_copy(rows_v, out_hbm.at[pl.ds(base, SC_WINDOW)])

    return gather(table, idx)


def _sc_scatter_rows(rows, idx):
    info = plsc.get_sparse_core_info()
    n_workers = info.num_cores * info.num_subcores
    n_out, (n_src, width) = idx.shape[0], rows.shape
    per_worker = n_out // n_workers
    assert per_worker * n_workers == n_out and per_worker % SC_WINDOW == 0 and n_src % per_worker == 0

    @functools.partial(
        pl.kernel,
        mesh=plsc.VectorSubcoreMesh(core_axis_name="c", subcore_axis_name="s"),
        out_type=jax.ShapeDtypeStruct((n_out, width), rows.dtype),
        scratch_types=[pltpu.VMEM((SC_WINDOW,), jnp.int32), pltpu.VMEM((SC_WINDOW, width), rows.dtype),
                       pltpu.SemaphoreType.DMA],
    )
    def scatter(rows_hbm, idx_hbm, out_hbm, idx_v, rows_v, sem):
        worker = lax.axis_index("s") * info.num_cores + lax.axis_index("c")

        @pl.loop(0, per_worker // SC_WINDOW)
        def _(j):
            base = pl.multiple_of(worker * per_worker + j * SC_WINDOW, SC_WINDOW)
            src = pl.multiple_of(lax.rem(base, n_src), SC_WINDOW)
            pltpu.sync_copy(idx_hbm.at[pl.ds(base, SC_WINDOW)], idx_v)
            pltpu.sync_copy(rows_hbm.at[pl.ds(src, SC_WINDOW)], rows_v)
            pltpu.async_copy(rows_v, out_hbm.at[idx_v], sem).wait()

    return scatter(rows, idx)


def _row_copy(src_ref, src_row, dst_ref, dst_row, sem):
    src = src_ref.at[pl.ds(pl.multiple_of(src_row * PIECES, PIECES), PIECES)]
    dst = dst_ref.at[pl.ds(pl.multiple_of(dst_row * PIECES, PIECES), PIECES)]
    return pltpu.make_async_copy(src, dst, sem)


def _dispatch_kernel(starts_ref, idx_ref, hn_ref, xs_ref, sem, *, tm):
    def issue(t, carry):
        for k in range(TOP_K):
            dest = starts_ref[idx_ref[0, 0, 4 * t + k]] + idx_ref[0, 0, 4 * t + 2 + k]
            _row_copy(hn_ref, t, xs_ref, dest, sem).start(priority=k % 2)
        return carry

    lax.fori_loop(0, tm, issue, 0, unroll=ISSUE_UNROLL)
    for k in range(TOP_K):
        pltpu.make_async_copy(hn_ref, xs_ref.at[pl.ds(0, tm * PIECES)], sem).wait()


def _dispatch(starts, idx3, hn, tm):
    t = hn.shape[0] // PIECES
    return pl.pallas_call(
        functools.partial(_dispatch_kernel, tm=tm),
        grid_spec=pltpu.PrefetchScalarGridSpec(
            num_scalar_prefetch=1,
            grid=(t // tm,),
            in_specs=[pl.BlockSpec((1, 1, 4 * tm), lambda i, s: (i, 0, 0), memory_space=pltpu.SMEM),
                      pl.BlockSpec((tm * PIECES, LANES), lambda i, s: (i, 0))],
            out_specs=pl.BlockSpec(memory_space=pl.ANY),
            scratch_shapes=[pltpu.SemaphoreType.DMA(())],
        ),
        out_shape=jax.ShapeDtypeStruct((t * TOP_K * PIECES, LANES), U32),
        compiler_params=pltpu.CompilerParams(dimension_semantics=("arbitrary",), vmem_limit_bytes=VMEM_LIMIT),
        name="dispatch",
    )(starts, idx3, hn)


EXPERT_SPLIT = 1


def _expert_kernel(blk_ref, exp_ref, lo_ref, hi_ref, xs_ref, wg_ref, wu_ref, wd_ref, ys_ref, wgu_b, wd_b, acc):
    p = pl.program_id(0)

    @pl.when(jnp.logical_or(p == 0, exp_ref[p] != exp_ref[jnp.maximum(p - 1, 0)]))
    def _():
        wgu_b[:, 0:EXPERT_FF] = wg_ref[0].astype(BF16)
        wgu_b[:, EXPERT_FF:2 * EXPERT_FF] = wu_ref[0].astype(BF16)
        wd_b[...] = wd_ref[0].astype(BF16)

    @pl.when(p == 0)
    def _():
        acc[...] = jnp.zeros_like(acc)

    first = jnp.logical_or(p == 0, blk_ref[p] != blk_ref[jnp.maximum(p - 1, 0)])
    sub = SLOT_BLOCK // EXPERT_SPLIT
    for r in range(EXPERT_SPLIT):
        x_lo, x_hi = _unpack_words(xs_ref[pl.ds(r * sub, sub), :])
        ab = _dot(x_lo.astype(BF16), wgu_b[0:HALF, :]) + _dot(x_hi.astype(BF16), wgu_b[HALF:D_MODEL, :])
        hb = _silu(ab[:, 0:EXPERT_FF]) * ab[:, EXPERT_FF:2 * EXPERT_FF]
        y = _dot(hb.astype(BF16), wd_b[...])
        slot = blk_ref[p] * SLOT_BLOCK + r * sub + lax.broadcasted_iota(jnp.int32, (sub, 1), 0)
        y = jnp.where((slot >= lo_ref[p]) & (slot < hi_ref[p]), y, 0.0)
        total = y + jnp.where(first, 0.0, acc[pl.ds(r * sub, sub), :])
        acc[pl.ds(r * sub, sub), :] = total
        ys_ref[pl.ds(r * sub, sub), :] = _pack_words(total)


def _experts(blk, exp, lo, hi, xs, wg, wu, wd):
    n_pairs = blk.shape[0]
    return pl.pallas_call(
        _expert_kernel,
        grid_spec=pltpu.PrefetchScalarGridSpec(
            num_scalar_prefetch=4,
            grid=(n_pairs,),
            in_specs=[pl.BlockSpec((SLOT_BLOCK, HALF), lambda p, blk, exp, lo, hi: (blk[p], 0)),
                      pl.BlockSpec((1, D_MODEL, EXPERT_FF), lambda p, blk, exp, lo, hi: (exp[p], 0, 0)),
                      pl.BlockSpec((1, D_MODEL, EXPERT_FF), lambda p, blk, exp, lo, hi: (exp[p], 0, 0)),
                      pl.BlockSpec((1, EXPERT_FF, D_MODEL), lambda p, blk, exp, lo, hi: (exp[p], 0, 0))],
            out_specs=pl.BlockSpec((SLOT_BLOCK, HALF), lambda p, blk, exp, lo, hi: (blk[p], 0)),
            scratch_shapes=[pltpu.VMEM((D_MODEL, 2 * EXPERT_FF), BF16), pltpu.VMEM((EXPERT_FF, D_MODEL), BF16),
                            pltpu.VMEM((SLOT_BLOCK, D_MODEL), F32)],
        ),
        out_shape=jax.ShapeDtypeStruct(xs.shape, U32),
        compiler_params=pltpu.CompilerParams(dimension_semantics=("arbitrary",), vmem_limit_bytes=VMEM_LIMIT),
        name="experts",
    )(blk, exp, lo, hi, xs, wg, wu, wd)


def _combine_kernel(h_ref, rg_ref, fw_ref, y0_ref, y1_ref, o_ref):
    rg = rg_ref[...]
    y = [jnp.concatenate(_unpack_words(ref[...]), axis=1) for ref in (y0_ref, y1_ref)]
    h = h_ref[...] + rg[:, 0:1] * y[0] + rg[:, 1:2] * y[1]
    o_ref[...] = h * lax.rsqrt(jnp.mean(h * h, axis=-1, keepdims=True) + NORM_EPS) * fw_ref[...]


def _combine(h, rg, final_w, y2, tm):
    t = h.shape[0]
    n_tiles = t // tm
    return pl.pallas_call(
        _combine_kernel,
        grid=(n_tiles,),
        in_specs=[pl.BlockSpec((tm, D_MODEL), lambda i: (i, 0)),
                  pl.BlockSpec((tm, LANES), lambda i: (i, 0)),
                  pl.BlockSpec((1, D_MODEL), lambda i: (0, 0)),
                  pl.BlockSpec((tm, HALF), lambda i: (i, 0)),
                  pl.BlockSpec((tm, HALF), lambda i: (n_tiles + i, 0))],
        out_specs=pl.BlockSpec((tm, D_MODEL), lambda i: (i, 0)),
        out_shape=jax.ShapeDtypeStruct((t, D_MODEL), F32),
        compiler_params=pltpu.CompilerParams(dimension_semantics=("arbitrary",), vmem_limit_bytes=VMEM_LIMIT),
        name="combine",
    )(h, rg, final_w, y2, y2)


def _pair_schedule(counts, n_slots):
    n_blocks = n_slots // SLOT_BLOCK
    n_pairs = n_blocks + N_EXPERTS - 1
    ends = jnp.cumsum(counts)
    starts = ends - counts
    first = starts // SLOT_BLOCK
    last = jnp.maximum(ends - 1, starts) // SLOT_BLOCK
    per_expert = jnp.where(counts > 0, last - first + 1, 0)
    cum = jnp.cumsum(per_expert)
    p = jnp.arange(n_pairs, dtype=jnp.int32)
    e = jnp.minimum(jnp.sum(cum[None, :] <= p[:, None], axis=1), N_EXPERTS - 1).astype(jnp.int32)
    valid = p < cum[-1]
    onehot = e[:, None] == jnp.arange(N_EXPERTS, dtype=jnp.int32)[None, :]
    pick = lambda table: jnp.sum(jnp.where(onehot, table[None, :], 0), axis=1)
    blk = jnp.where(valid, pick(first) + p - pick(cum - per_expert), n_blocks - 1).astype(jnp.int32)
    lo = jnp.where(valid, pick(starts), 0).astype(jnp.int32)
    hi = jnp.where(valid, pick(ends), 0).astype(jnp.int32)
    return starts.astype(jnp.int32), blk, e, lo, hi


def _layer(x, norm_mix_w, w_in, lb_logits, hgrn_onorm_w, gdn_conv_w, gdn_a_log, gdn_dt_bias, gdn_onorm_w, w_out,
           norm_moe_w, router_group_w, router_expert_w, w_gate, w_up, w_down, final_w, *, tm, n_chunks):
    batch, seq, _ = x.shape
    t = batch * seq
    x2 = x.reshape(t, D_MODEL)
    n_hg = 4 * MIX_HALF
    n_gd = GDN_QKV + MIX_HALF
    wb16 = w_in.astype(BF16)
    wa = wb16[:, 0:n_hg]
    wb = wb16[:, n_hg:n_hg + n_gd]
    wc = jnp.pad(wb16[:, n_hg + n_gd:], ((0, 0), (0, LANES - 2 * HEADS)))
    pa, pb, pc = _inproj(x2, norm_mix_w.reshape(1, D_MODEL), wa, wb, wc, gdn_conv_w, 2 * tm, seq // (2 * tm))

    o_hg = _hgrn(pa, lb_logits, hgrn_onorm_w.reshape(1, HEAD_DIM), batch, seq, n_chunks)
    alog_row = jnp.pad(gdn_a_log.reshape(1, HEADS), ((0, 0), (0, LANES - HEADS)))
    dtb_row = jnp.pad(gdn_dt_bias.reshape(1, HEADS), ((0, 0), (0, LANES - HEADS)))
    o_gd = _gdn(pb, pc, alog_row, dtb_row, gdn_onorm_w.reshape(1, HEAD_DIM), batch, seq, n_chunks)

    wr = jnp.pad(jnp.concatenate([router_group_w, router_expert_w], axis=1),
                 ((0, 0), (0, LANES - N_GROUPS - N_EXPERTS)))
    wr_hi = wr.astype(BF16)
    wr = jnp.concatenate([wr_hi, (wr - wr_hi.astype(F32)).astype(BF16)], axis=1)
    h, hn, ri, rg, cnt = _route(x2, o_hg, o_gd, w_out.astype(BF16), norm_moe_w.reshape(1, D_MODEL), wr, 2 * tm)

    counts = cnt[0, N_GROUPS:N_GROUPS + N_EXPERTS].astype(jnp.int32)
    starts, blk, exp, lo, hi = _pair_schedule(counts, t * TOP_K)
    experts_kt = jnp.transpose(ri[:, 0:TOP_K])
    onehot = experts_kt[:, :, None] == jnp.arange(N_EXPERTS, dtype=jnp.int32)
    slot_kt = jnp.sum(jnp.where(onehot, starts, 0), axis=-1) + jnp.transpose(ri[:, TOP_K:2 * TOP_K])
    slots = slot_kt.reshape(TOP_K * t)
    xs = _sc_scatter_rows(hn, slots)
    ys = _experts(blk, exp, lo, hi, xs, w_gate, w_up, w_down)
    y2 = _sc_gather_rows(ys, slots)
    out = _combine(h, rg, final_w.reshape(1, D_MODEL), y2, 2 * tm)
    return out.reshape(batch, seq, D_MODEL)


def kernel(x, norm_mix_w, w_in, hgrn_lb_logits, hgrn_onorm_w, gdn_conv_w, gdn_a_log, gdn_dt_bias, gdn_onorm_w, w_out, norm_moe_w, router_group_w, router_expert_w, expert_w_gate, expert_w_up, expert_w_down, final_norm_w):
    return _layer(x, norm_mix_w[0], w_in[0], hgrn_lb_logits, hgrn_onorm_w[0], gdn_conv_w[0], gdn_a_log[0],
                  gdn_dt_bias[0], gdn_onorm_w[0], w_out[0], norm_moe_w[0], router_group_w[0], router_expert_w[0],
                  expert_w_gate[0], expert_w_up[0], expert_w_down[0], final_norm_w, tm=256, n_chunks=8)
```

```python
import functools

import numpy as np
import jax
import jax.numpy as jnp
from jax import lax
from jax.experimental import pallas as pl
from jax.experimental.pallas import tpu as pltpu
from jax.experimental.pallas import tpu_sc as plsc

F32 = jnp.float32
BF16 = jnp.bfloat16
HI = lax.Precision.HIGHEST

D_MODEL = 1024
HEADS = 4
HEAD_DIM = 128
MIX_HALF = HEADS * HEAD_DIM
CHUNK = 64
GROUP = 2
N_STACKS = HEADS // GROUP
STACK = GROUP * CHUNK
CONV_K = 4
N_GROUPS = 8
EXPERTS_PER_GROUP = 8
N_EXPERTS = N_GROUPS * EXPERTS_PER_GROUP
TOP_K = 2
EXPERT_FF = 256
NORM_EPS = 1e-6
LANES = 128
BF16_ROWS = 16
SLOT_BLOCK = 512
VMEM_LIMIT = 56 * 1024 * 1024


def _sigmoid(x):
    return 0.5 * jnp.tanh(0.5 * x) + 0.5


def _silu(x):
    return x * _sigmoid(x)


def _dot(a, b, precision=None):
    return jnp.dot(a, b, preferred_element_type=F32, precision=precision)


def _dot_nt(a, b, precision=None):
    return lax.dot_general(a, b, (((1,), (1,)), ((), ())), preferred_element_type=F32, precision=precision)


def _dot_tn(a, b, precision=None):
    return lax.dot_general(a, b, (((0,), (0,)), ((), ())), preferred_element_type=F32, precision=precision)


def _bdot(a, b):
    return _dot(a.astype(BF16), b.astype(BF16))


def _masked_sum(mask3, x):
    hi = x.astype(BF16)
    r1 = x - hi.astype(F32)
    mid = r1.astype(BF16)
    lo = (r1 - mid.astype(F32)).astype(BF16)
    return _dot(mask3, jnp.concatenate([hi, mid, lo], axis=0))


def _triple(mask):
    return np.concatenate([mask, mask, mask], axis=1)


def _stack_heads(a, p):
    return jnp.concatenate([a[:, h * HEAD_DIM:(h + 1) * HEAD_DIM] for h in range(p * GROUP, (p + 1) * GROUP)],
                           axis=0)


def _each(f, *lists):
    return [f(*args) for args in zip(*lists)]


INPROJ_GROUP = 256


def _l2norm_heads(a):
    return jnp.concatenate(
        [a[:, h * HEAD_DIM:(h + 1) * HEAD_DIM]
         * lax.rsqrt(jnp.sum(jnp.square(a[:, h * HEAD_DIM:(h + 1) * HEAD_DIM]), axis=-1, keepdims=True) + 1e-6)
         for h in range(a.shape[1] // HEAD_DIM)], axis=1)


def _inproj_kernel(x_ref, nw_ref, wa_ref, wb_ref, wc_ref, cw_ref, oa_ref, ob_ref, oc_ref, ubuf, *, tiles_per_seq):
    tm = x_ref.shape[0]

    @pl.when(lax.rem(pl.program_id(0), tiles_per_seq) == 0)
    def _():
        ubuf[0:HIST, :] = jnp.zeros((HIST, GDN_QKV), F32)

    x = x_ref[...]
    ms = jnp.mean(x * x, axis=-1, keepdims=True)
    hn = (x * lax.rsqrt(ms + NORM_EPS) * nw_ref[...]).astype(BF16)
    cw = cw_ref[...]
    width = INPROJ_GROUP
    group = lambda g: slice(g * width, (g + 1) * width)

    def project(w_ref, o_ref, g, row0=0):
        o_ref[row0:row0 + tm, group(g)] = _dot(hn, w_ref[:, group(g)])

    def conv_group(g):
        cols = group(g)
        conv = cw[CONV_K - 1:CONV_K, cols] * ubuf[HIST:HIST + tm, cols]
        for j in range(1, CONV_K):
            conv = conv + cw[CONV_K - 1 - j:CONV_K - j, cols] * ubuf[HIST - j:HIST - j + tm, cols]
        ubuf[0:HIST, cols] = ubuf[tm:tm + HIST, cols]
        act = _silu(conv)
        if g * width < MIX_HALF:
            act = _l2norm_heads(act) * (HEAD_DIM ** -0.5)
        elif g * width < 2 * MIX_HALF:
            act = _l2norm_heads(act)
        ob_ref[:, cols] = act

    n_conv = GDN_QKV // width
    n_a = wa_ref.shape[1] // width
    others = [functools.partial(project, wa_ref, oa_ref, g) for g in range(n_a)]
    others += [functools.partial(project, wb_ref, ob_ref, g) for g in range(n_conv, wb_ref.shape[1] // width)]
    project(wb_ref, ubuf, 0, HIST)
    for g in range(n_conv):
        if g + 1 < n_conv:
            project(wb_ref, ubuf, g + 1, HIST)
        conv_group(g)
        others.pop(0)()
    for task in others:
        task()
    oc_ref[...] = _dot(hn, wc_ref[...])


def _inproj(x2, norm_w, wa, wb, wc, conv_w, tm, tiles_per_seq):
    t = x2.shape[0]
    na, nb, nc = wa.shape[1], wb.shape[1], wc.shape[1]
    const = lambda i: (0, 0)
    return pl.pallas_call(
        functools.partial(_inproj_kernel, tiles_per_seq=tiles_per_seq),
        grid=(t // tm,),
        in_specs=[
            pl.BlockSpec((tm, D_MODEL), lambda i: (i, 0)),
            pl.BlockSpec((1, D_MODEL), const),
            pl.BlockSpec((D_MODEL, na), const),
            pl.BlockSpec((D_MODEL, nb), const),
            pl.BlockSpec((D_MODEL, nc), const),
            pl.BlockSpec((CONV_K, GDN_QKV), const),
        ],
        out_specs=[
            pl.BlockSpec((tm, na), lambda i: (i, 0)),
            pl.BlockSpec((tm, nb), lambda i: (i, 0)),
            pl.BlockSpec((tm, nc), lambda i: (i, 0)),
        ],
        out_shape=[
            jax.ShapeDtypeStruct((t, na), F32),
            jax.ShapeDtypeStruct((t, nb), F32),
            jax.ShapeDtypeStruct((t, nc), F32),
        ],
        scratch_shapes=[pltpu.VMEM((HIST + tm, GDN_QKV), F32)],
        compiler_params=pltpu.CompilerParams(dimension_semantics=("arbitrary",), vmem_limit_bytes=VMEM_LIMIT),
        name="inproj",
    )(x2, norm_w, wa, wb, wc, conv_w)


HGRN_LEVELS = (32, 16, 8, 4, 2, 1)
DIAG_CODE = len(HGRN_LEVELS)
NONE_CODE = DIAG_CODE + 1


def _hgrn_arg_matrix():
    t = np.arange(CHUNK)[:, None]
    u = np.arange(CHUNK)[None, :]
    mats = [u <= t]
    for b in HGRN_LEVELS:
        odd = (t // b) % 2 == 1
        start = (t // b) * b
        mats.append(np.where(odd, (u > start) & (u <= t), (u > t) & (u <= start + b)))
    return np.concatenate(mats, axis=0).astype(np.float32)


def _hgrn_level_codes():
    idx = np.arange(STACK)
    h, t = idx // CHUNK, idx % CHUNK
    same = h[:, None] == h[None, :]
    tt, ss = t[:, None], t[None, :]
    code = np.full((STACK, STACK), NONE_CODE, np.int32)
    code[same & (tt == ss)] = DIAG_CODE
    for l, b in enumerate(HGRN_LEVELS):
        sib = (tt // (2 * b) == ss // (2 * b)) & ((tt // b) % 2 == 1) & ((ss // b) % 2 == 0)
        code[same & sib] = l
    return code


def _hgrn_kernel(q_ref, f_ref, i_ref, g_ref, lbl_ref, onw_ref, marg_ref, code_ref, o_ref, st_ref, *, n_chunks):
    @pl.when(pl.program_id(1) == 0)
    def _():
        st_ref[...] = jnp.zeros_like(st_ref)

    lbl = lbl_ref[...]
    lmax = jnp.max(lbl, axis=0, keepdims=True)
    lexp = jnp.exp(lbl - lmax)
    lb = lexp[0:1, :] / jnp.sum(lexp, axis=0, keepdims=True)
    onw = onw_ref[...]
    marg = marg_ref[...]
    code = code_ref[...]
    trow = lax.broadcasted_iota(jnp.int32, (STACK, HEAD_DIM), 0) & (CHUNK - 1)

    q_blk = _silu(q_ref[...]) * (HEAD_DIM ** -0.5)
    fg_blk = lb + (1.0 - lb) * _sigmoid(f_ref[...])
    k_blk = 1.0 - fg_blk
    lf_blk = jnp.log(fg_blk)
    v_blk = i_ref[...]

    rows = [slice(c * CHUNK, (c + 1) * CHUNK) for c in range(n_chunks)]
    args = [_masked_sum(marg, lf_blk[r]) for r in rows]
    e_chunk = [jnp.exp(a) for a in args]
    suf_chunk = [jnp.exp(a[CHUNK - 1:CHUNK] - a[0:CHUNK]) for a in args]
    units = [(c, p) for c in range(n_chunks) for p in range(N_STACKS)]
    qs = [_stack_heads(q_blk[rows[c]], p) for c, p in units]
    ks = [_stack_heads(k_blk[rows[c]], p) for c, p in units]
    vs = [_stack_heads(v_blk[rows[c]], p).astype(BF16) for c, p in units]
    e_part = lambda n: [_stack_heads(e_chunk[c][n * CHUNK:(n + 1) * CHUNK], p) for c, p in units]
    cum_e = e_part(0)
    suf_e = [_stack_heads(suf_chunk[c], p) for c, p in units]
    att = _each(lambda q, k: jnp.where(code == DIAG_CODE, jnp.sum(q * k, axis=-1, keepdims=True), 0.0), qs, ks)

    def level_update(a, x, l, b):
        if b < BF16_ROWS:
            return jnp.where(code == l, _dot_nt(x, x), a)
        blocks = [slice(s0, s0 + b) for s0 in range(0, STACK, b)]
        r = _dot_nt(jnp.concatenate([x[bl] for bl in blocks[1::2]], axis=0), x)
        return jnp.concatenate(
            [a[bl] if n % 2 == 0 else jnp.where(code[bl] == l, r[(n // 2) * b:(n // 2 + 1) * b], a[bl])
             for n, bl in enumerate(blocks)], axis=0)

    for l, b in enumerate(HGRN_LEVELS):
        x = _each(lambda q, k, el: (jnp.where((trow & b) != 0, q, k) * el).astype(BF16), qs, ks, e_part(1 + l))
        att = _each(lambda xi, a: level_update(a, xi, l, b), x, att)
    o_intra = _each(lambda a, v: _dot(a.astype(BF16), v), att, vs)
    qc = _each(lambda q, e: (q * e).astype(BF16), qs, cum_e)
    kd = _each(lambda k, e: (k * e).astype(BF16), ks, suf_e)

    state = [st_ref[h] for h in range(HEADS)]
    for u, (c, p) in enumerate(units):
        for i in range(GROUP):
            h = p * GROUP + i
            hr = slice(i * CHUNK, (i + 1) * CHUNK)
            hc = slice(h * HEAD_DIM, (h + 1) * HEAD_DIM)
            o = _dot_nt(qc[u][hr], state[h].astype(BF16)) + o_intra[u][hr]
            decay = cum_e[u][i * CHUNK + CHUNK - 1:i * CHUNK + CHUNK, :]
            state[h] = decay * state[h] + _dot_tn(vs[u][hr], kd[u][hr])
            y = o * lax.rsqrt(jnp.mean(o * o, axis=-1, keepdims=True) + NORM_EPS) * onw
            o_ref[rows[c], hc] = (y * _silu(g_ref[rows[c], hc])).astype(o_ref.dtype)
    for h in range(HEADS):
        st_ref[h] = state[h]


def _hgrn(pa, lb_logits, onorm_w, batch, seq, n_chunks):
    rows = n_chunks * CHUNK
    steps = seq // rows
    col = lambda j: pl.BlockSpec((rows, MIX_HALF), lambda b, s, j=j: (b * steps + s, j))
    const = lambda b, s: (0, 0)
    marg = jnp.asarray(_triple(_hgrn_arg_matrix()), BF16)
    code = jnp.asarray(_hgrn_level_codes())
    return pl.pallas_call(
        functools.partial(_hgrn_kernel, n_chunks=n_chunks),
        grid=(batch, steps),
        in_specs=[col(0), col(1), col(2), col(3),
                  pl.BlockSpec(lb_logits.shape, const),
                  pl.BlockSpec((1, HEAD_DIM), const),
                  pl.BlockSpec(marg.shape, const),
                  pl.BlockSpec(code.shape, const)],
        out_specs=pl.BlockSpec((rows, MIX_HALF), lambda b, s: (b * steps + s, 0)),
        out_shape=jax.ShapeDtypeStruct((batch * seq, MIX_HALF), BF16),
        scratch_shapes=[pltpu.VMEM((HEADS, HEAD_DIM, HEAD_DIM), F32)],
        compiler_params=pltpu.CompilerParams(dimension_semantics=("arbitrary", "arbitrary"),
                                             vmem_limit_bytes=VMEM_LIMIT),
        name="hgrn2",
    )(pa, pa, pa, pa, lb_logits, onorm_w, marg, code)


GDN_QKV = 3 * MIX_HALF
HIST = 8
C_NONE, C_DIAG, C_B16, C_B32, C_B64 = 0, 1, 2, 3, 4
MASKED_EXPONENT = -1e30


def _gdn_codes():
    idx = np.arange(STACK)
    h, t = idx // CHUNK, idx % CHUNK
    same = h[:, None] == h[None, :]
    tt, ss = t[:, None], t[None, :]
    code = np.full((STACK, STACK), C_NONE, np.int32)
    low = same & (ss < tt)
    code[low] = C_B64
    code[low & (tt // 32 == ss // 32)] = C_B32
    code[low & (tt // 16 == ss // 16)] = C_B16
    code[same & (tt == ss)] = C_DIAG
    incl = np.tril(np.ones((CHUNK, CHUNK), np.float32))
    return code, incl


def _gdn_kernel(qkv_ref, z_ref, ab_ref, alog_ref, dtb_ref, onw_ref, code_ref, incl_ref,
                o_ref, st_ref, *, n_chunks):
    @pl.when(pl.program_id(1) == 0)
    def _():
        st_ref[...] = jnp.zeros_like(st_ref)

    onw = onw_ref[...]
    code = code_ref[...]
    incl_m = incl_ref[...]
    incl = code >= C_DIAG
    eye = (code == C_DIAG).astype(F32)

    rows = [slice(c * CHUNK, (c + 1) * CHUNK) for c in range(n_chunks)]
    units = [(c, p) for c in range(n_chunks) for p in range(N_STACKS)]

    qs = [_stack_heads(qkv_ref[rows[c], 0:MIX_HALF], p) for c, p in units]
    ks = [_stack_heads(qkv_ref[rows[c], MIX_HALF:2 * MIX_HALF], p) for c, p in units]
    vs = [_stack_heads(qkv_ref[rows[c], 2 * MIX_HALF:3 * MIX_HALF], p) for c, p in units]
    ab_blk = ab_ref[...]
    xa = ab_blk + dtb_ref[...]
    softplus = jnp.maximum(xa, 0.0) + jnp.log(1.0 + jnp.exp(-jnp.abs(xa)))
    g_all = -jnp.exp(alog_ref[...]) * softplus
    beta_all = _sigmoid(ab_blk)

    def head_cols(a, c, p, first_lane):
        return jnp.concatenate([a[c][:, first_lane + h:first_lane + h + 1]
                                for h in range(p * GROUP, (p + 1) * GROUP)], axis=0)

    g_cum = [_masked_sum(incl_m, g_all[r]) for r in rows]
    beta_chunk = [beta_all[r] for r in rows]
    beta_st = [head_cols(beta_chunk, c, p, HEADS) for c, p in units]
    gc = [jnp.broadcast_to(head_cols(g_cum, c, p, 0), (STACK, HEAD_DIM)) for c, p in units]
    dec = [jnp.exp(jnp.where(incl, g - jnp.transpose(g)[0:1, :], MASKED_EXPONENT)) for g in gc]
    kb = [k.astype(BF16) for k in ks]
    kq = _each(lambda k, q: _dot_nt(jnp.concatenate([k, q.astype(BF16)], axis=0), k), kb, qs)
    kk = [a[0:STACK] for a in kq]
    qk = [a[STACK:] for a in kq]
    am = _each(lambda b, k2, d: b * k2 * d, beta_st, kk, dec)
    a16 = [jnp.where(code == C_B16, a, 0.0) for a in am]
    n32 = [jnp.where(code == C_B32, a, 0.0) for a in am]
    n64 = [jnp.where(code == C_B64, a, 0.0) for a in am]
    apow = _each(_bdot, a16, a16)
    pinv = [eye - a for a in a16]
    for _ in range(2):
        both = _each(lambda ai, pi: _bdot(jnp.concatenate([ai, pi], axis=0), ai), apow, pinv)
        pinv = _each(lambda pi, b: pi + b[STACK:], pinv, both)
        apow = [b[0:STACK] for b in both]
    pinv = _each(lambda pi, ai: pi + _bdot(pi, ai), pinv, apow)
    for nlev in (n32, n64):
        t = _each(_bdot, nlev, pinv)
        pinv = _each(lambda pi, ti: pi - _bdot(pi, ti), pinv, t)
    egc = [jnp.exp(g) for g in gc]
    rhs = _each(lambda k, v, b, e: jnp.concatenate([k * (b * e), v * b], axis=1), ks, vs, beta_st, egc)
    wu = _each(_bdot, pinv, rhs)
    w_c = [a[:, 0:HEAD_DIM].astype(BF16) for a in wu]
    u_c = [a[:, HEAD_DIM:] for a in wu]
    qkm = _each(lambda a, d: (a * d).astype(BF16), qk, dec)
    qg = _each(lambda q, e: (q * e).astype(BF16), qs, egc)
    glast = [[g[i * CHUNK + CHUNK - 1:(i + 1) * CHUNK, :] for i in range(GROUP)] for g in gc]
    kdec = [[(k[i * CHUNK:(i + 1) * CHUNK] * jnp.exp(gl[i] - g[i * CHUNK:(i + 1) * CHUNK])).astype(BF16)
             for i in range(GROUP)] for k, g, gl in zip(ks, gc, glast)]
    gend = [[jnp.exp(gi) for gi in gl] for gl in glast]

    state = [st_ref[h] for h in range(HEADS)]
    for u, (c, p) in enumerate(units):
        heads = range(p * GROUP, (p + 1) * GROUP)
        hrs = [slice(i * CHUNK, (i + 1) * CHUNK) for i in range(GROUP)]
        ws = [_dot_nt(jnp.concatenate([w_c[u][hr], qg[u][hr]], axis=0), state[h].astype(BF16))
              for hr, h in zip(hrs, heads)]
        v_new = jnp.concatenate([u_c[u][hr] - a[0:CHUNK] for hr, a in zip(hrs, ws)], axis=0).astype(BF16)
        o_st = jnp.concatenate([a[CHUNK:] for a in ws], axis=0) + _dot(qkm[u], v_new)
        for i, h in enumerate(heads):
            hc = slice(h * HEAD_DIM, (h + 1) * HEAD_DIM)
            state[h] = gend[u][i] * state[h] + _dot_tn(v_new[hrs[i]], kdec[u][i])
            o = o_st[hrs[i]]
            y = o * lax.rsqrt(jnp.mean(o * o, axis=-1, keepdims=True) + NORM_EPS) * onw
            o_ref[rows[c], hc] = (y * _silu(z_ref[rows[c], hc])).astype(o_ref.dtype)
    for h in range(HEADS):
        st_ref[h] = state[h]


def _gdn(pb, pc, alog_row, dtb_row, onorm_w, batch, seq, n_chunks):
    rows = n_chunks * CHUNK
    steps = seq // rows
    const = lambda b, s: (0, 0)
    code, incl = _gdn_codes()
    code, incl = jnp.asarray(code), jnp.asarray(_triple(incl), BF16)
    sq = pl.BlockSpec((STACK, STACK), const)
    row = pl.BlockSpec((1, LANES), const)
    return pl.pallas_call(
        functools.partial(_gdn_kernel, n_chunks=n_chunks),
        grid=(batch, steps),
        in_specs=[pl.BlockSpec((rows, GDN_QKV), lambda b, s: (b * steps + s, 0)),
                  pl.BlockSpec((rows, MIX_HALF), lambda b, s: (b * steps + s, GDN_QKV // MIX_HALF)),
                  pl.BlockSpec((rows, LANES), lambda b, s: (b * steps + s, 0)),
                  row, row, row, sq, pl.BlockSpec((CHUNK, 3 * CHUNK), const)],
        out_specs=pl.BlockSpec((rows, MIX_HALF), lambda b, s: (b * steps + s, 0)),
        out_shape=jax.ShapeDtypeStruct((batch * seq, MIX_HALF), BF16),
        scratch_shapes=[pltpu.VMEM((HEADS, HEAD_DIM, HEAD_DIM), F32)],
        compiler_params=pltpu.CompilerParams(dimension_semantics=("arbitrary", "arbitrary"),
                                             vmem_limit_bytes=VMEM_LIMIT),
        name="gdn",
    )(pb, pb, pc, alog_row, dtb_row, onorm_w, code, incl)


def _route_kernel(x_ref, ohg_ref, ogd_ref, wo_ref, nw_ref, wr_ref, ltri_ref,
                  h_ref, hn_ref, ri_ref, rg_ref, cnt_ref, cnt_scr):
    @pl.when(pl.program_id(0) == 0)
    def _():
        cnt_scr[...] = jnp.zeros_like(cnt_scr)

    mix = _dot(ohg_ref[...], wo_ref[0:MIX_HALF, :]) + _dot(ogd_ref[...], wo_ref[MIX_HALF:2 * MIX_HALF, :])
    h = x_ref[...] + mix
    h_ref[...] = h
    hn = h * lax.rsqrt(jnp.mean(h * h, axis=-1, keepdims=True) + NORM_EPS) * nw_ref[...]
    hn_ref[...] = _pack_words(hn)
    hn_hi = hn.astype(BF16)
    hn_lo = (hn - hn_hi.astype(F32)).astype(BF16)
    part = _dot(hn_hi, wr_ref[...])
    logits = (_dot(hn_lo, wr_ref[:, 0:LANES]) + part[:, LANES:]) + part[:, 0:LANES]
    tm = logits.shape[0]
    lane = lax.broadcasted_iota(jnp.int32, (tm, LANES), 1)
    neg = jnp.float32(-jnp.inf)
    big = jnp.int32(LANES)

    def first_max(vals):
        m = jnp.max(vals, axis=-1, keepdims=True)
        return m, jnp.min(jnp.where(vals == m, lane, big), axis=-1, keepdims=True)

    gl = jnp.where(lane < N_GROUPS, logits, neg)
    gmax, gidx = first_max(gl)
    p_group = 1.0 / jnp.sum(jnp.exp(gl - gmax), axis=-1, keepdims=True)
    lo = N_GROUPS + EXPERTS_PER_GROUP * gidx
    el = jnp.where((lane >= lo) & (lane < lo + EXPERTS_PER_GROUP), logits, neg)
    m1, i1 = first_max(el)
    m2, i2 = first_max(jnp.where(lane == i1, neg, el))
    r = jnp.exp(m2 - m1)
    gate1 = p_group / (1.0 + r)
    gate2 = p_group * r / (1.0 + r)
    hot1 = lane == i1
    hot2 = lane == i2
    onehot = jnp.where(hot1 | hot2, 1.0, 0.0)
    before = _dot(ltri_ref[...], onehot.astype(BF16)) + cnt_scr[...]
    rank1 = jnp.sum(jnp.where(hot1, before, 0.0), axis=-1, keepdims=True)
    rank2 = jnp.sum(jnp.where(hot2, before, 0.0), axis=-1, keepdims=True)
    cnt = cnt_scr[...] + jnp.sum(onehot, axis=0, keepdims=True)
    cnt_scr[...] = cnt
    cnt_ref[...] = cnt
    ri = jnp.where(lane == 0, i1 - N_GROUPS,
                   jnp.where(lane == 1, i2 - N_GROUPS,
                             jnp.where(lane == 2, rank1.astype(jnp.int32),
                                       jnp.where(lane == 3, rank2.astype(jnp.int32), 0))))
    ri_ref[...] = ri
    rg_ref[...] = jnp.where(lane == 0, gate1, jnp.where(lane == 1, gate2, 0.0))


def _route(x2, ohg, ogd, wo, norm_w, wr, tm):
    t = x2.shape[0]
    ltri = jnp.asarray(np.tril(np.ones((tm, tm), np.float32), -1), BF16)
    const = lambda i: (0, 0)
    tile = lambda n: pl.BlockSpec((tm, n), lambda i: (i, 0))
    return pl.pallas_call(
        _route_kernel,
        grid=(t // tm,),
        in_specs=[tile(D_MODEL), tile(MIX_HALF), tile(MIX_HALF),
                  pl.BlockSpec((D_MODEL, D_MODEL), const),
                  pl.BlockSpec((1, D_MODEL), const),
                  pl.BlockSpec((D_MODEL, 2 * LANES), const),
                  pl.BlockSpec((tm, tm), const)],
        out_specs=[tile(D_MODEL), tile(HALF), tile(LANES), tile(LANES),
                   pl.BlockSpec((1, LANES), const)],
        out_shape=[jax.ShapeDtypeStruct((t, D_MODEL), F32),
                   jax.ShapeDtypeStruct((t, HALF), U32),
                   jax.ShapeDtypeStruct((t, LANES), jnp.int32),
                   jax.ShapeDtypeStruct((t, LANES), F32),
                   jax.ShapeDtypeStruct((1, LANES), F32)],
        scratch_shapes=[pltpu.VMEM((1, LANES), F32)],
        compiler_params=pltpu.CompilerParams(dimension_semantics=("arbitrary",), vmem_limit_bytes=VMEM_LIMIT),
        name="route",
    )(x2, ohg, ogd, wo, norm_w, wr, ltri)


ISSUE_UNROLL = 8


HALF = D_MODEL // 2
PIECES = HALF // LANES
U32 = jnp.uint32
HIGH_HALF = np.uint32(0xFFFF0000)


def _pack_words(x):
    bits = lambda a: lax.bitcast_convert_type(a.astype(BF16).astype(F32), U32)
    return (bits(x[:, 0:HALF]) >> 16) | (bits(x[:, HALF:]) & HIGH_HALF)


def _unpack_words(words):
    return (lax.bitcast_convert_type(words << 16, F32), lax.bitcast_convert_type(words & HIGH_HALF, F32))


def _store_rows_as_tiles(ref, x):
    n = x.shape[0]
    words = _pack_words(x)
    for s in range(PIECES):
        ref[pl.ds(s, n, stride=PIECES), :] = words[:, s * LANES:(s + 1) * LANES]


def _load_rows_from_tiles(ref, n):
    return _unpack_words(jnp.concatenate([ref[pl.ds(s, n, stride=PIECES), :] for s in range(PIECES)], axis=1))


SC_WINDOW = 64


def _sc_permute_rows(rows, idx, gather):
    info = plsc.get_sparse_core_info()
    n_workers = info.num_cores * info.num_subcores
    n_idx, (n_src, width) = idx.shape[0], rows.shape
    per_worker = n_idx // n_workers
    n_win = per_worker // SC_WINDOW
    assert per_worker * n_workers == n_idx and n_win * SC_WINDOW == per_worker and n_win % 2 == 0
    assert gather or n_src % per_worker == 0
    idx2 = idx.reshape(n_idx // SC_WINDOW, SC_WINDOW)

    @functools.partial(
        pl.kernel,
        mesh=plsc.VectorSubcoreMesh(core_axis_name="c", subcore_axis_name="s"),
        out_type=jax.ShapeDtypeStruct((n_idx, width), rows.dtype),
        scratch_types=[pltpu.VMEM((n_win, SC_WINDOW), jnp.int32), pltpu.VMEM((2, SC_WINDOW, width), rows.dtype),
                       pltpu.SemaphoreType.DMA((2,)), pltpu.SemaphoreType.DMA((2,))],
    )
    def permute(rows_hbm, idx_hbm, out_hbm, idx_v, buf, fill_sem, drain_sem):
        worker = lax.axis_index("s") * info.num_cores + lax.axis_index("c")
        first_row = worker * per_worker
        pltpu.sync_copy(idx_hbm.at[pl.ds(worker * n_win, n_win)], idx_v)

        def window(ref, start, j):
            return ref.at[pl.ds(pl.multiple_of(start + j * SC_WINDOW, SC_WINDOW), SC_WINDOW)]

        def fill(j, b):
            src = rows_hbm.at[idx_v.at[j]] if gather else window(rows_hbm, lax.rem(first_row, n_src), j)
            return pltpu.make_async_copy(src, buf.at[b], fill_sem.at[b])

        def drain(j, b):
            dst = window(out_hbm, first_row, j) if gather else out_hbm.at[idx_v.at[j]]
            return pltpu.make_async_copy(buf.at[b], dst, drain_sem.at[b])

        fill(0, 0).start()

        @pl.loop(0, n_win, step=2)
        def _(j0):
            for b in range(2):
                j = j0 + b
                fill(j, b).wait()

                @pl.when(j >= 1)
                def _():
                    drain(j - 1, 1 - b).wait()

                @pl.when(j + 1 < n_win)
                def _():
                    fill(j + 1, 1 - b).start()

                drain(j, b).start()

        drain(n_win - 1, 1).wait()

    return permute(rows, idx2)


def _sc_gather_rows(table, idx):
    return _sc_permute_rows(table, idx, gather=True)


def _sc_scatter_rows(rows, idx):
    return _sc_permute_rows(rows, idx, gather=False)


def _row_copy(src_ref, src_row, dst_ref, dst_row, sem):
    src = src_ref.at[pl.ds(pl.multiple_of(src_row * PIECES, PIECES), PIECES)]
    dst = dst_ref.at[pl.ds(pl.multiple_of(dst_row * PIECES, PIECES), PIECES)]
    return pltpu.make_async_copy(src, dst, sem)


def _dispatch_kernel(starts_ref, idx_ref, hn_ref, xs_ref, sem, *, tm):
    def issue(t, carry):
        for k in range(TOP_K):
            dest = starts_ref[idx_ref[0, 0, 4 * t + k]] + idx_ref[0, 0, 4 * t + 2 + k]
            _row_copy(hn_ref, t, xs_ref, dest, sem).start(priority=k % 2)
        return carry

    lax.fori_loop(0, tm, issue, 0, unroll=ISSUE_UNROLL)
    for k in range(TOP_K):
        pltpu.make_async_copy(hn_ref, xs_ref.at[pl.ds(0, tm * PIECES)], sem).wait()


def _dispatch(starts, idx3, hn, tm):
    t = hn.shape[0] // PIECES
    return pl.pallas_call(
        functools.partial(_dispatch_kernel, tm=tm),
        grid_spec=pltpu.PrefetchScalarGridSpec(
            num_scalar_prefetch=1,
            grid=(t // tm,),
            in_specs=[pl.BlockSpec((1, 1, 4 * tm), lambda i, s: (i, 0, 0), memory_space=pltpu.SMEM),
                      pl.BlockSpec((tm * PIECES, LANES), lambda i, s: (i, 0))],
            out_specs=pl.BlockSpec(memory_space=pl.ANY),
            scratch_shapes=[pltpu.SemaphoreType.DMA(())],
        ),
        out_shape=jax.ShapeDtypeStruct((t * TOP_K * PIECES, LANES), U32),
        compiler_params=pltpu.CompilerParams(dimension_semantics=("arbitrary",), vmem_limit_bytes=VMEM_LIMIT),
        name="dispatch",
    )(starts, idx3, hn)


EXPERT_SPLIT = 1


def _expert_kernel(blk_ref, exp_ref, lo_ref, hi_ref, xs_ref, wg_ref, wu_ref, wd_ref, ys_ref, wgu_b, wd_b, acc):
    p = pl.program_id(0)

    @pl.when(jnp.logical_or(p == 0, exp_ref[p] != exp_ref[jnp.maximum(p - 1, 0)]))
    def _():
        wgu_b[:, 0:EXPERT_FF] = wg_ref[0].astype(BF16)
        wgu_b[:, EXPERT_FF:2 * EXPERT_FF] = wu_ref[0].astype(BF16)
        wd_b[...] = wd_ref[0].astype(BF16)

    @pl.when(p == 0)
    def _():
        acc[...] = jnp.zeros_like(acc)

    first = jnp.logical_or(p == 0, blk_ref[p] != blk_ref[jnp.maximum(p - 1, 0)])
    sub = SLOT_BLOCK // EXPERT_SPLIT
    for r in range(EXPERT_SPLIT):
        x_lo, x_hi = _unpack_words(xs_ref[pl.ds(r * sub, sub), :])
        ab = _dot(x_lo.astype(BF16), wgu_b[0:HALF, :]) + _dot(x_hi.astype(BF16), wgu_b[HALF:D_MODEL, :])
        hb = _silu(ab[:, 0:EXPERT_FF]) * ab[:, EXPERT_FF:2 * EXPERT_FF]
        y = _dot(hb.astype(BF16), wd_b[...])
        slot = blk_ref[p] * SLOT_BLOCK + r * sub + lax.broadcasted_iota(jnp.int32, (sub, 1), 0)
        y = jnp.where((slot >= lo_ref[p]) & (slot < hi_ref[p]), y, 0.0)
        total = y + jnp.where(first, 0.0, acc[pl.ds(r * sub, sub), :])
        acc[pl.ds(r * sub, sub), :] = total
        ys_ref[pl.ds(r * sub, sub), :] = _pack_words(total)


def _experts(blk, exp, lo, hi, xs, wg, wu, wd):
    n_pairs = blk.shape[0]
    return pl.pallas_call(
        _expert_kernel,
        grid_spec=pltpu.PrefetchScalarGridSpec(
            num_scalar_prefetch=4,
            grid=(n_pairs,),
            in_specs=[pl.BlockSpec((SLOT_BLOCK, HALF), lambda p, blk, exp, lo, hi: (blk[p], 0)),
                      pl.BlockSpec((1, D_MODEL, EXPERT_FF), lambda p, blk, exp, lo, hi: (exp[p], 0, 0)),
                      pl.BlockSpec((1, D_MODEL, EXPERT_FF), lambda p, blk, exp, lo, hi: (exp[p], 0, 0)),
                      pl.BlockSpec((1, EXPERT_FF, D_MODEL), lambda p, blk, exp, lo, hi: (exp[p], 0, 0))],
            out_specs=pl.BlockSpec((SLOT_BLOCK, HALF), lambda p, blk, exp, lo, hi: (blk[p], 0)),
            scratch_shapes=[pltpu.VMEM((D_MODEL, 2 * EXPERT_FF), BF16), pltpu.VMEM((EXPERT_FF, D_MODEL), BF16),
                            pltpu.VMEM((SLOT_BLOCK, D_MODEL), F32)],
        ),
        out_shape=jax.ShapeDtypeStruct(xs.shape, U32),
        compiler_params=pltpu.CompilerParams(dimension_semantics=("arbitrary",), vmem_limit_bytes=VMEM_LIMIT),
        name="experts",
    )(blk, exp, lo, hi, xs, wg, wu, wd)


def _combine_kernel(h_ref, rg_ref, fw_ref, y0_ref, y1_ref, o_ref):
    rg = rg_ref[...]
    y = [jnp.concatenate(_unpack_words(ref[...]), axis=1) for ref in (y0_ref, y1_ref)]
    h = h_ref[...] + rg[:, 0:1] * y[0] + rg[:, 1:2] * y[1]
    o_ref[...] = h * lax.rsqrt(jnp.mean(h * h, axis=-1, keepdims=True) + NORM_EPS) * fw_ref[...]


def _combine(h, rg, final_w, y2, tm):
    t = h.shape[0]
    n_tiles = t // tm
    return pl.pallas_call(
        _combine_kernel,
        grid=(n_tiles,),
        in_specs=[pl.BlockSpec((tm, D_MODEL), lambda i: (i, 0)),
                  pl.BlockSpec((tm, LANES), lambda i: (i, 0)),
                  pl.BlockSpec((1, D_MODEL), lambda i: (0, 0)),
                  pl.BlockSpec((tm, HALF), lambda i: (i, 0)),
                  pl.BlockSpec((tm, HALF), lambda i: (n_tiles + i, 0))],
        out_specs=pl.BlockSpec((tm, D_MODEL), lambda i: (i, 0)),
        out_shape=jax.ShapeDtypeStruct((t, D_MODEL), F32),
        compiler_params=pltpu.CompilerParams(dimension_semantics=("arbitrary",), vmem_limit_bytes=VMEM_LIMIT),
        name="combine",
    )(h, rg, final_w, y2, y2)


def _pair_schedule(counts, n_slots):
    n_blocks = n_slots // SLOT_BLOCK
    n_pairs = n_blocks + N_EXPERTS - 1
    ends = jnp.cumsum(counts)
    starts = ends - counts
    first = starts // SLOT_BLOCK
    last = jnp.maximum(ends - 1, starts) // SLOT_BLOCK
    per_expert = jnp.where(counts > 0, last - first + 1, 0)
    cum = jnp.cumsum(per_expert)
    p = jnp.arange(n_pairs, dtype=jnp.int32)
    e = jnp.minimum(jnp.sum(cum[None, :] <= p[:, None], axis=1), N_EXPERTS - 1).astype(jnp.int32)
    valid = p < cum[-1]
    onehot = e[:, None] == jnp.arange(N_EXPERTS, dtype=jnp.int32)[None, :]
    pick = lambda table: jnp.sum(jnp.where(onehot, table[None, :], 0), axis=1)
    blk = jnp.where(valid, pick(first) + p - pick(cum - per_expert), n_blocks - 1).astype(jnp.int32)
    lo = jnp.where(valid, pick(starts), 0).astype(jnp.int32)
    hi = jnp.where(valid, pick(ends), 0).astype(jnp.int32)
    return starts.astype(jnp.int32), blk, e, lo, hi


def _layer(x, norm_mix_w, w_in, lb_logits, hgrn_onorm_w, gdn_conv_w, gdn_a_log, gdn_dt_bias, gdn_onorm_w, w_out,
           norm_moe_w, router_group_w, router_expert_w, w_gate, w_up, w_down, final_w, *, tm, n_chunks):
    batch, seq, _ = x.shape
    t = batch * seq
    x2 = x.reshape(t, D_MODEL)
    n_hg = 4 * MIX_HALF
    n_gd = GDN_QKV + MIX_HALF
    wb16 = w_in.astype(BF16)
    wa = wb16[:, 0:n_hg]
    wb = wb16[:, n_hg:n_hg + n_gd]
    wc = jnp.pad(wb16[:, n_hg + n_gd:], ((0, 0), (0, LANES - 2 * HEADS)))
    pa, pb, pc = _inproj(x2, norm_mix_w.reshape(1, D_MODEL), wa, wb, wc, gdn_conv_w, 2 * tm, seq // (2 * tm))

    o_hg = _hgrn(pa, lb_logits, hgrn_onorm_w.reshape(1, HEAD_DIM), batch, seq, n_chunks)
    alog_row = jnp.pad(gdn_a_log.reshape(1, HEADS), ((0, 0), (0, LANES - HEADS)))
    dtb_row = jnp.pad(gdn_dt_bias.reshape(1, HEADS), ((0, 0), (0, LANES - HEADS)))
    o_gd = _gdn(pb, pc, alog_row, dtb_row, gdn_onorm_w.reshape(1, HEAD_DIM), batch, seq, n_chunks)

    wr = jnp.pad(jnp.concatenate([router_group_w, router_expert_w], axis=1),
                 ((0, 0), (0, LANES - N_GROUPS - N_EXPERTS)))
    wr_hi = wr.astype(BF16)
    wr = jnp.concatenate([wr_hi, (wr - wr_hi.astype(F32)).astype(BF16)], axis=1)
    h, hn, ri, rg, cnt = _route(x2, o_hg, o_gd, w_out.astype(BF16), norm_moe_w.reshape(1, D_MODEL), wr, 2 * tm)

    counts = cnt[0, N_GROUPS:N_GROUPS + N_EXPERTS].astype(jnp.int32)
    starts, blk, exp, lo, hi = _pair_schedule(counts, t * TOP_K)
    experts_kt = jnp.transpose(ri[:, 0:TOP_K])
    onehot = experts_kt[:, :, None] == jnp.arange(N_EXPERTS, dtype=jnp.int32)
    slot_kt = jnp.sum(jnp.where(onehot, starts, 0), axis=-1) + jnp.transpose(ri[:, TOP_K:2 * TOP_K])
    slots = slot_kt.reshape(TOP_K * t)
    xs = _sc_scatter_rows(hn, slots)
    ys = _experts(blk, exp, lo, hi, xs, w_gate, w_up, w_down)
    y2 = _sc_gather_rows(ys, slots)
    out = _combine(h, rg, final_w.reshape(1, D_MODEL), y2, 2 * tm)
    return out.reshape(batch, seq, D_MODEL)


def kernel(x, norm_mix_w, w_in, hgrn_lb_logits, hgrn_onorm_w, gdn_conv_w, gdn_a_log, gdn_dt_bias, gdn_onorm_w, w_out, norm_moe_w, router_group_w, router_expert_w, expert_w_gate, expert_w_up, expert_w_down, final_norm_w):
    return _layer(x, norm_mix_w[0], w_in[0], hgrn_lb_logits, hgrn_onorm_w[0], gdn_conv_w[0], gdn_a_log[0],
                  gdn_dt_bias[0], gdn_onorm_w[0], w_out[0], norm_moe_w[0], router_group_w[0], router_expert_w[0],
                  expert_w_gate[0], expert_w_up[0], expert_w_down[0], final_norm_w, tm=256, n_chunks=8)
```

```python
import functools

import numpy as np
import jax
import jax.numpy as jnp
from jax import lax
from jax.experimental import pallas as pl
from jax.experimental.pallas import tpu as pltpu
from jax.experimental.pallas import tpu_sc as plsc

F32 = jnp.float32
BF16 = jnp.bfloat16
HI = lax.Precision.HIGHEST

D_MODEL = 1024
HEADS = 4
HEAD_DIM = 128
MIX_HALF = HEADS * HEAD_DIM
CHUNK = 64
GROUP = 2
N_STACKS = HEADS // GROUP
STACK = GROUP * CHUNK
CONV_K = 4
N_GROUPS = 8
EXPERTS_PER_GROUP = 8
N_EXPERTS = N_GROUPS * EXPERTS_PER_GROUP
TOP_K = 2
EXPERT_FF = 256
NORM_EPS = 1e-6
LANES = 128
BF16_ROWS = 16
SLOT_BLOCK = 512
VMEM_LIMIT = 56 * 1024 * 1024


def _sigmoid(x):
    return 0.5 * jnp.tanh(0.5 * x) + 0.5


def _silu(x):
    return x * _sigmoid(x)


def _dot(a, b, precision=None):
    return jnp.dot(a, b, preferred_element_type=F32, precision=precision)


def _dot_nt(a, b, precision=None):
    return lax.dot_general(a, b, (((1,), (1,)), ((), ())), preferred_element_type=F32, precision=precision)


def _dot_tn(a, b, precision=None):
    return lax.dot_general(a, b, (((0,), (0,)), ((), ())), preferred_element_type=F32, precision=precision)


def _bdot(a, b):
    return _dot(a.astype(BF16), b.astype(BF16))


def _masked_sum(mask3, x):
    hi = x.astype(BF16)
    r1 = x - hi.astype(F32)
    mid = r1.astype(BF16)
    lo = (r1 - mid.astype(F32)).astype(BF16)
    return _dot(mask3, jnp.concatenate([hi, mid, lo], axis=0))


def _triple(mask):
    return np.concatenate([mask, mask, mask], axis=1)


def _stack_heads(a, p):
    return jnp.concatenate([a[:, h * HEAD_DIM:(h + 1) * HEAD_DIM] for h in range(p * GROUP, (p + 1) * GROUP)],
                           axis=0)


def _each(f, *lists):
    return [f(*args) for args in zip(*lists)]


INPROJ_GROUP = 256


def _l2norm_heads(a):
    return jnp.concatenate(
        [a[:, h * HEAD_DIM:(h + 1) * HEAD_DIM]
         * lax.rsqrt(jnp.sum(jnp.square(a[:, h * HEAD_DIM:(h + 1) * HEAD_DIM]), axis=-1, keepdims=True) + 1e-6)
         for h in range(a.shape[1] // HEAD_DIM)], axis=1)


def _inproj_kernel(x_ref, nw_ref, wa_ref, wb_ref, wc_ref, cw_ref, oa_ref, ob_ref, oc_ref, ubuf, *, tiles_per_seq):
    tm = x_ref.shape[0]

    @pl.when(lax.rem(pl.program_id(0), tiles_per_seq) == 0)
    def _():
        ubuf[0:HIST, :] = jnp.zeros((HIST, GDN_QKV), F32)

    x = x_ref[...]
    ms = jnp.mean(x * x, axis=-1, keepdims=True)
    hn = (x * lax.rsqrt(ms + NORM_EPS) * nw_ref[...]).astype(BF16)
    cw = cw_ref[...]
    width = INPROJ_GROUP
    group = lambda g: slice(g * width, (g + 1) * width)

    def project(w_ref, o_ref, g, row0=0):
        o_ref[row0:row0 + tm, group(g)] = _dot(hn, w_ref[:, group(g)])

    def conv_group(g):
        cols = group(g)
        conv = cw[CONV_K - 1:CONV_K, cols] * ubuf[HIST:HIST + tm, cols]
        for j in range(1, CONV_K):
            conv = conv + cw[CONV_K - 1 - j:CONV_K - j, cols] * ubuf[HIST - j:HIST - j + tm, cols]
        ubuf[0:HIST, cols] = ubuf[tm:tm + HIST, cols]
        act = _silu(conv)
        if g * width < MIX_HALF:
            act = _l2norm_heads(act) * (HEAD_DIM ** -0.5)
        elif g * width < 2 * MIX_HALF:
            act = _l2norm_heads(act)
        ob_ref[:, cols] = act

    n_conv = GDN_QKV // width
    n_a = wa_ref.shape[1] // width
    others = [functools.partial(project, wa_ref, oa_ref, g) for g in range(n_a)]
    others += [functools.partial(project, wb_ref, ob_ref, g) for g in range(n_conv, wb_ref.shape[1] // width)]
    project(wb_ref, ubuf, 0, HIST)
    for g in range(n_conv):
        if g + 1 < n_conv:
            project(wb_ref, ubuf, g + 1, HIST)
        conv_group(g)
        others.pop(0)()
    for task in others:
        task()
    oc_ref[...] = _dot(hn, wc_ref[...])


def _inproj(x2, norm_w, wa, wb, wc, conv_w, tm, tiles_per_seq):
    t = x2.shape[0]
    na, nb, nc = wa.shape[1], wb.shape[1], wc.shape[1]
    const = lambda i: (0, 0)
    return pl.pallas_call(
        functools.partial(_inproj_kernel, tiles_per_seq=tiles_per_seq),
        grid=(t // tm,),
        in_specs=[
            pl.BlockSpec((tm, D_MODEL), lambda i: (i, 0)),
            pl.BlockSpec((1, D_MODEL), const),
            pl.BlockSpec((D_MODEL, na), const),
            pl.BlockSpec((D_MODEL, nb), const),
            pl.BlockSpec((D_MODEL, nc), const),
            pl.BlockSpec((CONV_K, GDN_QKV), const),
        ],
        out_specs=[
            pl.BlockSpec((tm, na), lambda i: (i, 0)),
            pl.BlockSpec((tm, nb), lambda i: (i, 0)),
            pl.BlockSpec((tm, nc), lambda i: (i, 0)),
        ],
        out_shape=[
            jax.ShapeDtypeStruct((t, na), F32),
            jax.ShapeDtypeStruct((t, nb), F32),
            jax.ShapeDtypeStruct((t, nc), F32),
        ],
        scratch_shapes=[pltpu.VMEM((HIST + tm, GDN_QKV), F32)],
        compiler_params=pltpu.CompilerParams(dimension_semantics=("arbitrary",), vmem_limit_bytes=VMEM_LIMIT),
        name="inproj",
    )(x2, norm_w, wa, wb, wc, conv_w)


HGRN_LEVELS = (32, 16, 8, 4, 2, 1)
DIAG_CODE = len(HGRN_LEVELS)
NONE_CODE = DIAG_CODE + 1


def _hgrn_arg_matrix():
    t = np.arange(CHUNK)[:, None]
    u = np.arange(CHUNK)[None, :]
    mats = [u <= t]
    for b in HGRN_LEVELS:
        odd = (t // b) % 2 == 1
        start = (t // b) * b
        mats.append(np.where(odd, (u > start) & (u <= t), (u > t) & (u <= start + b)))
    return np.concatenate(mats, axis=0).astype(np.float32)


def _hgrn_level_codes():
    idx = np.arange(STACK)
    h, t = idx // CHUNK, idx % CHUNK
    same = h[:, None] == h[None, :]
    tt, ss = t[:, None], t[None, :]
    code = np.full((STACK, STACK), NONE_CODE, np.int32)
    code[same & (tt == ss)] = DIAG_CODE
    for l, b in enumerate(HGRN_LEVELS):
        sib = (tt // (2 * b) == ss // (2 * b)) & ((tt // b) % 2 == 1) & ((ss // b) % 2 == 0)
        code[same & sib] = l
    return code


def _hgrn_kernel(q_ref, f_ref, i_ref, g_ref, lbl_ref, onw_ref, marg_ref, code_ref, o_ref, st_ref, *, n_chunks):
    @pl.when(pl.program_id(1) == 0)
    def _():
        st_ref[...] = jnp.zeros_like(st_ref)

    lbl = lbl_ref[...]
    lmax = jnp.max(lbl, axis=0, keepdims=True)
    lexp = jnp.exp(lbl - lmax)
    lb = lexp[0:1, :] / jnp.sum(lexp, axis=0, keepdims=True)
    onw = onw_ref[...]
    marg = marg_ref[...]
    code = code_ref[...]
    trow = lax.broadcasted_iota(jnp.int32, (STACK, HEAD_DIM), 0) & (CHUNK - 1)

    q_blk = _silu(q_ref[...]) * (HEAD_DIM ** -0.5)
    fg_blk = lb + (1.0 - lb) * _sigmoid(f_ref[...])
    k_blk = 1.0 - fg_blk
    lf_blk = jnp.log(fg_blk)
    v_blk = i_ref[...]

    rows = [slice(c * CHUNK, (c + 1) * CHUNK) for c in range(n_chunks)]
    args = [_masked_sum(marg, lf_blk[r]) for r in rows]
    e_chunk = [jnp.exp(a) for a in args]
    suf_chunk = [jnp.exp(a[CHUNK - 1:CHUNK] - a[0:CHUNK]) for a in args]
    units = [(c, p) for c in range(n_chunks) for p in range(N_STACKS)]
    qs = [_stack_heads(q_blk[rows[c]], p) for c, p in units]
    ks = [_stack_heads(k_blk[rows[c]], p) for c, p in units]
    vs = [_stack_heads(v_blk[rows[c]], p).astype(BF16) for c, p in units]
    e_part = lambda n: [_stack_heads(e_chunk[c][n * CHUNK:(n + 1) * CHUNK], p) for c, p in units]
    cum_e = e_part(0)
    suf_e = [_stack_heads(suf_chunk[c], p) for c, p in units]
    att = _each(lambda q, k: jnp.where(code == DIAG_CODE, jnp.sum(q * k, axis=-1, keepdims=True), 0.0), qs, ks)

    def level_update(a, x, l, b):
        if b < BF16_ROWS:
            return jnp.where(code == l, _dot_nt(x, x), a)
        blocks = [slice(s0, s0 + b) for s0 in range(0, STACK, b)]
        r = _dot_nt(jnp.concatenate([x[bl] for bl in blocks[1::2]], axis=0), x)
        return jnp.concatenate(
            [a[bl] if n % 2 == 0 else jnp.where(code[bl] == l, r[(n // 2) * b:(n // 2 + 1) * b], a[bl])
             for n, bl in enumerate(blocks)], axis=0)

    for l, b in enumerate(HGRN_LEVELS):
        x = _each(lambda q, k, el: (jnp.where((trow & b) != 0, q, k) * el).astype(BF16), qs, ks, e_part(1 + l))
        att = _each(lambda xi, a: level_update(a, xi, l, b), x, att)
    o_intra = _each(lambda a, v: _dot(a.astype(BF16), v), att, vs)
    qc = _each(lambda q, e: (q * e).astype(BF16), qs, cum_e)
    kd = _each(lambda k, e: (k * e).astype(BF16), ks, suf_e)

    state = [st_ref[h] for h in range(HEADS)]
    for u, (c, p) in enumerate(units):
        for i in range(GROUP):
            h = p * GROUP + i
            hr = slice(i * CHUNK, (i + 1) * CHUNK)
            hc = slice(h * HEAD_DIM, (h + 1) * HEAD_DIM)
            o = _dot_nt(qc[u][hr], state[h].astype(BF16)) + o_intra[u][hr]
            decay = cum_e[u][i * CHUNK + CHUNK - 1:i * CHUNK + CHUNK, :]
            state[h] = decay * state[h] + _dot_tn(vs[u][hr], kd[u][hr])
            y = o * lax.rsqrt(jnp.mean(o * o, axis=-1, keepdims=True) + NORM_EPS) * onw
            o_ref[rows[c], hc] = (y * _silu(g_ref[rows[c], hc])).astype(o_ref.dtype)
    for h in range(HEADS):
        st_ref[h] = state[h]


def _hgrn(pa, lb_logits, onorm_w, batch, seq, n_chunks):
    rows = n_chunks * CHUNK
    steps = seq // rows
    col = lambda j: pl.BlockSpec((rows, MIX_HALF), lambda b, s, j=j: (b * steps + s, j))
    const = lambda b, s: (0, 0)
    marg = jnp.asarray(_triple(_hgrn_arg_matrix()), BF16)
    code = jnp.asarray(_hgrn_level_codes())
    return pl.pallas_call(
        functools.partial(_hgrn_kernel, n_chunks=n_chunks),
        grid=(batch, steps),
        in_specs=[col(0), col(1), col(2), col(3),
                  pl.BlockSpec(lb_logits.shape, const),
                  pl.BlockSpec((1, HEAD_DIM), const),
                  pl.BlockSpec(marg.shape, const),
                  pl.BlockSpec(code.shape, const)],
        out_specs=pl.BlockSpec((rows, MIX_HALF), lambda b, s: (b * steps + s, 0)),
        out_shape=jax.ShapeDtypeStruct((batch * seq, MIX_HALF), BF16),
        scratch_shapes=[pltpu.VMEM((HEADS, HEAD_DIM, HEAD_DIM), F32)],
        compiler_params=pltpu.CompilerParams(dimension_semantics=("arbitrary", "arbitrary"),
                                             vmem_limit_bytes=VMEM_LIMIT),
        name="hgrn2",
    )(pa, pa, pa, pa, lb_logits, onorm_w, marg, code)


GDN_QKV = 3 * MIX_HALF
HIST = 8
C_NONE, C_DIAG, C_B16, C_B32, C_B64 = 0, 1, 2, 3, 4
MASKED_EXPONENT = -1e30


def _gdn_codes():
    idx = np.arange(STACK)
    h, t = idx // CHUNK, idx % CHUNK
    same = h[:, None] == h[None, :]
    tt, ss = t[:, None], t[None, :]
    code = np.full((STACK, STACK), C_NONE, np.int32)
    low = same & (ss < tt)
    code[low] = C_B64
    code[low & (tt // 32 == ss // 32)] = C_B32
    code[low & (tt // 16 == ss // 16)] = C_B16
    code[same & (tt == ss)] = C_DIAG
    incl = np.tril(np.ones((CHUNK, CHUNK), np.float32))
    return code, incl


def _gdn_kernel(qkv_ref, z_ref, ab_ref, alog_ref, dtb_ref, onw_ref, code_ref, incl_ref,
                o_ref, st_ref, *, n_chunks):
    @pl.when(pl.program_id(1) == 0)
    def _():
        st_ref[...] = jnp.zeros_like(st_ref)

    onw = onw_ref[...]
    code = code_ref[...]
    incl_m = incl_ref[...]
    incl = code >= C_DIAG
    eye = (code == C_DIAG).astype(F32)

    rows = [slice(c * CHUNK, (c + 1) * CHUNK) for c in range(n_chunks)]
    units = [(c, p) for c in range(n_chunks) for p in range(N_STACKS)]

    qs = [_stack_heads(qkv_ref[rows[c], 0:MIX_HALF], p) for c, p in units]
    ks = [_stack_heads(qkv_ref[rows[c], MIX_HALF:2 * MIX_HALF], p) for c, p in units]
    vs = [_stack_heads(qkv_ref[rows[c], 2 * MIX_HALF:3 * MIX_HALF], p) for c, p in units]
    ab_blk = ab_ref[...]
    xa = ab_blk + dtb_ref[...]
    softplus = jnp.maximum(xa, 0.0) + jnp.log(1.0 + jnp.exp(-jnp.abs(xa)))
    g_all = -jnp.exp(alog_ref[...]) * softplus
    beta_all = _sigmoid(ab_blk)

    def head_cols(a, c, p, first_lane):
        return jnp.concatenate([a[c][:, first_lane + h:first_lane + h + 1]
                                for h in range(p * GROUP, (p + 1) * GROUP)], axis=0)

    g_cum = [_masked_sum(incl_m, g_all[r]) for r in rows]
    beta_chunk = [beta_all[r] for r in rows]
    beta_st = [head_cols(beta_chunk, c, p, HEADS) for c, p in units]
    gc = [jnp.broadcast_to(head_cols(g_cum, c, p, 0), (STACK, HEAD_DIM)) for c, p in units]
    dec = [jnp.exp(jnp.where(incl, g - jnp.transpose(g)[0:1, :], MASKED_EXPONENT)) for g in gc]
    kb = [k.astype(BF16) for k in ks]
    kq = _each(lambda k, q: _dot_nt(jnp.concatenate([k, q.astype(BF16)], axis=0), k), kb, qs)
    kk = [a[0:STACK] for a in kq]
    qk = [a[STACK:] for a in kq]
    am = _each(lambda b, k2, d: b * k2 * d, beta_st, kk, dec)
    a16 = [jnp.where(code == C_B16, a, 0.0) for a in am]
    n32 = [jnp.where(code == C_B32, a, 0.0) for a in am]
    n64 = [jnp.where(code == C_B64, a, 0.0) for a in am]
    apow = _each(_bdot, a16, a16)
    pinv = [eye - a for a in a16]
    for _ in range(2):
        both = _each(lambda ai, pi: _bdot(jnp.concatenate([ai, pi], axis=0), ai), apow, pinv)
        pinv = _each(lambda pi, b: pi + b[STACK:], pinv, both)
        apow = [b[0:STACK] for b in both]
    pinv = _each(lambda pi, ai: pi + _bdot(pi, ai), pinv, apow)
    for nlev in (n32, n64):
        t = _each(_bdot, nlev, pinv)
        pinv = _each(lambda pi, ti: pi - _bdot(pi, ti), pinv, t)
    egc = [jnp.exp(g) for g in gc]
    rhs = _each(lambda k, v, b, e: jnp.concatenate([k * (b * e), v * b], axis=1), ks, vs, beta_st, egc)
    wu = _each(_bdot, pinv, rhs)
    w_c = [a[:, 0:HEAD_DIM].astype(BF16) for a in wu]
    u_c = [a[:, HEAD_DIM:] for a in wu]
    qkm = _each(lambda a, d: (a * d).astype(BF16), qk, dec)
    qg = _each(lambda q, e: (q * e).astype(BF16), qs, egc)
    glast = [[g[i * CHUNK + CHUNK - 1:(i + 1) * CHUNK, :] for i in range(GROUP)] for g in gc]
    kdec = [[(k[i * CHUNK:(i + 1) * CHUNK] * jnp.exp(gl[i] - g[i * CHUNK:(i + 1) * CHUNK])).astype(BF16)
             for i in range(GROUP)] for k, g, gl in zip(ks, gc, glast)]
    gend = [[jnp.exp(gi) for gi in gl] for gl in glast]

    state = [st_ref[h] for h in range(HEADS)]
    for u, (c, p) in enumerate(units):
        heads = range(p * GROUP, (p + 1) * GROUP)
        hrs = [slice(i * CHUNK, (i + 1) * CHUNK) for i in range(GROUP)]
        ws = [_dot_nt(jnp.concatenate([w_c[u][hr], qg[u][hr]], axis=0), state[h].astype(BF16))
              for hr, h in zip(hrs, heads)]
        v_new = jnp.concatenate([u_c[u][hr] - a[0:CHUNK] for hr, a in zip(hrs, ws)], axis=0).astype(BF16)
        o_st = jnp.concatenate([a[CHUNK:] for a in ws], axis=0) + _dot(qkm[u], v_new)
        for i, h in enumerate(heads):
            hc = slice(h * HEAD_DIM, (h + 1) * HEAD_DIM)
            state[h] = gend[u][i] * state[h] + _dot_tn(v_new[hrs[i]], kdec[u][i])
            o = o_st[hrs[i]]
            y = o * lax.rsqrt(jnp.mean(o * o, axis=-1, keepdims=True) + NORM_EPS) * onw
            o_ref[rows[c], hc] = (y * _silu(z_ref[rows[c], hc])).astype(o_ref.dtype)
    for h in range(HEADS):
        st_ref[h] = state[h]


def _gdn(pb, pc, alog_row, dtb_row, onorm_w, batch, seq, n_chunks):
    rows = n_chunks * CHUNK
    steps = seq // rows
    const = lambda b, s: (0, 0)
    code, incl = _gdn_codes()
    code, incl = jnp.asarray(code), jnp.asarray(_triple(incl), BF16)
    sq = pl.BlockSpec((STACK, STACK), const)
    row = pl.BlockSpec((1, LANES), const)
    return pl.pallas_call(
        functools.partial(_gdn_kernel, n_chunks=n_chunks),
        grid=(batch, steps),
        in_specs=[pl.BlockSpec((rows, GDN_QKV), lambda b, s: (b * steps + s, 0)),
                  pl.BlockSpec((rows, MIX_HALF), lambda b, s: (b * steps + s, GDN_QKV // MIX_HALF)),
                  pl.BlockSpec((rows, LANES), lambda b, s: (b * steps + s, 0)),
                  row, row, row, sq, pl.BlockSpec((CHUNK, 3 * CHUNK), const)],
        out_specs=pl.BlockSpec((rows, MIX_HALF), lambda b, s: (b * steps + s, 0)),
        out_shape=jax.ShapeDtypeStruct((batch * seq, MIX_HALF), BF16),
        scratch_shapes=[pltpu.VMEM((HEADS, HEAD_DIM, HEAD_DIM), F32)],
        compiler_params=pltpu.CompilerParams(dimension_semantics=("arbitrary", "arbitrary"),
                                             vmem_limit_bytes=VMEM_LIMIT),
        name="gdn",
    )(pb, pb, pc, alog_row, dtb_row, onorm_w, code, incl)


def _route_kernel(x_ref, ohg_ref, ogd_ref, wo_ref, nw_ref, wr_ref, ltri_ref,
                  h_ref, hn_ref, ri_ref, rg_ref, cnt_ref, cnt_scr):
    @pl.when(pl.program_id(0) == 0)
    def _():
        cnt_scr[...] = jnp.zeros_like(cnt_scr)

    mix = _dot(ohg_ref[...], wo_ref[0:MIX_HALF, :]) + _dot(ogd_ref[...], wo_ref[MIX_HALF:2 * MIX_HALF, :])
    h = x_ref[...] + mix
    h_ref[...] = h
    hn = h * lax.rsqrt(jnp.mean(h * h, axis=-1, keepdims=True) + NORM_EPS) * nw_ref[...]
    hn_ref[...] = _pack_words(hn)
    hn_hi = hn.astype(BF16)
    hn_lo = (hn - hn_hi.astype(F32)).astype(BF16)
    part = _dot(hn_hi, wr_ref[...])
    logits = (_dot(hn_lo, wr_ref[:, 0:LANES]) + part[:, LANES:]) + part[:, 0:LANES]
    tm = logits.shape[0]
    lane = lax.broadcasted_iota(jnp.int32, (tm, LANES), 1)
    neg = jnp.float32(-jnp.inf)
    big = jnp.int32(LANES)

    def first_max(vals):
        m = jnp.max(vals, axis=-1, keepdims=True)
        return m, jnp.min(jnp.where(vals == m, lane, big), axis=-1, keepdims=True)

    gl = jnp.where(lane < N_GROUPS, logits, neg)
    gmax, gidx = first_max(gl)
    p_group = 1.0 / jnp.sum(jnp.exp(gl - gmax), axis=-1, keepdims=True)
    lo = N_GROUPS + EXPERTS_PER_GROUP * gidx
    el = jnp.where((lane >= lo) & (lane < lo + EXPERTS_PER_GROUP), logits, neg)
    m1, i1 = first_max(el)
    m2, i2 = first_max(jnp.where(lane == i1, neg, el))
    r = jnp.exp(m2 - m1)
    gate1 = p_group / (1.0 + r)
    gate2 = p_group * r / (1.0 + r)
    hot1 = lane == i1
    hot2 = lane == i2
    onehot = jnp.where(hot1 | hot2, 1.0, 0.0)
    before = _dot(ltri_ref[...], onehot.astype(BF16)) + cnt_scr[...]
    rank1 = jnp.sum(jnp.where(hot1, before, 0.0), axis=-1, keepdims=True)
    rank2 = jnp.sum(jnp.where(hot2, before, 0.0), axis=-1, keepdims=True)
    cnt = cnt_scr[...] + jnp.sum(onehot, axis=0, keepdims=True)
    cnt_scr[...] = cnt
    cnt_ref[...] = cnt
    ri = jnp.where(lane == 0, i1 - N_GROUPS,
                   jnp.where(lane == 1, i2 - N_GROUPS,
                             jnp.where(lane == 2, rank1.astype(jnp.int32),
                                       jnp.where(lane == 3, rank2.astype(jnp.int32), 0))))
    ri_ref[...] = ri
    rg_ref[...] = jnp.where(lane == 0, gate1, jnp.where(lane == 1, gate2, 0.0))


def _route(x2, ohg, ogd, wo, norm_w, wr, tm):
    t = x2.shape[0]
    ltri = jnp.asarray(np.tril(np.ones((tm, tm), np.float32), -1), BF16)
    const = lambda i: (0, 0)
    tile = lambda n: pl.BlockSpec((tm, n), lambda i: (i, 0))
    return pl.pallas_call(
        _route_kernel,
        grid=(t // tm,),
        in_specs=[tile(D_MODEL), tile(MIX_HALF), tile(MIX_HALF),
                  pl.BlockSpec((D_MODEL, D_MODEL), const),
                  pl.BlockSpec((1, D_MODEL), const),
                  pl.BlockSpec((D_MODEL, 2 * LANES), const),
                  pl.BlockSpec((tm, tm), const)],
        out_specs=[tile(D_MODEL), tile(HALF), tile(LANES), tile(LANES),
                   pl.BlockSpec((1, LANES), const)],
        out_shape=[jax.ShapeDtypeStruct((t, D_MODEL), F32),
                   jax.ShapeDtypeStruct((t, HALF), U32),
                   jax.ShapeDtypeStruct((t, LANES), jnp.int32),
                   jax.ShapeDtypeStruct((t, LANES), F32),
                   jax.ShapeDtypeStruct((1, LANES), F32)],
        scratch_shapes=[pltpu.VMEM((1, LANES), F32)],
        compiler_params=pltpu.CompilerParams(dimension_semantics=("arbitrary",), vmem_limit_bytes=VMEM_LIMIT),
        name="route",
    )(x2, ohg, ogd, wo, norm_w, wr, ltri)


ISSUE_UNROLL = 8


HALF = D_MODEL // 2
PIECES = HALF // LANES
U32 = jnp.uint32
HIGH_HALF = np.uint32(0xFFFF0000)


def _pack_words(x):
    bits = lambda a: lax.bitcast_convert_type(a.astype(BF16).astype(F32), U32)
    return (bits(x[:, 0:HALF]) >> 16) | (bits(x[:, HALF:]) & HIGH_HALF)


def _unpack_words(words):
    return (lax.bitcast_convert_type(words << 16, F32), lax.bitcast_convert_type(words & HIGH_HALF, F32))


def _store_rows_as_tiles(ref, x):
    n = x.shape[0]
    words = _pack_words(x)
    for s in range(PIECES):
        ref[pl.ds(s, n, stride=PIECES), :] = words[:, s * LANES:(s + 1) * LANES]


def _load_rows_from_tiles(ref, n):
    return _unpack_words(jnp.concatenate([ref[pl.ds(s, n, stride=PIECES), :] for s in range(PIECES)], axis=1))


SC_WINDOW = 64


def _sc_permute_rows(rows, idx, gather):
    info = plsc.get_sparse_core_info()
    n_workers = info.num_cores * info.num_subcores
    n_idx, (n_src, width) = idx.shape[0], rows.shape
    copies = 1 if gather else n_idx // n_src
    per_worker = n_idx // copies // n_workers
    n_win = per_worker // SC_WINDOW
    assert per_worker * n_workers * copies == n_idx and n_win * SC_WINDOW == per_worker and n_win % 2 == 0
    idx2 = idx.reshape(n_idx // SC_WINDOW, SC_WINDOW)
    win_per_copy = n_idx // copies // SC_WINDOW

    @functools.partial(
        pl.kernel,
        mesh=plsc.VectorSubcoreMesh(core_axis_name="c", subcore_axis_name="s"),
        out_type=jax.ShapeDtypeStruct((n_idx, width), rows.dtype),
        scratch_types=[pltpu.VMEM((copies, n_win, SC_WINDOW), jnp.int32),
                       pltpu.VMEM((2, SC_WINDOW, width), rows.dtype),
                       pltpu.SemaphoreType.DMA((2,)), pltpu.SemaphoreType.DMA((2,))],
    )
    def permute(rows_hbm, idx_hbm, out_hbm, idx_v, buf, fill_sem, drain_sem):
        worker = lax.axis_index("s") * info.num_cores + lax.axis_index("c")
        first_row = worker * per_worker
        for c in range(copies):
            pltpu.sync_copy(idx_hbm.at[pl.ds(c * win_per_copy + worker * n_win, n_win)], idx_v.at[c])

        def window(ref, j):
            return ref.at[pl.ds(pl.multiple_of(first_row + j * SC_WINDOW, SC_WINDOW), SC_WINDOW)]

        def fill(j, b):
            src = rows_hbm.at[idx_v.at[0, j]] if gather else window(rows_hbm, j)
            return pltpu.make_async_copy(src, buf.at[b], fill_sem.at[b])

        def drains(j, b):
            dsts = [window(out_hbm, j)] if gather else [out_hbm.at[idx_v.at[c, j]] for c in range(copies)]
            return [pltpu.make_async_copy(buf.at[b], dst, drain_sem.at[b]) for dst in dsts]

        fill(0, 0).start()

        @pl.loop(0, n_win, step=2)
        def _(j0):
            for b in range(2):
                j = j0 + b
                fill(j, b).wait()

                @pl.when(j >= 1)
                def _():
                    for d in drains(j - 1, 1 - b):
                        d.wait()

                @pl.when(j + 1 < n_win)
                def _():
                    fill(j + 1, 1 - b).start()

                for d in drains(j, b):
                    d.start()

        for d in drains(n_win - 1, 1):
            d.wait()

    return permute(rows, idx2)


def _sc_gather_rows(table, idx):
    return _sc_permute_rows(table, idx, gather=True)


def _sc_scatter_rows(rows, idx):
    return _sc_permute_rows(rows, idx, gather=False)


def _row_copy(src_ref, src_row, dst_ref, dst_row, sem):
    src = src_ref.at[pl.ds(pl.multiple_of(src_row * PIECES, PIECES), PIECES)]
    dst = dst_ref.at[pl.ds(pl.multiple_of(dst_row * PIECES, PIECES), PIECES)]
    return pltpu.make_async_copy(src, dst, sem)


def _dispatch_kernel(starts_ref, idx_ref, hn_ref, xs_ref, sem, *, tm):
    def issue(t, carry):
        for k in range(TOP_K):
            dest = starts_ref[idx_ref[0, 0, 4 * t + k]] + idx_ref[0, 0, 4 * t + 2 + k]
            _row_copy(hn_ref, t, xs_ref, dest, sem).start(priority=k % 2)
        return carry

    lax.fori_loop(0, tm, issue, 0, unroll=ISSUE_UNROLL)
    for k in range(TOP_K):
        pltpu.make_async_copy(hn_ref, xs_ref.at[pl.ds(0, tm * PIECES)], sem).wait()


def _dispatch(starts, idx3, hn, tm):
    t = hn.shape[0] // PIECES
    return pl.pallas_call(
        functools.partial(_dispatch_kernel, tm=tm),
        grid_spec=pltpu.PrefetchScalarGridSpec(
            num_scalar_prefetch=1,
            grid=(t // tm,),
            in_specs=[pl.BlockSpec((1, 1, 4 * tm), lambda i, s: (i, 0, 0), memory_space=pltpu.SMEM),
                      pl.BlockSpec((tm * PIECES, LANES), lambda i, s: (i, 0))],
            out_specs=pl.BlockSpec(memory_space=pl.ANY),
            scratch_shapes=[pltpu.SemaphoreType.DMA(())],
        ),
        out_shape=jax.ShapeDtypeStruct((t * TOP_K * PIECES, LANES), U32),
        compiler_params=pltpu.CompilerParams(dimension_semantics=("arbitrary",), vmem_limit_bytes=VMEM_LIMIT),
        name="dispatch",
    )(starts, idx3, hn)


EXPERT_SPLIT = 1


def _expert_kernel(blk_ref, exp_ref, lo_ref, hi_ref, xs_ref, wg_ref, wu_ref, wd_ref, ys_ref, wgu_b, wd_b, acc):
    p = pl.program_id(0)

    @pl.when(jnp.logical_or(p == 0, exp_ref[p] != exp_ref[jnp.maximum(p - 1, 0)]))
    def _():
        wgu_b[:, 0:EXPERT_FF] = wg_ref[0].astype(BF16)
        wgu_b[:, EXPERT_FF:2 * EXPERT_FF] = wu_ref[0].astype(BF16)
        wd_b[...] = wd_ref[0].astype(BF16)

    @pl.when(p == 0)
    def _():
        acc[...] = jnp.zeros_like(acc)

    first = jnp.logical_or(p == 0, blk_ref[p] != blk_ref[jnp.maximum(p - 1, 0)])
    sub = SLOT_BLOCK // EXPERT_SPLIT
    for r in range(EXPERT_SPLIT):
        x_lo, x_hi = _unpack_words(xs_ref[pl.ds(r * sub, sub), :])
        ab = _dot(x_lo.astype(BF16), wgu_b[0:HALF, :]) + _dot(x_hi.astype(BF16), wgu_b[HALF:D_MODEL, :])
        hb = _silu(ab[:, 0:EXPERT_FF]) * ab[:, EXPERT_FF:2 * EXPERT_FF]
        y = _dot(hb.astype(BF16), wd_b[...])
        slot = blk_ref[p] * SLOT_BLOCK + r * sub + lax.broadcasted_iota(jnp.int32, (sub, 1), 0)
        y = jnp.where((slot >= lo_ref[p]) & (slot < hi_ref[p]), y, 0.0)
        total = y + jnp.where(first, 0.0, acc[pl.ds(r * sub, sub), :])
        acc[pl.ds(r * sub, sub), :] = total
        ys_ref[pl.ds(r * sub, sub), :] = _pack_words(total)


def _experts(blk, exp, lo, hi, xs, wg, wu, wd):
    n_pairs = blk.shape[0]
    return pl.pallas_call(
        _expert_kernel,
        grid_spec=pltpu.PrefetchScalarGridSpec(
            num_scalar_prefetch=4,
            grid=(n_pairs,),
            in_specs=[pl.BlockSpec((SLOT_BLOCK, HALF), lambda p, blk, exp, lo, hi: (blk[p], 0)),
                      pl.BlockSpec((1, D_MODEL, EXPERT_FF), lambda p, blk, exp, lo, hi: (exp[p], 0, 0)),
                      pl.BlockSpec((1, D_MODEL, EXPERT_FF), lambda p, blk, exp, lo, hi: (exp[p], 0, 0)),
                      pl.BlockSpec((1, EXPERT_FF, D_MODEL), lambda p, blk, exp, lo, hi: (exp[p], 0, 0))],
            out_specs=pl.BlockSpec((SLOT_BLOCK, HALF), lambda p, blk, exp, lo, hi: (blk[p], 0)),
            scratch_shapes=[pltpu.VMEM((D_MODEL, 2 * EXPERT_FF), BF16), pltpu.VMEM((EXPERT_FF, D_MODEL), BF16),
                            pltpu.VMEM((SLOT_BLOCK, D_MODEL), F32)],
        ),
        out_shape=jax.ShapeDtypeStruct(xs.shape, U32),
        compiler_params=pltpu.CompilerParams(dimension_semantics=("arbitrary",), vmem_limit_bytes=VMEM_LIMIT),
        name="experts",
    )(blk, exp, lo, hi, xs, wg, wu, wd)


def _combine_kernel(h_ref, rg_ref, fw_ref, y0_ref, y1_ref, o_ref):
    rg = rg_ref[...]
    y = [jnp.concatenate(_unpack_words(ref[...]), axis=1) for ref in (y0_ref, y1_ref)]
    h = h_ref[...] + rg[:, 0:1] * y[0] + rg[:, 1:2] * y[1]
    o_ref[...] = h * lax.rsqrt(jnp.mean(h * h, axis=-1, keepdims=True) + NORM_EPS) * fw_ref[...]


def _combine(h, rg, final_w, y2, tm):
    t = h.shape[0]
    n_tiles = t // tm
    return pl.pallas_call(
        _combine_kernel,
        grid=(n_tiles,),
        in_specs=[pl.BlockSpec((tm, D_MODEL), lambda i: (i, 0)),
                  pl.BlockSpec((tm, LANES), lambda i: (i, 0)),
                  pl.BlockSpec((1, D_MODEL), lambda i: (0, 0)),
                  pl.BlockSpec((tm, HALF), lambda i: (i, 0)),
                  pl.BlockSpec((tm, HALF), lambda i: (n_tiles + i, 0))],
        out_specs=pl.BlockSpec((tm, D_MODEL), lambda i: (i, 0)),
        out_shape=jax.ShapeDtypeStruct((t, D_MODEL), F32),
        compiler_params=pltpu.CompilerParams(dimension_semantics=("arbitrary",), vmem_limit_bytes=VMEM_LIMIT),
        name="combine",
    )(h, rg, final_w, y2, y2)


def _pair_schedule(counts, n_slots):
    n_blocks = n_slots // SLOT_BLOCK
    n_pairs = n_blocks + N_EXPERTS - 1
    ends = jnp.cumsum(counts)
    starts = ends - counts
    first = starts // SLOT_BLOCK
    last = jnp.maximum(ends - 1, starts) // SLOT_BLOCK
    per_expert = jnp.where(counts > 0, last - first + 1, 0)
    cum = jnp.cumsum(per_expert)
    p = jnp.arange(n_pairs, dtype=jnp.int32)
    e = jnp.minimum(jnp.sum(cum[None, :] <= p[:, None], axis=1), N_EXPERTS - 1).astype(jnp.int32)
    valid = p < cum[-1]
    onehot = e[:, None] == jnp.arange(N_EXPERTS, dtype=jnp.int32)[None, :]
    pick = lambda table: jnp.sum(jnp.where(onehot, table[None, :], 0), axis=1)
    blk = jnp.where(valid, pick(first) + p - pick(cum - per_expert), n_blocks - 1).astype(jnp.int32)
    lo = jnp.where(valid, pick(starts), 0).astype(jnp.int32)
    hi = jnp.where(valid, pick(ends), 0).astype(jnp.int32)
    return starts.astype(jnp.int32), blk, e, lo, hi


def _layer(x, norm_mix_w, w_in, lb_logits, hgrn_onorm_w, gdn_conv_w, gdn_a_log, gdn_dt_bias, gdn_onorm_w, w_out,
           norm_moe_w, router_group_w, router_expert_w, w_gate, w_up, w_down, final_w, *, tm, n_chunks):
    batch, seq, _ = x.shape
    t = batch * seq
    x2 = x.reshape(t, D_MODEL)
    n_hg = 4 * MIX_HALF
    n_gd = GDN_QKV + MIX_HALF
    wb16 = w_in.astype(BF16)
    wa = wb16[:, 0:n_hg]
    wb = wb16[:, n_hg:n_hg + n_gd]
    wc = jnp.pad(wb16[:, n_hg + n_gd:], ((0, 0), (0, LANES - 2 * HEADS)))
    pa, pb, pc = _inproj(x2, norm_mix_w.reshape(1, D_MODEL), wa, wb, wc, gdn_conv_w, 2 * tm, seq // (2 * tm))

    o_hg = _hgrn(pa, lb_logits, hgrn_onorm_w.reshape(1, HEAD_DIM), batch, seq, n_chunks)
    alog_row = jnp.pad(gdn_a_log.reshape(1, HEADS), ((0, 0), (0, LANES - HEADS)))
    dtb_row = jnp.pad(gdn_dt_bias.reshape(1, HEADS), ((0, 0), (0, LANES - HEADS)))
    o_gd = _gdn(pb, pc, alog_row, dtb_row, gdn_onorm_w.reshape(1, HEAD_DIM), batch, seq, n_chunks)

    wr = jnp.pad(jnp.concatenate([router_group_w, router_expert_w], axis=1),
                 ((0, 0), (0, LANES - N_GROUPS - N_EXPERTS)))
    wr_hi = wr.astype(BF16)
    wr = jnp.concatenate([wr_hi, (wr - wr_hi.astype(F32)).astype(BF16)], axis=1)
    h, hn, ri, rg, cnt = _route(x2, o_hg, o_gd, w_out.astype(BF16), norm_moe_w.reshape(1, D_MODEL), wr, 2 * tm)

    counts = cnt[0, N_GROUPS:N_GROUPS + N_EXPERTS].astype(jnp.int32)
    starts, blk, exp, lo, hi = _pair_schedule(counts, t * TOP_K)
    experts_kt = jnp.transpose(ri[:, 0:TOP_K])
    onehot = experts_kt[:, :, None] == jnp.arange(N_EXPERTS, dtype=jnp.int32)
    slot_kt = jnp.sum(jnp.where(onehot, starts, 0), axis=-1) + jnp.transpose(ri[:, TOP_K:2 * TOP_K])
    slots = slot_kt.reshape(TOP_K * t)
    xs = _sc_scatter_rows(hn, slots)
    ys = _experts(blk, exp, lo, hi, xs, w_gate, w_up, w_down)
    y2 = _sc_gather_rows(ys, slots)
    out = _combine(h, rg, final_w.reshape(1, D_MODEL), y2, 2 * tm)
    return out.reshape(batch, seq, D_MODEL)


def kernel(x, norm_mix_w, w_in, hgrn_lb_logits, hgrn_onorm_w, gdn_conv_w, gdn_a_log, gdn_dt_bias, gdn_onorm_w, w_out, norm_moe_w, router_group_w, router_expert_w, expert_w_gate, expert_w_up, expert_w_down, final_norm_w):
    return _layer(x, norm_mix_w[0], w_in[0], hgrn_lb_logits, hgrn_onorm_w[0], gdn_conv_w[0], gdn_a_log[0],
                  gdn_dt_bias[0], gdn_onorm_w[0], w_out[0], norm_moe_w[0], router_group_w[0], router_expert_w[0],
                  expert_w_gate[0], expert_w_up[0], expert_w_down[0], final_norm_w, tm=256, n_chunks=8)
```

```python
import functools

import numpy as np
import jax
import jax.numpy as jnp
from jax import lax
from jax.experimental import pallas as pl
from jax.experimental.pallas import tpu as pltpu
from jax.experimental.pallas import tpu_sc as plsc

F32 = jnp.float32
BF16 = jnp.bfloat16
HI = lax.Precision.HIGHEST

D_MODEL = 1024
HEADS = 4
HEAD_DIM = 128
MIX_HALF = HEADS * HEAD_DIM
CHUNK = 64
GROUP = 2
N_STACKS = HEADS // GROUP
STACK = GROUP * CHUNK
CONV_K = 4
N_GROUPS = 8
EXPERTS_PER_GROUP = 8
N_EXPERTS = N_GROUPS * EXPERTS_PER_GROUP
TOP_K = 2
EXPERT_FF = 256
NORM_EPS = 1e-6
LANES = 128
BF16_ROWS = 16
SLOT_BLOCK = 512
VMEM_LIMIT = 56 * 1024 * 1024


def _sigmoid(x):
    return 0.5 * jnp.tanh(0.5 * x) + 0.5


def _silu(x):
    return x * _sigmoid(x)


def _dot(a, b, precision=None):
    return jnp.dot(a, b, preferred_element_type=F32, precision=precision)


def _dot_nt(a, b, precision=None):
    return lax.dot_general(a, b, (((1,), (1,)), ((), ())), preferred_element_type=F32, precision=precision)


def _dot_tn(a, b, precision=None):
    return lax.dot_general(a, b, (((0,), (0,)), ((), ())), preferred_element_type=F32, precision=precision)


def _bdot(a, b):
    return _dot(a.astype(BF16), b.astype(BF16))


def _masked_sum(mask3, x):
    hi = x.astype(BF16)
    r1 = x - hi.astype(F32)
    mid = r1.astype(BF16)
    lo = (r1 - mid.astype(F32)).astype(BF16)
    return _dot(mask3, jnp.concatenate([hi, mid, lo], axis=0))


def _triple(mask):
    return np.concatenate([mask, mask, mask], axis=1)


def _stack_heads(a, p):
    return jnp.concatenate([a[:, h * HEAD_DIM:(h + 1) * HEAD_DIM] for h in range(p * GROUP, (p + 1) * GROUP)],
                           axis=0)


def _each(f, *lists):
    return [f(*args) for args in zip(*lists)]


INPROJ_GROUP = 256


def _l2norm_heads(a):
    return jnp.concatenate(
        [a[:, h * HEAD_DIM:(h + 1) * HEAD_DIM]
         * lax.rsqrt(jnp.sum(jnp.square(a[:, h * HEAD_DIM:(h + 1) * HEAD_DIM]), axis=-1, keepdims=True) + 1e-6)
         for h in range(a.shape[1] // HEAD_DIM)], axis=1)


def _inproj_kernel(x_ref, nw_ref, wa_ref, wb_ref, wc_ref, cw_ref, oa_ref, ob_ref, oc_ref, ubuf, *, tiles_per_seq):
    tm = x_ref.shape[0]

    @pl.when(lax.rem(pl.program_id(0), tiles_per_seq) == 0)
    def _():
        ubuf[0:HIST, :] = jnp.zeros((HIST, GDN_QKV), F32)

    x = x_ref[...]
    ms = jnp.mean(x * x, axis=-1, keepdims=True)
    hn = (x * lax.rsqrt(ms + NORM_EPS) * nw_ref[...]).astype(BF16)
    cw = cw_ref[...]
    width = INPROJ_GROUP
    group = lambda g: slice(g * width, (g + 1) * width)

    def project(w_ref, o_ref, g, row0=0):
        o_ref[row0:row0 + tm, group(g)] = _dot(hn, w_ref[:, group(g)])

    def conv_group(g):
        cols = group(g)
        conv = cw[CONV_K - 1:CONV_K, cols] * ubuf[HIST:HIST + tm, cols]
        for j in range(1, CONV_K):
            conv = conv + cw[CONV_K - 1 - j:CONV_K - j, cols] * ubuf[HIST - j:HIST - j + tm, cols]
        ubuf[0:HIST, cols] = ubuf[tm:tm + HIST, cols]
        act = _silu(conv)
        if g * width < MIX_HALF:
            act = _l2norm_heads(act) * (HEAD_DIM ** -0.5)
        elif g * width < 2 * MIX_HALF:
            act = _l2norm_heads(act)
        ob_ref[:, cols] = act

    n_conv = GDN_QKV // width
    n_a = wa_ref.shape[1] // width
    others = [functools.partial(project, wa_ref, oa_ref, g) for g in range(n_a)]
    others += [functools.partial(project, wb_ref, ob_ref, g) for g in range(n_conv, wb_ref.shape[1] // width)]
    project(wb_ref, ubuf, 0, HIST)
    for g in range(n_conv):
        if g + 1 < n_conv:
            project(wb_ref, ubuf, g + 1, HIST)
        conv_group(g)
        others.pop(0)()
    for task in others:
        task()
    oc_ref[...] = _dot(hn, wc_ref[...])


def _inproj(x2, norm_w, wa, wb, wc, conv_w, tm, tiles_per_seq):
    t = x2.shape[0]
    na, nb, nc = wa.shape[1], wb.shape[1], wc.shape[1]
    const = lambda i: (0, 0)
    return pl.pallas_call(
        functools.partial(_inproj_kernel, tiles_per_seq=tiles_per_seq),
        grid=(t // tm,),
        in_specs=[
            pl.BlockSpec((tm, D_MODEL), lambda i: (i, 0)),
            pl.BlockSpec((1, D_MODEL), const),
            pl.BlockSpec((D_MODEL, na), const),
            pl.BlockSpec((D_MODEL, nb), const),
            pl.BlockSpec((D_MODEL, nc), const),
            pl.BlockSpec((CONV_K, GDN_QKV), const),
        ],
        out_specs=[
            pl.BlockSpec((tm, na), lambda i: (i, 0)),
            pl.BlockSpec((tm, nb), lambda i: (i, 0)),
            pl.BlockSpec((tm, nc), lambda i: (i, 0)),
        ],
        out_shape=[
            jax.ShapeDtypeStruct((t, na), F32),
            jax.ShapeDtypeStruct((t, nb), F32),
            jax.ShapeDtypeStruct((t, nc), F32),
        ],
        scratch_shapes=[pltpu.VMEM((HIST + tm, GDN_QKV), F32)],
        compiler_params=pltpu.CompilerParams(dimension_semantics=("arbitrary",), vmem_limit_bytes=VMEM_LIMIT),
        name="inproj",
    )(x2, norm_w, wa, wb, wc, conv_w)


HGRN_LEVELS = (32, 16, 8, 4, 2, 1)
DIAG_CODE = len(HGRN_LEVELS)
NONE_CODE = DIAG_CODE + 1


def _hgrn_arg_matrix():
    t = np.arange(CHUNK)[:, None]
    u = np.arange(CHUNK)[None, :]
    mats = [u <= t]
    for b in HGRN_LEVELS:
        odd = (t // b) % 2 == 1
        start = (t // b) * b
        mats.append(np.where(odd, (u > start) & (u <= t), (u > t) & (u <= start + b)))
    return np.concatenate(mats, axis=0).astype(np.float32)


def _hgrn_level_codes():
    idx = np.arange(STACK)
    h, t = idx // CHUNK, idx % CHUNK
    same = h[:, None] == h[None, :]
    tt, ss = t[:, None], t[None, :]
    code = np.full((STACK, STACK), NONE_CODE, np.int32)
    code[same & (tt == ss)] = DIAG_CODE
    for l, b in enumerate(HGRN_LEVELS):
        sib = (tt // (2 * b) == ss // (2 * b)) & ((tt // b) % 2 == 1) & ((ss // b) % 2 == 0)
        code[same & sib] = l
    return code


def _hgrn_kernel(q_ref, f_ref, i_ref, g_ref, lbl_ref, onw_ref, marg_ref, code_ref, o_ref, st_ref, *, n_chunks):
    @pl.when(pl.program_id(1) == 0)
    def _():
        st_ref[...] = jnp.zeros_like(st_ref)

    lbl = lbl_ref[...]
    lmax = jnp.max(lbl, axis=0, keepdims=True)
    lexp = jnp.exp(lbl - lmax)
    lb = lexp[0:1, :] / jnp.sum(lexp, axis=0, keepdims=True)
    onw = onw_ref[...]
    marg = marg_ref[...]
    code = code_ref[...]
    trow = lax.broadcasted_iota(jnp.int32, (STACK, HEAD_DIM), 0) & (CHUNK - 1)

    q_blk = _silu(q_ref[...]) * (HEAD_DIM ** -0.5)
    fg_blk = lb + (1.0 - lb) * _sigmoid(f_ref[...])
    k_blk = 1.0 - fg_blk
    lf_blk = jnp.log(fg_blk)
    v_blk = i_ref[...]

    rows = [slice(c * CHUNK, (c + 1) * CHUNK) for c in range(n_chunks)]
    args = [_masked_sum(marg, lf_blk[r]) for r in rows]
    e_chunk = [jnp.exp(a) for a in args]
    suf_chunk = [jnp.exp(a[CHUNK - 1:CHUNK] - a[0:CHUNK]) for a in args]
    units = [(c, p) for c in range(n_chunks) for p in range(N_STACKS)]
    qs = [_stack_heads(q_blk[rows[c]], p) for c, p in units]
    ks = [_stack_heads(k_blk[rows[c]], p) for c, p in units]
    vs = [_stack_heads(v_blk[rows[c]], p).astype(BF16) for c, p in units]
    e_part = lambda n: [_stack_heads(e_chunk[c][n * CHUNK:(n + 1) * CHUNK], p) for c, p in units]
    cum_e = e_part(0)
    suf_e = [_stack_heads(suf_chunk[c], p) for c, p in units]
    att = _each(lambda q, k: jnp.where(code == DIAG_CODE, jnp.sum(q * k, axis=-1, keepdims=True), 0.0), qs, ks)

    def level_update(a, x, l, b):
        if b < BF16_ROWS:
            return jnp.where(code == l, _dot_nt(x, x), a)
        blocks = [slice(s0, s0 + b) for s0 in range(0, STACK, b)]
        r = _dot_nt(jnp.concatenate([x[bl] for bl in blocks[1::2]], axis=0), x)
        return jnp.concatenate(
            [a[bl] if n % 2 == 0 else jnp.where(code[bl] == l, r[(n // 2) * b:(n // 2 + 1) * b], a[bl])
             for n, bl in enumerate(blocks)], axis=0)

    for l, b in enumerate(HGRN_LEVELS):
        x = _each(lambda q, k, el: (jnp.where((trow & b) != 0, q, k) * el).astype(BF16), qs, ks, e_part(1 + l))
        att = _each(lambda xi, a: level_update(a, xi, l, b), x, att)
    o_intra = _each(lambda a, v: _dot(a.astype(BF16), v), att, vs)
    qc = _each(lambda q, e: (q * e).astype(BF16), qs, cum_e)
    kd = _each(lambda k, e: (k * e).astype(BF16), ks, suf_e)

    state = [st_ref[h] for h in range(HEADS)]
    for u, (c, p) in enumerate(units):
        for i in range(GROUP):
            h = p * GROUP + i
            hr = slice(i * CHUNK, (i + 1) * CHUNK)
            hc = slice(h * HEAD_DIM, (h + 1) * HEAD_DIM)
            o = _dot_nt(qc[u][hr], state[h].astype(BF16)) + o_intra[u][hr]
            decay = cum_e[u][i * CHUNK + CHUNK - 1:i * CHUNK + CHUNK, :]
            state[h] = decay * state[h] + _dot_tn(vs[u][hr], kd[u][hr])
            y = o * lax.rsqrt(jnp.mean(o * o, axis=-1, keepdims=True) + NORM_EPS) * onw
            o_ref[rows[c], hc] = (y * _silu(g_ref[rows[c], hc])).astype(o_ref.dtype)
    for h in range(HEADS):
        st_ref[h] = state[h]


def _hgrn(pa, lb_logits, onorm_w, batch, seq, n_chunks):
    rows = n_chunks * CHUNK
    steps = seq // rows
    col = lambda j: pl.BlockSpec((rows, MIX_HALF), lambda b, s, j=j: (b * steps + s, j))
    const = lambda b, s: (0, 0)
    marg = jnp.asarray(_triple(_hgrn_arg_matrix()), BF16)
    code = jnp.asarray(_hgrn_level_codes())
    return pl.pallas_call(
        functools.partial(_hgrn_kernel, n_chunks=n_chunks),
        grid=(batch, steps),
        in_specs=[col(0), col(1), col(2), col(3),
                  pl.BlockSpec(lb_logits.shape, const),
                  pl.BlockSpec((1, HEAD_DIM), const),
                  pl.BlockSpec(marg.shape, const),
                  pl.BlockSpec(code.shape, const)],
        out_specs=pl.BlockSpec((rows, MIX_HALF), lambda b, s: (b * steps + s, 0)),
        out_shape=jax.ShapeDtypeStruct((batch * seq, MIX_HALF), BF16),
        scratch_shapes=[pltpu.VMEM((HEADS, HEAD_DIM, HEAD_DIM), F32)],
        compiler_params=pltpu.CompilerParams(dimension_semantics=("arbitrary", "arbitrary"),
                                             vmem_limit_bytes=VMEM_LIMIT),
        name="hgrn2",
    )(pa, pa, pa, pa, lb_logits, onorm_w, marg, code)


GDN_QKV = 3 * MIX_HALF
HIST = 8
C_NONE, C_DIAG, C_B16, C_B32, C_B64 = 0, 1, 2, 3, 4
MASKED_EXPONENT = -1e30


def _gdn_codes():
    idx = np.arange(STACK)
    h, t = idx // CHUNK, idx % CHUNK
    same = h[:, None] == h[None, :]
    tt, ss = t[:, None], t[None, :]
    code = np.full((STACK, STACK), C_NONE, np.int32)
    low = same & (ss < tt)
    code[low] = C_B64
    code[low & (tt // 32 == ss // 32)] = C_B32
    code[low & (tt // 16 == ss // 16)] = C_B16
    code[same & (tt == ss)] = C_DIAG
    incl = np.tril(np.ones((CHUNK, CHUNK), np.float32))
    return code, incl


def _gdn_kernel(qkv_ref, z_ref, ab_ref, alog_ref, dtb_ref, onw_ref, code_ref, incl_ref,
                o_ref, st_ref, *, n_chunks):
    @pl.when(pl.program_id(1) == 0)
    def _():
        st_ref[...] = jnp.zeros_like(st_ref)

    onw = onw_ref[...]
    code = code_ref[...]
    incl_m = incl_ref[...]
    incl = code >= C_DIAG
    eye = (code == C_DIAG).astype(F32)

    rows = [slice(c * CHUNK, (c + 1) * CHUNK) for c in range(n_chunks)]
    units = [(c, p) for c in range(n_chunks) for p in range(N_STACKS)]

    qs = [_stack_heads(qkv_ref[rows[c], 0:MIX_HALF], p) for c, p in units]
    ks = [_stack_heads(qkv_ref[rows[c], MIX_HALF:2 * MIX_HALF], p) for c, p in units]
    vs = [_stack_heads(qkv_ref[rows[c], 2 * MIX_HALF:3 * MIX_HALF], p) for c, p in units]
    ab_blk = ab_ref[...]
    xa = ab_blk + dtb_ref[...]
    softplus = jnp.maximum(xa, 0.0) + jnp.log(1.0 + jnp.exp(-jnp.abs(xa)))
    g_all = -jnp.exp(alog_ref[...]) * softplus
    beta_all = _sigmoid(ab_blk)

    def head_cols(a, c, p, first_lane):
        return jnp.concatenate([a[c][:, first_lane + h:first_lane + h + 1]
                                for h in range(p * GROUP, (p + 1) * GROUP)], axis=0)

    g_cum = [_masked_sum(incl_m, g_all[r]) for r in rows]
    beta_chunk = [beta_all[r] for r in rows]
    beta_st = [head_cols(beta_chunk, c, p, HEADS) for c, p in units]
    gc = [jnp.broadcast_to(head_cols(g_cum, c, p, 0), (STACK, HEAD_DIM)) for c, p in units]
    dec = [jnp.exp(jnp.where(incl, g - jnp.transpose(g)[0:1, :], MASKED_EXPONENT)) for g in gc]
    kb = [k.astype(BF16) for k in ks]
    kq = _each(lambda k, q: _dot_nt(jnp.concatenate([k, q.astype(BF16)], axis=0), k), kb, qs)
    kk = [a[0:STACK] for a in kq]
    qk = [a[STACK:] for a in kq]
    am = _each(lambda b, k2, d: b * k2 * d, beta_st, kk, dec)
    a16 = [jnp.where(code == C_B16, a, 0.0) for a in am]
    n32 = [jnp.where(code == C_B32, a, 0.0) for a in am]
    n64 = [jnp.where(code == C_B64, a, 0.0) for a in am]
    apow = _each(_bdot, a16, a16)
    pinv = [eye - a for a in a16]
    for _ in range(2):
        both = _each(lambda ai, pi: _bdot(jnp.concatenate([ai, pi], axis=0), ai), apow, pinv)
        pinv = _each(lambda pi, b: pi + b[STACK:], pinv, both)
        apow = [b[0:STACK] for b in both]
    pinv = _each(lambda pi, ai: pi + _bdot(pi, ai), pinv, apow)
    for nlev in (n32, n64):
        t = _each(_bdot, nlev, pinv)
        pinv = _each(lambda pi, ti: pi - _bdot(pi, ti), pinv, t)
    egc = [jnp.exp(g) for g in gc]
    rhs = _each(lambda k, v, b, e: jnp.concatenate([k * (b * e), v * b], axis=1), ks, vs, beta_st, egc)
    wu = _each(_bdot, pinv, rhs)
    w_c = [a[:, 0:HEAD_DIM].astype(BF16) for a in wu]
    u_c = [a[:, HEAD_DIM:] for a in wu]
    qkm = _each(lambda a, d: (a * d).astype(BF16), qk, dec)
    qg = _each(lambda q, e: (q * e).astype(BF16), qs, egc)
    glast = [[g[i * CHUNK + CHUNK - 1:(i + 1) * CHUNK, :] for i in range(GROUP)] for g in gc]
    kdec = [[(k[i * CHUNK:(i + 1) * CHUNK] * jnp.exp(gl[i] - g[i * CHUNK:(i + 1) * CHUNK])).astype(BF16)
             for i in range(GROUP)] for k, g, gl in zip(ks, gc, glast)]
    gend = [[jnp.exp(gi) for gi in gl] for gl in glast]

    state = [st_ref[h] for h in range(HEADS)]
    for u, (c, p) in enumerate(units):
        heads = range(p * GROUP, (p + 1) * GROUP)
        hrs = [slice(i * CHUNK, (i + 1) * CHUNK) for i in range(GROUP)]
        ws = [_dot_nt(jnp.concatenate([w_c[u][hr], qg[u][hr]], axis=0), state[h].astype(BF16))
              for hr, h in zip(hrs, heads)]
        v_new = jnp.concatenate([u_c[u][hr] - a[0:CHUNK] for hr, a in zip(hrs, ws)], axis=0).astype(BF16)
        o_st = jnp.concatenate([a[CHUNK:] for a in ws], axis=0) + _dot(qkm[u], v_new)
        for i, h in enumerate(heads):
            hc = slice(h * HEAD_DIM, (h + 1) * HEAD_DIM)
            state[h] = gend[u][i] * state[h] + _dot_tn(v_new[hrs[i]], kdec[u][i])
            o = o_st[hrs[i]]
            y = o * lax.rsqrt(jnp.mean(o * o, axis=-1, keepdims=True) + NORM_EPS) * onw
            o_ref[rows[c], hc] = (y * _silu(z_ref[rows[c], hc])).astype(o_ref.dtype)
    for h in range(HEADS):
        st_ref[h] = state[h]


def _gdn(pb, pc, alog_row, dtb_row, onorm_w, batch, seq, n_chunks):
    rows = n_chunks * CHUNK
    steps = seq // rows
    const = lambda b, s: (0, 0)
    code, incl = _gdn_codes()
    code, incl = jnp.asarray(code), jnp.asarray(_triple(incl), BF16)
    sq = pl.BlockSpec((STACK, STACK), const)
    row = pl.BlockSpec((1, LANES), const)
    return pl.pallas_call(
        functools.partial(_gdn_kernel, n_chunks=n_chunks),
        grid=(batch, steps),
        in_specs=[pl.BlockSpec((rows, GDN_QKV), lambda b, s: (b * steps + s, 0)),
                  pl.BlockSpec((rows, MIX_HALF), lambda b, s: (b * steps + s, GDN_QKV // MIX_HALF)),
                  pl.BlockSpec((rows, LANES), lambda b, s: (b * steps + s, 0)),
                  row, row, row, sq, pl.BlockSpec((CHUNK, 3 * CHUNK), const)],
        out_specs=pl.BlockSpec((rows, MIX_HALF), lambda b, s: (b * steps + s, 0)),
        out_shape=jax.ShapeDtypeStruct((batch * seq, MIX_HALF), BF16),
        scratch_shapes=[pltpu.VMEM((HEADS, HEAD_DIM, HEAD_DIM), F32)],
        compiler_params=pltpu.CompilerParams(dimension_semantics=("arbitrary", "arbitrary"),
                                             vmem_limit_bytes=VMEM_LIMIT),
        name="gdn",
    )(pb, pb, pc, alog_row, dtb_row, onorm_w, code, incl)


ROUTE_ROWS = 8


def _route_kernel(x_ref, ohg_ref, ogd_ref, wo_ref, nw_ref, wr_ref, ltri_ref,
                  h_ref, hn_ref, ri_ref, rg_ref, cnt_ref, cnt_scr):
    @pl.when(pl.program_id(0) == 0)
    def _():
        cnt_scr[...] = jnp.zeros_like(cnt_scr)

    mix = _dot(ohg_ref[...], wo_ref[0:MIX_HALF, :]) + _dot(ogd_ref[...], wo_ref[MIX_HALF:2 * MIX_HALF, :])
    h = x_ref[...] + mix
    h_ref[...] = h
    hn = h * lax.rsqrt(jnp.mean(h * h, axis=-1, keepdims=True) + NORM_EPS) * nw_ref[...]
    hn_ref[...] = _pack_words(hn)
    hn_hi = hn.astype(BF16)
    hn_lo = (hn - hn_hi.astype(F32)).astype(BF16)
    part = _dot(hn_hi, wr_ref[...])
    logits = (_dot(hn_lo, wr_ref[:, 0:LANES]) + part[:, LANES:]) + part[:, 0:LANES]
    tm = logits.shape[0]
    lane = lax.broadcasted_iota(jnp.int32, (tm, LANES), 1)
    neg = jnp.float32(-jnp.inf)
    big = jnp.int32(LANES)

    def first_max(vals):
        m = jnp.max(vals, axis=-1, keepdims=True)
        return m, jnp.min(jnp.where(vals == m, lane, big), axis=-1, keepdims=True)

    gl = jnp.where(lane < N_GROUPS, logits, neg)
    gmax, gidx = first_max(gl)
    p_group = 1.0 / jnp.sum(jnp.exp(gl - gmax), axis=-1, keepdims=True)
    lo = N_GROUPS + EXPERTS_PER_GROUP * gidx
    el = jnp.where((lane >= lo) & (lane < lo + EXPERTS_PER_GROUP), logits, neg)
    m1, i1 = first_max(el)
    m2, i2 = first_max(jnp.where(lane == i1, neg, el))
    r = jnp.exp(m2 - m1)
    gate1 = p_group / (1.0 + r)
    gate2 = p_group * r / (1.0 + r)
    hot1 = lane == i1
    hot2 = lane == i2
    onehot = jnp.where(hot1 | hot2, 1.0, 0.0)
    before = _dot(ltri_ref[...], onehot.astype(BF16)) + cnt_scr[...]
    rank1 = jnp.sum(jnp.where(hot1, before, 0.0), axis=-1, keepdims=True)
    rank2 = jnp.sum(jnp.where(hot2, before, 0.0), axis=-1, keepdims=True)
    cnt = cnt_scr[...] + jnp.sum(onehot, axis=0, keepdims=True)
    cnt_scr[...] = cnt
    cnt_ref[...] = cnt
    ri = jnp.where(lane == 0, i1 - N_GROUPS,
                   jnp.where(lane == 1, i2 - N_GROUPS,
                             jnp.where(lane == 2, rank1.astype(jnp.int32),
                                       jnp.where(lane == 3, rank2.astype(jnp.int32), 0))))
    ri_ref[...] = jnp.transpose(ri)[0:ROUTE_ROWS, :]
    rg_ref[...] = jnp.where(lane == 0, gate1, jnp.where(lane == 1, gate2, 0.0))


def _route(x2, ohg, ogd, wo, norm_w, wr, tm):
    t = x2.shape[0]
    ltri = jnp.asarray(np.tril(np.ones((tm, tm), np.float32), -1), BF16)
    const = lambda i: (0, 0)
    tile = lambda n: pl.BlockSpec((tm, n), lambda i: (i, 0))
    return pl.pallas_call(
        _route_kernel,
        grid=(t // tm,),
        in_specs=[tile(D_MODEL), tile(MIX_HALF), tile(MIX_HALF),
                  pl.BlockSpec((D_MODEL, D_MODEL), const),
                  pl.BlockSpec((1, D_MODEL), const),
                  pl.BlockSpec((D_MODEL, 2 * LANES), const),
                  pl.BlockSpec((tm, tm), const)],
        out_specs=[tile(D_MODEL), tile(HALF), pl.BlockSpec((ROUTE_ROWS, tm), lambda i: (0, i)), tile(LANES),
                   pl.BlockSpec((1, LANES), const)],
        out_shape=[jax.ShapeDtypeStruct((t, D_MODEL), F32),
                   jax.ShapeDtypeStruct((t, HALF), U32),
                   jax.ShapeDtypeStruct((ROUTE_ROWS, t), jnp.int32),
                   jax.ShapeDtypeStruct((t, LANES), F32),
                   jax.ShapeDtypeStruct((1, LANES), F32)],
        scratch_shapes=[pltpu.VMEM((1, LANES), F32)],
        compiler_params=pltpu.CompilerParams(dimension_semantics=("arbitrary",), vmem_limit_bytes=VMEM_LIMIT),
        name="route",
    )(x2, ohg, ogd, wo, norm_w, wr, ltri)


ISSUE_UNROLL = 8


HALF = D_MODEL // 2
PIECES = HALF // LANES
U32 = jnp.uint32
HIGH_HALF = np.uint32(0xFFFF0000)


def _pack_words(x):
    bits = lambda a: lax.bitcast_convert_type(a.astype(BF16).astype(F32), U32)
    return (bits(x[:, 0:HALF]) >> 16) | (bits(x[:, HALF:]) & HIGH_HALF)


def _unpack_words(words):
    return (lax.bitcast_convert_type(words << 16, F32), lax.bitcast_convert_type(words & HIGH_HALF, F32))


def _store_rows_as_tiles(ref, x):
    n = x.shape[0]
    words = _pack_words(x)
    for s in range(PIECES):
        ref[pl.ds(s, n, stride=PIECES), :] = words[:, s * LANES:(s + 1) * LANES]


def _load_rows_from_tiles(ref, n):
    return _unpack_words(jnp.concatenate([ref[pl.ds(s, n, stride=PIECES), :] for s in range(PIECES)], axis=1))


SC_WINDOW = 64


def _sc_permute_rows(rows, idx, gather):
    info = plsc.get_sparse_core_info()
    n_workers = info.num_cores * info.num_subcores
    n_idx, (n_src, width) = idx.shape[0], rows.shape
    copies = 1 if gather else n_idx // n_src
    per_worker = n_idx // copies // n_workers
    n_win = per_worker // SC_WINDOW
    assert per_worker * n_workers * copies == n_idx and n_win * SC_WINDOW == per_worker and n_win % 2 == 0
    idx2 = idx.reshape(n_idx // SC_WINDOW, SC_WINDOW)
    win_per_copy = n_idx // copies // SC_WINDOW

    @functools.partial(
        pl.kernel,
        mesh=plsc.VectorSubcoreMesh(core_axis_name="c", subcore_axis_name="s"),
        out_type=jax.ShapeDtypeStruct((n_idx, width), rows.dtype),
        scratch_types=[pltpu.VMEM((copies, n_win, SC_WINDOW), jnp.int32),
                       pltpu.VMEM((2, SC_WINDOW, width), rows.dtype),
                       pltpu.SemaphoreType.DMA((2,)), pltpu.SemaphoreType.DMA((2,))],
    )
    def permute(rows_hbm, idx_hbm, out_hbm, idx_v, buf, fill_sem, drain_sem):
        worker = lax.axis_index("s") * info.num_cores + lax.axis_index("c")
        first_row = worker * per_worker
        for c in range(copies):
            pltpu.sync_copy(idx_hbm.at[pl.ds(c * win_per_copy + worker * n_win, n_win)], idx_v.at[c])

        def window(ref, j):
            return ref.at[pl.ds(pl.multiple_of(first_row + j * SC_WINDOW, SC_WINDOW), SC_WINDOW)]

        def fill(j, b):
            src = rows_hbm.at[idx_v.at[0, j]] if gather else window(rows_hbm, j)
            return pltpu.make_async_copy(src, buf.at[b], fill_sem.at[b])

        def drains(j, b):
            dsts = [window(out_hbm, j)] if gather else [out_hbm.at[idx_v.at[c, j]] for c in range(copies)]
            return [pltpu.make_async_copy(buf.at[b], dst, drain_sem.at[b]) for dst in dsts]

        fill(0, 0).start()

        @pl.loop(0, n_win, step=2)
        def _(j0):
            for b in range(2):
                j = j0 + b
                fill(j, b).wait()

                @pl.when(j >= 1)
                def _():
                    for d in drains(j - 1, 1 - b):
                        d.wait()

                @pl.when(j + 1 < n_win)
                def _():
                    fill(j + 1, 1 - b).start()

                for d in drains(j, b):
                    d.start()

        for d in drains(n_win - 1, 1):
            d.wait()

    return permute(rows, idx2)


def _sc_gather_rows(table, idx):
    return _sc_permute_rows(table, idx, gather=True)


def _sc_scatter_rows(rows, idx):
    return _sc_permute_rows(rows, idx, gather=False)


def _row_copy(src_ref, src_row, dst_ref, dst_row, sem):
    src = src_ref.at[pl.ds(pl.multiple_of(src_row * PIECES, PIECES), PIECES)]
    dst = dst_ref.at[pl.ds(pl.multiple_of(dst_row * PIECES, PIECES), PIECES)]
    return pltpu.make_async_copy(src, dst, sem)


def _dispatch_kernel(starts_ref, idx_ref, hn_ref, xs_ref, sem, *, tm):
    def issue(t, carry):
        for k in range(TOP_K):
            dest = starts_ref[idx_ref[0, 0, 4 * t + k]] + idx_ref[0, 0, 4 * t + 2 + k]
            _row_copy(hn_ref, t, xs_ref, dest, sem).start(priority=k % 2)
        return carry

    lax.fori_loop(0, tm, issue, 0, unroll=ISSUE_UNROLL)
    for k in range(TOP_K):
        pltpu.make_async_copy(hn_ref, xs_ref.at[pl.ds(0, tm * PIECES)], sem).wait()


def _dispatch(starts, idx3, hn, tm):
    t = hn.shape[0] // PIECES
    return pl.pallas_call(
        functools.partial(_dispatch_kernel, tm=tm),
        grid_spec=pltpu.PrefetchScalarGridSpec(
            num_scalar_prefetch=1,
            grid=(t // tm,),
            in_specs=[pl.BlockSpec((1, 1, 4 * tm), lambda i, s: (i, 0, 0), memory_space=pltpu.SMEM),
                      pl.BlockSpec((tm * PIECES, LANES), lambda i, s: (i, 0))],
            out_specs=pl.BlockSpec(memory_space=pl.ANY),
            scratch_shapes=[pltpu.SemaphoreType.DMA(())],
        ),
        out_shape=jax.ShapeDtypeStruct((t * TOP_K * PIECES, LANES), U32),
        compiler_params=pltpu.CompilerParams(dimension_semantics=("arbitrary",), vmem_limit_bytes=VMEM_LIMIT),
        name="dispatch",
    )(starts, idx3, hn)


EXPERT_SPLIT = 1


def _expert_kernel(blk_ref, exp_ref, lo_ref, hi_ref, xs_ref, wg_ref, wu_ref, wd_ref, ys_ref, wgu_b, wd_b, acc):
    p = pl.program_id(0)

    @pl.when(jnp.logical_or(p == 0, exp_ref[p] != exp_ref[jnp.maximum(p - 1, 0)]))
    def _():
        wgu_b[:, 0:EXPERT_FF] = wg_ref[0].astype(BF16)
        wgu_b[:, EXPERT_FF:2 * EXPERT_FF] = wu_ref[0].astype(BF16)
        wd_b[...] = wd_ref[0].astype(BF16)

    @pl.when(p == 0)
    def _():
        acc[...] = jnp.zeros_like(acc)

    first = jnp.logical_or(p == 0, blk_ref[p] != blk_ref[jnp.maximum(p - 1, 0)])
    sub = SLOT_BLOCK // EXPERT_SPLIT
    for r in range(EXPERT_SPLIT):
        x_lo, x_hi = _unpack_words(xs_ref[pl.ds(r * sub, sub), :])
        ab = _dot(x_lo.astype(BF16), wgu_b[0:HALF, :]) + _dot(x_hi.astype(BF16), wgu_b[HALF:D_MODEL, :])
        hb = _silu(ab[:, 0:EXPERT_FF]) * ab[:, EXPERT_FF:2 * EXPERT_FF]
        y = _dot(hb.astype(BF16), wd_b[...])
        slot = blk_ref[p] * SLOT_BLOCK + r * sub + lax.broadcasted_iota(jnp.int32, (sub, 1), 0)
        y = jnp.where((slot >= lo_ref[p]) & (slot < hi_ref[p]), y, 0.0)
        total = y + jnp.where(first, 0.0, acc[pl.ds(r * sub, sub), :])
        acc[pl.ds(r * sub, sub), :] = total
        ys_ref[pl.ds(r * sub, sub), :] = _pack_words(total)


def _experts(blk, exp, lo, hi, xs, wg, wu, wd):
    n_pairs = blk.shape[0]
    return pl.pallas_call(
        _expert_kernel,
        grid_spec=pltpu.PrefetchScalarGridSpec(
            num_scalar_prefetch=4,
            grid=(n_pairs,),
            in_specs=[pl.BlockSpec((SLOT_BLOCK, HALF), lambda p, blk, exp, lo, hi: (blk[p], 0)),
                      pl.BlockSpec((1, D_MODEL, EXPERT_FF), lambda p, blk, exp, lo, hi: (exp[p], 0, 0)),
                      pl.BlockSpec((1, D_MODEL, EXPERT_FF), lambda p, blk, exp, lo, hi: (exp[p], 0, 0)),
                      pl.BlockSpec((1, EXPERT_FF, D_MODEL), lambda p, blk, exp, lo, hi: (exp[p], 0, 0))],
            out_specs=pl.BlockSpec((SLOT_BLOCK, HALF), lambda p, blk, exp, lo, hi: (blk[p], 0)),
            scratch_shapes=[pltpu.VMEM((D_MODEL, 2 * EXPERT_FF), BF16), pltpu.VMEM((EXPERT_FF, D_MODEL), BF16),
                            pltpu.VMEM((SLOT_BLOCK, D_MODEL), F32)],
        ),
        out_shape=jax.ShapeDtypeStruct(xs.shape, U32),
        compiler_params=pltpu.CompilerParams(dimension_semantics=("arbitrary",), vmem_limit_bytes=VMEM_LIMIT),
        name="experts",
    )(blk, exp, lo, hi, xs, wg, wu, wd)


def _combine_kernel(h_ref, rg_ref, fw_ref, y0_ref, y1_ref, o_ref):
    rg = rg_ref[...]
    y = [jnp.concatenate(_unpack_words(ref[...]), axis=1) for ref in (y0_ref, y1_ref)]
    h = h_ref[...] + rg[:, 0:1] * y[0] + rg[:, 1:2] * y[1]
    o_ref[...] = h * lax.rsqrt(jnp.mean(h * h, axis=-1, keepdims=True) + NORM_EPS) * fw_ref[...]


def _combine(h, rg, final_w, y2, tm):
    t = h.shape[0]
    n_tiles = t // tm
    return pl.pallas_call(
        _combine_kernel,
        grid=(n_tiles,),
        in_specs=[pl.BlockSpec((tm, D_MODEL), lambda i: (i, 0)),
                  pl.BlockSpec((tm, LANES), lambda i: (i, 0)),
                  pl.BlockSpec((1, D_MODEL), lambda i: (0, 0)),
                  pl.BlockSpec((tm, HALF), lambda i: (i, 0)),
                  pl.BlockSpec((tm, HALF), lambda i: (n_tiles + i, 0))],
        out_specs=pl.BlockSpec((tm, D_MODEL), lambda i: (i, 0)),
        out_shape=jax.ShapeDtypeStruct((t, D_MODEL), F32),
        compiler_params=pltpu.CompilerParams(dimension_semantics=("arbitrary",), vmem_limit_bytes=VMEM_LIMIT),
        name="combine",
    )(h, rg, final_w, y2, y2)


def _pair_schedule(counts, n_slots):
    n_blocks = n_slots // SLOT_BLOCK
    n_pairs = n_blocks + N_EXPERTS - 1
    ends = jnp.cumsum(counts)
    starts = ends - counts
    first = starts // SLOT_BLOCK
    last = jnp.maximum(ends - 1, starts) // SLOT_BLOCK
    per_expert = jnp.where(counts > 0, last - first + 1, 0)
    cum = jnp.cumsum(per_expert)
    p = jnp.arange(n_pairs, dtype=jnp.int32)
    e = jnp.minimum(jnp.sum(cum[None, :] <= p[:, None], axis=1), N_EXPERTS - 1).astype(jnp.int32)
    valid = p < cum[-1]
    onehot = e[:, None] == jnp.arange(N_EXPERTS, dtype=jnp.int32)[None, :]
    pick = lambda table: jnp.sum(jnp.where(onehot, table[None, :], 0), axis=1)
    blk = jnp.where(valid, pick(first) + p - pick(cum - per_expert), n_blocks - 1).astype(jnp.int32)
    lo = jnp.where(valid, pick(starts), 0).astype(jnp.int32)
    hi = jnp.where(valid, pick(ends), 0).astype(jnp.int32)
    return starts.astype(jnp.int32), blk, e, lo, hi


def _layer(x, norm_mix_w, w_in, lb_logits, hgrn_onorm_w, gdn_conv_w, gdn_a_log, gdn_dt_bias, gdn_onorm_w, w_out,
           norm_moe_w, router_group_w, router_expert_w, w_gate, w_up, w_down, final_w, *, tm, n_chunks):
    batch, seq, _ = x.shape
    t = batch * seq
    x2 = x.reshape(t, D_MODEL)
    n_hg = 4 * MIX_HALF
    n_gd = GDN_QKV + MIX_HALF
    wb16 = w_in.astype(BF16)
    wa = wb16[:, 0:n_hg]
    wb = wb16[:, n_hg:n_hg + n_gd]
    wc = jnp.pad(wb16[:, n_hg + n_gd:], ((0, 0), (0, LANES - 2 * HEADS)))
    pa, pb, pc = _inproj(x2, norm_mix_w.reshape(1, D_MODEL), wa, wb, wc, gdn_conv_w, 2 * tm, seq // (2 * tm))

    o_hg = _hgrn(pa, lb_logits, hgrn_onorm_w.reshape(1, HEAD_DIM), batch, seq, n_chunks)
    alog_row = jnp.pad(gdn_a_log.reshape(1, HEADS), ((0, 0), (0, LANES - HEADS)))
    dtb_row = jnp.pad(gdn_dt_bias.reshape(1, HEADS), ((0, 0), (0, LANES - HEADS)))
    o_gd = _gdn(pb, pc, alog_row, dtb_row, gdn_onorm_w.reshape(1, HEAD_DIM), batch, seq, n_chunks)

    wr = jnp.pad(jnp.concatenate([router_group_w, router_expert_w], axis=1),
                 ((0, 0), (0, LANES - N_GROUPS - N_EXPERTS)))
    wr_hi = wr.astype(BF16)
    wr = jnp.concatenate([wr_hi, (wr - wr_hi.astype(F32)).astype(BF16)], axis=1)
    h, hn, ri, rg, cnt = _route(x2, o_hg, o_gd, w_out.astype(BF16), norm_moe_w.reshape(1, D_MODEL), wr, 2 * tm)

    counts = cnt[0, N_GROUPS:N_GROUPS + N_EXPERTS].astype(jnp.int32)
    starts, blk, exp, lo, hi = _pair_schedule(counts, t * TOP_K)
    onehot = ri[0:TOP_K, :, None] == jnp.arange(N_EXPERTS, dtype=jnp.int32)
    slot_kt = jnp.sum(jnp.where(onehot, starts, 0), axis=-1) + ri[TOP_K:2 * TOP_K]
    slots = slot_kt.reshape(TOP_K * t)
    xs = _sc_scatter_rows(hn, slots)
    ys = _experts(blk, exp, lo, hi, xs, w_gate, w_up, w_down)
    y2 = _sc_gather_rows(ys, slots)
    out = _combine(h, rg, final_w.reshape(1, D_MODEL), y2, 2 * tm)
    return out.reshape(batch, seq, D_MODEL)


def kernel(x, norm_mix_w, w_in, hgrn_lb_logits, hgrn_onorm_w, gdn_conv_w, gdn_a_log, gdn_dt_bias, gdn_onorm_w, w_out, norm_moe_w, router_group_w, router_expert_w, expert_w_gate, expert_w_up, expert_w_down, final_norm_w):
    return _layer(x, norm_mix_w[0], w_in[0], hgrn_lb_logits, hgrn_onorm_w[0], gdn_conv_w[0], gdn_a_log[0],
                  gdn_dt_bias[0], gdn_onorm_w[0], w_out[0], norm_moe_w[0], router_group_w[0], router_expert_w[0],
                  expert_w_gate[0], expert_w_up[0], expert_w_down[0], final_norm_w, tm=256, n_chunks=8)
```

```python
import functools

import numpy as np
import jax
import jax.numpy as jnp
from jax import lax
from jax.experimental import pallas as pl
from jax.experimental.pallas import tpu as pltpu
from jax.experimental.pallas import tpu_sc as plsc

F32 = jnp.float32
BF16 = jnp.bfloat16
HI = lax.Precision.HIGHEST

D_MODEL = 1024
HEADS = 4
HEAD_DIM = 128
MIX_HALF = HEADS * HEAD_DIM
CHUNK = 64
GROUP = 2
N_STACKS = HEADS // GROUP
STACK = GROUP * CHUNK
CONV_K = 4
N_GROUPS = 8
EXPERTS_PER_GROUP = 8
N_EXPERTS = N_GROUPS * EXPERTS_PER_GROUP
TOP_K = 2
EXPERT_FF = 256
NORM_EPS = 1e-6
LANES = 128
F32_ROWS = 8
SLOT_BLOCK = 512
VMEM_LIMIT = 56 * 1024 * 1024


def _sigmoid(x):
    return 0.5 * jnp.tanh(0.5 * x) + 0.5


def _silu(x):
    return x * _sigmoid(x)


def _dot(a, b, precision=None):
    return jnp.dot(a, b, preferred_element_type=F32, precision=precision)


def _dot_nt(a, b, precision=None):
    return lax.dot_general(a, b, (((1,), (1,)), ((), ())), preferred_element_type=F32, precision=precision)


def _dot_tn(a, b, precision=None):
    return lax.dot_general(a, b, (((0,), (0,)), ((), ())), preferred_element_type=F32, precision=precision)


def _bdot(a, b):
    return _dot(a.astype(BF16), b.astype(BF16))


def _masked_sum(mask3, x):
    hi = x.astype(BF16)
    r1 = x - hi.astype(F32)
    mid = r1.astype(BF16)
    lo = (r1 - mid.astype(F32)).astype(BF16)
    return _dot(mask3, jnp.concatenate([hi, mid, lo], axis=0))


def _triple(mask):
    return np.concatenate([mask, mask, mask], axis=1)


def _stack_heads(a, p):
    return jnp.concatenate([a[:, h * HEAD_DIM:(h + 1) * HEAD_DIM] for h in range(p * GROUP, (p + 1) * GROUP)],
                           axis=0)


def _each(f, *lists):
    return [f(*args) for args in zip(*lists)]


INPROJ_GROUP = 256


def _l2norm_heads(a):
    return jnp.concatenate(
        [a[:, h * HEAD_DIM:(h + 1) * HEAD_DIM]
         * lax.rsqrt(jnp.sum(jnp.square(a[:, h * HEAD_DIM:(h + 1) * HEAD_DIM]), axis=-1, keepdims=True) + 1e-6)
         for h in range(a.shape[1] // HEAD_DIM)], axis=1)


def _inproj_kernel(x_ref, nw_ref, wa_ref, wb_ref, wc_ref, cw_ref, oa_ref, ob_ref, oc_ref, ubuf, *, tiles_per_seq):
    tm = x_ref.shape[0]

    @pl.when(lax.rem(pl.program_id(0), tiles_per_seq) == 0)
    def _():
        ubuf[0:HIST, :] = jnp.zeros((HIST, GDN_QKV), F32)

    x = x_ref[...]
    ms = jnp.mean(x * x, axis=-1, keepdims=True)
    hn = (x * lax.rsqrt(ms + NORM_EPS) * nw_ref[...]).astype(BF16)
    cw = cw_ref[...]
    width = INPROJ_GROUP
    group = lambda g: slice(g * width, (g + 1) * width)

    def project(w_ref, o_ref, g, row0=0):
        o_ref[row0:row0 + tm, group(g)] = _dot(hn, w_ref[:, group(g)])

    def conv_group(g):
        cols = group(g)
        conv = cw[CONV_K - 1:CONV_K, cols] * ubuf[HIST:HIST + tm, cols]
        for j in range(1, CONV_K):
            conv = conv + cw[CONV_K - 1 - j:CONV_K - j, cols] * ubuf[HIST - j:HIST - j + tm, cols]
        ubuf[0:HIST, cols] = ubuf[tm:tm + HIST, cols]
        act = _silu(conv)
        if g * width < MIX_HALF:
            act = _l2norm_heads(act) * (HEAD_DIM ** -0.5)
        elif g * width < 2 * MIX_HALF:
            act = _l2norm_heads(act)
        ob_ref[:, cols] = act

    n_conv = GDN_QKV // width
    n_a = wa_ref.shape[1] // width
    others = [functools.partial(project, wa_ref, oa_ref, g) for g in range(n_a)]
    others += [functools.partial(project, wb_ref, ob_ref, g) for g in range(n_conv, wb_ref.shape[1] // width)]
    project(wb_ref, ubuf, 0, HIST)
    for g in range(n_conv):
        if g + 1 < n_conv:
            project(wb_ref, ubuf, g + 1, HIST)
        conv_group(g)
        others.pop(0)()
    for task in others:
        task()
    oc_ref[...] = _dot(hn, wc_ref[...])


def _inproj(x2, norm_w, wa, wb, wc, conv_w, tm, tiles_per_seq):
    t = x2.shape[0]
    na, nb, nc = wa.shape[1], wb.shape[1], wc.shape[1]
    const = lambda i: (0, 0)
    return pl.pallas_call(
        functools.partial(_inproj_kernel, tiles_per_seq=tiles_per_seq),
        grid=(t // tm,),
        in_specs=[
            pl.BlockSpec((tm, D_MODEL), lambda i: (i, 0)),
            pl.BlockSpec((1, D_MODEL), const),
            pl.BlockSpec((D_MODEL, na), const),
            pl.BlockSpec((D_MODEL, nb), const),
            pl.BlockSpec((D_MODEL, nc), const),
            pl.BlockSpec((CONV_K, GDN_QKV), const),
        ],
        out_specs=[
            pl.BlockSpec((tm, na), lambda i: (i, 0)),
            pl.BlockSpec((tm, nb), lambda i: (i, 0)),
            pl.BlockSpec((tm, nc), lambda i: (i, 0)),
        ],
        out_shape=[
            jax.ShapeDtypeStruct((t, na), F32),
            jax.ShapeDtypeStruct((t, nb), F32),
            jax.ShapeDtypeStruct((t, nc), F32),
        ],
        scratch_shapes=[pltpu.VMEM((HIST + tm, GDN_QKV), F32)],
        compiler_params=pltpu.CompilerParams(dimension_semantics=("arbitrary",), vmem_limit_bytes=VMEM_LIMIT),
        name="inproj",
    )(x2, norm_w, wa, wb, wc, conv_w)


HGRN_LEVELS = (32, 16, 8, 4, 2, 1)
DIAG_CODE = len(HGRN_LEVELS)
NONE_CODE = DIAG_CODE + 1


def _hgrn_arg_matrix():
    t = np.arange(CHUNK)[:, None]
    u = np.arange(CHUNK)[None, :]
    mats = [u <= t]
    for b in HGRN_LEVELS:
        odd = (t // b) % 2 == 1
        start = (t // b) * b
        mats.append(np.where(odd, (u > start) & (u <= t), (u > t) & (u <= start + b)))
    return np.concatenate(mats, axis=0).astype(np.float32)


def _hgrn_level_codes():
    idx = np.arange(STACK)
    h, t = idx // CHUNK, idx % CHUNK
    same = h[:, None] == h[None, :]
    tt, ss = t[:, None], t[None, :]
    code = np.full((STACK, STACK), NONE_CODE, np.int32)
    code[same & (tt == ss)] = DIAG_CODE
    for l, b in enumerate(HGRN_LEVELS):
        sib = (tt // (2 * b) == ss // (2 * b)) & ((tt // b) % 2 == 1) & ((ss // b) % 2 == 0)
        code[same & sib] = l
    return code


def _hgrn_kernel(q_ref, f_ref, i_ref, g_ref, lbl_ref, onw_ref, marg_ref, code_ref, o_ref, st_ref, *, n_chunks):
    @pl.when(pl.program_id(1) == 0)
    def _():
        st_ref[...] = jnp.zeros_like(st_ref)

    lbl = lbl_ref[...]
    lmax = jnp.max(lbl, axis=0, keepdims=True)
    lexp = jnp.exp(lbl - lmax)
    lb = lexp[0:1, :] / jnp.sum(lexp, axis=0, keepdims=True)
    onw = onw_ref[...]
    marg = marg_ref[...]
    code = code_ref[...]
    trow = lax.broadcasted_iota(jnp.int32, (STACK, HEAD_DIM), 0) & (CHUNK - 1)

    q_blk = _silu(q_ref[...]) * (HEAD_DIM ** -0.5)
    fg_blk = lb + (1.0 - lb) * _sigmoid(f_ref[...])
    k_blk = 1.0 - fg_blk
    lf_blk = jnp.log(fg_blk)
    v_blk = i_ref[...]

    rows = [slice(c * CHUNK, (c + 1) * CHUNK) for c in range(n_chunks)]
    args = [_masked_sum(marg, lf_blk[r]) for r in rows]
    e_chunk = [jnp.exp(a) for a in args]
    suf_chunk = [jnp.exp(a[CHUNK - 1:CHUNK] - a[0:CHUNK]) for a in args]
    units = [(c, p) for c in range(n_chunks) for p in range(N_STACKS)]
    qs = [_stack_heads(q_blk[rows[c]], p) for c, p in units]
    ks = [_stack_heads(k_blk[rows[c]], p) for c, p in units]
    vs = [_stack_heads(v_blk[rows[c]], p).astype(BF16) for c, p in units]
    e_part = lambda n: [_stack_heads(e_chunk[c][n * CHUNK:(n + 1) * CHUNK], p) for c, p in units]
    cum_e = e_part(0)
    suf_e = [_stack_heads(suf_chunk[c], p) for c, p in units]
    att = _each(lambda q, k: jnp.where(code == DIAG_CODE, jnp.sum(q * k, axis=-1, keepdims=True), 0.0), qs, ks)

    def level_update(a, x, l, b):
        xb = x.astype(BF16)
        if b < F32_ROWS:
            return jnp.where(code == l, _dot_nt(xb, xb), a)
        blocks = [slice(s0, s0 + b) for s0 in range(0, STACK, b)]
        r = _dot_nt(jnp.concatenate([x[bl] for bl in blocks[1::2]], axis=0).astype(BF16), xb)
        return jnp.concatenate(
            [a[bl] if n % 2 == 0 else jnp.where(code[bl] == l, r[(n // 2) * b:(n // 2 + 1) * b], a[bl])
             for n, bl in enumerate(blocks)], axis=0)

    for l, b in enumerate(HGRN_LEVELS):
        x = _each(lambda q, k, el: jnp.where((trow & b) != 0, q, k) * el, qs, ks, e_part(1 + l))
        att = _each(lambda xi, a: level_update(a, xi, l, b), x, att)
    o_intra = _each(lambda a, v: _dot(a.astype(BF16), v), att, vs)
    qc = _each(lambda q, e: (q * e).astype(BF16), qs, cum_e)
    kd = _each(lambda k, e: (k * e).astype(BF16), ks, suf_e)

    state = [st_ref[h] for h in range(HEADS)]
    for u, (c, p) in enumerate(units):
        for i in range(GROUP):
            h = p * GROUP + i
            hr = slice(i * CHUNK, (i + 1) * CHUNK)
            hc = slice(h * HEAD_DIM, (h + 1) * HEAD_DIM)
            o = _dot_nt(qc[u][hr], state[h].astype(BF16)) + o_intra[u][hr]
            decay = cum_e[u][i * CHUNK + CHUNK - 1:i * CHUNK + CHUNK, :]
            state[h] = decay * state[h] + _dot_tn(vs[u][hr], kd[u][hr])
            y = o * lax.rsqrt(jnp.mean(o * o, axis=-1, keepdims=True) + NORM_EPS) * onw
            o_ref[rows[c], hc] = (y * _silu(g_ref[rows[c], hc])).astype(o_ref.dtype)
    for h in range(HEADS):
        st_ref[h] = state[h]


def _hgrn(pa, lb_logits, onorm_w, batch, seq, n_chunks):
    rows = n_chunks * CHUNK
    steps = seq // rows
    col = lambda j: pl.BlockSpec((rows, MIX_HALF), lambda b, s, j=j: (b * steps + s, j))
    const = lambda b, s: (0, 0)
    marg = jnp.asarray(_triple(_hgrn_arg_matrix()), BF16)
    code = jnp.asarray(_hgrn_level_codes())
    return pl.pallas_call(
        functools.partial(_hgrn_kernel, n_chunks=n_chunks),
        grid=(batch, steps),
        in_specs=[col(0), col(1), col(2), col(3),
                  pl.BlockSpec(lb_logits.shape, const),
                  pl.BlockSpec((1, HEAD_DIM), const),
                  pl.BlockSpec(marg.shape, const),
                  pl.BlockSpec(code.shape, const)],
        out_specs=pl.BlockSpec((rows, MIX_HALF), lambda b, s: (b * steps + s, 0)),
        out_shape=jax.ShapeDtypeStruct((batch * seq, MIX_HALF), BF16),
        scratch_shapes=[pltpu.VMEM((HEADS, HEAD_DIM, HEAD_DIM), F32)],
        compiler_params=pltpu.CompilerParams(dimension_semantics=("arbitrary", "arbitrary"),
                                             vmem_limit_bytes=VMEM_LIMIT),
        name="hgrn2",
    )(pa, pa, pa, pa, lb_logits, onorm_w, marg, code)


GDN_QKV = 3 * MIX_HALF
HIST = 8
C_NONE, C_DIAG, C_B16, C_B32, C_B64 = 0, 1, 2, 3, 4
MASKED_EXPONENT = -1e30


def _gdn_codes():
    idx = np.arange(STACK)
    h, t = idx // CHUNK, idx % CHUNK
    same = h[:, None] == h[None, :]
    tt, ss = t[:, None], t[None, :]
    code = np.full((STACK, STACK), C_NONE, np.int32)
    low = same & (ss < tt)
    code[low] = C_B64
    code[low & (tt // 32 == ss // 32)] = C_B32
    code[low & (tt // 16 == ss // 16)] = C_B16
    code[same & (tt == ss)] = C_DIAG
    incl = np.tril(np.ones((CHUNK, CHUNK), np.float32))
    return code, incl


def _gdn_kernel(qkv_ref, z_ref, ab_ref, alog_ref, dtb_ref, onw_ref, code_ref, incl_ref,
                o_ref, st_ref, *, n_chunks):
    @pl.when(pl.program_id(1) == 0)
    def _():
        st_ref[...] = jnp.zeros_like(st_ref)

    onw = onw_ref[...]
    code = code_ref[...]
    incl_m = incl_ref[...]
    incl = code >= C_DIAG
    eye = (code == C_DIAG).astype(F32)

    rows = [slice(c * CHUNK, (c + 1) * CHUNK) for c in range(n_chunks)]
    units = [(c, p) for c in range(n_chunks) for p in range(N_STACKS)]

    qs = [_stack_heads(qkv_ref[rows[c], 0:MIX_HALF], p) for c, p in units]
    ks = [_stack_heads(qkv_ref[rows[c], MIX_HALF:2 * MIX_HALF], p) for c, p in units]
    vs = [_stack_heads(qkv_ref[rows[c], 2 * MIX_HALF:3 * MIX_HALF], p) for c, p in units]
    ab_blk = ab_ref[...]
    xa = ab_blk + dtb_ref[...]
    softplus = jnp.maximum(xa, 0.0) + jnp.log(1.0 + jnp.exp(-jnp.abs(xa)))
    g_all = -jnp.exp(alog_ref[...]) * softplus
    beta_all = _sigmoid(ab_blk)

    def head_cols(a, c, p, first_lane):
        return jnp.concatenate([a[c][:, first_lane + h:first_lane + h + 1]
                                for h in range(p * GROUP, (p + 1) * GROUP)], axis=0)

    g_cum = [_masked_sum(incl_m, g_all[r]) for r in rows]
    beta_chunk = [beta_all[r] for r in rows]
    beta_st = [head_cols(beta_chunk, c, p, HEADS) for c, p in units]
    gc = [jnp.broadcast_to(head_cols(g_cum, c, p, 0), (STACK, HEAD_DIM)) for c, p in units]
    dec = [jnp.exp(jnp.where(incl, g - jnp.transpose(g)[0:1, :], MASKED_EXPONENT)) for g in gc]
    kb = [k.astype(BF16) for k in ks]
    kq = _each(lambda k, q: _dot_nt(jnp.concatenate([k, q.astype(BF16)], axis=0), k), kb, qs)
    kk = [a[0:STACK] for a in kq]
    qk = [a[STACK:] for a in kq]
    am = _each(lambda b, k2, d: b * k2 * d, beta_st, kk, dec)
    a16 = [jnp.where(code == C_B16, a, 0.0) for a in am]
    n32 = [jnp.where(code == C_B32, a, 0.0) for a in am]
    n64 = [jnp.where(code == C_B64, a, 0.0) for a in am]
    apow = _each(_bdot, a16, a16)
    pinv = [eye - a for a in a16]
    for _ in range(2):
        both = _each(lambda ai, pi: _bdot(jnp.concatenate([ai, pi], axis=0), ai), apow, pinv)
        pinv = _each(lambda pi, b: pi + b[STACK:], pinv, both)
        apow = [b[0:STACK] for b in both]
    pinv = _each(lambda pi, ai: pi + _bdot(pi, ai), pinv, apow)
    for nlev in (n32, n64):
        t = _each(_bdot, nlev, pinv)
        pinv = _each(lambda pi, ti: pi - _bdot(pi, ti), pinv, t)
    egc = [jnp.exp(g) for g in gc]
    rhs = _each(lambda k, v, b, e: jnp.concatenate([k * (b * e), v * b], axis=1), ks, vs, beta_st, egc)
    wu = _each(_bdot, pinv, rhs)
    w_c = [a[:, 0:HEAD_DIM].astype(BF16) for a in wu]
    u_c = [a[:, HEAD_DIM:] for a in wu]
    qkm = _each(lambda a, d: (a * d).astype(BF16), qk, dec)
    qg = _each(lambda q, e: (q * e).astype(BF16), qs, egc)
    glast = [[g[i * CHUNK + CHUNK - 1:(i + 1) * CHUNK, :] for i in range(GROUP)] for g in gc]
    kdec = [[(k[i * CHUNK:(i + 1) * CHUNK] * jnp.exp(gl[i] - g[i * CHUNK:(i + 1) * CHUNK])).astype(BF16)
             for i in range(GROUP)] for k, g, gl in zip(ks, gc, glast)]
    gend = [[jnp.exp(gi) for gi in gl] for gl in glast]

    state = [st_ref[h] for h in range(HEADS)]
    for u, (c, p) in enumerate(units):
        heads = range(p * GROUP, (p + 1) * GROUP)
        hrs = [slice(i * CHUNK, (i + 1) * CHUNK) for i in range(GROUP)]
        ws = [_dot_nt(jnp.concatenate([w_c[u][hr], qg[u][hr]], axis=0), state[h].astype(BF16))
              for hr, h in zip(hrs, heads)]
        v_new = jnp.concatenate([u_c[u][hr] - a[0:CHUNK] for hr, a in zip(hrs, ws)], axis=0).astype(BF16)
        o_st = jnp.concatenate([a[CHUNK:] for a in ws], axis=0) + _dot(qkm[u], v_new)
        for i, h in enumerate(heads):
            hc = slice(h * HEAD_DIM, (h + 1) * HEAD_DIM)
            state[h] = gend[u][i] * state[h] + _dot_tn(v_new[hrs[i]], kdec[u][i])
            o = o_st[hrs[i]]
            y = o * lax.rsqrt(jnp.mean(o * o, axis=-1, keepdims=True) + NORM_EPS) * onw
            o_ref[rows[c], hc] = (y * _silu(z_ref[rows[c], hc])).astype(o_ref.dtype)
    for h in range(HEADS):
        st_ref[h] = state[h]


def _gdn(pb, pc, alog_row, dtb_row, onorm_w, batch, seq, n_chunks):
    rows = n_chunks * CHUNK
    steps = seq // rows
    const = lambda b, s: (0, 0)
    code, incl = _gdn_codes()
    code, incl = jnp.asarray(code), jnp.asarray(_triple(incl), BF16)
    sq = pl.BlockSpec((STACK, STACK), const)
    row = pl.BlockSpec((1, LANES), const)
    return pl.pallas_call(
        functools.partial(_gdn_kernel, n_chunks=n_chunks),
        grid=(batch, steps),
        in_specs=[pl.BlockSpec((rows, GDN_QKV), lambda b, s: (b * steps + s, 0)),
                  pl.BlockSpec((rows, MIX_HALF), lambda b, s: (b * steps + s, GDN_QKV // MIX_HALF)),
                  pl.BlockSpec((rows, LANES), lambda b, s: (b * steps + s, 0)),
                  row, row, row, sq, pl.BlockSpec((CHUNK, 3 * CHUNK), const)],
        out_specs=pl.BlockSpec((rows, MIX_HALF), lambda b, s: (b * steps + s, 0)),
        out_shape=jax.ShapeDtypeStruct((batch * seq, MIX_HALF), BF16),
        scratch_shapes=[pltpu.VMEM((HEADS, HEAD_DIM, HEAD_DIM), F32)],
        compiler_params=pltpu.CompilerParams(dimension_semantics=("arbitrary", "arbitrary"),
                                             vmem_limit_bytes=VMEM_LIMIT),
        name="gdn",
    )(pb, pb, pc, alog_row, dtb_row, onorm_w, code, incl)


ROUTE_ROWS = 8


def _route_kernel(x_ref, ohg_ref, ogd_ref, wo_ref, nw_ref, wr_ref, ltri_ref,
                  h_ref, hn_ref, ri_ref, rg_ref, cnt_ref, cnt_scr):
    @pl.when(pl.program_id(0) == 0)
    def _():
        cnt_scr[...] = jnp.zeros_like(cnt_scr)

    mix = _dot(ohg_ref[...], wo_ref[0:MIX_HALF, :]) + _dot(ogd_ref[...], wo_ref[MIX_HALF:2 * MIX_HALF, :])
    h = x_ref[...] + mix
    h_ref[...] = h
    hn = h * lax.rsqrt(jnp.mean(h * h, axis=-1, keepdims=True) + NORM_EPS) * nw_ref[...]
    hn_ref[...] = _pack_words(hn)
    hn_hi = hn.astype(BF16)
    hn_lo = (hn - hn_hi.astype(F32)).astype(BF16)
    part = _dot(hn_hi, wr_ref[...])
    logits = (_dot(hn_lo, wr_ref[:, 0:LANES]) + part[:, LANES:]) + part[:, 0:LANES]
    tm = logits.shape[0]
    lane = lax.broadcasted_iota(jnp.int32, (tm, LANES), 1)
    neg = jnp.float32(-jnp.inf)
    big = jnp.int32(LANES)

    def first_max(vals):
        m = jnp.max(vals, axis=-1, keepdims=True)
        return m, jnp.min(jnp.where(vals == m, lane, big), axis=-1, keepdims=True)

    gl = jnp.where(lane < N_GROUPS, logits, neg)
    gmax, gidx = first_max(gl)
    p_group = 1.0 / jnp.sum(jnp.exp(gl - gmax), axis=-1, keepdims=True)
    lo = N_GROUPS + EXPERTS_PER_GROUP * gidx
    el = jnp.where((lane >= lo) & (lane < lo + EXPERTS_PER_GROUP), logits, neg)
    m1, i1 = first_max(el)
    m2, i2 = first_max(jnp.where(lane == i1, neg, el))
    r = jnp.exp(m2 - m1)
    gate1 = p_group / (1.0 + r)
    gate2 = p_group * r / (1.0 + r)
    hot1 = lane == i1
    hot2 = lane == i2
    onehot = jnp.where(hot1 | hot2, 1.0, 0.0)
    before = _dot(ltri_ref[...], onehot.astype(BF16)) + cnt_scr[...]
    rank1 = jnp.sum(jnp.where(hot1, before, 0.0), axis=-1, keepdims=True)
    rank2 = jnp.sum(jnp.where(hot2, before, 0.0), axis=-1, keepdims=True)
    cnt = cnt_scr[...] + jnp.sum(onehot, axis=0, keepdims=True)
    cnt_scr[...] = cnt
    cnt_ref[...] = cnt
    ri = jnp.where(lane == 0, i1 - N_GROUPS,
                   jnp.where(lane == 1, i2 - N_GROUPS,
                             jnp.where(lane == 2, rank1.astype(jnp.int32),
                                       jnp.where(lane == 3, rank2.astype(jnp.int32), 0))))
    ri_ref[...] = jnp.transpose(ri)[0:ROUTE_ROWS, :]
    rg_ref[...] = jnp.where(lane == 0, gate1, jnp.where(lane == 1, gate2, 0.0))


def _route(x2, ohg, ogd, wo, norm_w, wr, tm, part, n_parts):
    t = x2.shape[0] // n_parts
    first = part * (t // tm)
    ltri = jnp.asarray(np.tril(np.ones((tm, tm), np.float32), -1), BF16)
    const = lambda i: (0, 0)
    tile = lambda n: pl.BlockSpec((tm, n), lambda i: (i, 0))
    tile_in = lambda n: pl.BlockSpec((tm, n), lambda i: (first + i, 0))
    return pl.pallas_call(
        _route_kernel,
        grid=(t // tm,),
        in_specs=[tile_in(D_MODEL), tile_in(MIX_HALF), tile_in(MIX_HALF),
                  pl.BlockSpec((D_MODEL, D_MODEL), const),
                  pl.BlockSpec((1, D_MODEL), const),
                  pl.BlockSpec((D_MODEL, 2 * LANES), const),
                  pl.BlockSpec((tm, tm), const)],
        out_specs=[tile(D_MODEL), tile(HALF), pl.BlockSpec((ROUTE_ROWS, tm), lambda i: (0, i)), tile(LANES),
                   pl.BlockSpec((1, LANES), const)],
        out_shape=[jax.ShapeDtypeStruct((t, D_MODEL), F32),
                   jax.ShapeDtypeStruct((t, HALF), WORD),
                   jax.ShapeDtypeStruct((ROUTE_ROWS, t), jnp.int32),
                   jax.ShapeDtypeStruct((t, LANES), F32),
                   jax.ShapeDtypeStruct((1, LANES), F32)],
        scratch_shapes=[pltpu.VMEM((1, LANES), F32)],
        compiler_params=pltpu.CompilerParams(dimension_semantics=("arbitrary",), vmem_limit_bytes=VMEM_LIMIT),
        name="route",
    )(x2, ohg, ogd, wo, norm_w, wr, ltri)


HALF = D_MODEL // 2
WORD = jnp.uint32
HIGH_HALF = np.uint32(0xFFFF0000)


def _pack_words(x):
    bits = lambda a: lax.bitcast_convert_type(a.astype(BF16).astype(F32), WORD)
    return (bits(x[:, 0:HALF]) >> 16) | (bits(x[:, HALF:]) & HIGH_HALF)


def _unpack_words(words):
    return (lax.bitcast_convert_type(words << 16, F32), lax.bitcast_convert_type(words & HIGH_HALF, F32))


SC_WINDOW = 64


def _sc_permute_rows(rows, idx, gather):
    info = plsc.get_sparse_core_info()
    n_workers = info.num_cores * info.num_subcores
    n_idx, (n_src, width) = idx.shape[0], rows.shape
    copies = 1 if gather else n_idx // n_src
    per_worker = n_idx // copies // n_workers
    n_win = per_worker // SC_WINDOW
    assert per_worker * n_workers * copies == n_idx and n_win * SC_WINDOW == per_worker and n_win % 2 == 0
    idx2 = idx.reshape(n_idx // SC_WINDOW, SC_WINDOW)
    win_per_copy = n_idx // copies // SC_WINDOW

    @functools.partial(
        pl.kernel,
        mesh=plsc.VectorSubcoreMesh(core_axis_name="c", subcore_axis_name="s"),
        out_type=jax.ShapeDtypeStruct((n_idx, width), rows.dtype),
        scratch_types=[pltpu.VMEM((copies, n_win, SC_WINDOW), jnp.int32),
                       pltpu.VMEM((2, SC_WINDOW, width), rows.dtype),
                       pltpu.SemaphoreType.DMA((2,)), pltpu.SemaphoreType.DMA((2,))],
    )
    def permute(rows_hbm, idx_hbm, out_hbm, idx_v, buf, fill_sem, drain_sem):
        worker = lax.axis_index("s") * info.num_cores + lax.axis_index("c")
        first_row = worker * per_worker
        for c in range(copies):
            pltpu.sync_copy(idx_hbm.at[pl.ds(c * win_per_copy + worker * n_win, n_win)], idx_v.at[c])

        def window(ref, j):
            return ref.at[pl.ds(pl.multiple_of(first_row + j * SC_WINDOW, SC_WINDOW), SC_WINDOW)]

        def fill(j, b):
            src = rows_hbm.at[idx_v.at[0, j]] if gather else window(rows_hbm, j)
            return pltpu.make_async_copy(src, buf.at[b], fill_sem.at[b])

        def drains(j, b):
            dsts = [window(out_hbm, j)] if gather else [out_hbm.at[idx_v.at[c, j]] for c in range(copies)]
            return [pltpu.make_async_copy(buf.at[b], dst, drain_sem.at[b]) for dst in dsts]

        fill(0, 0).start()

        @pl.loop(0, n_win, step=2)
        def _(j0):
            for b in range(2):
                j = j0 + b
                fill(j, b).wait()

                @pl.when(j >= 1)
                def _():
                    for d in drains(j - 1, 1 - b):
                        d.wait()

                @pl.when(j + 1 < n_win)
                def _():
                    fill(j + 1, 1 - b).start()

                for d in drains(j, b):
                    d.start()

        for d in drains(n_win - 1, 1):
            d.wait()

    return permute(rows, idx2)


def _sc_gather_rows(table, idx):
    return _sc_permute_rows(table, idx, gather=True)


def _sc_scatter_rows(rows, idx):
    return _sc_permute_rows(rows, idx, gather=False)


EXPERT_SPLIT = 1


def _expert_kernel(blk_ref, exp_ref, lo_ref, hi_ref, xs_ref, wg_ref, wu_ref, wd_ref, ys_ref, wgu_b, wd_b, acc):
    p = pl.program_id(0)

    @pl.when(jnp.logical_or(p == 0, exp_ref[p] != exp_ref[jnp.maximum(p - 1, 0)]))
    def _():
        wgu_b[:, 0:EXPERT_FF] = wg_ref[0].astype(BF16)
        wgu_b[:, EXPERT_FF:2 * EXPERT_FF] = wu_ref[0].astype(BF16)
        wd_b[...] = wd_ref[0].astype(BF16)

    @pl.when(p == 0)
    def _():
        acc[...] = jnp.zeros_like(acc)

    first = jnp.logical_or(p == 0, blk_ref[p] != blk_ref[jnp.maximum(p - 1, 0)])
    sub = SLOT_BLOCK // EXPERT_SPLIT
    for r in range(EXPERT_SPLIT):
        x_lo, x_hi = _unpack_words(xs_ref[pl.ds(r * sub, sub), :])
        ab = _dot(x_lo.astype(BF16), wgu_b[0:HALF, :]) + _dot(x_hi.astype(BF16), wgu_b[HALF:D_MODEL, :])
        hb = _silu(ab[:, 0:EXPERT_FF]) * ab[:, EXPERT_FF:2 * EXPERT_FF]
        y = _dot(hb.astype(BF16), wd_b[...])
        slot = blk_ref[p] * SLOT_BLOCK + r * sub + lax.broadcasted_iota(jnp.int32, (sub, 1), 0)
        y = jnp.where((slot >= lo_ref[p]) & (slot < hi_ref[p]), y, 0.0)
        total = y + jnp.where(first, 0.0, acc[pl.ds(r * sub, sub), :])
        acc[pl.ds(r * sub, sub), :] = total
        ys_ref[pl.ds(r * sub, sub), :] = _pack_words(total)


def _experts(blk, exp, lo, hi, xs, wg, wu, wd):
    n_pairs = blk.shape[0]
    return pl.pallas_call(
        _expert_kernel,
        grid_spec=pltpu.PrefetchScalarGridSpec(
            num_scalar_prefetch=4,
            grid=(n_pairs,),
            in_specs=[pl.BlockSpec((SLOT_BLOCK, HALF), lambda p, blk, exp, lo, hi: (blk[p], 0)),
                      pl.BlockSpec((1, D_MODEL, EXPERT_FF), lambda p, blk, exp, lo, hi: (exp[p], 0, 0)),
                      pl.BlockSpec((1, D_MODEL, EXPERT_FF), lambda p, blk, exp, lo, hi: (exp[p], 0, 0)),
                      pl.BlockSpec((1, EXPERT_FF, D_MODEL), lambda p, blk, exp, lo, hi: (exp[p], 0, 0))],
            out_specs=pl.BlockSpec((SLOT_BLOCK, HALF), lambda p, blk, exp, lo, hi: (blk[p], 0)),
            scratch_shapes=[pltpu.VMEM((D_MODEL, 2 * EXPERT_FF), BF16), pltpu.VMEM((EXPERT_FF, D_MODEL), BF16),
                            pltpu.VMEM((SLOT_BLOCK, D_MODEL), F32)],
        ),
        out_shape=jax.ShapeDtypeStruct(xs.shape, WORD),
        compiler_params=pltpu.CompilerParams(dimension_semantics=("arbitrary",), vmem_limit_bytes=VMEM_LIMIT),
        name="experts",
    )(blk, exp, lo, hi, xs, wg, wu, wd)


MOE_PARTS = 2


def _combine_kernel(h_ref, rg_ref, fw_ref, y0_ref, y1_ref, *rest):
    o_ref = rest[-1]
    rg = rg_ref[...]
    y = [jnp.concatenate(_unpack_words(ref[...]), axis=1) for ref in (y0_ref, y1_ref)]
    h = h_ref[...] + rg[:, 0:1] * y[0] + rg[:, 1:2] * y[1]
    o_ref[...] = h * lax.rsqrt(jnp.mean(h * h, axis=-1, keepdims=True) + NORM_EPS) * fw_ref[...]


def _combine(h, rg, final_w, y2, tm, part, n_parts, out_so_far=None):
    t = h.shape[0] * n_parts
    n_tiles = h.shape[0] // tm
    first = part * n_tiles
    in_specs = [pl.BlockSpec((tm, D_MODEL), lambda i: (i, 0)),
                pl.BlockSpec((tm, LANES), lambda i: (i, 0)),
                pl.BlockSpec((1, D_MODEL), lambda i: (0, 0)),
                pl.BlockSpec((tm, HALF), lambda i: (i, 0)),
                pl.BlockSpec((tm, HALF), lambda i: (n_tiles + i, 0))]
    args = [h, rg, final_w, y2, y2]
    aliases = {}
    if out_so_far is not None:
        in_specs.append(pl.BlockSpec(memory_space=pl.ANY))
        aliases = {len(args): 0}
        args.append(out_so_far)
    return pl.pallas_call(
        _combine_kernel,
        grid=(n_tiles,),
        in_specs=in_specs,
        out_specs=pl.BlockSpec((tm, D_MODEL), lambda i: (first + i, 0)),
        out_shape=jax.ShapeDtypeStruct((t, D_MODEL), F32),
        input_output_aliases=aliases,
        compiler_params=pltpu.CompilerParams(dimension_semantics=("arbitrary",), vmem_limit_bytes=VMEM_LIMIT),
        name="combine",
    )(*args)


def _pair_schedule(counts, n_slots):
    n_blocks = n_slots // SLOT_BLOCK
    n_pairs = n_blocks + N_EXPERTS - 1
    ends = jnp.cumsum(counts)
    starts = ends - counts
    first = starts // SLOT_BLOCK
    last = jnp.maximum(ends - 1, starts) // SLOT_BLOCK
    per_expert = jnp.where(counts > 0, last - first + 1, 0)
    cum = jnp.cumsum(per_expert)
    p = jnp.arange(n_pairs, dtype=jnp.int32)
    e = jnp.minimum(jnp.sum(cum[None, :] <= p[:, None], axis=1), N_EXPERTS - 1).astype(jnp.int32)
    valid = p < cum[-1]
    onehot = e[:, None] == jnp.arange(N_EXPERTS, dtype=jnp.int32)[None, :]
    pick = lambda table: jnp.sum(jnp.where(onehot, table[None, :], 0), axis=1)
    blk = jnp.where(valid, pick(first) + p - pick(cum - per_expert), n_blocks - 1).astype(jnp.int32)
    lo = jnp.where(valid, pick(starts), 0).astype(jnp.int32)
    hi = jnp.where(valid, pick(ends), 0).astype(jnp.int32)
    return starts.astype(jnp.int32), blk, e, lo, hi


def _layer(x, norm_mix_w, w_in, lb_logits, hgrn_onorm_w, gdn_conv_w, gdn_a_log, gdn_dt_bias, gdn_onorm_w, w_out,
           norm_moe_w, router_group_w, router_expert_w, w_gate, w_up, w_down, final_w, *, tm, n_chunks):
    batch, seq, _ = x.shape
    t = batch * seq
    x2 = x.reshape(t, D_MODEL)
    n_hg = 4 * MIX_HALF
    n_gd = GDN_QKV + MIX_HALF
    wb16 = w_in.astype(BF16)
    wa = wb16[:, 0:n_hg]
    wb = wb16[:, n_hg:n_hg + n_gd]
    wc = jnp.pad(wb16[:, n_hg + n_gd:], ((0, 0), (0, LANES - 2 * HEADS)))
    pa, pb, pc = _inproj(x2, norm_mix_w.reshape(1, D_MODEL), wa, wb, wc, gdn_conv_w, 2 * tm, seq // (2 * tm))

    o_hg = _hgrn(pa, lb_logits, hgrn_onorm_w.reshape(1, HEAD_DIM), batch, seq, n_chunks)
    alog_row = jnp.pad(gdn_a_log.reshape(1, HEADS), ((0, 0), (0, LANES - HEADS)))
    dtb_row = jnp.pad(gdn_dt_bias.reshape(1, HEADS), ((0, 0), (0, LANES - HEADS)))
    o_gd = _gdn(pb, pc, alog_row, dtb_row, gdn_onorm_w.reshape(1, HEAD_DIM), batch, seq, n_chunks)

    wr = jnp.pad(jnp.concatenate([router_group_w, router_expert_w], axis=1),
                 ((0, 0), (0, LANES - N_GROUPS - N_EXPERTS)))
    wr_hi = wr.astype(BF16)
    wr = jnp.concatenate([wr_hi, (wr - wr_hi.astype(F32)).astype(BF16)], axis=1)
    wo = w_out.astype(BF16)
    tp = t // MOE_PARTS
    routed = [_route(x2, o_hg, o_gd, wo, norm_moe_w.reshape(1, D_MODEL), wr, 2 * tm, part, MOE_PARTS)
              for part in range(MOE_PARTS)]
    out = None
    for part, (h, hn, ri, rg, cnt) in enumerate(routed):
        counts = cnt[0, N_GROUPS:N_GROUPS + N_EXPERTS].astype(jnp.int32)
        starts, blk, exp, lo, hi = _pair_schedule(counts, tp * TOP_K)
        onehot = ri[0:TOP_K, :, None] == jnp.arange(N_EXPERTS, dtype=jnp.int32)
        slots = (jnp.sum(jnp.where(onehot, starts, 0), axis=-1) + ri[TOP_K:2 * TOP_K]).reshape(TOP_K * tp)
        xs = _sc_scatter_rows(hn, slots)
        ys = _experts(blk, exp, lo, hi, xs, w_gate, w_up, w_down)
        y2 = _sc_gather_rows(ys, slots)
        out = _combine(h, rg, final_w.reshape(1, D_MODEL), y2, 2 * tm, part, MOE_PARTS, out)
    return out.reshape(batch, seq, D_MODEL)


def kernel(x, norm_mix_w, w_in, hgrn_lb_logits, hgrn_onorm_w, gdn_conv_w, gdn_a_log, gdn_dt_bias, gdn_onorm_w, w_out, norm_moe_w, router_group_w, router_expert_w, expert_w_gate, expert_w_up, expert_w_down, final_norm_w):
    return _layer(x, norm_mix_w[0], w_in[0], hgrn_lb_logits, hgrn_onorm_w[0], gdn_conv_w[0], gdn_a_log[0],
                  gdn_dt_bias[0], gdn_onorm_w[0], w_out[0], norm_moe_w[0], router_group_w[0], router_expert_w[0],
                  expert_w_gate[0], expert_w_up[0], expert_w_down[0], final_norm_w, tm=256, n_chunks=8)
```

```python
import functools

import numpy as np
import jax
import jax.numpy as jnp
from jax import lax
from jax.experimental import pallas as pl
from jax.experimental.pallas import tpu as pltpu
from jax.experimental.pallas import tpu_sc as plsc

F32 = jnp.float32
BF16 = jnp.bfloat16

D_MODEL = 1024
HEADS = 4
HEAD_DIM = 128
MIX_HALF = HEADS * HEAD_DIM
CHUNK = 64
GROUP = 2
N_STACKS = HEADS // GROUP
STACK = GROUP * CHUNK
CONV_K = 4
N_GROUPS = 8
EXPERTS_PER_GROUP = 8
N_EXPERTS = N_GROUPS * EXPERTS_PER_GROUP
TOP_K = 2
EXPERT_FF = 256
NORM_EPS = 1e-6
LANES = 128
F32_ROWS = 8
SLOT_BLOCK = 512
VMEM_LIMIT = 56 * 1024 * 1024


def _sigmoid(x):
    return 0.5 * jnp.tanh(0.5 * x) + 0.5


def _silu(x):
    return x * _sigmoid(x)


def _dot(a, b):
    return jnp.dot(a, b, preferred_element_type=F32)


def _dot_nt(a, b):
    return lax.dot_general(a, b, (((1,), (1,)), ((), ())), preferred_element_type=F32)


def _dot_tn(a, b):
    return lax.dot_general(a, b, (((0,), (0,)), ((), ())), preferred_element_type=F32)


def _bdot(a, b):
    return _dot(a.astype(BF16), b.astype(BF16))


def _masked_sum(mask3, x):
    hi = x.astype(BF16)
    r1 = x - hi.astype(F32)
    mid = r1.astype(BF16)
    lo = (r1 - mid.astype(F32)).astype(BF16)
    return _dot(mask3, jnp.concatenate([hi, mid, lo], axis=0))


def _triple(mask):
    return np.concatenate([mask, mask, mask], axis=1)


def _stack_heads(a, p):
    return jnp.concatenate([a[:, h * HEAD_DIM:(h + 1) * HEAD_DIM] for h in range(p * GROUP, (p + 1) * GROUP)],
                           axis=0)


def _each(f, *lists):
    return [f(*args) for args in zip(*lists)]


INPROJ_GROUP = 256


def _l2norm_heads(a):
    return jnp.concatenate(
        [a[:, h * HEAD_DIM:(h + 1) * HEAD_DIM]
         * lax.rsqrt(jnp.sum(jnp.square(a[:, h * HEAD_DIM:(h + 1) * HEAD_DIM]), axis=-1, keepdims=True) + 1e-6)
         for h in range(a.shape[1] // HEAD_DIM)], axis=1)


def _inproj_kernel(x_ref, nw_ref, wa_ref, wb_ref, wc_ref, cw_ref, oa_ref, ob_ref, oc_ref, ubuf, *, tiles_per_seq):
    tm = x_ref.shape[0]

    @pl.when(lax.rem(pl.program_id(0), tiles_per_seq) == 0)
    def _():
        ubuf[0:HIST, :] = jnp.zeros((HIST, GDN_QKV), F32)

    x = x_ref[...]
    ms = jnp.mean(x * x, axis=-1, keepdims=True)
    hn = (x * lax.rsqrt(ms + NORM_EPS) * nw_ref[...]).astype(BF16)
    cw = cw_ref[...]
    width = INPROJ_GROUP
    group = lambda g: slice(g * width, (g + 1) * width)

    def project(w_ref, o_ref, g, row0=0):
        o_ref[row0:row0 + tm, group(g)] = _dot(hn, w_ref[:, group(g)])

    def conv_group(g):
        cols = group(g)
        conv = cw[CONV_K - 1:CONV_K, cols] * ubuf[HIST:HIST + tm, cols]
        for j in range(1, CONV_K):
            conv = conv + cw[CONV_K - 1 - j:CONV_K - j, cols] * ubuf[HIST - j:HIST - j + tm, cols]
        ubuf[0:HIST, cols] = ubuf[tm:tm + HIST, cols]
        act = _silu(conv)
        if g * width < MIX_HALF:
            act = _l2norm_heads(act) * (HEAD_DIM ** -0.5)
        elif g * width < 2 * MIX_HALF:
            act = _l2norm_heads(act)
        ob_ref[:, cols] = act

    n_conv = GDN_QKV // width
    n_a = wa_ref.shape[1] // width
    others = [functools.partial(project, wa_ref, oa_ref, g) for g in range(n_a)]
    others += [functools.partial(project, wb_ref, ob_ref, g) for g in range(n_conv, wb_ref.shape[1] // width)]
    project(wb_ref, ubuf, 0, HIST)
    for g in range(n_conv):
        if g + 1 < n_conv:
            project(wb_ref, ubuf, g + 1, HIST)
        conv_group(g)
        others.pop(0)()
    for task in others:
        task()
    oc_ref[...] = _dot(hn, wc_ref[...])


def _inproj(x2, norm_w, wa, wb, wc, conv_w, tm, tiles_per_seq):
    t = x2.shape[0]
    na, nb, nc = wa.shape[1], wb.shape[1], wc.shape[1]
    const = lambda i: (0, 0)
    return pl.pallas_call(
        functools.partial(_inproj_kernel, tiles_per_seq=tiles_per_seq),
        grid=(t // tm,),
        in_specs=[
            pl.BlockSpec((tm, D_MODEL), lambda i: (i, 0)),
            pl.BlockSpec((1, D_MODEL), const),
            pl.BlockSpec((D_MODEL, na), const),
            pl.BlockSpec((D_MODEL, nb), const),
            pl.BlockSpec((D_MODEL, nc), const),
            pl.BlockSpec((CONV_K, GDN_QKV), const),
        ],
        out_specs=[
            pl.BlockSpec((tm, na), lambda i: (i, 0)),
            pl.BlockSpec((tm, nb), lambda i: (i, 0)),
            pl.BlockSpec((tm, nc), lambda i: (i, 0)),
        ],
        out_shape=[
            jax.ShapeDtypeStruct((t, na), F32),
            jax.ShapeDtypeStruct((t, nb), F32),
            jax.ShapeDtypeStruct((t, nc), F32),
        ],
        scratch_shapes=[pltpu.VMEM((HIST + tm, GDN_QKV), F32)],
        compiler_params=pltpu.CompilerParams(dimension_semantics=("arbitrary",), vmem_limit_bytes=VMEM_LIMIT),
        name="inproj",
    )(x2, norm_w, wa, wb, wc, conv_w)


HGRN_LEVELS = (32, 16, 8, 4, 2, 1)
DIAG_CODE = len(HGRN_LEVELS)
NONE_CODE = DIAG_CODE + 1


def _hgrn_arg_matrix():
    t = np.arange(CHUNK)[:, None]
    u = np.arange(CHUNK)[None, :]
    mats = [u <= t]
    for b in HGRN_LEVELS:
        odd = (t // b) % 2 == 1
        start = (t // b) * b
        mats.append(np.where(odd, (u > start) & (u <= t), (u > t) & (u <= start + b)))
    return np.concatenate(mats, axis=0).astype(np.float32)


def _hgrn_level_codes():
    idx = np.arange(STACK)
    h, t = idx // CHUNK, idx % CHUNK
    same = h[:, None] == h[None, :]
    tt, ss = t[:, None], t[None, :]
    code = np.full((STACK, STACK), NONE_CODE, np.int32)
    code[same & (tt == ss)] = DIAG_CODE
    for l, b in enumerate(HGRN_LEVELS):
        sib = (tt // (2 * b) == ss // (2 * b)) & ((tt // b) % 2 == 1) & ((ss // b) % 2 == 0)
        code[same & sib] = l
    return code


def _hgrn_steps(q_ref, f_ref, i_ref, g_ref, lbl_ref, onw_ref, marg_ref, code_ref, o_ref, st_ref, *, n_chunks):
    lbl = lbl_ref[...]
    lmax = jnp.max(lbl, axis=0, keepdims=True)
    lexp = jnp.exp(lbl - lmax)
    lb = lexp[0:1, :] / jnp.sum(lexp, axis=0, keepdims=True)
    onw = onw_ref[...]
    marg = marg_ref[...]
    code = code_ref[...]
    trow = lax.broadcasted_iota(jnp.int32, (STACK, HEAD_DIM), 0) & (CHUNK - 1)

    q_blk = _silu(q_ref[...]) * (HEAD_DIM ** -0.5)
    fg_blk = lb + (1.0 - lb) * _sigmoid(f_ref[...])
    k_blk = 1.0 - fg_blk
    lf_blk = jnp.log(fg_blk)
    v_blk = i_ref[...]
    yield

    rows = [slice(c * CHUNK, (c + 1) * CHUNK) for c in range(n_chunks)]
    args = [_masked_sum(marg, lf_blk[r]) for r in rows]
    yield
    e_chunk = [jnp.exp(a) for a in args]
    suf_chunk = [jnp.exp(a[CHUNK - 1:CHUNK] - a[0:CHUNK]) for a in args]
    yield
    units = [(c, p) for c in range(n_chunks) for p in range(N_STACKS)]
    qs = [_stack_heads(q_blk[rows[c]], p) for c, p in units]
    ks = [_stack_heads(k_blk[rows[c]], p) for c, p in units]
    vs = [_stack_heads(v_blk[rows[c]], p).astype(BF16) for c, p in units]
    e_part = lambda n: [_stack_heads(e_chunk[c][n * CHUNK:(n + 1) * CHUNK], p) for c, p in units]
    cum_e = e_part(0)
    suf_e = [_stack_heads(suf_chunk[c], p) for c, p in units]
    att = _each(lambda q, k: jnp.where(code == DIAG_CODE, jnp.sum(q * k, axis=-1, keepdims=True), 0.0), qs, ks)

    def level_update(a, x, l, b):
        xb = x.astype(BF16)
        if b < F32_ROWS:
            return jnp.where(code == l, _dot_nt(xb, xb), a)
        blocks = [slice(s0, s0 + b) for s0 in range(0, STACK, b)]
        r = _dot_nt(jnp.concatenate([x[bl] for bl in blocks[1::2]], axis=0).astype(BF16), xb)
        return jnp.concatenate(
            [a[bl] if n % 2 == 0 else jnp.where(code[bl] == l, r[(n // 2) * b:(n // 2 + 1) * b], a[bl])
             for n, bl in enumerate(blocks)], axis=0)

    for l, b in enumerate(HGRN_LEVELS):
        x = _each(lambda q, k, el: jnp.where((trow & b) != 0, q, k) * el, qs, ks, e_part(1 + l))
        yield
        att = _each(lambda xi, a: level_update(a, xi, l, b), x, att)
        yield
    o_intra = _each(lambda a, v: _dot(a.astype(BF16), v), att, vs)
    yield
    qc = _each(lambda q, e: (q * e).astype(BF16), qs, cum_e)
    kd = _each(lambda k, e: (k * e).astype(BF16), ks, suf_e)
    yield

    state = [st_ref[h] for h in range(HEADS)]
    for u, (c, p) in enumerate(units):
        for i in range(GROUP):
            h = p * GROUP + i
            hr = slice(i * CHUNK, (i + 1) * CHUNK)
            hc = slice(h * HEAD_DIM, (h + 1) * HEAD_DIM)
            o = _dot_nt(qc[u][hr], state[h].astype(BF16)) + o_intra[u][hr]
            decay = cum_e[u][i * CHUNK + CHUNK - 1:i * CHUNK + CHUNK, :]
            state[h] = decay * state[h] + _dot_tn(vs[u][hr], kd[u][hr])
            y = o * lax.rsqrt(jnp.mean(o * o, axis=-1, keepdims=True) + NORM_EPS) * onw
            o_ref[rows[c], hc] = (y * _silu(g_ref[rows[c], hc])).astype(o_ref.dtype)
        yield
    for h in range(HEADS):
        st_ref[h] = state[h]


def _hgrn(pa, lb_logits, onorm_w, batch, seq, n_chunks):
    rows = n_chunks * CHUNK
    steps = seq // rows
    col = lambda j: pl.BlockSpec((rows, MIX_HALF), lambda b, s, j=j: (b * steps + s, j))
    const = lambda b, s: (0, 0)
    marg = jnp.asarray(_triple(_hgrn_arg_matrix()), BF16)
    code = jnp.asarray(_hgrn_level_codes())
    return pl.pallas_call(
        functools.partial(_hgrn_kernel, n_chunks=n_chunks),
        grid=(batch, steps),
        in_specs=[col(0), col(1), col(2), col(3),
                  pl.BlockSpec(lb_logits.shape, const),
                  pl.BlockSpec((1, HEAD_DIM), const),
                  pl.BlockSpec(marg.shape, const),
                  pl.BlockSpec(code.shape, const)],
        out_specs=pl.BlockSpec((rows, MIX_HALF), lambda b, s: (b * steps + s, 0)),
        out_shape=jax.ShapeDtypeStruct((batch * seq, MIX_HALF), BF16),
        scratch_shapes=[pltpu.VMEM((HEADS, HEAD_DIM, HEAD_DIM), F32)],
        compiler_params=pltpu.CompilerParams(dimension_semantics=("arbitrary", "arbitrary"),
                                             vmem_limit_bytes=VMEM_LIMIT),
        name="hgrn2",
    )(pa, pa, pa, pa, lb_logits, onorm_w, marg, code)


GDN_QKV = 3 * MIX_HALF
HIST = 8
C_NONE, C_DIAG, C_B16, C_B32, C_B64 = 0, 1, 2, 3, 4
MASKED_EXPONENT = -1e30


def _gdn_codes():
    idx = np.arange(STACK)
    h, t = idx // CHUNK, idx % CHUNK
    same = h[:, None] == h[None, :]
    tt, ss = t[:, None], t[None, :]
    code = np.full((STACK, STACK), C_NONE, np.int32)
    low = same & (ss < tt)
    code[low] = C_B64
    code[low & (tt // 32 == ss // 32)] = C_B32
    code[low & (tt // 16 == ss // 16)] = C_B16
    code[same & (tt == ss)] = C_DIAG
    incl = np.tril(np.ones((CHUNK, CHUNK), np.float32))
    return code, incl


def _gdn_steps(qkv_ref, z_ref, ab_ref, alog_ref, dtb_ref, onw_ref, code_ref, incl_ref,
               o_ref, st_ref, *, n_chunks):
    onw = onw_ref[...]
    code = code_ref[...]
    incl_m = incl_ref[...]
    incl = code >= C_DIAG
    eye = (code == C_DIAG).astype(F32)

    rows = [slice(c * CHUNK, (c + 1) * CHUNK) for c in range(n_chunks)]
    units = [(c, p) for c in range(n_chunks) for p in range(N_STACKS)]

    qs = [_stack_heads(qkv_ref[rows[c], 0:MIX_HALF], p) for c, p in units]
    ks = [_stack_heads(qkv_ref[rows[c], MIX_HALF:2 * MIX_HALF], p) for c, p in units]
    vs = [_stack_heads(qkv_ref[rows[c], 2 * MIX_HALF:3 * MIX_HALF], p) for c, p in units]
    ab_blk = ab_ref[...]
    xa = ab_blk + dtb_ref[...]
    softplus = jnp.maximum(xa, 0.0) + jnp.log(1.0 + jnp.exp(-jnp.abs(xa)))
    g_all = -jnp.exp(alog_ref[...]) * softplus
    beta_all = _sigmoid(ab_blk)

    def head_cols(a, c, p, first_lane):
        return jnp.concatenate([a[c][:, first_lane + h:first_lane + h + 1]
                                for h in range(p * GROUP, (p + 1) * GROUP)], axis=0)

    yield
    g_cum = [_masked_sum(incl_m, g_all[r]) for r in rows]
    beta_chunk = [beta_all[r] for r in rows]
    beta_st = [head_cols(beta_chunk, c, p, HEADS) for c, p in units]
    gc = [jnp.broadcast_to(head_cols(g_cum, c, p, 0), (STACK, HEAD_DIM)) for c, p in units]
    yield
    dec = [jnp.exp(jnp.where(incl, g - jnp.transpose(g)[0:1, :], MASKED_EXPONENT)) for g in gc]
    yield
    kb = [k.astype(BF16) for k in ks]
    kq = _each(lambda k, q: _dot_nt(jnp.concatenate([k, q.astype(BF16)], axis=0), k), kb, qs)
    kk = [a[0:STACK] for a in kq]
    qk = [a[STACK:] for a in kq]
    yield
    am = _each(lambda b, k2, d: b * k2 * d, beta_st, kk, dec)
    a16 = [jnp.where(code == C_B16, a, 0.0) for a in am]
    n32 = [jnp.where(code == C_B32, a, 0.0) for a in am]
    n64 = [jnp.where(code == C_B64, a, 0.0) for a in am]
    yield
    apow = _each(_bdot, a16, a16)
    pinv = [eye - a for a in a16]
    yield
    for _ in range(2):
        both = _each(lambda ai, pi: _bdot(jnp.concatenate([ai, pi], axis=0), ai), apow, pinv)
        pinv = _each(lambda pi, b: pi + b[STACK:], pinv, both)
        apow = [b[0:STACK] for b in both]
        yield
    pinv = _each(lambda pi, ai: pi + _bdot(pi, ai), pinv, apow)
    yield
    for nlev in (n32, n64):
        t = _each(_bdot, nlev, pinv)
        yield
        pinv = _each(lambda pi, ti: pi - _bdot(pi, ti), pinv, t)
        yield
    egc = [jnp.exp(g) for g in gc]
    rhs = _each(lambda k, v, b, e: jnp.concatenate([k * (b * e), v * b], axis=1), ks, vs, beta_st, egc)
    wu = _each(_bdot, pinv, rhs)
    yield
    w_c = [a[:, 0:HEAD_DIM].astype(BF16) for a in wu]
    u_c = [a[:, HEAD_DIM:] for a in wu]
    qkm = _each(lambda a, d: (a * d).astype(BF16), qk, dec)
    qg = _each(lambda q, e: (q * e).astype(BF16), qs, egc)
    glast = [[g[i * CHUNK + CHUNK - 1:(i + 1) * CHUNK, :] for i in range(GROUP)] for g in gc]
    kdec = [[(k[i * CHUNK:(i + 1) * CHUNK] * jnp.exp(gl[i] - g[i * CHUNK:(i + 1) * CHUNK])).astype(BF16)
             for i in range(GROUP)] for k, g, gl in zip(ks, gc, glast)]
    gend = [[jnp.exp(gi) for gi in gl] for gl in glast]
    yield

    state = [st_ref[h] for h in range(HEADS)]
    for u, (c, p) in enumerate(units):
        heads = range(p * GROUP, (p + 1) * GROUP)
        hrs = [slice(i * CHUNK, (i + 1) * CHUNK) for i in range(GROUP)]
        ws = [_dot_nt(jnp.concatenate([w_c[u][hr], qg[u][hr]], axis=0), state[h].astype(BF16))
              for hr, h in zip(hrs, heads)]
        v_new = jnp.concatenate([u_c[u][hr] - a[0:CHUNK] for hr, a in zip(hrs, ws)], axis=0).astype(BF16)
        o_st = jnp.concatenate([a[CHUNK:] for a in ws], axis=0) + _dot(qkm[u], v_new)
        for i, h in enumerate(heads):
            hc = slice(h * HEAD_DIM, (h + 1) * HEAD_DIM)
            state[h] = gend[u][i] * state[h] + _dot_tn(v_new[hrs[i]], kdec[u][i])
            o = o_st[hrs[i]]
            y = o * lax.rsqrt(jnp.mean(o * o, axis=-1, keepdims=True) + NORM_EPS) * onw
            o_ref[rows[c], hc] = (y * _silu(z_ref[rows[c], hc])).astype(o_ref.dtype)
        yield
    for h in range(HEADS):
        st_ref[h] = state[h]


N_HGRN_REFS, N_GDN_REFS = 8, 8


def _mixers_kernel(*refs, n_chunks):
    hg_in = refs[0:N_HGRN_REFS]
    gd_in = refs[N_HGRN_REFS:N_HGRN_REFS + N_GDN_REFS]
    ohg_ref, ogd_ref, hg_st, gd_st = refs[N_HGRN_REFS + N_GDN_REFS:]

    @pl.when(pl.program_id(1) == 0)
    def _():
        hg_st[...] = jnp.zeros_like(hg_st)
        gd_st[...] = jnp.zeros_like(gd_st)

    bodies = [_hgrn_steps(*hg_in, ohg_ref, hg_st, n_chunks=n_chunks),
              _gdn_steps(*gd_in, ogd_ref, gd_st, n_chunks=n_chunks)]
    while bodies:
        for body in list(bodies):
            if next(body, StopIteration) is StopIteration:
                bodies.remove(body)


def _mixers(pa, pb, pc, lb_logits, hg_onorm_w, alog_row, dtb_row, gd_onorm_w, batch, seq, n_chunks):
    rows = n_chunks * CHUNK
    steps = seq // rows
    const = lambda b, s: (0, 0)
    blk = lambda width, j: pl.BlockSpec((rows, width), lambda b, s, j=j: (b * steps + s, j))
    marg = jnp.asarray(_triple(_hgrn_arg_matrix()), BF16)
    hg_code = jnp.asarray(_hgrn_level_codes())
    gd_code, incl = _gdn_codes()
    gd_code, incl = jnp.asarray(gd_code), jnp.asarray(_triple(incl), BF16)
    sq = pl.BlockSpec((STACK, STACK), const)
    row = pl.BlockSpec((1, LANES), const)
    state = pltpu.VMEM((HEADS, HEAD_DIM, HEAD_DIM), F32)
    return pl.pallas_call(
        functools.partial(_mixers_kernel, n_chunks=n_chunks),
        grid=(batch, steps),
        in_specs=[blk(MIX_HALF, 0), blk(MIX_HALF, 1), blk(MIX_HALF, 2), blk(MIX_HALF, 3),
                  pl.BlockSpec(lb_logits.shape, const), pl.BlockSpec((1, HEAD_DIM), const),
                  pl.BlockSpec(marg.shape, const), sq,
                  blk(GDN_QKV, 0), blk(MIX_HALF, GDN_QKV // MIX_HALF), blk(LANES, 0),
                  row, row, row, sq, pl.BlockSpec((CHUNK, 3 * CHUNK), const)],
        out_specs=[blk(MIX_HALF, 0), blk(MIX_HALF, 0)],
        out_shape=[jax.ShapeDtypeStruct((batch * seq, MIX_HALF), BF16)] * 2,
        scratch_shapes=[state, state],
        compiler_params=pltpu.CompilerParams(dimension_semantics=("arbitrary", "arbitrary"),
                                             vmem_limit_bytes=VMEM_LIMIT),
        name="mixers",
    )(pa, pa, pa, pa, lb_logits, hg_onorm_w, marg, hg_code, pb, pb, pc, alog_row, dtb_row, gd_onorm_w, gd_code, incl)


def _gdn(pb, pc, alog_row, dtb_row, onorm_w, batch, seq, n_chunks):
    rows = n_chunks * CHUNK
    steps = seq // rows
    const = lambda b, s: (0, 0)
    code, incl = _gdn_codes()
    code, incl = jnp.asarray(code), jnp.asarray(_triple(incl), BF16)
    sq = pl.BlockSpec((STACK, STACK), const)
    row = pl.BlockSpec((1, LANES), const)
    return pl.pallas_call(
        functools.partial(_gdn_kernel, n_chunks=n_chunks),
        grid=(batch, steps),
        in_specs=[pl.BlockSpec((rows, GDN_QKV), lambda b, s: (b * steps + s, 0)),
                  pl.BlockSpec((rows, MIX_HALF), lambda b, s: (b * steps + s, GDN_QKV // MIX_HALF)),
                  pl.BlockSpec((rows, LANES), lambda b, s: (b * steps + s, 0)),
                  row, row, row, sq, pl.BlockSpec((CHUNK, 3 * CHUNK), const)],
        out_specs=pl.BlockSpec((rows, MIX_HALF), lambda b, s: (b * steps + s, 0)),
        out_shape=jax.ShapeDtypeStruct((batch * seq, MIX_HALF), BF16),
        scratch_shapes=[pltpu.VMEM((HEADS, HEAD_DIM, HEAD_DIM), F32)],
        compiler_params=pltpu.CompilerParams(dimension_semantics=("arbitrary", "arbitrary"),
                                             vmem_limit_bytes=VMEM_LIMIT),
        name="gdn",
    )(pb, pb, pc, alog_row, dtb_row, onorm_w, code, incl)


ROUTE_ROWS = 8


def _route_kernel(x_ref, ohg_ref, ogd_ref, wo_ref, nw_ref, wr_ref, ltri_ref,
                  h_ref, hn_ref, ri_ref, rg_ref, cnt_ref, cnt_scr):
    @pl.when(pl.program_id(0) == 0)
    def _():
        cnt_scr[...] = jnp.zeros_like(cnt_scr)

    mix = _dot(ohg_ref[...], wo_ref[0:MIX_HALF, :]) + _dot(ogd_ref[...], wo_ref[MIX_HALF:2 * MIX_HALF, :])
    h = x_ref[...] + mix
    h_ref[...] = h
    hn = h * lax.rsqrt(jnp.mean(h * h, axis=-1, keepdims=True) + NORM_EPS) * nw_ref[...]
    hn_ref[...] = _pack_words(hn)
    hn_hi = hn.astype(BF16)
    hn_lo = (hn - hn_hi.astype(F32)).astype(BF16)
    part = _dot(hn_hi, wr_ref[...])
    logits = (_dot(hn_lo, wr_ref[:, 0:LANES]) + part[:, LANES:]) + part[:, 0:LANES]
    tm = logits.shape[0]
    lane = lax.broadcasted_iota(jnp.int32, (tm, LANES), 1)
    neg = jnp.float32(-jnp.inf)
    big = jnp.int32(LANES)

    def first_max(vals):
        m = jnp.max(vals, axis=-1, keepdims=True)
        return m, jnp.min(jnp.where(vals == m, lane, big), axis=-1, keepdims=True)

    gl = jnp.where(lane < N_GROUPS, logits, neg)
    gmax, gidx = first_max(gl)
    p_group = 1.0 / jnp.sum(jnp.exp(gl - gmax), axis=-1, keepdims=True)
    lo = N_GROUPS + EXPERTS_PER_GROUP * gidx
    el = jnp.where((lane >= lo) & (lane < lo + EXPERTS_PER_GROUP), logits, neg)
    m1, i1 = first_max(el)
    m2, i2 = first_max(jnp.where(lane == i1, neg, el))
    r = jnp.exp(m2 - m1)
    gate1 = p_group / (1.0 + r)
    gate2 = p_group * r / (1.0 + r)
    hot1 = lane == i1
    hot2 = lane == i2
    onehot = jnp.where(hot1 | hot2, 1.0, 0.0)
    before = _dot(ltri_ref[...], onehot.astype(BF16)) + cnt_scr[...]
    rank1 = jnp.sum(jnp.where(hot1, before, 0.0), axis=-1, keepdims=True)
    rank2 = jnp.sum(jnp.where(hot2, before, 0.0), axis=-1, keepdims=True)
    cnt = cnt_scr[...] + jnp.sum(onehot, axis=0, keepdims=True)
    cnt_scr[...] = cnt
    cnt_ref[...] = cnt
    ri = jnp.where(lane == 0, i1 - N_GROUPS,
                   jnp.where(lane == 1, i2 - N_GROUPS,
                             jnp.where(lane == 2, rank1.astype(jnp.int32),
                                       jnp.where(lane == 3, rank2.astype(jnp.int32), 0))))
    ri_ref[...] = jnp.transpose(ri)[0:ROUTE_ROWS, :]
    rg_ref[...] = jnp.where(lane == 0, gate1, jnp.where(lane == 1, gate2, 0.0))


def _route(x2, ohg, ogd, wo, norm_w, wr, tm):
    t = x2.shape[0]
    ltri = jnp.asarray(np.tril(np.ones((tm, tm), np.float32), -1), BF16)
    const = lambda i: (0, 0)
    tile = lambda n: pl.BlockSpec((tm, n), lambda i: (i, 0))
    return pl.pallas_call(
        _route_kernel,
        grid=(t // tm,),
        in_specs=[tile(D_MODEL), tile(MIX_HALF), tile(MIX_HALF),
                  pl.BlockSpec((D_MODEL, D_MODEL), const),
                  pl.BlockSpec((1, D_MODEL), const),
                  pl.BlockSpec((D_MODEL, 2 * LANES), const),
                  pl.BlockSpec((tm, tm), const)],
        out_specs=[tile(D_MODEL), tile(HALF), pl.BlockSpec((ROUTE_ROWS, tm), lambda i: (0, i)), tile(LANES),
                   pl.BlockSpec((1, LANES), const)],
        out_shape=[jax.ShapeDtypeStruct((t, D_MODEL), F32),
                   jax.ShapeDtypeStruct((t, HALF), WORD),
                   jax.ShapeDtypeStruct((ROUTE_ROWS, t), jnp.int32),
                   jax.ShapeDtypeStruct((t, LANES), F32),
                   jax.ShapeDtypeStruct((1, LANES), F32)],
        scratch_shapes=[pltpu.VMEM((1, LANES), F32)],
        compiler_params=pltpu.CompilerParams(dimension_semantics=("arbitrary",), vmem_limit_bytes=VMEM_LIMIT),
        name="route",
    )(x2, ohg, ogd, wo, norm_w, wr, ltri)


HALF = D_MODEL // 2
WORD = jnp.uint32
HIGH_HALF = np.uint32(0xFFFF0000)


def _pack_words(x):
    bits = lambda a: lax.bitcast_convert_type(a.astype(BF16).astype(F32), WORD)
    return (bits(x[:, 0:HALF]) >> 16) | (bits(x[:, HALF:]) & HIGH_HALF)


def _unpack_words(words):
    return (lax.bitcast_convert_type(words << 16, F32), lax.bitcast_convert_type(words & HIGH_HALF, F32))


SC_WINDOW = 64


def _sc_permute_rows(rows, idx, gather):
    info = plsc.get_sparse_core_info()
    n_workers = info.num_cores * info.num_subcores
    n_idx, (n_src, width) = idx.shape[0], rows.shape
    copies = 1 if gather else n_idx // n_src
    per_worker = n_idx // copies // n_workers
    n_win = per_worker // SC_WINDOW
    assert per_worker * n_workers * copies == n_idx and n_win * SC_WINDOW == per_worker and n_win % 2 == 0
    idx2 = idx.reshape(n_idx // SC_WINDOW, SC_WINDOW)
    win_per_copy = n_idx // copies // SC_WINDOW

    @functools.partial(
        pl.kernel,
        mesh=plsc.VectorSubcoreMesh(core_axis_name="c", subcore_axis_name="s"),
        out_type=jax.ShapeDtypeStruct((n_idx, width), rows.dtype),
        scratch_types=[pltpu.VMEM((copies, n_win, SC_WINDOW), jnp.int32),
                       pltpu.VMEM((2, SC_WINDOW, width), rows.dtype),
                       pltpu.SemaphoreType.DMA((2,)), pltpu.SemaphoreType.DMA((2,))],
    )
    def permute(rows_hbm, idx_hbm, out_hbm, idx_v, buf, fill_sem, drain_sem):
        worker = lax.axis_index("s") * info.num_cores + lax.axis_index("c")
        first_row = worker * per_worker
        for c in range(copies):
            pltpu.sync_copy(idx_hbm.at[pl.ds(c * win_per_copy + worker * n_win, n_win)], idx_v.at[c])

        def window(ref, j):
            return ref.at[pl.ds(pl.multiple_of(first_row + j * SC_WINDOW, SC_WINDOW), SC_WINDOW)]

        def fill(j, b):
            src = rows_hbm.at[idx_v.at[0, j]] if gather else window(rows_hbm, j)
            return pltpu.make_async_copy(src, buf.at[b], fill_sem.at[b])

        def drains(j, b):
            dsts = [window(out_hbm, j)] if gather else [out_hbm.at[idx_v.at[c, j]] for c in range(copies)]
            return [pltpu.make_async_copy(buf.at[b], dst, drain_sem.at[b]) for dst in dsts]

        fill(0, 0).start()

        @pl.loop(0, n_win, step=2)
        def _(j0):
            for b in range(2):
                j = j0 + b
                fill(j, b).wait()

                @pl.when(j >= 1)
                def _():
                    for d in drains(j - 1, 1 - b):
                        d.wait()

                @pl.when(j + 1 < n_win)
                def _():
                    fill(j + 1, 1 - b).start()

                for d in drains(j, b):
                    d.start()

        for d in drains(n_win - 1, 1):
            d.wait()

    return permute(rows, idx2)


def _sc_gather_rows(table, idx):
    return _sc_permute_rows(table, idx, gather=True)


def _sc_scatter_rows(rows, idx):
    return _sc_permute_rows(rows, idx, gather=False)


def _expert_kernel(blk_ref, exp_ref, lo_ref, hi_ref, xs_ref, wg_ref, wu_ref, wd_ref, ys_ref, wgu_b, wd_b, acc):
    p = pl.program_id(0)

    @pl.when(jnp.logical_or(p == 0, exp_ref[p] != exp_ref[jnp.maximum(p - 1, 0)]))
    def _():
        wgu_b[:, 0:EXPERT_FF] = wg_ref[0].astype(BF16)
        wgu_b[:, EXPERT_FF:2 * EXPERT_FF] = wu_ref[0].astype(BF16)
        wd_b[...] = wd_ref[0].astype(BF16)

    @pl.when(p == 0)
    def _():
        acc[...] = jnp.zeros_like(acc)

    first = jnp.logical_or(p == 0, blk_ref[p] != blk_ref[jnp.maximum(p - 1, 0)])
    x_lo, x_hi = _unpack_words(xs_ref[...])
    ab = _dot(x_lo.astype(BF16), wgu_b[0:HALF, :]) + _dot(x_hi.astype(BF16), wgu_b[HALF:D_MODEL, :])
    hb = _silu(ab[:, 0:EXPERT_FF]) * ab[:, EXPERT_FF:2 * EXPERT_FF]
    y = _dot(hb.astype(BF16), wd_b[...])
    slot = blk_ref[p] * SLOT_BLOCK + lax.broadcasted_iota(jnp.int32, (SLOT_BLOCK, 1), 0)
    y = jnp.where((slot >= lo_ref[p]) & (slot < hi_ref[p]), y, 0.0)
    total = y + jnp.where(first, 0.0, acc[...])
    acc[...] = total
    ys_ref[...] = _pack_words(total)


def _experts(blk, exp, lo, hi, xs, wg, wu, wd):
    n_pairs = blk.shape[0]
    return pl.pallas_call(
        _expert_kernel,
        grid_spec=pltpu.PrefetchScalarGridSpec(
            num_scalar_prefetch=4,
            grid=(n_pairs,),
            in_specs=[pl.BlockSpec((SLOT_BLOCK, HALF), lambda p, blk, exp, lo, hi: (blk[p], 0)),
                      pl.BlockSpec((1, D_MODEL, EXPERT_FF), lambda p, blk, exp, lo, hi: (exp[p], 0, 0)),
                      pl.BlockSpec((1, D_MODEL, EXPERT_FF), lambda p, blk, exp, lo, hi: (exp[p], 0, 0)),
                      pl.BlockSpec((1, EXPERT_FF, D_MODEL), lambda p, blk, exp, lo, hi: (exp[p], 0, 0))],
            out_specs=pl.BlockSpec((SLOT_BLOCK, HALF), lambda p, blk, exp, lo, hi: (blk[p], 0)),
            scratch_shapes=[pltpu.VMEM((D_MODEL, 2 * EXPERT_FF), BF16), pltpu.VMEM((EXPERT_FF, D_MODEL), BF16),
                            pltpu.VMEM((SLOT_BLOCK, D_MODEL), F32)],
        ),
        out_shape=jax.ShapeDtypeStruct(xs.shape, WORD),
        compiler_params=pltpu.CompilerParams(dimension_semantics=("arbitrary",), vmem_limit_bytes=VMEM_LIMIT),
        name="experts",
    )(blk, exp, lo, hi, xs, wg, wu, wd)


COMBINE_PARTS = 2


def _combine_kernel(h_ref, rg_ref, fw_ref, y0_ref, y1_ref, *rest):
    o_ref = rest[-1]
    rg = rg_ref[...]
    y = [jnp.concatenate(_unpack_words(ref[...]), axis=1) for ref in (y0_ref, y1_ref)]
    h = h_ref[...] + rg[:, 0:1] * y[0] + rg[:, 1:2] * y[1]
    o_ref[...] = h * lax.rsqrt(jnp.mean(h * h, axis=-1, keepdims=True) + NORM_EPS) * fw_ref[...]


def _combine(h, rg, final_w, y2, tm, part, n_parts, out_so_far=None):
    t = h.shape[0]
    n_tiles = t // tm // n_parts
    first = part * n_tiles
    in_specs = [pl.BlockSpec((tm, D_MODEL), lambda i: (first + i, 0)),
                pl.BlockSpec((tm, LANES), lambda i: (first + i, 0)),
                pl.BlockSpec((1, D_MODEL), lambda i: (0, 0)),
                pl.BlockSpec((tm, HALF), lambda i: (i, 0)),
                pl.BlockSpec((tm, HALF), lambda i: (n_tiles + i, 0))]
    args = [h, rg, final_w, y2, y2]
    aliases = {}
    if out_so_far is not None:
        in_specs.append(pl.BlockSpec(memory_space=pl.ANY))
        aliases = {len(args): 0}
        args.append(out_so_far)
    return pl.pallas_call(
        _combine_kernel,
        grid=(n_tiles,),
        in_specs=in_specs,
        out_specs=pl.BlockSpec((tm, D_MODEL), lambda i: (first + i, 0)),
        out_shape=jax.ShapeDtypeStruct((t, D_MODEL), F32),
        input_output_aliases=aliases,
        compiler_params=pltpu.CompilerParams(dimension_semantics=("arbitrary",), vmem_limit_bytes=VMEM_LIMIT),
        name="combine",
    )(*args)


def _pair_schedule(counts, n_slots):
    n_blocks = n_slots // SLOT_BLOCK
    n_pairs = n_blocks + N_EXPERTS - 1
    ends = jnp.cumsum(counts)
    starts = ends - counts
    first = starts // SLOT_BLOCK
    last = jnp.maximum(ends - 1, starts) // SLOT_BLOCK
    per_expert = jnp.where(counts > 0, last - first + 1, 0)
    cum = jnp.cumsum(per_expert)
    p = jnp.arange(n_pairs, dtype=jnp.int32)
    e = jnp.minimum(jnp.sum(cum[None, :] <= p[:, None], axis=1), N_EXPERTS - 1).astype(jnp.int32)
    valid = p < cum[-1]
    onehot = e[:, None] == jnp.arange(N_EXPERTS, dtype=jnp.int32)[None, :]
    pick = lambda table: jnp.sum(jnp.where(onehot, table[None, :], 0), axis=1)
    blk = jnp.where(valid, pick(first) + p - pick(cum - per_expert), n_blocks - 1).astype(jnp.int32)
    lo = jnp.where(valid, pick(starts), 0).astype(jnp.int32)
    hi = jnp.where(valid, pick(ends), 0).astype(jnp.int32)
    return starts.astype(jnp.int32), blk, e, lo, hi


def _layer(x, norm_mix_w, w_in, lb_logits, hgrn_onorm_w, gdn_conv_w, gdn_a_log, gdn_dt_bias, gdn_onorm_w, w_out,
           norm_moe_w, router_group_w, router_expert_w, w_gate, w_up, w_down, final_w, *, tm, n_chunks):
    batch, seq, _ = x.shape
    t = batch * seq
    x2 = x.reshape(t, D_MODEL)
    n_hg = 4 * MIX_HALF
    n_gd = GDN_QKV + MIX_HALF
    wb16 = w_in.astype(BF16)
    wa = wb16[:, 0:n_hg]
    wb = wb16[:, n_hg:n_hg + n_gd]
    wc = jnp.pad(wb16[:, n_hg + n_gd:], ((0, 0), (0, LANES - 2 * HEADS)))
    pa, pb, pc = _inproj(x2, norm_mix_w.reshape(1, D_MODEL), wa, wb, wc, gdn_conv_w, 2 * tm, seq // (2 * tm))

    alog_row = jnp.pad(gdn_a_log.reshape(1, HEADS), ((0, 0), (0, LANES - HEADS)))
    dtb_row = jnp.pad(gdn_dt_bias.reshape(1, HEADS), ((0, 0), (0, LANES - HEADS)))
    o_hg, o_gd = _mixers(pa, pb, pc, lb_logits, hgrn_onorm_w.reshape(1, HEAD_DIM), alog_row, dtb_row,
                         gdn_onorm_w.reshape(1, HEAD_DIM), batch, seq, n_chunks)

    wr = jnp.pad(jnp.concatenate([router_group_w, router_expert_w], axis=1),
                 ((0, 0), (0, LANES - N_GROUPS - N_EXPERTS)))
    wr_hi = wr.astype(BF16)
    wr = jnp.concatenate([wr_hi, (wr - wr_hi.astype(F32)).astype(BF16)], axis=1)
    h, hn, ri, rg, cnt = _route(x2, o_hg, o_gd, w_out.astype(BF16), norm_moe_w.reshape(1, D_MODEL), wr, 2 * tm)

    counts = cnt[0, N_GROUPS:N_GROUPS + N_EXPERTS].astype(jnp.int32)
    starts, blk, exp, lo, hi = _pair_schedule(counts, t * TOP_K)
    onehot = ri[0:TOP_K, :, None] == jnp.arange(N_EXPERTS, dtype=jnp.int32)
    slot_kt = jnp.sum(jnp.where(onehot, starts, 0), axis=-1) + ri[TOP_K:2 * TOP_K]
    xs = _sc_scatter_rows(hn, slot_kt.reshape(TOP_K * t))
    ys = _experts(blk, exp, lo, hi, xs, w_gate, w_up, w_down)
    out = None
    for part in range(COMBINE_PARTS):
        tp = t // COMBINE_PARTS
        part_slots = slot_kt[:, part * tp:(part + 1) * tp].reshape(TOP_K * tp)
        y2 = _sc_gather_rows(ys, part_slots)
        out = _combine(h, rg, final_w.reshape(1, D_MODEL), y2, 2 * tm, part, COMBINE_PARTS, out)
    return out.reshape(batch, seq, D_MODEL)


def kernel(x, norm_mix_w, w_in, hgrn_lb_logits, hgrn_onorm_w, gdn_conv_w, gdn_a_log, gdn_dt_bias, gdn_onorm_w, w_out, norm_moe_w, router_group_w, router_expert_w, expert_w_gate, expert_w_up, expert_w_down, final_norm_w):
    return _layer(x, norm_mix_w[0], w_in[0], hgrn_lb_logits, hgrn_onorm_w[0], gdn_conv_w[0], gdn_a_log[0],
                  gdn_dt_bias[0], gdn_onorm_w[0], w_out[0], norm_moe_w[0], router_group_w[0], router_expert_w[0],
                  expert_w_gate[0], expert_w_up[0], expert_w_down[0], final_norm_w, tm=256, n_chunks=8)
```

```python
import functools

import numpy as np
import jax
import jax.numpy as jnp
from jax import lax
from jax.experimental import pallas as pl
from jax.experimental.pallas import tpu as pltpu
from jax.experimental.pallas import tpu_sc as plsc

F32 = jnp.float32
BF16 = jnp.bfloat16

D_MODEL = 1024
HEADS = 4
HEAD_DIM = 128
MIX_HALF = HEADS * HEAD_DIM
CHUNK = 64
GROUP = 2
N_STACKS = HEADS // GROUP
STACK = GROUP * CHUNK
CONV_K = 4
N_GROUPS = 8
EXPERTS_PER_GROUP = 8
N_EXPERTS = N_GROUPS * EXPERTS_PER_GROUP
TOP_K = 2
EXPERT_FF = 256
NORM_EPS = 1e-6
LANES = 128
F32_ROWS = 8
SLOT_BLOCK = 512
VMEM_LIMIT = 56 * 1024 * 1024


def _sigmoid(x):
    return 0.5 * jnp.tanh(0.5 * x) + 0.5


def _silu(x):
    return x * _sigmoid(x)


def _dot(a, b):
    return jnp.dot(a, b, preferred_element_type=F32)


def _dot_nt(a, b):
    return lax.dot_general(a, b, (((1,), (1,)), ((), ())), preferred_element_type=F32)


def _dot_tn(a, b):
    return lax.dot_general(a, b, (((0,), (0,)), ((), ())), preferred_element_type=F32)


def _bdot(a, b):
    return _dot(a.astype(BF16), b.astype(BF16))


def _masked_sum(mask3, x):
    hi = x.astype(BF16)
    r1 = x - hi.astype(F32)
    mid = r1.astype(BF16)
    lo = (r1 - mid.astype(F32)).astype(BF16)
    return _dot(mask3, jnp.concatenate([hi, mid, lo], axis=0))


def _triple(mask):
    return np.concatenate([mask, mask, mask], axis=1)


def _stack_heads(a, p):
    return jnp.concatenate([a[:, h * HEAD_DIM:(h + 1) * HEAD_DIM] for h in range(p * GROUP, (p + 1) * GROUP)],
                           axis=0)


def _each(f, *lists):
    return [f(*args) for args in zip(*lists)]


INPROJ_GROUP = 256


def _l2norm_heads(a):
    return jnp.concatenate(
        [a[:, h * HEAD_DIM:(h + 1) * HEAD_DIM]
         * lax.rsqrt(jnp.sum(jnp.square(a[:, h * HEAD_DIM:(h + 1) * HEAD_DIM]), axis=-1, keepdims=True) + 1e-6)
         for h in range(a.shape[1] // HEAD_DIM)], axis=1)


def _inproj_kernel(x_ref, nw_ref, wa_ref, wb_ref, wc_ref, cw_ref, oa_ref, ob_ref, oc_ref, ubuf, *, tiles_per_seq):
    tm = x_ref.shape[0]

    @pl.when(lax.rem(pl.program_id(0), tiles_per_seq) == 0)
    def _():
        ubuf[0:HIST, :] = jnp.zeros((HIST, GDN_QKV), F32)

    x = x_ref[...]
    ms = jnp.mean(x * x, axis=-1, keepdims=True)
    hn = (x * lax.rsqrt(ms + NORM_EPS) * nw_ref[...]).astype(BF16)
    cw = cw_ref[...]
    width = INPROJ_GROUP
    group = lambda g: slice(g * width, (g + 1) * width)

    def project(w_ref, o_ref, g, row0=0):
        o_ref[row0:row0 + tm, group(g)] = _dot(hn, w_ref[:, group(g)])

    def conv_group(g):
        cols = group(g)
        conv = cw[CONV_K - 1:CONV_K, cols] * ubuf[HIST:HIST + tm, cols]
        for j in range(1, CONV_K):
            conv = conv + cw[CONV_K - 1 - j:CONV_K - j, cols] * ubuf[HIST - j:HIST - j + tm, cols]
        ubuf[0:HIST, cols] = ubuf[tm:tm + HIST, cols]
        act = _silu(conv)
        if g * width < MIX_HALF:
            act = _l2norm_heads(act) * (HEAD_DIM ** -0.5)
        elif g * width < 2 * MIX_HALF:
            act = _l2norm_heads(act)
        ob_ref[:, cols] = act

    n_conv = GDN_QKV // width
    n_a = wa_ref.shape[1] // width
    others = [functools.partial(project, wa_ref, oa_ref, g) for g in range(n_a)]
    others += [functools.partial(project, wb_ref, ob_ref, g) for g in range(n_conv, wb_ref.shape[1] // width)]
    project(wb_ref, ubuf, 0, HIST)
    for g in range(n_conv):
        if g + 1 < n_conv:
            project(wb_ref, ubuf, g + 1, HIST)
        conv_group(g)
        others.pop(0)()
    for task in others:
        task()
    oc_ref[...] = _dot(hn, wc_ref[...])


def _inproj(x2, norm_w, wa, wb, wc, conv_w, tm, tiles_per_seq):
    t = x2.shape[0]
    na, nb, nc = wa.shape[1], wb.shape[1], wc.shape[1]
    const = lambda i: (0, 0)
    return pl.pallas_call(
        functools.partial(_inproj_kernel, tiles_per_seq=tiles_per_seq),
        grid=(t // tm,),
        in_specs=[
            pl.BlockSpec((tm, D_MODEL), lambda i: (i, 0)),
            pl.BlockSpec((1, D_MODEL), const),
            pl.BlockSpec((D_MODEL, na), const),
            pl.BlockSpec((D_MODEL, nb), const),
            pl.BlockSpec((D_MODEL, nc), const),
            pl.BlockSpec((CONV_K, GDN_QKV), const),
        ],
        out_specs=[
            pl.BlockSpec((tm, na), lambda i: (i, 0)),
            pl.BlockSpec((tm, nb), lambda i: (i, 0)),
            pl.BlockSpec((tm, nc), lambda i: (i, 0)),
        ],
        out_shape=[
            jax.ShapeDtypeStruct((t, na), F32),
            jax.ShapeDtypeStruct((t, nb), F32),
            jax.ShapeDtypeStruct((t, nc), F32),
        ],
        scratch_shapes=[pltpu.VMEM((HIST + tm, GDN_QKV), F32)],
        compiler_params=pltpu.CompilerParams(dimension_semantics=("arbitrary",), vmem_limit_bytes=VMEM_LIMIT),
        name="inproj",
    )(x2, norm_w, wa, wb, wc, conv_w)


HGRN_LEVELS = (32, 16, 8, 4, 2, 1)
DIAG_CODE = len(HGRN_LEVELS)
NONE_CODE = DIAG_CODE + 1


def _hgrn_arg_matrix():
    t = np.arange(CHUNK)[:, None]
    u = np.arange(CHUNK)[None, :]
    mats = [u <= t]
    for b in HGRN_LEVELS:
        odd = (t // b) % 2 == 1
        start = (t // b) * b
        mats.append(np.where(odd, (u > start) & (u <= t), (u > t) & (u <= start + b)))
    return np.concatenate(mats, axis=0).astype(np.float32)


def _hgrn_level_codes():
    idx = np.arange(STACK)
    h, t = idx // CHUNK, idx % CHUNK
    same = h[:, None] == h[None, :]
    tt, ss = t[:, None], t[None, :]
    code = np.full((STACK, STACK), NONE_CODE, np.int32)
    code[same & (tt == ss)] = DIAG_CODE
    for l, b in enumerate(HGRN_LEVELS):
        sib = (tt // (2 * b) == ss // (2 * b)) & ((tt // b) % 2 == 1) & ((ss // b) % 2 == 0)
        code[same & sib] = l
    return code


def _hgrn_steps(q_ref, f_ref, i_ref, g_ref, lbl_ref, onw_ref, marg_ref, code_ref, o_ref, st_ref, *, n_chunks):
    lbl = lbl_ref[...]
    lmax = jnp.max(lbl, axis=0, keepdims=True)
    lexp = jnp.exp(lbl - lmax)
    lb = lexp[0:1, :] / jnp.sum(lexp, axis=0, keepdims=True)
    onw = onw_ref[...]
    marg = marg_ref[...]
    code = code_ref[...]
    trow = lax.broadcasted_iota(jnp.int32, (STACK, HEAD_DIM), 0) & (CHUNK - 1)

    q_blk = _silu(q_ref[...]) * (HEAD_DIM ** -0.5)
    fg_blk = lb + (1.0 - lb) * _sigmoid(f_ref[...])
    k_blk = 1.0 - fg_blk
    lf_blk = jnp.log(fg_blk)
    v_blk = i_ref[...]
    yield

    rows = [slice(c * CHUNK, (c + 1) * CHUNK) for c in range(n_chunks)]
    args = [_masked_sum(marg, lf_blk[r]) for r in rows]
    yield
    e_chunk = [jnp.exp(a) for a in args]
    suf_chunk = [jnp.exp(a[CHUNK - 1:CHUNK] - a[0:CHUNK]) for a in args]
    yield
    units = [(c, p) for c in range(n_chunks) for p in range(N_STACKS)]
    qs = [_stack_heads(q_blk[rows[c]], p) for c, p in units]
    ks = [_stack_heads(k_blk[rows[c]], p) for c, p in units]
    vs = [_stack_heads(v_blk[rows[c]], p).astype(BF16) for c, p in units]
    e_part = lambda n: [_stack_heads(e_chunk[c][n * CHUNK:(n + 1) * CHUNK], p) for c, p in units]
    cum_e = e_part(0)
    suf_e = [_stack_heads(suf_chunk[c], p) for c, p in units]
    att = _each(lambda q, k: jnp.where(code == DIAG_CODE, jnp.sum(q * k, axis=-1, keepdims=True), 0.0), qs, ks)

    def level_update(a, x, l, b):
        xb = x.astype(BF16)
        if b < F32_ROWS:
            return jnp.where(code == l, _dot_nt(xb, xb), a)
        blocks = [slice(s0, s0 + b) for s0 in range(0, STACK, b)]
        r = _dot_nt(jnp.concatenate([x[bl] for bl in blocks[1::2]], axis=0).astype(BF16), xb)
        return jnp.concatenate(
            [a[bl] if n % 2 == 0 else jnp.where(code[bl] == l, r[(n // 2) * b:(n // 2 + 1) * b], a[bl])
             for n, bl in enumerate(blocks)], axis=0)

    for l, b in enumerate(HGRN_LEVELS):
        x = _each(lambda q, k, el: jnp.where((trow & b) != 0, q, k) * el, qs, ks, e_part(1 + l))
        yield
        att = _each(lambda xi, a: level_update(a, xi, l, b), x, att)
        yield
    o_intra = _each(lambda a, v: _dot(a.astype(BF16), v), att, vs)
    yield
    qc = _each(lambda q, e: (q * e).astype(BF16), qs, cum_e)
    kd = _each(lambda k, e: (k * e).astype(BF16), ks, suf_e)
    yield

    state = [st_ref[h] for h in range(HEADS)]
    for u, (c, p) in enumerate(units):
        for i in range(GROUP):
            h = p * GROUP + i
            hr = slice(i * CHUNK, (i + 1) * CHUNK)
            hc = slice(h * HEAD_DIM, (h + 1) * HEAD_DIM)
            o = _dot_nt(qc[u][hr], state[h].astype(BF16)) + o_intra[u][hr]
            decay = cum_e[u][i * CHUNK + CHUNK - 1:i * CHUNK + CHUNK, :]
            state[h] = decay * state[h] + _dot_tn(vs[u][hr], kd[u][hr])
            y = o * lax.rsqrt(jnp.mean(o * o, axis=-1, keepdims=True) + NORM_EPS) * onw
            o_ref[rows[c], hc] = (y * _silu(g_ref[rows[c], hc])).astype(o_ref.dtype)
        yield
    for h in range(HEADS):
        st_ref[h] = state[h]


GDN_QKV = 3 * MIX_HALF
HIST = 8
C_NONE, C_DIAG, C_B16, C_B32, C_B64 = 0, 1, 2, 3, 4
MASKED_EXPONENT = -1e30


def _gdn_codes():
    idx = np.arange(STACK)
    h, t = idx // CHUNK, idx % CHUNK
    same = h[:, None] == h[None, :]
    tt, ss = t[:, None], t[None, :]
    code = np.full((STACK, STACK), C_NONE, np.int32)
    low = same & (ss < tt)
    code[low] = C_B64
    code[low & (tt // 32 == ss // 32)] = C_B32
    code[low & (tt // 16 == ss // 16)] = C_B16
    code[same & (tt == ss)] = C_DIAG
    incl = np.tril(np.ones((CHUNK, CHUNK), np.float32))
    return code, incl


def _gdn_steps(qkv_ref, z_ref, ab_ref, alog_ref, dtb_ref, onw_ref, code_ref, incl_ref,
               o_ref, st_ref, *, n_chunks):
    onw = onw_ref[...]
    code = code_ref[...]
    incl_m = incl_ref[...]
    incl = code >= C_DIAG
    eye = (code == C_DIAG).astype(F32)

    rows = [slice(c * CHUNK, (c + 1) * CHUNK) for c in range(n_chunks)]
    units = [(c, p) for c in range(n_chunks) for p in range(N_STACKS)]

    qs = [_stack_heads(qkv_ref[rows[c], 0:MIX_HALF], p) for c, p in units]
    ks = [_stack_heads(qkv_ref[rows[c], MIX_HALF:2 * MIX_HALF], p) for c, p in units]
    vs = [_stack_heads(qkv_ref[rows[c], 2 * MIX_HALF:3 * MIX_HALF], p) for c, p in units]
    ab_blk = ab_ref[...]
    xa = ab_blk + dtb_ref[...]
    softplus = jnp.maximum(xa, 0.0) + jnp.log(1.0 + jnp.exp(-jnp.abs(xa)))
    g_all = -jnp.exp(alog_ref[...]) * softplus
    beta_all = _sigmoid(ab_blk)

    def head_cols(a, c, p, first_lane):
        return jnp.concatenate([a[c][:, first_lane + h:first_lane + h + 1]
                                for h in range(p * GROUP, (p + 1) * GROUP)], axis=0)

    yield
    g_cum = [_masked_sum(incl_m, g_all[r]) for r in rows]
    beta_chunk = [beta_all[r] for r in rows]
    beta_st = [head_cols(beta_chunk, c, p, HEADS) for c, p in units]
    gc = [jnp.broadcast_to(head_cols(g_cum, c, p, 0), (STACK, HEAD_DIM)) for c, p in units]
    yield
    dec = [jnp.exp(jnp.where(incl, g - jnp.transpose(g)[0:1, :], MASKED_EXPONENT)) for g in gc]
    yield
    kb = [k.astype(BF16) for k in ks]
    kq = _each(lambda k, q: _dot_nt(jnp.concatenate([k, q.astype(BF16)], axis=0), k), kb, qs)
    kk = [a[0:STACK] for a in kq]
    qk = [a[STACK:] for a in kq]
    yield
    am = _each(lambda b, k2, d: b * k2 * d, beta_st, kk, dec)
    a16 = [jnp.where(code == C_B16, a, 0.0) for a in am]
    n32 = [jnp.where(code == C_B32, a, 0.0) for a in am]
    n64 = [jnp.where(code == C_B64, a, 0.0) for a in am]
    yield
    apow = _each(_bdot, a16, a16)
    pinv = [eye - a for a in a16]
    yield
    for _ in range(2):
        both = _each(lambda ai, pi: _bdot(jnp.concatenate([ai, pi], axis=0), ai), apow, pinv)
        pinv = _each(lambda pi, b: pi + b[STACK:], pinv, both)
        apow = [b[0:STACK] for b in both]
        yield
    pinv = _each(lambda pi, ai: pi + _bdot(pi, ai), pinv, apow)
    yield
    for nlev in (n32, n64):
        t = _each(_bdot, nlev, pinv)
        yield
        pinv = _each(lambda pi, ti: pi - _bdot(pi, ti), pinv, t)
        yield
    egc = [jnp.exp(g) for g in gc]
    rhs = _each(lambda k, v, b, e: jnp.concatenate([k * (b * e), v * b], axis=1), ks, vs, beta_st, egc)
    wu = _each(_bdot, pinv, rhs)
    yield
    w_c = [a[:, 0:HEAD_DIM].astype(BF16) for a in wu]
    u_c = [a[:, HEAD_DIM:] for a in wu]
    qkm = _each(lambda a, d: (a * d).astype(BF16), qk, dec)
    qg = _each(lambda q, e: (q * e).astype(BF16), qs, egc)
    glast = [[g[i * CHUNK + CHUNK - 1:(i + 1) * CHUNK, :] for i in range(GROUP)] for g in gc]
    kdec = [[(k[i * CHUNK:(i + 1) * CHUNK] * jnp.exp(gl[i] - g[i * CHUNK:(i + 1) * CHUNK])).astype(BF16)
             for i in range(GROUP)] for k, g, gl in zip(ks, gc, glast)]
    gend = [[jnp.exp(gi) for gi in gl] for gl in glast]
    yield

    state = [st_ref[h] for h in range(HEADS)]
    for u, (c, p) in enumerate(units):
        heads = range(p * GROUP, (p + 1) * GROUP)
        hrs = [slice(i * CHUNK, (i + 1) * CHUNK) for i in range(GROUP)]
        ws = [_dot_nt(jnp.concatenate([w_c[u][hr], qg[u][hr]], axis=0), state[h].astype(BF16))
              for hr, h in zip(hrs, heads)]
        v_new = jnp.concatenate([u_c[u][hr] - a[0:CHUNK] for hr, a in zip(hrs, ws)], axis=0).astype(BF16)
        o_st = jnp.concatenate([a[CHUNK:] for a in ws], axis=0) + _dot(qkm[u], v_new)
        for i, h in enumerate(heads):
            hc = slice(h * HEAD_DIM, (h + 1) * HEAD_DIM)
            state[h] = gend[u][i] * state[h] + _dot_tn(v_new[hrs[i]], kdec[u][i])
            o = o_st[hrs[i]]
            y = o * lax.rsqrt(jnp.mean(o * o, axis=-1, keepdims=True) + NORM_EPS) * onw
            o_ref[rows[c], hc] = (y * _silu(z_ref[rows[c], hc])).astype(o_ref.dtype)
        yield
    for h in range(HEADS):
        st_ref[h] = state[h]


N_HGRN_REFS, N_GDN_REFS = 8, 8


def _mixers_kernel(*refs, n_chunks):
    hg_in = refs[0:N_HGRN_REFS]
    gd_in = refs[N_HGRN_REFS:N_HGRN_REFS + N_GDN_REFS]
    ohg_ref, ogd_ref, hg_st, gd_st = refs[N_HGRN_REFS + N_GDN_REFS:]

    @pl.when(pl.program_id(1) == 0)
    def _():
        hg_st[...] = jnp.zeros_like(hg_st)
        gd_st[...] = jnp.zeros_like(gd_st)

    bodies = [_hgrn_steps(*hg_in, ohg_ref, hg_st, n_chunks=n_chunks),
              _gdn_steps(*gd_in, ogd_ref, gd_st, n_chunks=n_chunks)]
    while bodies:
        for body in list(bodies):
            if next(body, StopIteration) is StopIteration:
                bodies.remove(body)


def _mixers(pa, pb, pc, lb_logits, hg_onorm_w, alog_row, dtb_row, gd_onorm_w, batch, seq, n_chunks):
    rows = n_chunks * CHUNK
    steps = seq // rows
    const = lambda b, s: (0, 0)
    blk = lambda width, j: pl.BlockSpec((rows, width), lambda b, s, j=j: (b * steps + s, j))
    marg = jnp.asarray(_triple(_hgrn_arg_matrix()), BF16)
    hg_code = jnp.asarray(_hgrn_level_codes())
    gd_code, incl = _gdn_codes()
    gd_code, incl = jnp.asarray(gd_code), jnp.asarray(_triple(incl), BF16)
    sq = pl.BlockSpec((STACK, STACK), const)
    row = pl.BlockSpec((1, LANES), const)
    state = pltpu.VMEM((HEADS, HEAD_DIM, HEAD_DIM), F32)
    return pl.pallas_call(
        functools.partial(_mixers_kernel, n_chunks=n_chunks),
        grid=(batch, steps),
        in_specs=[blk(MIX_HALF, 0), blk(MIX_HALF, 1), blk(MIX_HALF, 2), blk(MIX_HALF, 3),
                  pl.BlockSpec(lb_logits.shape, const), pl.BlockSpec((1, HEAD_DIM), const),
                  pl.BlockSpec(marg.shape, const), sq,
                  blk(GDN_QKV, 0), blk(MIX_HALF, GDN_QKV // MIX_HALF), blk(LANES, 0),
                  row, row, row, sq, pl.BlockSpec((CHUNK, 3 * CHUNK), const)],
        out_specs=[blk(MIX_HALF, 0), blk(MIX_HALF, 0)],
        out_shape=[jax.ShapeDtypeStruct((batch * seq, MIX_HALF), BF16)] * 2,
        scratch_shapes=[state, state],
        compiler_params=pltpu.CompilerParams(dimension_semantics=("arbitrary", "arbitrary"),
                                             vmem_limit_bytes=VMEM_LIMIT),
        name="mixers",
    )(pa, pa, pa, pa, lb_logits, hg_onorm_w, marg, hg_code, pb, pb, pc, alog_row, dtb_row, gd_onorm_w, gd_code, incl)


ROUTE_ROWS = 8


def _route_kernel(x_ref, ohg_ref, ogd_ref, wo_ref, nw_ref, wr_ref, ltri_ref,
                  h_ref, hn_ref, ri_ref, rg_ref, cnt_ref, cnt_scr):
    @pl.when(pl.program_id(0) == 0)
    def _():
        cnt_scr[...] = jnp.zeros_like(cnt_scr)

    mix = _dot(ohg_ref[...], wo_ref[0:MIX_HALF, :]) + _dot(ogd_ref[...], wo_ref[MIX_HALF:2 * MIX_HALF, :])
    h = x_ref[...] + mix
    h_ref[...] = h
    hn = h * lax.rsqrt(jnp.mean(h * h, axis=-1, keepdims=True) + NORM_EPS) * nw_ref[...]
    hn_ref[...] = _pack_words(hn)
    hn_hi = hn.astype(BF16)
    hn_lo = (hn - hn_hi.astype(F32)).astype(BF16)
    part = _dot(hn_hi, wr_ref[...])
    logits = (_dot(hn_lo, wr_ref[:, 0:LANES]) + part[:, LANES:]) + part[:, 0:LANES]
    tm = logits.shape[0]
    lane = lax.broadcasted_iota(jnp.int32, (tm, LANES), 1)
    neg = jnp.float32(-jnp.inf)
    big = jnp.int32(LANES)

    def first_max(vals):
        m = jnp.max(vals, axis=-1, keepdims=True)
        return m, jnp.min(jnp.where(vals == m, lane, big), axis=-1, keepdims=True)

    gl = jnp.where(lane < N_GROUPS, logits, neg)
    gmax, gidx = first_max(gl)
    p_group = 1.0 / jnp.sum(jnp.exp(gl - gmax), axis=-1, keepdims=True)
    lo = N_GROUPS + EXPERTS_PER_GROUP * gidx
    el = jnp.where((lane >= lo) & (lane < lo + EXPERTS_PER_GROUP), logits, neg)
    m1, i1 = first_max(el)
    m2, i2 = first_max(jnp.where(lane == i1, neg, el))
    r = jnp.exp(m2 - m1)
    gate1 = p_group / (1.0 + r)
    gate2 = p_group * r / (1.0 + r)
    hot1 = lane == i1
    hot2 = lane == i2
    onehot = jnp.where(hot1 | hot2, 1.0, 0.0)
    before = _dot(ltri_ref[...], onehot.astype(BF16)) + cnt_scr[...]
    rank1 = jnp.sum(jnp.where(hot1, before, 0.0), axis=-1, keepdims=True)
    rank2 = jnp.sum(jnp.where(hot2, before, 0.0), axis=-1, keepdims=True)
    cnt = cnt_scr[...] + jnp.sum(onehot, axis=0, keepdims=True)
    cnt_scr[...] = cnt
    cnt_ref[...] = cnt
    ri = jnp.where(lane == 0, i1 - N_GROUPS,
                   jnp.where(lane == 1, i2 - N_GROUPS,
                             jnp.where(lane == 2, rank1.astype(jnp.int32),
                                       jnp.where(lane == 3, rank2.astype(jnp.int32), 0))))
    ri_ref[...] = jnp.transpose(ri)[0:ROUTE_ROWS, :]
    rg_ref[...] = jnp.where(lane == 0, gate1, jnp.where(lane == 1, gate2, 0.0))


def _route(x2, ohg, ogd, wo, norm_w, wr, tm):
    t = x2.shape[0]
    ltri = jnp.asarray(np.tril(np.ones((tm, tm), np.float32), -1), BF16)
    const = lambda i: (0, 0)
    tile = lambda n: pl.BlockSpec((tm, n), lambda i: (i, 0))
    return pl.pallas_call(
        _route_kernel,
        grid=(t // tm,),
        in_specs=[tile(D_MODEL), tile(MIX_HALF), tile(MIX_HALF),
                  pl.BlockSpec((D_MODEL, D_MODEL), const),
                  pl.BlockSpec((1, D_MODEL), const),
                  pl.BlockSpec((D_MODEL, 2 * LANES), const),
                  pl.BlockSpec((tm, tm), const)],
        out_specs=[tile(D_MODEL), tile(HALF), pl.BlockSpec((ROUTE_ROWS, tm), lambda i: (0, i)), tile(LANES),
                   pl.BlockSpec((1, LANES), const)],
        out_shape=[jax.ShapeDtypeStruct((t, D_MODEL), F32),
                   jax.ShapeDtypeStruct((t, HALF), WORD),
                   jax.ShapeDtypeStruct((ROUTE_ROWS, t), jnp.int32),
                   jax.ShapeDtypeStruct((t, LANES), F32),
                   jax.ShapeDtypeStruct((1, LANES), F32)],
        scratch_shapes=[pltpu.VMEM((1, LANES), F32)],
        compiler_params=pltpu.CompilerParams(dimension_semantics=("arbitrary",), vmem_limit_bytes=VMEM_LIMIT),
        name="route",
    )(x2, ohg, ogd, wo, norm_w, wr, ltri)


HALF = D_MODEL // 2
WORD = jnp.uint32
HIGH_HALF = np.uint32(0xFFFF0000)


def _pack_words(x):
    bits = lambda a: lax.bitcast_convert_type(a.astype(BF16).astype(F32), WORD)
    return (bits(x[:, 0:HALF]) >> 16) | (bits(x[:, HALF:]) & HIGH_HALF)


def _unpack_words(words):
    return (lax.bitcast_convert_type(words << 16, F32), lax.bitcast_convert_type(words & HIGH_HALF, F32))


SC_WINDOW = 64


def _sc_permute_rows(rows, idx, gather):
    info = plsc.get_sparse_core_info()
    n_workers = info.num_cores * info.num_subcores
    n_idx, (n_src, width) = idx.shape[0], rows.shape
    copies = 1 if gather else n_idx // n_src
    per_worker = n_idx // copies // n_workers
    n_win = per_worker // SC_WINDOW
    assert per_worker * n_workers * copies == n_idx and n_win * SC_WINDOW == per_worker and n_win % 2 == 0
    idx2 = idx.reshape(n_idx // SC_WINDOW, SC_WINDOW)
    win_per_copy = n_idx // copies // SC_WINDOW

    @functools.partial(
        pl.kernel,
        mesh=plsc.VectorSubcoreMesh(core_axis_name="c", subcore_axis_name="s"),
        out_type=jax.ShapeDtypeStruct((n_idx, width), rows.dtype),
        scratch_types=[pltpu.VMEM((copies, n_win, SC_WINDOW), jnp.int32),
                       pltpu.VMEM((2, SC_WINDOW, width), rows.dtype),
                       pltpu.SemaphoreType.DMA((2,)), pltpu.SemaphoreType.DMA((2,))],
    )
    def permute(rows_hbm, idx_hbm, out_hbm, idx_v, buf, fill_sem, drain_sem):
        worker = lax.axis_index("s") * info.num_cores + lax.axis_index("c")
        first_row = worker * per_worker
        for c in range(copies):
            pltpu.sync_copy(idx_hbm.at[pl.ds(c * win_per_copy + worker * n_win, n_win)], idx_v.at[c])

        def window(ref, j):
            return ref.at[pl.ds(pl.multiple_of(first_row + j * SC_WINDOW, SC_WINDOW), SC_WINDOW)]

        def fill(j, b):
            src = rows_hbm.at[idx_v.at[0, j]] if gather else window(rows_hbm, j)
            return pltpu.make_async_copy(src, buf.at[b], fill_sem.at[b])

        def drains(j, b):
            dsts = [window(out_hbm, j)] if gather else [out_hbm.at[idx_v.at[c, j]] for c in range(copies)]
            return [pltpu.make_async_copy(buf.at[b], dst, drain_sem.at[b]) for dst in dsts]

        fill(0, 0).start()

        @pl.loop(0, n_win, step=2)
        def _(j0):
            for b in range(2):
                j = j0 + b
                fill(j, b).wait()

                @pl.when(j >= 1)
                def _():
                    for d in drains(j - 1, 1 - b):
                        d.wait()

                @pl.when(j + 1 < n_win)
                def _():
                    fill(j + 1, 1 - b).start()

                for d in drains(j, b):
                    d.start()

        for d in drains(n_win - 1, 1):
            d.wait()

    return permute(rows, idx2)


def _sc_gather_rows(table, idx):
    return _sc_permute_rows(table, idx, gather=True)


def _sc_scatter_rows(rows, idx):
    return _sc_permute_rows(rows, idx, gather=False)


def _expert_kernel(blk_ref, exp_ref, lo_ref, hi_ref, xs_ref, wg_ref, wu_ref, wd_ref, ys_ref, wgu_b, wd_b, acc):
    p = pl.program_id(0)

    @pl.when(jnp.logical_or(p == 0, exp_ref[p] != exp_ref[jnp.maximum(p - 1, 0)]))
    def _():
        wgu_b[:, 0:EXPERT_FF] = wg_ref[0].astype(BF16)
        wgu_b[:, EXPERT_FF:2 * EXPERT_FF] = wu_ref[0].astype(BF16)
        wd_b[...] = wd_ref[0].astype(BF16)

    @pl.when(p == 0)
    def _():
        acc[...] = jnp.zeros_like(acc)

    first = jnp.logical_or(p == 0, blk_ref[p] != blk_ref[jnp.maximum(p - 1, 0)])
    x_lo, x_hi = _unpack_words(xs_ref[...])
    ab = _dot(x_lo.astype(BF16), wgu_b[0:HALF, :]) + _dot(x_hi.astype(BF16), wgu_b[HALF:D_MODEL, :])
    hb = _silu(ab[:, 0:EXPERT_FF]) * ab[:, EXPERT_FF:2 * EXPERT_FF]
    y = _dot(hb.astype(BF16), wd_b[...])
    slot = blk_ref[p] * SLOT_BLOCK + lax.broadcasted_iota(jnp.int32, (SLOT_BLOCK, 1), 0)
    y = jnp.where((slot >= lo_ref[p]) & (slot < hi_ref[p]), y, 0.0)
    total = y + jnp.where(first, 0.0, acc[...])
    acc[...] = total
    ys_ref[...] = _pack_words(total)


def _experts(blk, exp, lo, hi, xs, wg, wu, wd):
    n_pairs = blk.shape[0]
    return pl.pallas_call(
        _expert_kernel,
        grid_spec=pltpu.PrefetchScalarGridSpec(
            num_scalar_prefetch=4,
            grid=(n_pairs,),
            in_specs=[pl.BlockSpec((SLOT_BLOCK, HALF), lambda p, blk, exp, lo, hi: (blk[p], 0)),
                      pl.BlockSpec((1, D_MODEL, EXPERT_FF), lambda p, blk, exp, lo, hi: (exp[p], 0, 0)),
                      pl.BlockSpec((1, D_MODEL, EXPERT_FF), lambda p, blk, exp, lo, hi: (exp[p], 0, 0)),
                      pl.BlockSpec((1, EXPERT_FF, D_MODEL), lambda p, blk, exp, lo, hi: (exp[p], 0, 0))],
            out_specs=pl.BlockSpec((SLOT_BLOCK, HALF), lambda p, blk, exp, lo, hi: (blk[p], 0)),
            scratch_shapes=[pltpu.VMEM((D_MODEL, 2 * EXPERT_FF), BF16), pltpu.VMEM((EXPERT_FF, D_MODEL), BF16),
                            pltpu.VMEM((SLOT_BLOCK, D_MODEL), F32)],
        ),
        out_shape=jax.ShapeDtypeStruct(xs.shape, WORD),
        compiler_params=pltpu.CompilerParams(dimension_semantics=("arbitrary",), vmem_limit_bytes=VMEM_LIMIT),
        name="experts",
    )(blk, exp, lo, hi, xs, wg, wu, wd)


COMBINE_PARTS = 2


def _combine_kernel(h_ref, rg_ref, fw_ref, y0_ref, y1_ref, *rest):
    o_ref = rest[-1]
    rg = rg_ref[...]
    y = [jnp.concatenate(_unpack_words(ref[...]), axis=1) for ref in (y0_ref, y1_ref)]
    h = h_ref[...] + rg[:, 0:1] * y[0] + rg[:, 1:2] * y[1]
    o_ref[...] = h * lax.rsqrt(jnp.mean(h * h, axis=-1, keepdims=True) + NORM_EPS) * fw_ref[...]


def _combine(h, rg, final_w, y2, tm, part, n_parts, out_so_far=None):
    t = h.shape[0]
    n_tiles = t // tm // n_parts
    first = part * n_tiles
    in_specs = [pl.BlockSpec((tm, D_MODEL), lambda i: (first + i, 0)),
                pl.BlockSpec((tm, LANES), lambda i: (first + i, 0)),
                pl.BlockSpec((1, D_MODEL), lambda i: (0, 0)),
                pl.BlockSpec((tm, HALF), lambda i: (i, 0)),
                pl.BlockSpec((tm, HALF), lambda i: (n_tiles + i, 0))]
    args = [h, rg, final_w, y2, y2]
    aliases = {}
    if out_so_far is not None:
        in_specs.append(pl.BlockSpec(memory_space=pl.ANY))
        aliases = {len(args): 0}
        args.append(out_so_far)
    return pl.pallas_call(
        _combine_kernel,
        grid=(n_tiles,),
        in_specs=in_specs,
        out_specs=pl.BlockSpec((tm, D_MODEL), lambda i: (first + i, 0)),
        out_shape=jax.ShapeDtypeStruct((t, D_MODEL), F32),
        input_output_aliases=aliases,
        compiler_params=pltpu.CompilerParams(dimension_semantics=("arbitrary",), vmem_limit_bytes=VMEM_LIMIT),
        name="combine",
    )(*args)


def _pair_schedule(counts, n_slots):
    n_blocks = n_slots // SLOT_BLOCK
    n_pairs = n_blocks + N_EXPERTS - 1
    ends = jnp.cumsum(counts)
    starts = ends - counts
    first = starts // SLOT_BLOCK
    last = jnp.maximum(ends - 1, starts) // SLOT_BLOCK
    per_expert = jnp.where(counts > 0, last - first + 1, 0)
    cum = jnp.cumsum(per_expert)
    p = jnp.arange(n_pairs, dtype=jnp.int32)
    e = jnp.minimum(jnp.sum(cum[None, :] <= p[:, None], axis=1), N_EXPERTS - 1).astype(jnp.int32)
    valid = p < cum[-1]
    onehot = e[:, None] == jnp.arange(N_EXPERTS, dtype=jnp.int32)[None, :]
    pick = lambda table: jnp.sum(jnp.where(onehot, table[None, :], 0), axis=1)
    blk = jnp.where(valid, pick(first) + p - pick(cum - per_expert), n_blocks - 1).astype(jnp.int32)
    lo = jnp.where(valid, pick(starts), 0).astype(jnp.int32)
    hi = jnp.where(valid, pick(ends), 0).astype(jnp.int32)
    return starts.astype(jnp.int32), blk, e, lo, hi


def _layer(x, norm_mix_w, w_in, lb_logits, hgrn_onorm_w, gdn_conv_w, gdn_a_log, gdn_dt_bias, gdn_onorm_w, w_out,
           norm_moe_w, router_group_w, router_expert_w, w_gate, w_up, w_down, final_w, *, tm, n_chunks):
    batch, seq, _ = x.shape
    t = batch * seq
    x2 = x.reshape(t, D_MODEL)
    n_hg = 4 * MIX_HALF
    n_gd = GDN_QKV + MIX_HALF
    wb16 = w_in.astype(BF16)
    wa = wb16[:, 0:n_hg]
    wb = wb16[:, n_hg:n_hg + n_gd]
    wc = jnp.pad(wb16[:, n_hg + n_gd:], ((0, 0), (0, LANES - 2 * HEADS)))
    pa, pb, pc = _inproj(x2, norm_mix_w.reshape(1, D_MODEL), wa, wb, wc, gdn_conv_w, 2 * tm, seq // (2 * tm))

    alog_row = jnp.pad(gdn_a_log.reshape(1, HEADS), ((0, 0), (0, LANES - HEADS)))
    dtb_row = jnp.pad(gdn_dt_bias.reshape(1, HEADS), ((0, 0), (0, LANES - HEADS)))
    o_hg, o_gd = _mixers(pa, pb, pc, lb_logits, hgrn_onorm_w.reshape(1, HEAD_DIM), alog_row, dtb_row,
                         gdn_onorm_w.reshape(1, HEAD_DIM), batch, seq, n_chunks)

    wr = jnp.pad(jnp.concatenate([router_group_w, router_expert_w], axis=1),
                 ((0, 0), (0, LANES - N_GROUPS - N_EXPERTS)))
    wr_hi = wr.astype(BF16)
    wr = jnp.concatenate([wr_hi, (wr - wr_hi.astype(F32)).astype(BF16)], axis=1)
    h, hn, ri, rg, cnt = _route(x2, o_hg, o_gd, w_out.astype(BF16), norm_moe_w.reshape(1, D_MODEL), wr, 2 * tm)

    counts = cnt[0, N_GROUPS:N_GROUPS + N_EXPERTS].astype(jnp.int32)
    starts, blk, exp, lo, hi = _pair_schedule(counts, t * TOP_K)
    onehot = ri[0:TOP_K, :, None] == jnp.arange(N_EXPERTS, dtype=jnp.int32)
    slot_kt = jnp.sum(jnp.where(onehot, starts, 0), axis=-1) + ri[TOP_K:2 * TOP_K]
    xs = _sc_scatter_rows(hn, slot_kt.reshape(TOP_K * t))
    ys = _experts(blk, exp, lo, hi, xs, w_gate, w_up, w_down)
    out = None
    for part in range(COMBINE_PARTS):
        tp = t // COMBINE_PARTS
        part_slots = slot_kt[:, part * tp:(part + 1) * tp].reshape(TOP_K * tp)
        y2 = _sc_gather_rows(ys, part_slots)
        out = _combine(h, rg, final_w.reshape(1, D_MODEL), y2, 2 * tm, part, COMBINE_PARTS, out)
    return out.reshape(batch, seq, D_MODEL)


def kernel(x, norm_mix_w, w_in, hgrn_lb_logits, hgrn_onorm_w, gdn_conv_w, gdn_a_log, gdn_dt_bias, gdn_onorm_w, w_out, norm_moe_w, router_group_w, router_expert_w, expert_w_gate, expert_w_up, expert_w_down, final_norm_w):
    return _layer(x, norm_mix_w[0], w_in[0], hgrn_lb_logits, hgrn_onorm_w[0], gdn_conv_w[0], gdn_a_log[0],
                  gdn_dt_bias[0], gdn_onorm_w[0], w_out[0], norm_moe_w[0], router_group_w[0], router_expert_w[0],
                  expert_w_gate[0], expert_w_up[0], expert_w_down[0], final_norm_w, tm=256, n_chunks=8)
```

```python
import functools

import numpy as np
import jax
import jax.numpy as jnp
from jax import lax
from jax.experimental import pallas as pl
from jax.experimental.pallas import tpu as pltpu
from jax.experimental.pallas import tpu_sc as plsc

F32 = jnp.float32
BF16 = jnp.bfloat16

D_MODEL = 1024
HEADS = 4
HEAD_DIM = 128
MIX_HALF = HEADS * HEAD_DIM
CHUNK = 64
GROUP = 2
N_STACKS = HEADS // GROUP
STACK = GROUP * CHUNK
CONV_K = 4
N_GROUPS = 8
EXPERTS_PER_GROUP = 8
N_EXPERTS = N_GROUPS * EXPERTS_PER_GROUP
TOP_K = 2
EXPERT_FF = 256
NORM_EPS = 1e-6
LANES = 128
F32_ROWS = 8
SLOT_BLOCK = 512
VMEM_LIMIT = 56 * 1024 * 1024


def _sigmoid(x):
    return 0.5 * jnp.tanh(0.5 * x) + 0.5


def _silu(x):
    return x * _sigmoid(x)


def _dot(a, b):
    return jnp.dot(a, b, preferred_element_type=F32)


def _dot_nt(a, b):
    return lax.dot_general(a, b, (((1,), (1,)), ((), ())), preferred_element_type=F32)


def _dot_tn(a, b):
    return lax.dot_general(a, b, (((0,), (0,)), ((), ())), preferred_element_type=F32)


def _bdot(a, b):
    return _dot(a.astype(BF16), b.astype(BF16))


def _masked_sum(mask3, x):
    hi = x.astype(BF16)
    r1 = x - hi.astype(F32)
    mid = r1.astype(BF16)
    lo = (r1 - mid.astype(F32)).astype(BF16)
    return _dot(mask3, jnp.concatenate([hi, mid, lo], axis=0))


def _triple(mask):
    return np.concatenate([mask, mask, mask], axis=1)


def _stack_heads(a, p):
    return jnp.concatenate([a[:, h * HEAD_DIM:(h + 1) * HEAD_DIM] for h in range(p * GROUP, (p + 1) * GROUP)],
                           axis=0)


def _each(f, *lists):
    return [f(*args) for args in zip(*lists)]


INPROJ_GROUP = 256


def _l2norm_heads(a):
    return jnp.concatenate(
        [a[:, h * HEAD_DIM:(h + 1) * HEAD_DIM]
         * lax.rsqrt(jnp.sum(jnp.square(a[:, h * HEAD_DIM:(h + 1) * HEAD_DIM]), axis=-1, keepdims=True) + 1e-6)
         for h in range(a.shape[1] // HEAD_DIM)], axis=1)


def _inproj_kernel(x_ref, nw_ref, wa_ref, wb_ref, wc_ref, cw_ref, oa_ref, ob_ref, oc_ref, ubuf, *, tiles_per_seq):
    tm = x_ref.shape[0]

    @pl.when(lax.rem(pl.program_id(0), tiles_per_seq) == 0)
    def _():
        ubuf[0:HIST, :] = jnp.zeros((HIST, GDN_QKV), F32)

    x = x_ref[...]
    ms = jnp.mean(x * x, axis=-1, keepdims=True)
    hn = (x * lax.rsqrt(ms + NORM_EPS) * nw_ref[...]).astype(BF16)
    cw = cw_ref[...]
    width = INPROJ_GROUP
    group = lambda g: slice(g * width, (g + 1) * width)

    def project(w_ref, o_ref, g, row0=0):
        o_ref[row0:row0 + tm, group(g)] = _dot(hn, w_ref[:, group(g)])

    def conv_group(g):
        cols = group(g)
        conv = cw[CONV_K - 1:CONV_K, cols] * ubuf[HIST:HIST + tm, cols]
        for j in range(1, CONV_K):
            conv = conv + cw[CONV_K - 1 - j:CONV_K - j, cols] * ubuf[HIST - j:HIST - j + tm, cols]
        ubuf[0:HIST, cols] = ubuf[tm:tm + HIST, cols]
        act = _silu(conv)
        if g * width < MIX_HALF:
            act = _l2norm_heads(act) * (HEAD_DIM ** -0.5)
        elif g * width < 2 * MIX_HALF:
            act = _l2norm_heads(act)
        ob_ref[:, cols] = act

    n_conv = GDN_QKV // width
    n_a = wa_ref.shape[1] // width
    others = [functools.partial(project, wa_ref, oa_ref, g) for g in range(n_a)]
    others += [functools.partial(project, wb_ref, ob_ref, g) for g in range(n_conv, wb_ref.shape[1] // width)]
    project(wb_ref, ubuf, 0, HIST)
    for g in range(n_conv):
        if g + 1 < n_conv:
            project(wb_ref, ubuf, g + 1, HIST)
        conv_group(g)
        others.pop(0)()
    for task in others:
        task()
    oc_ref[...] = _dot(hn, wc_ref[...])


def _inproj(x2, norm_w, wa, wb, wc, conv_w, tm, tiles_per_seq):
    t = x2.shape[0]
    na, nb, nc = wa.shape[1], wb.shape[1], wc.shape[1]
    const = lambda i: (0, 0)
    return pl.pallas_call(
        functools.partial(_inproj_kernel, tiles_per_seq=tiles_per_seq),
        grid=(t // tm,),
        in_specs=[
            pl.BlockSpec((tm, D_MODEL), lambda i: (i, 0)),
            pl.BlockSpec((1, D_MODEL), const),
            pl.BlockSpec((D_MODEL, na), const),
            pl.BlockSpec((D_MODEL, nb), const),
            pl.BlockSpec((D_MODEL, nc), const),
            pl.BlockSpec((CONV_K, GDN_QKV), const),
        ],
        out_specs=[
            pl.BlockSpec((tm, na), lambda i: (i, 0)),
            pl.BlockSpec((tm, nb), lambda i: (i, 0)),
            pl.BlockSpec((tm, nc), lambda i: (i, 0)),
        ],
        out_shape=[
            jax.ShapeDtypeStruct((t, na), F32),
            jax.ShapeDtypeStruct((t, nb), F32),
            jax.ShapeDtypeStruct((t, nc), F32),
        ],
        scratch_shapes=[pltpu.VMEM((HIST + tm, GDN_QKV), F32)],
        compiler_params=pltpu.CompilerParams(dimension_semantics=("arbitrary",), vmem_limit_bytes=VMEM_LIMIT),
        name="inproj",
    )(x2, norm_w, wa, wb, wc, conv_w)


HGRN_LEVELS = (32, 16, 8, 4, 2, 1)
VPU_ARG_LEVELS = tuple(b for b in HGRN_LEVELS if b >= F32_ROWS)
MXU_ARG_LEVELS = tuple(b for b in HGRN_LEVELS if b < F32_ROWS)
DIAG_CODE = len(HGRN_LEVELS)
NONE_CODE = DIAG_CODE + 1


def _hgrn_arg_matrix():
    t = np.arange(CHUNK)[:, None]
    u = np.arange(CHUNK)[None, :]
    mats = [u <= t]
    for b in MXU_ARG_LEVELS:
        odd = (t // b) % 2 == 1
        start = (t // b) * b
        mats.append(np.where(odd, (u > start) & (u <= t), (u > t) & (u <= start + b)))
    return np.concatenate(mats, axis=0).astype(np.float32)


def _hgrn_level_codes():
    idx = np.arange(STACK)
    h, t = idx // CHUNK, idx % CHUNK
    same = h[:, None] == h[None, :]
    tt, ss = t[:, None], t[None, :]
    code = np.full((STACK, STACK), NONE_CODE, np.int32)
    code[same & (tt == ss)] = DIAG_CODE
    for l, b in enumerate(HGRN_LEVELS):
        sib = (tt // (2 * b) == ss // (2 * b)) & ((tt // b) % 2 == 1) & ((ss // b) % 2 == 0)
        code[same & sib] = l
    return code


def _hgrn_steps(q_ref, f_ref, i_ref, g_ref, lbl_ref, onw_ref, marg_ref, code_ref, o_ref, st_ref, *, n_chunks):
    lbl = lbl_ref[...]
    lmax = jnp.max(lbl, axis=0, keepdims=True)
    lexp = jnp.exp(lbl - lmax)
    lb = lexp[0:1, :] / jnp.sum(lexp, axis=0, keepdims=True)
    onw = onw_ref[...]
    marg = marg_ref[...]
    code = code_ref[...]
    trow = lax.broadcasted_iota(jnp.int32, (STACK, HEAD_DIM), 0) & (CHUNK - 1)

    q_blk = _silu(q_ref[...]) * (HEAD_DIM ** -0.5)
    fg_blk = lb + (1.0 - lb) * _sigmoid(f_ref[...])
    k_blk = 1.0 - fg_blk
    lf_blk = jnp.log(fg_blk)
    v_blk = i_ref[...]
    yield

    rows = [slice(c * CHUNK, (c + 1) * CHUNK) for c in range(n_chunks)]
    args = [_masked_sum(marg, lf_blk[r]) for r in rows]
    yield
    trow_c = lax.broadcasted_iota(jnp.int32, (CHUNK, MIX_HALF), 0)

    def vpu_level_arg(cum, b):
        starts = cum.reshape(CHUNK // b, b, MIX_HALF)[:, 0:1, :]
        spread = lambda a: jnp.broadcast_to(a, (CHUNK // b, b, MIX_HALF)).reshape(CHUNK, MIX_HALF)
        own = spread(starts)
        nxt = spread(jnp.concatenate([starts[1:], starts[-1:]], axis=0))
        return jnp.where((trow_c & b) != 0, cum - own, nxt - cum)

    args = [jnp.concatenate([a[0:CHUNK]] + [vpu_level_arg(a[0:CHUNK], b) for b in VPU_ARG_LEVELS] + [a[CHUNK:]], axis=0)
            for a in args]
    yield
    e_chunk = [jnp.exp(a) for a in args]
    suf_chunk = [jnp.exp(a[CHUNK - 1:CHUNK] - a[0:CHUNK]) for a in args]
    yield
    units = [(c, p) for c in range(n_chunks) for p in range(N_STACKS)]
    qs = [_stack_heads(q_blk[rows[c]], p) for c, p in units]
    ks = [_stack_heads(k_blk[rows[c]], p) for c, p in units]
    vs = [_stack_heads(v_blk[rows[c]], p).astype(BF16) for c, p in units]
    e_part = lambda n: [_stack_heads(e_chunk[c][n * CHUNK:(n + 1) * CHUNK], p) for c, p in units]
    cum_e = e_part(0)
    suf_e = [_stack_heads(suf_chunk[c], p) for c, p in units]
    att = _each(lambda q, k: jnp.where(code == DIAG_CODE, jnp.sum(q * k, axis=-1, keepdims=True), 0.0), qs, ks)

    def level_update(a, x, l, b):
        xb = x.astype(BF16)
        if b < F32_ROWS:
            return jnp.where(code == l, _dot_nt(xb, xb), a)
        blocks = [slice(s0, s0 + b) for s0 in range(0, STACK, b)]
        r = _dot_nt(jnp.concatenate([x[bl] for bl in blocks[1::2]], axis=0).astype(BF16), xb)
        return jnp.concatenate(
            [a[bl] if n % 2 == 0 else jnp.where(code[bl] == l, r[(n // 2) * b:(n // 2 + 1) * b], a[bl])
             for n, bl in enumerate(blocks)], axis=0)

    for l, b in enumerate(HGRN_LEVELS):
        x = _each(lambda q, k, el: jnp.where((trow & b) != 0, q, k) * el, qs, ks, e_part(1 + l))
        yield
        att = _each(lambda xi, a: level_update(a, xi, l, b), x, att)
        yield
    o_intra = _each(lambda a, v: _dot(a.astype(BF16), v), att, vs)
    yield
    qc = _each(lambda q, e: (q * e).astype(BF16), qs, cum_e)
    kd = _each(lambda k, e: (k * e).astype(BF16), ks, suf_e)
    yield

    state = [st_ref[h] for h in range(HEADS)]
    for u, (c, p) in enumerate(units):
        for i in range(GROUP):
            h = p * GROUP + i
            hr = slice(i * CHUNK, (i + 1) * CHUNK)
            hc = slice(h * HEAD_DIM, (h + 1) * HEAD_DIM)
            o = _dot_nt(qc[u][hr], state[h].astype(BF16)) + o_intra[u][hr]
            decay = cum_e[u][i * CHUNK + CHUNK - 1:i * CHUNK + CHUNK, :]
            state[h] = decay * state[h] + _dot_tn(vs[u][hr], kd[u][hr])
            y = o * lax.rsqrt(jnp.mean(o * o, axis=-1, keepdims=True) + NORM_EPS) * onw
            o_ref[rows[c], hc] = (y * _silu(g_ref[rows[c], hc])).astype(o_ref.dtype)
        yield
    for h in range(HEADS):
        st_ref[h] = state[h]


GDN_QKV = 3 * MIX_HALF
HIST = 8
C_NONE, C_DIAG, C_B16, C_B32, C_B64 = 0, 1, 2, 3, 4
MASKED_EXPONENT = -1e30


def _gdn_codes():
    idx = np.arange(STACK)
    h, t = idx // CHUNK, idx % CHUNK
    same = h[:, None] == h[None, :]
    tt, ss = t[:, None], t[None, :]
    code = np.full((STACK, STACK), C_NONE, np.int32)
    low = same & (ss < tt)
    code[low] = C_B64
    code[low & (tt // 32 == ss // 32)] = C_B32
    code[low & (tt // 16 == ss // 16)] = C_B16
    code[same & (tt == ss)] = C_DIAG
    incl = np.tril(np.ones((CHUNK, CHUNK), np.float32))
    return code, incl


def _gdn_steps(qkv_ref, z_ref, ab_ref, alog_ref, dtb_ref, onw_ref, code_ref, incl_ref,
               o_ref, st_ref, *, n_chunks):
    onw = onw_ref[...]
    code = code_ref[...]
    incl_m = incl_ref[...]
    incl = code >= C_DIAG
    eye = (code == C_DIAG).astype(F32)

    rows = [slice(c * CHUNK, (c + 1) * CHUNK) for c in range(n_chunks)]
    units = [(c, p) for c in range(n_chunks) for p in range(N_STACKS)]

    qs = [_stack_heads(qkv_ref[rows[c], 0:MIX_HALF], p) for c, p in units]
    ks = [_stack_heads(qkv_ref[rows[c], MIX_HALF:2 * MIX_HALF], p) for c, p in units]
    vs = [_stack_heads(qkv_ref[rows[c], 2 * MIX_HALF:3 * MIX_HALF], p) for c, p in units]
    ab_blk = ab_ref[...]
    xa = ab_blk + dtb_ref[...]
    softplus = jnp.maximum(xa, 0.0) + jnp.log(1.0 + jnp.exp(-jnp.abs(xa)))
    g_all = -jnp.exp(alog_ref[...]) * softplus
    beta_all = _sigmoid(ab_blk)

    def head_cols(a, c, p, first_lane):
        return jnp.concatenate([a[c][:, first_lane + h:first_lane + h + 1]
                                for h in range(p * GROUP, (p + 1) * GROUP)], axis=0)

    yield
    g_cum = [_masked_sum(incl_m, g_all[r]) for r in rows]
    beta_chunk = [beta_all[r] for r in rows]
    beta_st = [head_cols(beta_chunk, c, p, HEADS) for c, p in units]
    gc = [jnp.broadcast_to(head_cols(g_cum, c, p, 0), (STACK, HEAD_DIM)) for c, p in units]
    yield
    dec = [jnp.exp(jnp.where(incl, g - jnp.transpose(g)[0:1, :], MASKED_EXPONENT)) for g in gc]
    yield
    kb = [k.astype(BF16) for k in ks]
    kq = _each(lambda k, q: _dot_nt(jnp.concatenate([k, q.astype(BF16)], axis=0), k), kb, qs)
    kk = [a[0:STACK] for a in kq]
    qk = [a[STACK:] for a in kq]
    yield
    am = _each(lambda b, k2, d: b * k2 * d, beta_st, kk, dec)
    a16 = [jnp.where(code == C_B16, a, 0.0) for a in am]
    n32 = [jnp.where(code == C_B32, a, 0.0) for a in am]
    n64 = [jnp.where(code == C_B64, a, 0.0) for a in am]
    yield
    apow = _each(_bdot, a16, a16)
    pinv = [eye - a for a in a16]
    yield
    for _ in range(2):
        both = _each(lambda ai, pi: _bdot(jnp.concatenate([ai, pi], axis=0), ai), apow, pinv)
        pinv = _each(lambda pi, b: pi + b[STACK:], pinv, both)
        apow = [b[0:STACK] for b in both]
        yield
    pinv = _each(lambda pi, ai: pi + _bdot(pi, ai), pinv, apow)
    yield
    for nlev in (n32, n64):
        t = _each(_bdot, nlev, pinv)
        yield
        pinv = _each(lambda pi, ti: pi - _bdot(pi, ti), pinv, t)
        yield
    egc = [jnp.exp(g) for g in gc]
    rhs = _each(lambda k, v, b, e: jnp.concatenate([k * (b * e), v * b], axis=1), ks, vs, beta_st, egc)
    wu = _each(_bdot, pinv, rhs)
    yield
    w_c = [a[:, 0:HEAD_DIM].astype(BF16) for a in wu]
    u_c = [a[:, HEAD_DIM:] for a in wu]
    qkm = _each(lambda a, d: (a * d).astype(BF16), qk, dec)
    qg = _each(lambda q, e: (q * e).astype(BF16), qs, egc)
    glast = [[g[i * CHUNK + CHUNK - 1:(i + 1) * CHUNK, :] for i in range(GROUP)] for g in gc]
    kdec = [[(k[i * CHUNK:(i + 1) * CHUNK] * jnp.exp(gl[i] - g[i * CHUNK:(i + 1) * CHUNK])).astype(BF16)
             for i in range(GROUP)] for k, g, gl in zip(ks, gc, glast)]
    gend = [[jnp.exp(gi) for gi in gl] for gl in glast]
    yield

    state = [st_ref[h] for h in range(HEADS)]
    for u, (c, p) in enumerate(units):
        heads = range(p * GROUP, (p + 1) * GROUP)
        hrs = [slice(i * CHUNK, (i + 1) * CHUNK) for i in range(GROUP)]
        ws = [_dot_nt(jnp.concatenate([w_c[u][hr], qg[u][hr]], axis=0), state[h].astype(BF16))
              for hr, h in zip(hrs, heads)]
        v_new = jnp.concatenate([u_c[u][hr] - a[0:CHUNK] for hr, a in zip(hrs, ws)], axis=0).astype(BF16)
        o_st = jnp.concatenate([a[CHUNK:] for a in ws], axis=0) + _dot(qkm[u], v_new)
        for i, h in enumerate(heads):
            hc = slice(h * HEAD_DIM, (h + 1) * HEAD_DIM)
            state[h] = gend[u][i] * state[h] + _dot_tn(v_new[hrs[i]], kdec[u][i])
            o = o_st[hrs[i]]
            y = o * lax.rsqrt(jnp.mean(o * o, axis=-1, keepdims=True) + NORM_EPS) * onw
            o_ref[rows[c], hc] = (y * _silu(z_ref[rows[c], hc])).astype(o_ref.dtype)
        yield
    for h in range(HEADS):
        st_ref[h] = state[h]


N_HGRN_REFS, N_GDN_REFS = 8, 8


def _mixers_kernel(*refs, n_chunks):
    hg_in = refs[0:N_HGRN_REFS]
    gd_in = refs[N_HGRN_REFS:N_HGRN_REFS + N_GDN_REFS]
    ohg_ref, ogd_ref, hg_st, gd_st = refs[N_HGRN_REFS + N_GDN_REFS:]

    @pl.when(pl.program_id(1) == 0)
    def _():
        hg_st[...] = jnp.zeros_like(hg_st)
        gd_st[...] = jnp.zeros_like(gd_st)

    bodies = [_hgrn_steps(*hg_in, ohg_ref, hg_st, n_chunks=n_chunks),
              _gdn_steps(*gd_in, ogd_ref, gd_st, n_chunks=n_chunks)]
    while bodies:
        for body in list(bodies):
            if next(body, StopIteration) is StopIteration:
                bodies.remove(body)


def _mixers(pa, pb, pc, lb_logits, hg_onorm_w, alog_row, dtb_row, gd_onorm_w, batch, seq, n_chunks):
    rows = n_chunks * CHUNK
    steps = seq // rows
    const = lambda b, s: (0, 0)
    blk = lambda width, j: pl.BlockSpec((rows, width), lambda b, s, j=j: (b * steps + s, j))
    marg = jnp.asarray(_triple(_hgrn_arg_matrix()), BF16)
    hg_code = jnp.asarray(_hgrn_level_codes())
    gd_code, incl = _gdn_codes()
    gd_code, incl = jnp.asarray(gd_code), jnp.asarray(_triple(incl), BF16)
    sq = pl.BlockSpec((STACK, STACK), const)
    row = pl.BlockSpec((1, LANES), const)
    state = pltpu.VMEM((HEADS, HEAD_DIM, HEAD_DIM), F32)
    return pl.pallas_call(
        functools.partial(_mixers_kernel, n_chunks=n_chunks),
        grid=(batch, steps),
        in_specs=[blk(MIX_HALF, 0), blk(MIX_HALF, 1), blk(MIX_HALF, 2), blk(MIX_HALF, 3),
                  pl.BlockSpec(lb_logits.shape, const), pl.BlockSpec((1, HEAD_DIM), const),
                  pl.BlockSpec(marg.shape, const), sq,
                  blk(GDN_QKV, 0), blk(MIX_HALF, GDN_QKV // MIX_HALF), blk(LANES, 0),
                  row, row, row, sq, pl.BlockSpec((CHUNK, 3 * CHUNK), const)],
        out_specs=[blk(MIX_HALF, 0), blk(MIX_HALF, 0)],
        out_shape=[jax.ShapeDtypeStruct((batch * seq, MIX_HALF), BF16)] * 2,
        scratch_shapes=[state, state],
        compiler_params=pltpu.CompilerParams(dimension_semantics=("arbitrary", "arbitrary"),
                                             vmem_limit_bytes=VMEM_LIMIT),
        name="mixers",
    )(pa, pa, pa, pa, lb_logits, hg_onorm_w, marg, hg_code, pb, pb, pc, alog_row, dtb_row, gd_onorm_w, gd_code, incl)


ROUTE_ROWS = 8


ROUTE_SPLIT = 4


def _route_steps(x_ref, ohg_ref, ogd_ref, wo_ref, nw_ref, wr_ref, ltri_ref, h_ref, hn_ref, ri_ref, rg_ref,
                 rows, counts):
    n = rows.stop - rows.start
    mix = _dot(ohg_ref[rows, :], wo_ref[0:MIX_HALF, :]) + _dot(ogd_ref[rows, :], wo_ref[MIX_HALF:2 * MIX_HALF, :])
    yield
    h = x_ref[rows, :] + mix
    h_ref[rows, :] = h
    hn = h * lax.rsqrt(jnp.mean(h * h, axis=-1, keepdims=True) + NORM_EPS) * nw_ref[...]
    hn_ref[rows, :] = _pack_words(hn)
    yield
    hn_hi = hn.astype(BF16)
    hn_lo = (hn - hn_hi.astype(F32)).astype(BF16)
    part = _dot(hn_hi, wr_ref[...])
    logits = (_dot(hn_lo, wr_ref[:, 0:LANES]) + part[:, LANES:]) + part[:, 0:LANES]
    yield
    lane = lax.broadcasted_iota(jnp.int32, (n, LANES), 1)
    neg = jnp.float32(-jnp.inf)
    big = jnp.int32(LANES)

    def first_max(vals):
        m = jnp.max(vals, axis=-1, keepdims=True)
        return m, jnp.min(jnp.where(vals == m, lane, big), axis=-1, keepdims=True)

    gl = jnp.where(lane < N_GROUPS, logits, neg)
    gmax, gidx = first_max(gl)
    yield
    p_group = 1.0 / jnp.sum(jnp.exp(gl - gmax), axis=-1, keepdims=True)
    lo = N_GROUPS + EXPERTS_PER_GROUP * gidx
    el = jnp.where((lane >= lo) & (lane < lo + EXPERTS_PER_GROUP), logits, neg)
    m1, i1 = first_max(el)
    yield
    m2, i2 = first_max(jnp.where(lane == i1, neg, el))
    yield
    r = jnp.exp(m2 - m1)
    gate1 = p_group / (1.0 + r)
    gate2 = p_group * r / (1.0 + r)
    hot1 = lane == i1
    hot2 = lane == i2
    onehot = jnp.where(hot1 | hot2, 1.0, 0.0)
    within = _dot(ltri_ref[0:n, 0:n], onehot.astype(BF16))
    yield
    before = within + counts[0]
    counts[0] = counts[0] + jnp.sum(onehot, axis=0, keepdims=True)
    rank1 = jnp.sum(jnp.where(hot1, before, 0.0), axis=-1, keepdims=True)
    rank2 = jnp.sum(jnp.where(hot2, before, 0.0), axis=-1, keepdims=True)
    yield
    ri = jnp.where(lane == 0, i1 - N_GROUPS,
                   jnp.where(lane == 1, i2 - N_GROUPS,
                             jnp.where(lane == 2, rank1.astype(jnp.int32),
                                       jnp.where(lane == 3, rank2.astype(jnp.int32), 0))))
    ri_ref[:, rows] = jnp.transpose(ri)[0:ROUTE_ROWS, :]
    rg_ref[rows, :] = jnp.where(lane == 0, gate1, jnp.where(lane == 1, gate2, 0.0))


def _route_kernel(x_ref, ohg_ref, ogd_ref, wo_ref, nw_ref, wr_ref, ltri_ref,
                  h_ref, hn_ref, ri_ref, rg_ref, cnt_ref, cnt_scr):
    @pl.when(pl.program_id(0) == 0)
    def _():
        cnt_scr[...] = jnp.zeros_like(cnt_scr)

    tm = x_ref.shape[0]
    sub = tm // ROUTE_SPLIT
    counts = [cnt_scr[...]]
    bodies = [_route_steps(x_ref, ohg_ref, ogd_ref, wo_ref, nw_ref, wr_ref, ltri_ref, h_ref, hn_ref, ri_ref, rg_ref,
                           slice(s * sub, (s + 1) * sub), counts) for s in range(ROUTE_SPLIT)]
    while bodies:
        for body in list(bodies):
            if next(body, StopIteration) is StopIteration:
                bodies.remove(body)
    cnt_scr[...] = counts[0]
    cnt_ref[...] = counts[0]


def _route(x2, ohg, ogd, wo, norm_w, wr, tm):
    t = x2.shape[0]
    ltri = jnp.asarray(np.tril(np.ones((tm, tm), np.float32), -1), BF16)
    const = lambda i: (0, 0)
    tile = lambda n: pl.BlockSpec((tm, n), lambda i: (i, 0))
    return pl.pallas_call(
        _route_kernel,
        grid=(t // tm,),
        in_specs=[tile(D_MODEL), tile(MIX_HALF), tile(MIX_HALF),
                  pl.BlockSpec((D_MODEL, D_MODEL), const),
                  pl.BlockSpec((1, D_MODEL), const),
                  pl.BlockSpec((D_MODEL, 2 * LANES), const),
                  pl.BlockSpec((tm, tm), const)],
        out_specs=[tile(D_MODEL), tile(HALF), pl.BlockSpec((ROUTE_ROWS, tm), lambda i: (0, i)), tile(LANES),
                   pl.BlockSpec((1, LANES), const)],
        out_shape=[jax.ShapeDtypeStruct((t, D_MODEL), F32),
                   jax.ShapeDtypeStruct((t, HALF), WORD),
                   jax.ShapeDtypeStruct((ROUTE_ROWS, t), jnp.int32),
                   jax.ShapeDtypeStruct((t, LANES), F32),
                   jax.ShapeDtypeStruct((1, LANES), F32)],
        scratch_shapes=[pltpu.VMEM((1, LANES), F32)],
        compiler_params=pltpu.CompilerParams(dimension_semantics=("arbitrary",), vmem_limit_bytes=VMEM_LIMIT),
        name="route",
    )(x2, ohg, ogd, wo, norm_w, wr, ltri)


HALF = D_MODEL // 2
WORD = jnp.uint32
HIGH_HALF = np.uint32(0xFFFF0000)


def _pack_words(x):
    bits = lambda a: lax.bitcast_convert_type(a.astype(BF16).astype(F32), WORD)
    return (bits(x[:, 0:HALF]) >> 16) | (bits(x[:, HALF:]) & HIGH_HALF)


def _unpack_words(words):
    return (lax.bitcast_convert_type(words << 16, F32), lax.bitcast_convert_type(words & HIGH_HALF, F32))


SC_WINDOW = 64


def _sc_permute_rows(rows, idx, gather):
    info = plsc.get_sparse_core_info()
    n_workers = info.num_cores * info.num_subcores
    n_idx, (n_src, width) = idx.shape[0], rows.shape
    copies = 1 if gather else n_idx // n_src
    per_worker = n_idx // copies // n_workers
    n_win = per_worker // SC_WINDOW
    assert per_worker * n_workers * copies == n_idx and n_win * SC_WINDOW == per_worker and n_win % 2 == 0
    idx2 = idx.reshape(n_idx // SC_WINDOW, SC_WINDOW)
    win_per_copy = n_idx // copies // SC_WINDOW

    @functools.partial(
        pl.kernel,
        mesh=plsc.VectorSubcoreMesh(core_axis_name="c", subcore_axis_name="s"),
        out_type=jax.ShapeDtypeStruct((n_idx, width), rows.dtype),
        scratch_types=[pltpu.VMEM((copies, n_win, SC_WINDOW), jnp.int32),
                       pltpu.VMEM((2, SC_WINDOW, width), rows.dtype),
                       pltpu.SemaphoreType.DMA((2,)), pltpu.SemaphoreType.DMA((2,))],
    )
    def permute(rows_hbm, idx_hbm, out_hbm, idx_v, buf, fill_sem, drain_sem):
        worker = lax.axis_index("s") * info.num_cores + lax.axis_index("c")
        first_row = worker * per_worker
        for c in range(copies):
            pltpu.sync_copy(idx_hbm.at[pl.ds(c * win_per_copy + worker * n_win, n_win)], idx_v.at[c])

        def window(ref, j):
            return ref.at[pl.ds(pl.multiple_of(first_row + j * SC_WINDOW, SC_WINDOW), SC_WINDOW)]

        def fill(j, b):
            src = rows_hbm.at[idx_v.at[0, j]] if gather else window(rows_hbm, j)
            return pltpu.make_async_copy(src, buf.at[b], fill_sem.at[b])

        def drains(j, b):
            dsts = [window(out_hbm, j)] if gather else [out_hbm.at[idx_v.at[c, j]] for c in range(copies)]
            return [pltpu.make_async_copy(buf.at[b], dst, drain_sem.at[b]) for dst in dsts]

        fill(0, 0).start()

        @pl.loop(0, n_win, step=2)
        def _(j0):
            for b in range(2):
                j = j0 + b
                fill(j, b).wait()

                @pl.when(j >= 1)
                def _():
                    for d in drains(j - 1, 1 - b):
                        d.wait()

                @pl.when(j + 1 < n_win)
                def _():
                    fill(j + 1, 1 - b).start()

                for d in drains(j, b):
                    d.start()

        for d in drains(n_win - 1, 1):
            d.wait()

    return permute(rows, idx2)


def _sc_gather_rows(table, idx):
    return _sc_permute_rows(table, idx, gather=True)


def _sc_scatter_rows(rows, idx):
    return _sc_permute_rows(rows, idx, gather=False)


def _expert_kernel(blk_ref, exp_ref, lo_ref, hi_ref, xs_ref, wg_ref, wu_ref, wd_ref, ys_ref, wgu_b, wd_b, acc):
    p = pl.program_id(0)

    @pl.when(jnp.logical_or(p == 0, exp_ref[p] != exp_ref[jnp.maximum(p - 1, 0)]))
    def _():
        wgu_b[:, 0:EXPERT_FF] = wg_ref[0].astype(BF16)
        wgu_b[:, EXPERT_FF:2 * EXPERT_FF] = wu_ref[0].astype(BF16)
        wd_b[...] = wd_ref[0].astype(BF16)

    @pl.when(p == 0)
    def _():
        acc[...] = jnp.zeros_like(acc)

    first = jnp.logical_or(p == 0, blk_ref[p] != blk_ref[jnp.maximum(p - 1, 0)])
    x_lo, x_hi = _unpack_words(xs_ref[...])
    ab = _dot(x_lo.astype(BF16), wgu_b[0:HALF, :]) + _dot(x_hi.astype(BF16), wgu_b[HALF:D_MODEL, :])
    hb = _silu(ab[:, 0:EXPERT_FF]) * ab[:, EXPERT_FF:2 * EXPERT_FF]
    y = _dot(hb.astype(BF16), wd_b[...])
    slot = blk_ref[p] * SLOT_BLOCK + lax.broadcasted_iota(jnp.int32, (SLOT_BLOCK, 1), 0)
    y = jnp.where((slot >= lo_ref[p]) & (slot < hi_ref[p]), y, 0.0)
    total = y + jnp.where(first, 0.0, acc[...])
    acc[...] = total
    ys_ref[...] = _pack_words(total)


def _experts(blk, exp, lo, hi, xs, wg, wu, wd):
    n_pairs = blk.shape[0]
    return pl.pallas_call(
        _expert_kernel,
        grid_spec=pltpu.PrefetchScalarGridSpec(
            num_scalar_prefetch=4,
            grid=(n_pairs,),
            in_specs=[pl.BlockSpec((SLOT_BLOCK, HALF), lambda p, blk, exp, lo, hi: (blk[p], 0)),
                      pl.BlockSpec((1, D_MODEL, EXPERT_FF), lambda p, blk, exp, lo, hi: (exp[p], 0, 0)),
                      pl.BlockSpec((1, D_MODEL, EXPERT_FF), lambda p, blk, exp, lo, hi: (exp[p], 0, 0)),
                      pl.BlockSpec((1, EXPERT_FF, D_MODEL), lambda p, blk, exp, lo, hi: (exp[p], 0, 0))],
            out_specs=pl.BlockSpec((SLOT_BLOCK, HALF), lambda p, blk, exp, lo, hi: (blk[p], 0)),
            scratch_shapes=[pltpu.VMEM((D_MODEL, 2 * EXPERT_FF), BF16), pltpu.VMEM((EXPERT_FF, D_MODEL), BF16),
                            pltpu.VMEM((SLOT_BLOCK, D_MODEL), F32)],
        ),
        out_shape=jax.ShapeDtypeStruct(xs.shape, WORD),
        compiler_params=pltpu.CompilerParams(dimension_semantics=("arbitrary",), vmem_limit_bytes=VMEM_LIMIT),
        name="experts",
    )(blk, exp, lo, hi, xs, wg, wu, wd)


COMBINE_PARTS = 2


def _combine_kernel(h_ref, rg_ref, fw_ref, y0_ref, y1_ref, *rest):
    o_ref = rest[-1]
    rg = rg_ref[...]
    y = [jnp.concatenate(_unpack_words(ref[...]), axis=1) for ref in (y0_ref, y1_ref)]
    h = h_ref[...] + rg[:, 0:1] * y[0] + rg[:, 1:2] * y[1]
    o_ref[...] = h * lax.rsqrt(jnp.mean(h * h, axis=-1, keepdims=True) + NORM_EPS) * fw_ref[...]


def _combine(h, rg, final_w, y2, tm, part, n_parts, out_so_far=None):
    t = h.shape[0]
    n_tiles = t // tm // n_parts
    first = part * n_tiles
    in_specs = [pl.BlockSpec((tm, D_MODEL), lambda i: (first + i, 0)),
                pl.BlockSpec((tm, LANES), lambda i: (first + i, 0)),
                pl.BlockSpec((1, D_MODEL), lambda i: (0, 0)),
                pl.BlockSpec((tm, HALF), lambda i: (i, 0)),
                pl.BlockSpec((tm, HALF), lambda i: (n_tiles + i, 0))]
    args = [h, rg, final_w, y2, y2]
    aliases = {}
    if out_so_far is not None:
        in_specs.append(pl.BlockSpec(memory_space=pl.ANY))
        aliases = {len(args): 0}
        args.append(out_so_far)
    return pl.pallas_call(
        _combine_kernel,
        grid=(n_tiles,),
        in_specs=in_specs,
        out_specs=pl.BlockSpec((tm, D_MODEL), lambda i: (first + i, 0)),
        out_shape=jax.ShapeDtypeStruct((t, D_MODEL), F32),
        input_output_aliases=aliases,
        compiler_params=pltpu.CompilerParams(dimension_semantics=("arbitrary",), vmem_limit_bytes=VMEM_LIMIT),
        name="combine",
    )(*args)


def _pair_schedule(counts, n_slots):
    n_blocks = n_slots // SLOT_BLOCK
    n_pairs = n_blocks + N_EXPERTS - 1
    ends = jnp.cumsum(counts)
    starts = ends - counts
    first = starts // SLOT_BLOCK
    last = jnp.maximum(ends - 1, starts) // SLOT_BLOCK
    per_expert = jnp.where(counts > 0, last - first + 1, 0)
    cum = jnp.cumsum(per_expert)
    p = jnp.arange(n_pairs, dtype=jnp.int32)
    e = jnp.minimum(jnp.sum(cum[None, :] <= p[:, None], axis=1), N_EXPERTS - 1).astype(jnp.int32)
    valid = p < cum[-1]
    onehot = e[:, None] == jnp.arange(N_EXPERTS, dtype=jnp.int32)[None, :]
    pick = lambda table: jnp.sum(jnp.where(onehot, table[None, :], 0), axis=1)
    blk = jnp.where(valid, pick(first) + p - pick(cum - per_expert), n_blocks - 1).astype(jnp.int32)
    lo = jnp.where(valid, pick(starts), 0).astype(jnp.int32)
    hi = jnp.where(valid, pick(ends), 0).astype(jnp.int32)
    return starts.astype(jnp.int32), blk, e, lo, hi


def _layer(x, norm_mix_w, w_in, lb_logits, hgrn_onorm_w, gdn_conv_w, gdn_a_log, gdn_dt_bias, gdn_onorm_w, w_out,
           norm_moe_w, router_group_w, router_expert_w, w_gate, w_up, w_down, final_w, *, tm, n_chunks):
    batch, seq, _ = x.shape
    t = batch * seq
    x2 = x.reshape(t, D_MODEL)
    n_hg = 4 * MIX_HALF
    n_gd = GDN_QKV + MIX_HALF
    wb16 = w_in.astype(BF16)
    wa = wb16[:, 0:n_hg]
    wb = wb16[:, n_hg:n_hg + n_gd]
    wc = jnp.pad(wb16[:, n_hg + n_gd:], ((0, 0), (0, LANES - 2 * HEADS)))
    pa, pb, pc = _inproj(x2, norm_mix_w.reshape(1, D_MODEL), wa, wb, wc, gdn_conv_w, 2 * tm, seq // (2 * tm))

    alog_row = jnp.pad(gdn_a_log.reshape(1, HEADS), ((0, 0), (0, LANES - HEADS)))
    dtb_row = jnp.pad(gdn_dt_bias.reshape(1, HEADS), ((0, 0), (0, LANES - HEADS)))
    o_hg, o_gd = _mixers(pa, pb, pc, lb_logits, hgrn_onorm_w.reshape(1, HEAD_DIM), alog_row, dtb_row,
                         gdn_onorm_w.reshape(1, HEAD_DIM), batch, seq, n_chunks)

    wr = jnp.pad(jnp.concatenate([router_group_w, router_expert_w], axis=1),
                 ((0, 0), (0, LANES - N_GROUPS - N_EXPERTS)))
    wr_hi = wr.astype(BF16)
    wr = jnp.concatenate([wr_hi, (wr - wr_hi.astype(F32)).astype(BF16)], axis=1)
    h, hn, ri, rg, cnt = _route(x2, o_hg, o_gd, w_out.astype(BF16), norm_moe_w.reshape(1, D_MODEL), wr, 2 * tm)

    counts = cnt[0, N_GROUPS:N_GROUPS + N_EXPERTS].astype(jnp.int32)
    starts, blk, exp, lo, hi = _pair_schedule(counts, t * TOP_K)
    onehot = ri[0:TOP_K, :, None] == jnp.arange(N_EXPERTS, dtype=jnp.int32)
    slot_kt = jnp.sum(jnp.where(onehot, starts, 0), axis=-1) + ri[TOP_K:2 * TOP_K]
    xs = _sc_scatter_rows(hn, slot_kt.reshape(TOP_K * t))
    ys = _experts(blk, exp, lo, hi, xs, w_gate, w_up, w_down)
    out = None
    for part in range(COMBINE_PARTS):
        tp = t // COMBINE_PARTS
        part_slots = slot_kt[:, part * tp:(part + 1) * tp].reshape(TOP_K * tp)
        y2 = _sc_gather_rows(ys, part_slots)
        out = _combine(h, rg, final_w.reshape(1, D_MODEL), y2, 2 * tm, part, COMBINE_PARTS, out)
    return out.reshape(batch, seq, D_MODEL)


def kernel(x, norm_mix_w, w_in, hgrn_lb_logits, hgrn_onorm_w, gdn_conv_w, gdn_a_log, gdn_dt_bias, gdn_onorm_w, w_out, norm_moe_w, router_group_w, router_expert_w, expert_w_gate, expert_w_up, expert_w_down, final_norm_w):
    return _layer(x, norm_mix_w[0], w_in[0], hgrn_lb_logits, hgrn_onorm_w[0], gdn_conv_w[0], gdn_a_log[0],
                  gdn_dt_bias[0], gdn_onorm_w[0], w_out[0], norm_moe_w[0], router_group_w[0], router_expert_w[0],
                  expert_w_gate[0], expert_w_up[0], expert_w_down[0], final_norm_w, tm=256, n_chunks=8)
```

```python
import functools

import numpy as np
import jax
import jax.numpy as jnp
from jax import lax
from jax.experimental import pallas as pl
from jax.experimental.pallas import tpu as pltpu
from jax.experimental.pallas import tpu_sc as plsc

F32 = jnp.float32
BF16 = jnp.bfloat16

D_MODEL = 1024
HEADS = 4
HEAD_DIM = 128
MIX_HALF = HEADS * HEAD_DIM
CHUNK = 64
GROUP = 2
N_STACKS = HEADS // GROUP
STACK = GROUP * CHUNK
CONV_K = 4
N_GROUPS = 8
EXPERTS_PER_GROUP = 8
N_EXPERTS = N_GROUPS * EXPERTS_PER_GROUP
TOP_K = 2
EXPERT_FF = 256
NORM_EPS = 1e-6
LANES = 128
F32_ROWS = 8
SLOT_BLOCK = 512
VMEM_LIMIT = 56 * 1024 * 1024


def _sigmoid(x):
    return 0.5 * jnp.tanh(0.5 * x) + 0.5


def _silu(x):
    return x * _sigmoid(x)


def _dot(a, b):
    return jnp.dot(a, b, preferred_element_type=F32)


def _dot_nt(a, b):
    return lax.dot_general(a, b, (((1,), (1,)), ((), ())), preferred_element_type=F32)


def _dot_tn(a, b):
    return lax.dot_general(a, b, (((0,), (0,)), ((), ())), preferred_element_type=F32)


def _bdot(a, b):
    return _dot(a.astype(BF16), b.astype(BF16))


def _masked_sum(mask3, x):
    hi = x.astype(BF16)
    r1 = x - hi.astype(F32)
    mid = r1.astype(BF16)
    lo = (r1 - mid.astype(F32)).astype(BF16)
    return _dot(mask3, jnp.concatenate([hi, mid, lo], axis=0))


def _triple(mask):
    return np.concatenate([mask, mask, mask], axis=1)


def _stack_heads(a, p):
    return jnp.concatenate([a[:, h * HEAD_DIM:(h + 1) * HEAD_DIM] for h in range(p * GROUP, (p + 1) * GROUP)],
                           axis=0)


def _each(f, *lists):
    return [f(*args) for args in zip(*lists)]


INPROJ_GROUP = 256


def _l2norm_heads(a):
    return jnp.concatenate(
        [a[:, h * HEAD_DIM:(h + 1) * HEAD_DIM]
         * lax.rsqrt(jnp.sum(jnp.square(a[:, h * HEAD_DIM:(h + 1) * HEAD_DIM]), axis=-1, keepdims=True) + 1e-6)
         for h in range(a.shape[1] // HEAD_DIM)], axis=1)


def _inproj_kernel(x_ref, nw_ref, wa_ref, wb_ref, wc_ref, cw_ref, oa_ref, ob_ref, oc_ref, ubuf, *, tiles_per_seq):
    tm = x_ref.shape[0]

    @pl.when(lax.rem(pl.program_id(0), tiles_per_seq) == 0)
    def _():
        ubuf[0:HIST, :] = jnp.zeros((HIST, GDN_QKV), F32)

    x = x_ref[...]
    ms = jnp.mean(x * x, axis=-1, keepdims=True)
    hn = (x * lax.rsqrt(ms + NORM_EPS) * nw_ref[...]).astype(BF16)
    cw = cw_ref[...]
    width = INPROJ_GROUP
    group = lambda g: slice(g * width, (g + 1) * width)

    def project(w_ref, o_ref, g, row0=0):
        o_ref[row0:row0 + tm, group(g)] = _dot(hn, w_ref[:, group(g)])

    def conv_group(g):
        cols = group(g)
        conv = cw[CONV_K - 1:CONV_K, cols] * ubuf[HIST:HIST + tm, cols]
        for j in range(1, CONV_K):
            conv = conv + cw[CONV_K - 1 - j:CONV_K - j, cols] * ubuf[HIST - j:HIST - j + tm, cols]
        ubuf[0:HIST, cols] = ubuf[tm:tm + HIST, cols]
        act = _silu(conv)
        if g * width < MIX_HALF:
            act = _l2norm_heads(act) * (HEAD_DIM ** -0.5)
        elif g * width < 2 * MIX_HALF:
            act = _l2norm_heads(act)
        ob_ref[:, cols] = act

    n_conv = GDN_QKV // width
    n_a = wa_ref.shape[1] // width
    others = [functools.partial(project, wa_ref, oa_ref, g) for g in range(n_a)]
    others += [functools.partial(project, wb_ref, ob_ref, g) for g in range(n_conv, wb_ref.shape[1] // width)]
    project(wb_ref, ubuf, 0, HIST)
    for g in range(n_conv):
        if g + 1 < n_conv:
            project(wb_ref, ubuf, g + 1, HIST)
        conv_group(g)
        others.pop(0)()
    for task in others:
        task()
    oc_ref[...] = _dot(hn, wc_ref[...])


def _inproj(x2, norm_w, wa, wb, wc, conv_w, tm, tiles_per_seq):
    t = x2.shape[0]
    na, nb, nc = wa.shape[1], wb.shape[1], wc.shape[1]
    const = lambda i: (0, 0)
    return pl.pallas_call(
        functools.partial(_inproj_kernel, tiles_per_seq=tiles_per_seq),
        grid=(t // tm,),
        in_specs=[
            pl.BlockSpec((tm, D_MODEL), lambda i: (i, 0)),
            pl.BlockSpec((1, D_MODEL), const),
            pl.BlockSpec((D_MODEL, na), const),
            pl.BlockSpec((D_MODEL, nb), const),
            pl.BlockSpec((D_MODEL, nc), const),
            pl.BlockSpec((CONV_K, GDN_QKV), const),
        ],
        out_specs=[
            pl.BlockSpec((tm, na), lambda i: (i, 0)),
            pl.BlockSpec((tm, nb), lambda i: (i, 0)),
            pl.BlockSpec((tm, nc), lambda i: (i, 0)),
        ],
        out_shape=[
            jax.ShapeDtypeStruct((t, na), F32),
            jax.ShapeDtypeStruct((t, nb), F32),
            jax.ShapeDtypeStruct((t, nc), F32),
        ],
        scratch_shapes=[pltpu.VMEM((HIST + tm, GDN_QKV), F32)],
        compiler_params=pltpu.CompilerParams(dimension_semantics=("arbitrary",), vmem_limit_bytes=VMEM_LIMIT),
        name="inproj",
    )(x2, norm_w, wa, wb, wc, conv_w)


HGRN_LEVELS = (32, 16, 8, 4, 2, 1)
VPU_ARG_LEVELS = tuple(b for b in HGRN_LEVELS if b >= F32_ROWS)
MXU_ARG_LEVELS = tuple(b for b in HGRN_LEVELS if b < F32_ROWS)
DIAG_CODE = len(HGRN_LEVELS)
NONE_CODE = DIAG_CODE + 1


def _hgrn_arg_matrix():
    t = np.arange(CHUNK)[:, None]
    u = np.arange(CHUNK)[None, :]
    mats = [u <= t]
    for b in MXU_ARG_LEVELS:
        odd = (t // b) % 2 == 1
        start = (t // b) * b
        mats.append(np.where(odd, (u > start) & (u <= t), (u > t) & (u <= start + b)))
    return np.concatenate(mats, axis=0).astype(np.float32)


def _hgrn_level_codes():
    idx = np.arange(STACK)
    h, t = idx // CHUNK, idx % CHUNK
    same = h[:, None] == h[None, :]
    tt, ss = t[:, None], t[None, :]
    code = np.full((STACK, STACK), NONE_CODE, np.int32)
    code[same & (tt == ss)] = DIAG_CODE
    for l, b in enumerate(HGRN_LEVELS):
        sib = (tt // (2 * b) == ss // (2 * b)) & ((tt // b) % 2 == 1) & ((ss // b) % 2 == 0)
        code[same & sib] = l
    return code


def _hgrn_steps(q_ref, f_ref, i_ref, g_ref, lbl_ref, onw_ref, marg_ref, code_ref, o_ref, st_ref, *, n_chunks):
    lbl = lbl_ref[...]
    lmax = jnp.max(lbl, axis=0, keepdims=True)
    lexp = jnp.exp(lbl - lmax)
    lb = lexp[0:1, :] / jnp.sum(lexp, axis=0, keepdims=True)
    onw = onw_ref[...]
    marg = marg_ref[...]
    code = code_ref[...]
    trow = lax.broadcasted_iota(jnp.int32, (STACK, HEAD_DIM), 0) & (CHUNK - 1)

    q_blk = _silu(q_ref[...]) * (HEAD_DIM ** -0.5)
    fg_blk = lb + (1.0 - lb) * _sigmoid(f_ref[...])
    k_blk = 1.0 - fg_blk
    lf_blk = jnp.log(fg_blk)
    v_blk = i_ref[...]
    yield

    rows = [slice(c * CHUNK, (c + 1) * CHUNK) for c in range(n_chunks)]
    args = [_masked_sum(marg, lf_blk[r]) for r in rows]
    yield
    trow_c = lax.broadcasted_iota(jnp.int32, (CHUNK, MIX_HALF), 0)

    def vpu_level_arg(cum, b):
        starts = cum.reshape(CHUNK // b, b, MIX_HALF)[:, 0:1, :]
        spread = lambda a: jnp.broadcast_to(a, (CHUNK // b, b, MIX_HALF)).reshape(CHUNK, MIX_HALF)
        own = spread(starts)
        nxt = spread(jnp.concatenate([starts[1:], starts[-1:]], axis=0))
        return jnp.where((trow_c & b) != 0, cum - own, nxt - cum)

    args = [jnp.concatenate([a[0:CHUNK]] + [vpu_level_arg(a[0:CHUNK], b) for b in VPU_ARG_LEVELS] + [a[CHUNK:]], axis=0)
            for a in args]
    yield
    e_chunk = [jnp.exp(a) for a in args]
    suf_chunk = [jnp.exp(a[CHUNK - 1:CHUNK] - a[0:CHUNK]) for a in args]
    yield
    units = [(c, p) for c in range(n_chunks) for p in range(N_STACKS)]
    qs = [_stack_heads(q_blk[rows[c]], p) for c, p in units]
    ks = [_stack_heads(k_blk[rows[c]], p) for c, p in units]
    vs = [_stack_heads(v_blk[rows[c]], p).astype(BF16) for c, p in units]
    e_part = lambda n: [_stack_heads(e_chunk[c][n * CHUNK:(n + 1) * CHUNK], p) for c, p in units]
    cum_e = e_part(0)
    suf_e = [_stack_heads(suf_chunk[c], p) for c, p in units]
    att = _each(lambda q, k: jnp.where(code == DIAG_CODE, jnp.sum(q * k, axis=-1, keepdims=True), 0.0), qs, ks)

    def level_update(a, x, l, b):
        xb = x.astype(BF16)
        if b < F32_ROWS:
            return jnp.where(code == l, _dot_nt(xb, xb), a)
        blocks = [slice(s0, s0 + b) for s0 in range(0, STACK, b)]
        r = _dot_nt(jnp.concatenate([x[bl] for bl in blocks[1::2]], axis=0).astype(BF16), xb)
        return jnp.concatenate(
            [a[bl] if n % 2 == 0 else jnp.where(code[bl] == l, r[(n // 2) * b:(n // 2 + 1) * b], a[bl])
             for n, bl in enumerate(blocks)], axis=0)

    for l, b in enumerate(HGRN_LEVELS):
        x = _each(lambda q, k, el: jnp.where((trow & b) != 0, q, k) * el, qs, ks, e_part(1 + l))
        yield
        att = _each(lambda xi, a: level_update(a, xi, l, b), x, att)
        yield
    o_intra = _each(lambda a, v: _dot(a.astype(BF16), v), att, vs)
    yield
    qc = _each(lambda q, e: (q * e).astype(BF16), qs, cum_e)
    kd = _each(lambda k, e: (k * e).astype(BF16), ks, suf_e)
    yield

    state = [st_ref[h] for h in range(HEADS)]
    for u, (c, p) in enumerate(units):
        for i in range(GROUP):
            h = p * GROUP + i
            hr = slice(i * CHUNK, (i + 1) * CHUNK)
            hc = slice(h * HEAD_DIM, (h + 1) * HEAD_DIM)
            o = _dot_nt(qc[u][hr], state[h].astype(BF16)) + o_intra[u][hr]
            decay = cum_e[u][i * CHUNK + CHUNK - 1:i * CHUNK + CHUNK, :]
            state[h] = decay * state[h] + _dot_tn(vs[u][hr], kd[u][hr])
            y = o * lax.rsqrt(jnp.mean(o * o, axis=-1, keepdims=True) + NORM_EPS) * onw
            o_ref[rows[c], hc] = (y * _silu(g_ref[rows[c], hc])).astype(o_ref.dtype)
        yield
    for h in range(HEADS):
        st_ref[h] = state[h]


GDN_QKV = 3 * MIX_HALF
HIST = 8
C_NONE, C_DIAG, C_B16, C_B32, C_B64 = 0, 1, 2, 3, 4
MASKED_EXPONENT = -1e30


def _gdn_codes():
    idx = np.arange(STACK)
    h, t = idx // CHUNK, idx % CHUNK
    same = h[:, None] == h[None, :]
    tt, ss = t[:, None], t[None, :]
    code = np.full((STACK, STACK), C_NONE, np.int32)
    low = same & (ss < tt)
    code[low] = C_B64
    code[low & (tt // 32 == ss // 32)] = C_B32
    code[low & (tt // 16 == ss // 16)] = C_B16
    code[same & (tt == ss)] = C_DIAG
    incl = np.tril(np.ones((CHUNK, CHUNK), np.float32))
    return code, incl


def _gdn_steps(qkv_ref, z_ref, ab_ref, alog_ref, dtb_ref, onw_ref, code_ref, incl_ref,
               o_ref, st_ref, *, n_chunks):
    onw = onw_ref[...]
    code = code_ref[...]
    incl_m = incl_ref[...]
    incl = code >= C_DIAG
    eye = (code == C_DIAG).astype(F32)

    rows = [slice(c * CHUNK, (c + 1) * CHUNK) for c in range(n_chunks)]
    units = [(c, p) for c in range(n_chunks) for p in range(N_STACKS)]

    qs = [_stack_heads(qkv_ref[rows[c], 0:MIX_HALF], p) for c, p in units]
    ks = [_stack_heads(qkv_ref[rows[c], MIX_HALF:2 * MIX_HALF], p) for c, p in units]
    vs = [_stack_heads(qkv_ref[rows[c], 2 * MIX_HALF:3 * MIX_HALF], p) for c, p in units]
    ab_blk = ab_ref[...]
    xa = ab_blk + dtb_ref[...]
    softplus = jnp.maximum(xa, 0.0) + jnp.log(1.0 + jnp.exp(-jnp.abs(xa)))
    g_all = -jnp.exp(alog_ref[...]) * softplus
    beta_all = _sigmoid(ab_blk)

    def head_cols(a, c, p, first_lane):
        return jnp.concatenate([a[c][:, first_lane + h:first_lane + h + 1]
                                for h in range(p * GROUP, (p + 1) * GROUP)], axis=0)

    yield
    g_cum = [_masked_sum(incl_m, g_all[r]) for r in rows]
    beta_chunk = [beta_all[r] for r in rows]
    beta_st = [head_cols(beta_chunk, c, p, HEADS) for c, p in units]
    gc = [jnp.broadcast_to(head_cols(g_cum, c, p, 0), (STACK, HEAD_DIM)) for c, p in units]
    yield
    dec = [jnp.exp(jnp.where(incl, g - jnp.transpose(g)[0:1, :], MASKED_EXPONENT)) for g in gc]
    yield
    kb = [k.astype(BF16) for k in ks]
    kq = _each(lambda k, q: _dot_nt(jnp.concatenate([k, q.astype(BF16)], axis=0), k), kb, qs)
    kk = [a[0:STACK] for a in kq]
    qk = [a[STACK:] for a in kq]
    yield
    am = _each(lambda b, k2, d: b * k2 * d, beta_st, kk, dec)
    a16 = [jnp.where(code == C_B16, a, 0.0) for a in am]
    n32 = [jnp.where(code == C_B32, a, 0.0) for a in am]
    n64 = [jnp.where(code == C_B64, a, 0.0) for a in am]
    yield
    apow = _each(_bdot, a16, a16)
    pinv = [eye - a for a in a16]
    yield
    for _ in range(2):
        both = _each(lambda ai, pi: _bdot(jnp.concatenate([ai, pi], axis=0), ai), apow, pinv)
        pinv = _each(lambda pi, b: pi + b[STACK:], pinv, both)
        apow = [b[0:STACK] for b in both]
        yield
    pinv = _each(lambda pi, ai: pi + _bdot(pi, ai), pinv, apow)
    yield
    for nlev in (n32, n64):
        t = _each(_bdot, nlev, pinv)
        yield
        pinv = _each(lambda pi, ti: pi - _bdot(pi, ti), pinv, t)
        yield
    egc = [jnp.exp(g) for g in gc]
    rhs = _each(lambda k, v, b, e: jnp.concatenate([k * (b * e), v * b], axis=1), ks, vs, beta_st, egc)
    wu = _each(_bdot, pinv, rhs)
    yield
    w_c = [a[:, 0:HEAD_DIM].astype(BF16) for a in wu]
    u_c = [a[:, HEAD_DIM:] for a in wu]
    qkm = _each(lambda a, d: (a * d).astype(BF16), qk, dec)
    qg = _each(lambda q, e: (q * e).astype(BF16), qs, egc)
    glast = [[g[i * CHUNK + CHUNK - 1:(i + 1) * CHUNK, :] for i in range(GROUP)] for g in gc]
    kdec = [[(k[i * CHUNK:(i + 1) * CHUNK] * jnp.exp(gl[i] - g[i * CHUNK:(i + 1) * CHUNK])).astype(BF16)
             for i in range(GROUP)] for k, g, gl in zip(ks, gc, glast)]
    gend = [[jnp.exp(gi) for gi in gl] for gl in glast]
    yield

    state = [st_ref[h] for h in range(HEADS)]
    for u, (c, p) in enumerate(units):
        heads = range(p * GROUP, (p + 1) * GROUP)
        hrs = [slice(i * CHUNK, (i + 1) * CHUNK) for i in range(GROUP)]
        ws = [_dot_nt(jnp.concatenate([w_c[u][hr], qg[u][hr]], axis=0), state[h].astype(BF16))
              for hr, h in zip(hrs, heads)]
        v_new = jnp.concatenate([u_c[u][hr] - a[0:CHUNK] for hr, a in zip(hrs, ws)], axis=0).astype(BF16)
        o_st = jnp.concatenate([a[CHUNK:] for a in ws], axis=0) + _dot(qkm[u], v_new)
        for i, h in enumerate(heads):
            hc = slice(h * HEAD_DIM, (h + 1) * HEAD_DIM)
            state[h] = gend[u][i] * state[h] + _dot_tn(v_new[hrs[i]], kdec[u][i])
            o = o_st[hrs[i]]
            y = o * lax.rsqrt(jnp.mean(o * o, axis=-1, keepdims=True) + NORM_EPS) * onw
            o_ref[rows[c], hc] = (y * _silu(z_ref[rows[c], hc])).astype(o_ref.dtype)
        yield
    for h in range(HEADS):
        st_ref[h] = state[h]


N_HGRN_REFS, N_GDN_REFS = 8, 8


def _mixers_kernel(*refs, n_chunks):
    hg_in = refs[0:N_HGRN_REFS]
    gd_in = refs[N_HGRN_REFS:N_HGRN_REFS + N_GDN_REFS]
    ohg_ref, ogd_ref, hg_st, gd_st = refs[N_HGRN_REFS + N_GDN_REFS:]

    @pl.when(pl.program_id(1) == 0)
    def _():
        hg_st[...] = jnp.zeros_like(hg_st)
        gd_st[...] = jnp.zeros_like(gd_st)

    bodies = [_hgrn_steps(*hg_in, ohg_ref, hg_st, n_chunks=n_chunks),
              _gdn_steps(*gd_in, ogd_ref, gd_st, n_chunks=n_chunks)]
    while bodies:
        for body in list(bodies):
            if next(body, StopIteration) is StopIteration:
                bodies.remove(body)


def _mixers(pa, pb, pc, lb_logits, hg_onorm_w, alog_row, dtb_row, gd_onorm_w, batch, seq, n_chunks):
    rows = n_chunks * CHUNK
    steps = seq // rows
    const = lambda b, s: (0, 0)
    blk = lambda width, j: pl.BlockSpec((rows, width), lambda b, s, j=j: (b * steps + s, j))
    marg = jnp.asarray(_triple(_hgrn_arg_matrix()), BF16)
    hg_code = jnp.asarray(_hgrn_level_codes())
    gd_code, incl = _gdn_codes()
    gd_code, incl = jnp.asarray(gd_code), jnp.asarray(_triple(incl), BF16)
    sq = pl.BlockSpec((STACK, STACK), const)
    row = pl.BlockSpec((1, LANES), const)
    state = pltpu.VMEM((HEADS, HEAD_DIM, HEAD_DIM), F32)
    return pl.pallas_call(
        functools.partial(_mixers_kernel, n_chunks=n_chunks),
        grid=(batch, steps),
        in_specs=[blk(MIX_HALF, 0), blk(MIX_HALF, 1), blk(MIX_HALF, 2), blk(MIX_HALF, 3),
                  pl.BlockSpec(lb_logits.shape, const), pl.BlockSpec((1, HEAD_DIM), const),
                  pl.BlockSpec(marg.shape, const), sq,
                  blk(GDN_QKV, 0), blk(MIX_HALF, GDN_QKV // MIX_HALF), blk(LANES, 0),
                  row, row, row, sq, pl.BlockSpec((CHUNK, 3 * CHUNK), const)],
        out_specs=[blk(MIX_HALF, 0), blk(MIX_HALF, 0)],
        out_shape=[jax.ShapeDtypeStruct((batch * seq, MIX_HALF), BF16)] * 2,
        scratch_shapes=[state, state],
        compiler_params=pltpu.CompilerParams(dimension_semantics=("arbitrary", "arbitrary"),
                                             vmem_limit_bytes=VMEM_LIMIT),
        name="mixers",
    )(pa, pa, pa, pa, lb_logits, hg_onorm_w, marg, hg_code, pb, pb, pc, alog_row, dtb_row, gd_onorm_w, gd_code, incl)


ROUTE_ROWS = 8


ROUTE_SPLIT = 8


def _route_steps(x_ref, ohg_ref, ogd_ref, wo_ref, nw_ref, wr_ref, ltri_ref, h_ref, hn_ref, ri_ref, rg_ref,
                 rows, counts):
    n = rows.stop - rows.start
    mix = _dot(ohg_ref[rows, :], wo_ref[0:MIX_HALF, :]) + _dot(ogd_ref[rows, :], wo_ref[MIX_HALF:2 * MIX_HALF, :])
    yield
    h = x_ref[rows, :] + mix
    h_ref[rows, :] = h
    hn = h * lax.rsqrt(jnp.mean(h * h, axis=-1, keepdims=True) + NORM_EPS) * nw_ref[...]
    hn_ref[rows, :] = _pack_words(hn)
    yield
    hn_hi = hn.astype(BF16)
    hn_lo = (hn - hn_hi.astype(F32)).astype(BF16)
    part = _dot(hn_hi, wr_ref[...])
    logits = (_dot(hn_lo, wr_ref[:, 0:LANES]) + part[:, LANES:]) + part[:, 0:LANES]
    yield
    lane = lax.broadcasted_iota(jnp.int32, (n, LANES), 1)
    neg = jnp.float32(-jnp.inf)
    big = jnp.int32(LANES)

    def first_max(vals):
        m = jnp.max(vals, axis=-1, keepdims=True)
        return m, jnp.min(jnp.where(vals == m, lane, big), axis=-1, keepdims=True)

    gl = jnp.where(lane < N_GROUPS, logits, neg)
    gmax, gidx = first_max(gl)
    yield
    p_group = 1.0 / jnp.sum(jnp.exp(gl - gmax), axis=-1, keepdims=True)
    lo = N_GROUPS + EXPERTS_PER_GROUP * gidx
    el = jnp.where((lane >= lo) & (lane < lo + EXPERTS_PER_GROUP), logits, neg)
    m1, i1 = first_max(el)
    yield
    m2, i2 = first_max(jnp.where(lane == i1, neg, el))
    yield
    r = jnp.exp(m2 - m1)
    gate1 = p_group / (1.0 + r)
    gate2 = p_group * r / (1.0 + r)
    hot1 = lane == i1
    hot2 = lane == i2
    onehot = jnp.where(hot1 | hot2, 1.0, 0.0)
    within = _dot(ltri_ref[...], onehot.astype(BF16))
    yield
    before = within + counts[0]
    counts[0] = counts[0] + jnp.sum(onehot, axis=0, keepdims=True)
    rank1 = jnp.sum(jnp.where(hot1, before, 0.0), axis=-1, keepdims=True)
    rank2 = jnp.sum(jnp.where(hot2, before, 0.0), axis=-1, keepdims=True)
    yield
    ri = jnp.where(lane == 0, i1 - N_GROUPS,
                   jnp.where(lane == 1, i2 - N_GROUPS,
                             jnp.where(lane == 2, rank1.astype(jnp.int32),
                                       jnp.where(lane == 3, rank2.astype(jnp.int32), 0))))
    ri_ref[:, rows] = jnp.transpose(ri)[0:ROUTE_ROWS, :]
    rg_ref[rows, :] = jnp.where(lane == 0, gate1, jnp.where(lane == 1, gate2, 0.0))


def _route_kernel(x_ref, ohg_ref, ogd_ref, wo_ref, nw_ref, wr_ref, ltri_ref,
                  h_ref, hn_ref, ri_ref, rg_ref, cnt_ref, cnt_scr):
    @pl.when(pl.program_id(0) == 0)
    def _():
        cnt_scr[...] = jnp.zeros_like(cnt_scr)

    tm = x_ref.shape[0]
    sub = tm // ROUTE_SPLIT
    counts = [cnt_scr[...]]
    bodies = [_route_steps(x_ref, ohg_ref, ogd_ref, wo_ref, nw_ref, wr_ref, ltri_ref, h_ref, hn_ref, ri_ref, rg_ref,
                           slice(s * sub, (s + 1) * sub), counts) for s in range(ROUTE_SPLIT)]
    while bodies:
        for body in list(bodies):
            if next(body, StopIteration) is StopIteration:
                bodies.remove(body)
    cnt_scr[...] = counts[0]
    cnt_ref[...] = counts[0]


def _route(x2, ohg, ogd, wo, norm_w, wr, tm):
    t = x2.shape[0]
    sub = tm // ROUTE_SPLIT
    ltri = jnp.asarray(np.tril(np.ones((sub, sub), np.float32), -1), BF16)
    const = lambda i: (0, 0)
    tile = lambda n: pl.BlockSpec((tm, n), lambda i: (i, 0))
    return pl.pallas_call(
        _route_kernel,
        grid=(t // tm,),
        in_specs=[tile(D_MODEL), tile(MIX_HALF), tile(MIX_HALF),
                  pl.BlockSpec((D_MODEL, D_MODEL), const),
                  pl.BlockSpec((1, D_MODEL), const),
                  pl.BlockSpec((D_MODEL, 2 * LANES), const),
                  pl.BlockSpec((sub, sub), const)],
        out_specs=[tile(D_MODEL), tile(HALF), pl.BlockSpec((ROUTE_ROWS, tm), lambda i: (0, i)), tile(LANES),
                   pl.BlockSpec((1, LANES), const)],
        out_shape=[jax.ShapeDtypeStruct((t, D_MODEL), F32),
                   jax.ShapeDtypeStruct((t, HALF), WORD),
                   jax.ShapeDtypeStruct((ROUTE_ROWS, t), jnp.int32),
                   jax.ShapeDtypeStruct((t, LANES), F32),
                   jax.ShapeDtypeStruct((1, LANES), F32)],
        scratch_shapes=[pltpu.VMEM((1, LANES), F32)],
        compiler_params=pltpu.CompilerParams(dimension_semantics=("arbitrary",), vmem_limit_bytes=VMEM_LIMIT),
        name="route",
    )(x2, ohg, ogd, wo, norm_w, wr, ltri)


HALF = D_MODEL // 2
WORD = jnp.uint32
HIGH_HALF = np.uint32(0xFFFF0000)


def _pack_words(x):
    bits = lambda a: lax.bitcast_convert_type(a.astype(BF16).astype(F32), WORD)
    return (bits(x[:, 0:HALF]) >> 16) | (bits(x[:, HALF:]) & HIGH_HALF)


def _unpack_words(words):
    return (lax.bitcast_convert_type(words << 16, F32), lax.bitcast_convert_type(words & HIGH_HALF, F32))


SC_WINDOW = 64


def _sc_permute_rows(rows, idx, gather):
    info = plsc.get_sparse_core_info()
    n_workers = info.num_cores * info.num_subcores
    n_idx, (n_src, width) = idx.shape[0], rows.shape
    copies = 1 if gather else n_idx // n_src
    per_worker = n_idx // copies // n_workers
    n_win = per_worker // SC_WINDOW
    assert per_worker * n_workers * copies == n_idx and n_win * SC_WINDOW == per_worker and n_win % 2 == 0
    idx2 = idx.reshape(n_idx // SC_WINDOW, SC_WINDOW)
    win_per_copy = n_idx // copies // SC_WINDOW

    @functools.partial(
        pl.kernel,
        mesh=plsc.VectorSubcoreMesh(core_axis_name="c", subcore_axis_name="s"),
        out_type=jax.ShapeDtypeStruct((n_idx, width), rows.dtype),
        scratch_types=[pltpu.VMEM((copies, n_win, SC_WINDOW), jnp.int32),
                       pltpu.VMEM((2, SC_WINDOW, width), rows.dtype),
                       pltpu.SemaphoreType.DMA((2,)), pltpu.SemaphoreType.DMA((2,))],
    )
    def permute(rows_hbm, idx_hbm, out_hbm, idx_v, buf, fill_sem, drain_sem):
        worker = lax.axis_index("s") * info.num_cores + lax.axis_index("c")
        first_row = worker * per_worker
        for c in range(copies):
            pltpu.sync_copy(idx_hbm.at[pl.ds(c * win_per_copy + worker * n_win, n_win)], idx_v.at[c])

        def window(ref, j):
            return ref.at[pl.ds(pl.multiple_of(first_row + j * SC_WINDOW, SC_WINDOW), SC_WINDOW)]

        def fill(j, b):
            src = rows_hbm.at[idx_v.at[0, j]] if gather else window(rows_hbm, j)
            return pltpu.make_async_copy(src, buf.at[b], fill_sem.at[b])

        def drains(j, b):
            dsts = [window(out_hbm, j)] if gather else [out_hbm.at[idx_v.at[c, j]] for c in range(copies)]
            return [pltpu.make_async_copy(buf.at[b], dst, drain_sem.at[b]) for dst in dsts]

        fill(0, 0).start()

        @pl.loop(0, n_win, step=2)
        def _(j0):
            for b in range(2):
                j = j0 + b
                fill(j, b).wait()

                @pl.when(j >= 1)
                def _():
                    for d in drains(j - 1, 1 - b):
                        d.wait()

                @pl.when(j + 1 < n_win)
                def _():
                    fill(j + 1, 1 - b).start()

                for d in drains(j, b):
                    d.start()

        for d in drains(n_win - 1, 1):
            d.wait()

    return permute(rows, idx2)


def _sc_gather_rows(table, idx):
    return _sc_permute_rows(table, idx, gather=True)


def _sc_scatter_rows(rows, idx):
    return _sc_permute_rows(rows, idx, gather=False)


def _expert_kernel(blk_ref, exp_ref, lo_ref, hi_ref, xs_ref, wg_ref, wu_ref, wd_ref, ys_ref, wgu_b, wd_b, acc):
    p = pl.program_id(0)

    @pl.when(jnp.logical_or(p == 0, exp_ref[p] != exp_ref[jnp.maximum(p - 1, 0)]))
    def _():
        wgu_b[:, 0:EXPERT_FF] = wg_ref[0].astype(BF16)
        wgu_b[:, EXPERT_FF:2 * EXPERT_FF] = wu_ref[0].astype(BF16)
        wd_b[...] = wd_ref[0].astype(BF16)

    @pl.when(p == 0)
    def _():
        acc[...] = jnp.zeros_like(acc)

    first = jnp.logical_or(p == 0, blk_ref[p] != blk_ref[jnp.maximum(p - 1, 0)])
    x_lo, x_hi = _unpack_words(xs_ref[...])
    ab = _dot(x_lo.astype(BF16), wgu_b[0:HALF, :]) + _dot(x_hi.astype(BF16), wgu_b[HALF:D_MODEL, :])
    hb = _silu(ab[:, 0:EXPERT_FF]) * ab[:, EXPERT_FF:2 * EXPERT_FF]
    y = _dot(hb.astype(BF16), wd_b[...])
    slot = blk_ref[p] * SLOT_BLOCK + lax.broadcasted_iota(jnp.int32, (SLOT_BLOCK, 1), 0)
    y = jnp.where((slot >= lo_ref[p]) & (slot < hi_ref[p]), y, 0.0)
    total = y + jnp.where(first, 0.0, acc[...])
    acc[...] = total
    ys_ref[...] = _pack_words(total)


def _experts(blk, exp, lo, hi, xs, wg, wu, wd):
    n_pairs = blk.shape[0]
    return pl.pallas_call(
        _expert_kernel,
        grid_spec=pltpu.PrefetchScalarGridSpec(
            num_scalar_prefetch=4,
            grid=(n_pairs,),
            in_specs=[pl.BlockSpec((SLOT_BLOCK, HALF), lambda p, blk, exp, lo, hi: (blk[p], 0)),
                      pl.BlockSpec((1, D_MODEL, EXPERT_FF), lambda p, blk, exp, lo, hi: (exp[p], 0, 0)),
                      pl.BlockSpec((1, D_MODEL, EXPERT_FF), lambda p, blk, exp, lo, hi: (exp[p], 0, 0)),
                      pl.BlockSpec((1, EXPERT_FF, D_MODEL), lambda p, blk, exp, lo, hi: (exp[p], 0, 0))],
            out_specs=pl.BlockSpec((SLOT_BLOCK, HALF), lambda p, blk, exp, lo, hi: (blk[p], 0)),
            scratch_shapes=[pltpu.VMEM((D_MODEL, 2 * EXPERT_FF), BF16), pltpu.VMEM((EXPERT_FF, D_MODEL), BF16),
                            pltpu.VMEM((SLOT_BLOCK, D_MODEL), F32)],
        ),
        out_shape=jax.ShapeDtypeStruct(xs.shape, WORD),
        compiler_params=pltpu.CompilerParams(dimension_semantics=("arbitrary",), vmem_limit_bytes=VMEM_LIMIT),
        name="experts",
    )(blk, exp, lo, hi, xs, wg, wu, wd)


COMBINE_PARTS = 2


def _combine_kernel(h_ref, rg_ref, fw_ref, y0_ref, y1_ref, *rest):
    o_ref = rest[-1]
    rg = rg_ref[...]
    y = [jnp.concatenate(_unpack_words(ref[...]), axis=1) for ref in (y0_ref, y1_ref)]
    h = h_ref[...] + rg[:, 0:1] * y[0] + rg[:, 1:2] * y[1]
    o_ref[...] = h * lax.rsqrt(jnp.mean(h * h, axis=-1, keepdims=True) + NORM_EPS) * fw_ref[...]


def _combine(h, rg, final_w, y2, tm, part, n_parts, out_so_far=None):
    t = h.shape[0]
    n_tiles = t // tm // n_parts
    first = part * n_tiles
    in_specs = [pl.BlockSpec((tm, D_MODEL), lambda i: (first + i, 0)),
                pl.BlockSpec((tm, LANES), lambda i: (first + i, 0)),
                pl.BlockSpec((1, D_MODEL), lambda i: (0, 0)),
                pl.BlockSpec((tm, HALF), lambda i: (i, 0)),
                pl.BlockSpec((tm, HALF), lambda i: (n_tiles + i, 0))]
    args = [h, rg, final_w, y2, y2]
    aliases = {}
    if out_so_far is not None:
        in_specs.append(pl.BlockSpec(memory_space=pl.ANY))
        aliases = {len(args): 0}
        args.append(out_so_far)
    return pl.pallas_call(
        _combine_kernel,
        grid=(n_tiles,),
        in_specs=in_specs,
        out_specs=pl.BlockSpec((tm, D_MODEL), lambda i: (first + i, 0)),
        out_shape=jax.ShapeDtypeStruct((t, D_MODEL), F32),
        input_output_aliases=aliases,
        compiler_params=pltpu.CompilerParams(dimension_semantics=("arbitrary",), vmem_limit_bytes=VMEM_LIMIT),
        name="combine",
    )(*args)


def _pair_schedule(counts, n_slots):
    n_blocks = n_slots // SLOT_BLOCK
    n_pairs = n_blocks + N_EXPERTS - 1
    ends = jnp.cumsum(counts)
    starts = ends - counts
    first = starts // SLOT_BLOCK
    last = jnp.maximum(ends - 1, starts) // SLOT_BLOCK
    per_expert = jnp.where(counts > 0, last - first + 1, 0)
    cum = jnp.cumsum(per_expert)
    p = jnp.arange(n_pairs, dtype=jnp.int32)
    e = jnp.minimum(jnp.sum(cum[None, :] <= p[:, None], axis=1), N_EXPERTS - 1).astype(jnp.int32)
    valid = p < cum[-1]
    onehot = e[:, None] == jnp.arange(N_EXPERTS, dtype=jnp.int32)[None, :]
    pick = lambda table: jnp.sum(jnp.where(onehot, table[None, :], 0), axis=1)
    blk = jnp.where(valid, pick(first) + p - pick(cum - per_expert), n_blocks - 1).astype(jnp.int32)
    lo = jnp.where(valid, pick(starts), 0).astype(jnp.int32)
    hi = jnp.where(valid, pick(ends), 0).astype(jnp.int32)
    return starts.astype(jnp.int32), blk, e, lo, hi


def _layer(x, norm_mix_w, w_in, lb_logits, hgrn_onorm_w, gdn_conv_w, gdn_a_log, gdn_dt_bias, gdn_onorm_w, w_out,
           norm_moe_w, router_group_w, router_expert_w, w_gate, w_up, w_down, final_w, *, tm, n_chunks):
    batch, seq, _ = x.shape
    t = batch * seq
    x2 = x.reshape(t, D_MODEL)
    n_hg = 4 * MIX_HALF
    n_gd = GDN_QKV + MIX_HALF
    wb16 = w_in.astype(BF16)
    wa = wb16[:, 0:n_hg]
    wb = wb16[:, n_hg:n_hg + n_gd]
    wc = jnp.pad(wb16[:, n_hg + n_gd:], ((0, 0), (0, LANES - 2 * HEADS)))
    pa, pb, pc = _inproj(x2, norm_mix_w.reshape(1, D_MODEL), wa, wb, wc, gdn_conv_w, 2 * tm, seq // (2 * tm))

    alog_row = jnp.pad(gdn_a_log.reshape(1, HEADS), ((0, 0), (0, LANES - HEADS)))
    dtb_row = jnp.pad(gdn_dt_bias.reshape(1, HEADS), ((0, 0), (0, LANES - HEADS)))
    o_hg, o_gd = _mixers(pa, pb, pc, lb_logits, hgrn_onorm_w.reshape(1, HEAD_DIM), alog_row, dtb_row,
                         gdn_onorm_w.reshape(1, HEAD_DIM), batch, seq, n_chunks)

    wr = jnp.pad(jnp.concatenate([router_group_w, router_expert_w], axis=1),
                 ((0, 0), (0, LANES - N_GROUPS - N_EXPERTS)))
    wr_hi = wr.astype(BF16)
    wr = jnp.concatenate([wr_hi, (wr - wr_hi.astype(F32)).astype(BF16)], axis=1)
    h, hn, ri, rg, cnt = _route(x2, o_hg, o_gd, w_out.astype(BF16), norm_moe_w.reshape(1, D_MODEL), wr, 4 * tm)

    counts = cnt[0, N_GROUPS:N_GROUPS + N_EXPERTS].astype(jnp.int32)
    starts, blk, exp, lo, hi = _pair_schedule(counts, t * TOP_K)
    onehot = ri[0:TOP_K, :, None] == jnp.arange(N_EXPERTS, dtype=jnp.int32)
    slot_kt = jnp.sum(jnp.where(onehot, starts, 0), axis=-1) + ri[TOP_K:2 * TOP_K]
    xs = _sc_scatter_rows(hn, slot_kt.reshape(TOP_K * t))
    ys = _experts(blk, exp, lo, hi, xs, w_gate, w_up, w_down)
    out = None
    for part in range(COMBINE_PARTS):
        tp = t // COMBINE_PARTS
        part_slots = slot_kt[:, part * tp:(part + 1) * tp].reshape(TOP_K * tp)
        y2 = _sc_gather_rows(ys, part_slots)
        out = _combine(h, rg, final_w.reshape(1, D_MODEL), y2, 2 * tm, part, COMBINE_PARTS, out)
    return out.reshape(batch, seq, D_MODEL)


def kernel(x, norm_mix_w, w_in, hgrn_lb_logits, hgrn_onorm_w, gdn_conv_w, gdn_a_log, gdn_dt_bias, gdn_onorm_w, w_out, norm_moe_w, router_group_w, router_expert_w, expert_w_gate, expert_w_up, expert_w_down, final_norm_w):
    return _layer(x, norm_mix_w[0], w_in[0], hgrn_lb_logits, hgrn_onorm_w[0], gdn_conv_w[0], gdn_a_log[0],
                  gdn_dt_bias[0], gdn_onorm_w[0], w_out[0], norm_moe_w[0], router_group_w[0], router_expert_w[0],
                  expert_w_gate[0], expert_w_up[0], expert_w_down[0], final_norm_w, tm=256, n_chunks=8)
```

```python
import functools

import numpy as np
import jax
import jax.numpy as jnp
from jax import lax
from jax.experimental import pallas as pl
from jax.experimental.pallas import tpu as pltpu
from jax.experimental.pallas import tpu_sc as plsc

F32 = jnp.float32
BF16 = jnp.bfloat16

D_MODEL = 1024
HEADS = 4
HEAD_DIM = 128
MIX_HALF = HEADS * HEAD_DIM
CHUNK = 64
GROUP = 2
N_STACKS = HEADS // GROUP
STACK = GROUP * CHUNK
CONV_K = 4
N_GROUPS = 8
EXPERTS_PER_GROUP = 8
N_EXPERTS = N_GROUPS * EXPERTS_PER_GROUP
TOP_K = 2
EXPERT_FF = 256
NORM_EPS = 1e-6
LANES = 128
F32_ROWS = 8
SLOT_BLOCK = 512
VMEM_LIMIT = 56 * 1024 * 1024


def _sigmoid(x):
    return 0.5 * jnp.tanh(0.5 * x) + 0.5


def _silu(x):
    return x * _sigmoid(x)


def _dot(a, b):
    return jnp.dot(a, b, preferred_element_type=F32)


def _dot_nt(a, b):
    return lax.dot_general(a, b, (((1,), (1,)), ((), ())), preferred_element_type=F32)


def _dot_tn(a, b):
    return lax.dot_general(a, b, (((0,), (0,)), ((), ())), preferred_element_type=F32)


def _bdot(a, b):
    return _dot(a.astype(BF16), b.astype(BF16))


def _masked_sum(mask3, x):
    hi = x.astype(BF16)
    r1 = x - hi.astype(F32)
    mid = r1.astype(BF16)
    lo = (r1 - mid.astype(F32)).astype(BF16)
    return _dot(mask3, jnp.concatenate([hi, mid, lo], axis=0))


def _triple(mask):
    return np.concatenate([mask, mask, mask], axis=1)


def _stack_heads(a, p):
    return jnp.concatenate([a[:, h * HEAD_DIM:(h + 1) * HEAD_DIM] for h in range(p * GROUP, (p + 1) * GROUP)],
                           axis=0)


def _each(f, *lists):
    return [f(*args) for args in zip(*lists)]


INPROJ_GROUP = 256


def _l2norm_heads(a):
    return jnp.concatenate(
        [a[:, h * HEAD_DIM:(h + 1) * HEAD_DIM]
         * lax.rsqrt(jnp.sum(jnp.square(a[:, h * HEAD_DIM:(h + 1) * HEAD_DIM]), axis=-1, keepdims=True) + 1e-6)
         for h in range(a.shape[1] // HEAD_DIM)], axis=1)


def _inproj_kernel(x_ref, nw_ref, wa_ref, wb_ref, wc_ref, cw_ref, oa_ref, ob_ref, oc_ref, ubuf, *, tiles_per_seq):
    tm = x_ref.shape[0]

    @pl.when(lax.rem(pl.program_id(0), tiles_per_seq) == 0)
    def _():
        ubuf[0:HIST, :] = jnp.zeros((HIST, GDN_QKV), F32)

    x = x_ref[...]
    ms = jnp.mean(x * x, axis=-1, keepdims=True)
    hn = (x * lax.rsqrt(ms + NORM_EPS) * nw_ref[...]).astype(BF16)
    cw = cw_ref[...]
    width = INPROJ_GROUP
    group = lambda g: slice(g * width, (g + 1) * width)

    def project(w_ref, o_ref, g, row0=0):
        o_ref[row0:row0 + tm, group(g)] = _dot(hn, w_ref[:, group(g)])

    def conv_group(g):
        cols = group(g)
        conv = cw[CONV_K - 1:CONV_K, cols] * ubuf[HIST:HIST + tm, cols]
        for j in range(1, CONV_K):
            conv = conv + cw[CONV_K - 1 - j:CONV_K - j, cols] * ubuf[HIST - j:HIST - j + tm, cols]
        ubuf[0:HIST, cols] = ubuf[tm:tm + HIST, cols]
        act = _silu(conv)
        if g * width < MIX_HALF:
            act = _l2norm_heads(act) * (HEAD_DIM ** -0.5)
        elif g * width < 2 * MIX_HALF:
            act = _l2norm_heads(act)
        ob_ref[:, cols] = act

    n_conv = GDN_QKV // width
    n_a = wa_ref.shape[1] // width
    others = [functools.partial(project, wa_ref, oa_ref, g) for g in range(n_a)]
    others += [functools.partial(project, wb_ref, ob_ref, g) for g in range(n_conv, wb_ref.shape[1] // width)]
    project(wb_ref, ubuf, 0, HIST)
    for g in range(n_conv):
        if g + 1 < n_conv:
            project(wb_ref, ubuf, g + 1, HIST)
        conv_group(g)
        others.pop(0)()
    for task in others:
        task()
    oc_ref[...] = _dot(hn, wc_ref[...])


def _inproj(x2, norm_w, wa, wb, wc, conv_w, tm, tiles_per_seq):
    t = x2.shape[0]
    na, nb, nc = wa.shape[1], wb.shape[1], wc.shape[1]
    const = lambda i: (0, 0)
    return pl.pallas_call(
        functools.partial(_inproj_kernel, tiles_per_seq=tiles_per_seq),
        grid=(t // tm,),
        in_specs=[
            pl.BlockSpec((tm, D_MODEL), lambda i: (i, 0)),
            pl.BlockSpec((1, D_MODEL), const),
            pl.BlockSpec((D_MODEL, na), const),
            pl.BlockSpec((D_MODEL, nb), const),
            pl.BlockSpec((D_MODEL, nc), const),
            pl.BlockSpec((CONV_K, GDN_QKV), const),
        ],
        out_specs=[
            pl.BlockSpec((tm, na), lambda i: (i, 0)),
            pl.BlockSpec((tm, nb), lambda i: (i, 0)),
            pl.BlockSpec((tm, nc), lambda i: (i, 0)),
        ],
        out_shape=[
            jax.ShapeDtypeStruct((t, na), F32),
            jax.ShapeDtypeStruct((t, nb), F32),
            jax.ShapeDtypeStruct((t, nc), F32),
        ],
        scratch_shapes=[pltpu.VMEM((HIST + tm, GDN_QKV), F32)],
        compiler_params=pltpu.CompilerParams(dimension_semantics=("arbitrary",), vmem_limit_bytes=VMEM_LIMIT),
        name="inproj",
    )(x2, norm_w, wa, wb, wc, conv_w)


HGRN_LEVELS = (32, 16, 8, 4, 2, 1)
VPU_ARG_LEVELS = tuple(b for b in HGRN_LEVELS if b >= F32_ROWS)
MXU_ARG_LEVELS = tuple(b for b in HGRN_LEVELS if b < F32_ROWS)
DIAG_CODE = len(HGRN_LEVELS)
NONE_CODE = DIAG_CODE + 1


def _hgrn_arg_matrix():
    t = np.arange(CHUNK)[:, None]
    u = np.arange(CHUNK)[None, :]
    mats = [u <= t]
    for b in MXU_ARG_LEVELS:
        odd = (t // b) % 2 == 1
        start = (t // b) * b
        mats.append(np.where(odd, (u > start) & (u <= t), (u > t) & (u <= start + b)))
    return np.concatenate(mats, axis=0).astype(np.float32)


def _hgrn_level_codes():
    idx = np.arange(STACK)
    h, t = idx // CHUNK, idx % CHUNK
    same = h[:, None] == h[None, :]
    tt, ss = t[:, None], t[None, :]
    code = np.full((STACK, STACK), NONE_CODE, np.int32)
    code[same & (tt == ss)] = DIAG_CODE
    for l, b in enumerate(HGRN_LEVELS):
        sib = (tt // (2 * b) == ss // (2 * b)) & ((tt // b) % 2 == 1) & ((ss // b) % 2 == 0)
        code[same & sib] = l
    return code


def _hgrn_steps(q_ref, f_ref, i_ref, g_ref, lbl_ref, onw_ref, marg_ref, code_ref, o_ref, st_ref, *, n_chunks):
    lbl = lbl_ref[...]
    lmax = jnp.max(lbl, axis=0, keepdims=True)
    lexp = jnp.exp(lbl - lmax)
    lb = lexp[0:1, :] / jnp.sum(lexp, axis=0, keepdims=True)
    onw = onw_ref[...]
    marg = marg_ref[...]
    code = code_ref[...]
    trow = lax.broadcasted_iota(jnp.int32, (STACK, HEAD_DIM), 0) & (CHUNK - 1)

    q_blk = _silu(q_ref[...]) * (HEAD_DIM ** -0.5)
    fg_blk = lb + (1.0 - lb) * _sigmoid(f_ref[...])
    k_blk = 1.0 - fg_blk
    lf_blk = jnp.log(fg_blk)
    v_blk = i_ref[...]
    yield

    rows = [slice(c * CHUNK, (c + 1) * CHUNK) for c in range(n_chunks)]
    args = [_masked_sum(marg, lf_blk[r]) for r in rows]
    yield
    trow_c = lax.broadcasted_iota(jnp.int32, (CHUNK, MIX_HALF), 0)

    def vpu_level_arg(cum, b):
        starts = cum.reshape(CHUNK // b, b, MIX_HALF)[:, 0:1, :]
        spread = lambda a: jnp.broadcast_to(a, (CHUNK // b, b, MIX_HALF)).reshape(CHUNK, MIX_HALF)
        own = spread(starts)
        nxt = spread(jnp.concatenate([starts[1:], starts[-1:]], axis=0))
        return jnp.where((trow_c & b) != 0, cum - own, nxt - cum)

    args = [jnp.concatenate([a[0:CHUNK]] + [vpu_level_arg(a[0:CHUNK], b) for b in VPU_ARG_LEVELS] + [a[CHUNK:]], axis=0)
            for a in args]
    yield
    e_chunk = [jnp.exp(a) for a in args]
    suf_chunk = [jnp.exp(a[CHUNK - 1:CHUNK] - a[0:CHUNK]) for a in args]
    yield
    units = [(c, p) for c in range(n_chunks) for p in range(N_STACKS)]
    qs = [_stack_heads(q_blk[rows[c]], p) for c, p in units]
    ks = [_stack_heads(k_blk[rows[c]], p) for c, p in units]
    vs = [_stack_heads(v_blk[rows[c]], p).astype(BF16) for c, p in units]
    e_part = lambda n: [_stack_heads(e_chunk[c][n * CHUNK:(n + 1) * CHUNK], p) for c, p in units]
    cum_e = e_part(0)
    suf_e = [_stack_heads(suf_chunk[c], p) for c, p in units]
    att = _each(lambda q, k: jnp.where(code == DIAG_CODE, jnp.sum(q * k, axis=-1, keepdims=True), 0.0), qs, ks)

    def level_update(a, x, l, b):
        xb = x.astype(BF16)
        if b < F32_ROWS:
            return jnp.where(code == l, _dot_nt(xb, xb), a)
        blocks = [slice(s0, s0 + b) for s0 in range(0, STACK, b)]
        r = _dot_nt(jnp.concatenate([x[bl] for bl in blocks[1::2]], axis=0).astype(BF16), xb)
        return jnp.concatenate(
            [a[bl] if n % 2 == 0 else jnp.where(code[bl] == l, r[(n // 2) * b:(n // 2 + 1) * b], a[bl])
             for n, bl in enumerate(blocks)], axis=0)

    for l, b in enumerate(HGRN_LEVELS):
        x = _each(lambda q, k, el: jnp.where((trow & b) != 0, q, k) * el, qs, ks, e_part(1 + l))
        yield
        att = _each(lambda xi, a: level_update(a, xi, l, b), x, att)
        yield
    o_intra = _each(lambda a, v: _dot(a.astype(BF16), v), att, vs)
    yield
    qc = _each(lambda q, e: (q * e).astype(BF16), qs, cum_e)
    kd = _each(lambda k, e: (k * e).astype(BF16), ks, suf_e)
    yield

    state = [st_ref[h] for h in range(HEADS)]
    for u, (c, p) in enumerate(units):
        for i in range(GROUP):
            h = p * GROUP + i
            hr = slice(i * CHUNK, (i + 1) * CHUNK)
            hc = slice(h * HEAD_DIM, (h + 1) * HEAD_DIM)
            o = _dot_nt(qc[u][hr], state[h].astype(BF16)) + o_intra[u][hr]
            decay = cum_e[u][i * CHUNK + CHUNK - 1:i * CHUNK + CHUNK, :]
            state[h] = decay * state[h] + _dot_tn(vs[u][hr], kd[u][hr])
            y = o * lax.rsqrt(jnp.mean(o * o, axis=-1, keepdims=True) + NORM_EPS) * onw
            o_ref[rows[c], hc] = (y * _silu(g_ref[rows[c], hc])).astype(o_ref.dtype)
        yield
    for h in range(HEADS):
        st_ref[h] = state[h]


GDN_QKV = 3 * MIX_HALF
HIST = 8
C_NONE, C_DIAG, C_B16, C_B32, C_B64 = 0, 1, 2, 3, 4
MASKED_EXPONENT = -1e30


def _gdn_codes():
    idx = np.arange(STACK)
    h, t = idx // CHUNK, idx % CHUNK
    same = h[:, None] == h[None, :]
    tt, ss = t[:, None], t[None, :]
    code = np.full((STACK, STACK), C_NONE, np.int32)
    low = same & (ss < tt)
    code[low] = C_B64
    code[low & (tt // 32 == ss // 32)] = C_B32
    code[low & (tt // 16 == ss // 16)] = C_B16
    code[same & (tt == ss)] = C_DIAG
    incl = np.tril(np.ones((CHUNK, CHUNK), np.float32))
    return code, incl


def _gdn_steps(qkv_ref, z_ref, ab_ref, alog_ref, dtb_ref, onw_ref, code_ref, incl_ref,
               o_ref, st_ref, *, n_chunks):
    onw = onw_ref[...]
    code = code_ref[...]
    incl_m = incl_ref[...]
    incl = code >= C_DIAG
    eye = (code == C_DIAG).astype(F32)

    rows = [slice(c * CHUNK, (c + 1) * CHUNK) for c in range(n_chunks)]
    units = [(c, p) for c in range(n_chunks) for p in range(N_STACKS)]

    qs = [_stack_heads(qkv_ref[rows[c], 0:MIX_HALF], p) for c, p in units]
    ks = [_stack_heads(qkv_ref[rows[c], MIX_HALF:2 * MIX_HALF], p) for c, p in units]
    vs = [_stack_heads(qkv_ref[rows[c], 2 * MIX_HALF:3 * MIX_HALF], p) for c, p in units]
    ab_blk = ab_ref[...]
    xa = ab_blk + dtb_ref[...]
    softplus = jnp.maximum(xa, 0.0) + jnp.log(1.0 + jnp.exp(-jnp.abs(xa)))
    g_all = -jnp.exp(alog_ref[...]) * softplus
    beta_all = _sigmoid(ab_blk)

    def head_cols(a, c, p, first_lane):
        return jnp.concatenate([a[c][:, first_lane + h:first_lane + h + 1]
                                for h in range(p * GROUP, (p + 1) * GROUP)], axis=0)

    yield
    g_cum = [_masked_sum(incl_m, g_all[r]) for r in rows]
    beta_chunk = [beta_all[r] for r in rows]
    beta_st = [head_cols(beta_chunk, c, p, HEADS) for c, p in units]
    gc = [jnp.broadcast_to(head_cols(g_cum, c, p, 0), (STACK, HEAD_DIM)) for c, p in units]
    yield
    dec = [jnp.exp(jnp.where(incl, g - jnp.transpose(g)[0:1, :], MASKED_EXPONENT)) for g in gc]
    yield
    kb = [k.astype(BF16) for k in ks]
    kq = _each(lambda k, q: _dot_nt(jnp.concatenate([k, q.astype(BF16)], axis=0), k), kb, qs)
    kk = [a[0:STACK] for a in kq]
    qk = [a[STACK:] for a in kq]
    yield
    am = _each(lambda b, k2, d: b * k2 * d, beta_st, kk, dec)
    a16 = [jnp.where(code == C_B16, a, 0.0) for a in am]
    n32 = [jnp.where(code == C_B32, a, 0.0) for a in am]
    n64 = [jnp.where(code == C_B64, a, 0.0) for a in am]
    yield
    apow = _each(_bdot, a16, a16)
    pinv = [eye - a for a in a16]
    yield
    for _ in range(2):
        both = _each(lambda ai, pi: _bdot(jnp.concatenate([ai, pi], axis=0), ai), apow, pinv)
        pinv = _each(lambda pi, b: pi + b[STACK:], pinv, both)
        apow = [b[0:STACK] for b in both]
        yield
    pinv = _each(lambda pi, ai: pi + _bdot(pi, ai), pinv, apow)
    yield
    for nlev in (n32, n64):
        t = _each(_bdot, nlev, pinv)
        yield
        pinv = _each(lambda pi, ti: pi - _bdot(pi, ti), pinv, t)
        yield
    egc = [jnp.exp(g) for g in gc]
    rhs = _each(lambda k, v, b, e: jnp.concatenate([k * (b * e), v * b], axis=1), ks, vs, beta_st, egc)
    wu = _each(_bdot, pinv, rhs)
    yield
    w_c = [a[:, 0:HEAD_DIM].astype(BF16) for a in wu]
    u_c = [a[:, HEAD_DIM:] for a in wu]
    qkm = _each(lambda a, d: (a * d).astype(BF16), qk, dec)
    qg = _each(lambda q, e: (q * e).astype(BF16), qs, egc)
    glast = [[g[i * CHUNK + CHUNK - 1:(i + 1) * CHUNK, :] for i in range(GROUP)] for g in gc]
    kdec = [[(k[i * CHUNK:(i + 1) * CHUNK] * jnp.exp(gl[i] - g[i * CHUNK:(i + 1) * CHUNK])).astype(BF16)
             for i in range(GROUP)] for k, g, gl in zip(ks, gc, glast)]
    gend = [[jnp.exp(gi) for gi in gl] for gl in glast]
    yield

    state = [st_ref[h] for h in range(HEADS)]
    for u, (c, p) in enumerate(units):
        heads = range(p * GROUP, (p + 1) * GROUP)
        hrs = [slice(i * CHUNK, (i + 1) * CHUNK) for i in range(GROUP)]
        ws = [_dot_nt(jnp.concatenate([w_c[u][hr], qg[u][hr]], axis=0), state[h].astype(BF16))
              for hr, h in zip(hrs, heads)]
        v_new = jnp.concatenate([u_c[u][hr] - a[0:CHUNK] for hr, a in zip(hrs, ws)], axis=0).astype(BF16)
        o_st = jnp.concatenate([a[CHUNK:] for a in ws], axis=0) + _dot(qkm[u], v_new)
        for i, h in enumerate(heads):
            hc = slice(h * HEAD_DIM, (h + 1) * HEAD_DIM)
            state[h] = gend[u][i] * state[h] + _dot_tn(v_new[hrs[i]], kdec[u][i])
            o = o_st[hrs[i]]
            y = o * lax.rsqrt(jnp.mean(o * o, axis=-1, keepdims=True) + NORM_EPS) * onw
            o_ref[rows[c], hc] = (y * _silu(z_ref[rows[c], hc])).astype(o_ref.dtype)
        yield
    for h in range(HEADS):
        st_ref[h] = state[h]


N_HGRN_REFS, N_GDN_REFS = 8, 8


def _mixers_kernel(*refs, n_chunks):
    hg_in = refs[0:N_HGRN_REFS]
    gd_in = refs[N_HGRN_REFS:N_HGRN_REFS + N_GDN_REFS]
    ohg_ref, ogd_ref, hg_st, gd_st = refs[N_HGRN_REFS + N_GDN_REFS:]

    @pl.when(pl.program_id(1) == 0)
    def _():
        hg_st[...] = jnp.zeros_like(hg_st)
        gd_st[...] = jnp.zeros_like(gd_st)

    bodies = [_hgrn_steps(*hg_in, ohg_ref, hg_st, n_chunks=n_chunks),
              _gdn_steps(*gd_in, ogd_ref, gd_st, n_chunks=n_chunks)]
    while bodies:
        for body in list(bodies):
            if next(body, StopIteration) is StopIteration:
                bodies.remove(body)


def _mixers(pa, pb, pc, lb_logits, hg_onorm_w, alog_row, dtb_row, gd_onorm_w, batch, seq, n_chunks):
    rows = n_chunks * CHUNK
    steps = seq // rows
    const = lambda b, s: (0, 0)
    blk = lambda width, j: pl.BlockSpec((rows, width), lambda b, s, j=j: (b * steps + s, j))
    marg = jnp.asarray(_triple(_hgrn_arg_matrix()), BF16)
    hg_code = jnp.asarray(_hgrn_level_codes())
    gd_code, incl = _gdn_codes()
    gd_code, incl = jnp.asarray(gd_code), jnp.asarray(_triple(incl), BF16)
    sq = pl.BlockSpec((STACK, STACK), const)
    row = pl.BlockSpec((1, LANES), const)
    state = pltpu.VMEM((HEADS, HEAD_DIM, HEAD_DIM), F32)
    return pl.pallas_call(
        functools.partial(_mixers_kernel, n_chunks=n_chunks),
        grid=(batch, steps),
        in_specs=[blk(MIX_HALF, 0), blk(MIX_HALF, 1), blk(MIX_HALF, 2), blk(MIX_HALF, 3),
                  pl.BlockSpec(lb_logits.shape, const), pl.BlockSpec((1, HEAD_DIM), const),
                  pl.BlockSpec(marg.shape, const), sq,
                  blk(GDN_QKV, 0), blk(MIX_HALF, GDN_QKV // MIX_HALF), blk(LANES, 0),
                  row, row, row, sq, pl.BlockSpec((CHUNK, 3 * CHUNK), const)],
        out_specs=[blk(MIX_HALF, 0), blk(MIX_HALF, 0)],
        out_shape=[jax.ShapeDtypeStruct((batch * seq, MIX_HALF), BF16)] * 2,
        scratch_shapes=[state, state],
        compiler_params=pltpu.CompilerParams(dimension_semantics=("arbitrary", "arbitrary"),
                                             vmem_limit_bytes=VMEM_LIMIT),
        name="mixers",
    )(pa, pa, pa, pa, lb_logits, hg_onorm_w, marg, hg_code, pb, pb, pc, alog_row, dtb_row, gd_onorm_w, gd_code, incl)


ROUTE_ROWS = 8


ROUTE_SPLIT = 8


def _route_steps(x_ref, ohg_ref, ogd_ref, wo_ref, nw_ref, wr_ref, ltri_ref, h_ref, hn_ref, ri_ref, rg_ref,
                 rows, counts):
    n = rows.stop - rows.start
    mix = _dot(ohg_ref[rows, :], wo_ref[0:MIX_HALF, :]) + _dot(ogd_ref[rows, :], wo_ref[MIX_HALF:2 * MIX_HALF, :])
    yield
    h = x_ref[rows, :] + mix
    h_ref[rows, :] = h
    hn = h * lax.rsqrt(jnp.mean(h * h, axis=-1, keepdims=True) + NORM_EPS) * nw_ref[...]
    hn_ref[rows, :] = _pack_words(hn)
    yield
    hn_hi = hn.astype(BF16)
    hn_lo = (hn - hn_hi.astype(F32)).astype(BF16)
    part = _dot(hn_hi, wr_ref[...])
    logits = (_dot(hn_lo, wr_ref[:, 0:LANES]) + part[:, LANES:]) + part[:, 0:LANES]
    yield
    lane = lax.broadcasted_iota(jnp.int32, (n, LANES), 1)
    neg = jnp.float32(-jnp.inf)
    big = jnp.int32(LANES)

    def first_max(vals):
        m = jnp.max(vals, axis=-1, keepdims=True)
        return m, jnp.min(jnp.where(vals == m, lane, big), axis=-1, keepdims=True)

    gl = jnp.where(lane < N_GROUPS, logits, neg)
    gmax, gidx = first_max(gl)
    yield
    p_group = 1.0 / jnp.sum(jnp.exp(gl - gmax), axis=-1, keepdims=True)
    lo = N_GROUPS + EXPERTS_PER_GROUP * gidx
    el = jnp.where((lane >= lo) & (lane < lo + EXPERTS_PER_GROUP), logits, neg)
    m1, i1 = first_max(el)
    yield
    m2, i2 = first_max(jnp.where(lane == i1, neg, el))
    yield
    r = jnp.exp(m2 - m1)
    gate1 = p_group / (1.0 + r)
    gate2 = p_group * r / (1.0 + r)
    hot1 = lane == i1
    hot2 = lane == i2
    onehot = jnp.where(hot1 | hot2, 1.0, 0.0)
    within = _dot(ltri_ref[...], onehot.astype(BF16))
    yield
    before = within + counts[0]
    counts[0] = counts[0] + jnp.sum(onehot, axis=0, keepdims=True)
    rank1 = jnp.sum(jnp.where(hot1, before, 0.0), axis=-1, keepdims=True)
    rank2 = jnp.sum(jnp.where(hot2, before, 0.0), axis=-1, keepdims=True)
    yield
    ri = jnp.where(lane == 0, i1 - N_GROUPS,
                   jnp.where(lane == 1, i2 - N_GROUPS,
                             jnp.where(lane == 2, rank1.astype(jnp.int32),
                                       jnp.where(lane == 3, rank2.astype(jnp.int32), 0))))
    ri_ref[:, rows] = jnp.transpose(ri)[0:ROUTE_ROWS, :]
    rg_ref[rows, :] = jnp.where(lane == 0, gate1, jnp.where(lane == 1, gate2, 0.0))


def _route_kernel(x_ref, ohg_ref, ogd_ref, wo_ref, nw_ref, wr_ref, ltri_ref,
                  h_ref, hn_ref, ri_ref, rg_ref, cnt_ref, cnt_scr):
    @pl.when(pl.program_id(0) == 0)
    def _():
        cnt_scr[...] = jnp.zeros_like(cnt_scr)

    tm = x_ref.shape[0]
    sub = tm // ROUTE_SPLIT
    counts = [cnt_scr[...]]
    bodies = [_route_steps(x_ref, ohg_ref, ogd_ref, wo_ref, nw_ref, wr_ref, ltri_ref, h_ref, hn_ref, ri_ref, rg_ref,
                           slice(s * sub, (s + 1) * sub), counts) for s in range(ROUTE_SPLIT)]
    while bodies:
        for body in list(bodies):
            if next(body, StopIteration) is StopIteration:
                bodies.remove(body)
    cnt_scr[...] = counts[0]
    cnt_ref[...] = counts[0]


def _route(x2, ohg, ogd, wo, norm_w, wr, tm):
    t = x2.shape[0]
    sub = tm // ROUTE_SPLIT
    ltri = jnp.asarray(np.tril(np.ones((sub, sub), np.float32), -1), BF16)
    const = lambda i: (0, 0)
    tile = lambda n: pl.BlockSpec((tm, n), lambda i: (i, 0))
    return pl.pallas_call(
        _route_kernel,
        grid=(t // tm,),
        in_specs=[tile(D_MODEL), tile(MIX_HALF), tile(MIX_HALF),
                  pl.BlockSpec((D_MODEL, D_MODEL), const),
                  pl.BlockSpec((1, D_MODEL), const),
                  pl.BlockSpec((D_MODEL, 2 * LANES), const),
                  pl.BlockSpec((sub, sub), const)],
        out_specs=[tile(D_MODEL), tile(HALF), pl.BlockSpec((ROUTE_ROWS, tm), lambda i: (0, i)), tile(LANES),
                   pl.BlockSpec((1, LANES), const)],
        out_shape=[jax.ShapeDtypeStruct((t, D_MODEL), F32),
                   jax.ShapeDtypeStruct((t, HALF), WORD),
                   jax.ShapeDtypeStruct((ROUTE_ROWS, t), jnp.int32),
                   jax.ShapeDtypeStruct((t, LANES), F32),
                   jax.ShapeDtypeStruct((1, LANES), F32)],
        scratch_shapes=[pltpu.VMEM((1, LANES), F32)],
        compiler_params=pltpu.CompilerParams(dimension_semantics=("arbitrary",), vmem_limit_bytes=VMEM_LIMIT),
        name="route",
    )(x2, ohg, ogd, wo, norm_w, wr, ltri)


HALF = D_MODEL // 2
WORD = jnp.uint32
HIGH_HALF = np.uint32(0xFFFF0000)


def _pack_words(x):
    bits = lambda a: lax.bitcast_convert_type(a.astype(BF16).astype(F32), WORD)
    return (bits(x[:, 0:HALF]) >> 16) | (bits(x[:, HALF:]) & HIGH_HALF)


def _unpack_words(words):
    return (lax.bitcast_convert_type(words << 16, F32), lax.bitcast_convert_type(words & HIGH_HALF, F32))


SC_WINDOW = 64


def _sc_permute_rows(rows, idx, gather):
    info = plsc.get_sparse_core_info()
    n_workers = info.num_cores * info.num_subcores
    n_idx, (n_src, width) = idx.shape[0], rows.shape
    copies = 1 if gather else n_idx // n_src
    per_worker = n_idx // copies // n_workers
    n_win = per_worker // SC_WINDOW
    assert per_worker * n_workers * copies == n_idx and n_win * SC_WINDOW == per_worker and n_win % 2 == 0
    idx2 = idx.reshape(n_idx // SC_WINDOW, SC_WINDOW)
    win_per_copy = n_idx // copies // SC_WINDOW

    @functools.partial(
        pl.kernel,
        mesh=plsc.VectorSubcoreMesh(core_axis_name="c", subcore_axis_name="s"),
        out_type=jax.ShapeDtypeStruct((n_idx, width), rows.dtype),
        scratch_types=[pltpu.VMEM((copies, n_win, SC_WINDOW), jnp.int32),
                       pltpu.VMEM((2, SC_WINDOW, width), rows.dtype),
                       pltpu.SemaphoreType.DMA((2,)), pltpu.SemaphoreType.DMA((2,))],
    )
    def permute(rows_hbm, idx_hbm, out_hbm, idx_v, buf, fill_sem, drain_sem):
        worker = lax.axis_index("s") * info.num_cores + lax.axis_index("c")
        first_row = worker * per_worker
        for c in range(copies):
            pltpu.sync_copy(idx_hbm.at[pl.ds(c * win_per_copy + worker * n_win, n_win)], idx_v.at[c])

        def window(ref, j):
            return ref.at[pl.ds(pl.multiple_of(first_row + j * SC_WINDOW, SC_WINDOW), SC_WINDOW)]

        def fill(j, b):
            src = rows_hbm.at[idx_v.at[0, j]] if gather else window(rows_hbm, j)
            return pltpu.make_async_copy(src, buf.at[b], fill_sem.at[b])

        def drains(j, b):
            dsts = [window(out_hbm, j)] if gather else [out_hbm.at[idx_v.at[c, j]] for c in range(copies)]
            return [pltpu.make_async_copy(buf.at[b], dst, drain_sem.at[b]) for dst in dsts]

        fill(0, 0).start()

        @pl.loop(0, n_win, step=2)
        def _(j0):
            for b in range(2):
                j = j0 + b
                fill(j, b).wait()

                @pl.when(j >= 1)
                def _():
                    for d in drains(j - 1, 1 - b):
                        d.wait()

                @pl.when(j + 1 < n_win)
                def _():
                    fill(j + 1, 1 - b).start()

                for d in drains(j, b):
                    d.start()

        for d in drains(n_win - 1, 1):
            d.wait()

    return permute(rows, idx2)


def _sc_gather_rows(table, idx):
    return _sc_permute_rows(table, idx, gather=True)


def _sc_scatter_rows(rows, idx):
    return _sc_permute_rows(rows, idx, gather=False)


def _expert_kernel(blk_ref, exp_ref, lo_ref, hi_ref, xs_ref, wg_ref, wu_ref, wd_ref, ys_ref, wgu_b, wd_b, acc):
    p = pl.program_id(0)

    @pl.when(jnp.logical_or(p == 0, exp_ref[p] != exp_ref[jnp.maximum(p - 1, 0)]))
    def _():
        wgu_b[:, 0:EXPERT_FF] = wg_ref[0].astype(BF16)
        wgu_b[:, EXPERT_FF:2 * EXPERT_FF] = wu_ref[0].astype(BF16)
        wd_b[...] = wd_ref[0].astype(BF16)

    @pl.when(p == 0)
    def _():
        acc[...] = jnp.zeros_like(acc)

    first = jnp.logical_or(p == 0, blk_ref[p] != blk_ref[jnp.maximum(p - 1, 0)])
    x_lo, x_hi = _unpack_words(xs_ref[...])
    ab = _dot(x_lo.astype(BF16), wgu_b[0:HALF, :]) + _dot(x_hi.astype(BF16), wgu_b[HALF:D_MODEL, :])
    hb = _silu(ab[:, 0:EXPERT_FF]) * ab[:, EXPERT_FF:2 * EXPERT_FF]
    y = _dot(hb.astype(BF16), wd_b[...])
    slot = blk_ref[p] * SLOT_BLOCK + lax.broadcasted_iota(jnp.int32, (SLOT_BLOCK, 1), 0)
    y = jnp.where((slot >= lo_ref[p]) & (slot < hi_ref[p]), y, 0.0)
    total = y + jnp.where(first, 0.0, acc[...])
    acc[...] = total
    ys_ref[...] = _pack_words(total)


def _experts(blk, exp, lo, hi, xs, wg, wu, wd):
    n_pairs = blk.shape[0]
    return pl.pallas_call(
        _expert_kernel,
        grid_spec=pltpu.PrefetchScalarGridSpec(
            num_scalar_prefetch=4,
            grid=(n_pairs,),
            in_specs=[pl.BlockSpec((SLOT_BLOCK, HALF), lambda p, blk, exp, lo, hi: (blk[p], 0)),
                      pl.BlockSpec((1, D_MODEL, EXPERT_FF), lambda p, blk, exp, lo, hi: (exp[p], 0, 0)),
                      pl.BlockSpec((1, D_MODEL, EXPERT_FF), lambda p, blk, exp, lo, hi: (exp[p], 0, 0)),
                      pl.BlockSpec((1, EXPERT_FF, D_MODEL), lambda p, blk, exp, lo, hi: (exp[p], 0, 0))],
            out_specs=pl.BlockSpec((SLOT_BLOCK, HALF), lambda p, blk, exp, lo, hi: (blk[p], 0)),
            scratch_shapes=[pltpu.VMEM((D_MODEL, 2 * EXPERT_FF), BF16), pltpu.VMEM((EXPERT_FF, D_MODEL), BF16),
                            pltpu.VMEM((SLOT_BLOCK, D_MODEL), F32)],
        ),
        out_shape=jax.ShapeDtypeStruct(xs.shape, WORD),
        compiler_params=pltpu.CompilerParams(dimension_semantics=("arbitrary",), vmem_limit_bytes=VMEM_LIMIT),
        name="experts",
    )(blk, exp, lo, hi, xs, wg, wu, wd)


COMBINE_PARTS = 2


def _combine_kernel(h_ref, rg_ref, fw_ref, y0_ref, y1_ref, *rest):
    o_ref = rest[-1]
    rg = rg_ref[...]
    y = [jnp.concatenate(_unpack_words(ref[...]), axis=1) for ref in (y0_ref, y1_ref)]
    h = h_ref[...] + rg[:, 0:1] * y[0] + rg[:, 1:2] * y[1]
    o_ref[...] = h * lax.rsqrt(jnp.mean(h * h, axis=-1, keepdims=True) + NORM_EPS) * fw_ref[...]


def _combine(h, rg, final_w, y2, tm, part, n_parts, out_so_far=None):
    t = h.shape[0]
    n_tiles = t // tm // n_parts
    first = part * n_tiles
    in_specs = [pl.BlockSpec((tm, D_MODEL), lambda i: (first + i, 0)),
                pl.BlockSpec((tm, LANES), lambda i: (first + i, 0)),
                pl.BlockSpec((1, D_MODEL), lambda i: (0, 0)),
                pl.BlockSpec((tm, HALF), lambda i: (i, 0)),
                pl.BlockSpec((tm, HALF), lambda i: (n_tiles + i, 0))]
    args = [h, rg, final_w, y2, y2]
    aliases = {}
    if out_so_far is not None:
        in_specs.append(pl.BlockSpec(memory_space=pl.ANY))
        aliases = {len(args): 0}
        args.append(out_so_far)
    return pl.pallas_call(
        _combine_kernel,
        grid=(n_tiles,),
        in_specs=in_specs,
        out_specs=pl.BlockSpec((tm, D_MODEL), lambda i: (first + i, 0)),
        out_shape=jax.ShapeDtypeStruct((t, D_MODEL), F32),
        input_output_aliases=aliases,
        compiler_params=pltpu.CompilerParams(dimension_semantics=("arbitrary",), vmem_limit_bytes=VMEM_LIMIT),
        name="combine",
    )(*args)


def _pair_schedule(counts, n_slots):
    n_blocks = n_slots // SLOT_BLOCK
    n_pairs = n_blocks + N_EXPERTS - 1
    ends = jnp.cumsum(counts)
    starts = ends - counts
    first = starts // SLOT_BLOCK
    last = jnp.maximum(ends - 1, starts) // SLOT_BLOCK
    per_expert = jnp.where(counts > 0, last - first + 1, 0)
    cum = jnp.cumsum(per_expert)
    p = jnp.arange(n_pairs, dtype=jnp.int32)
    e = jnp.minimum(jnp.sum(cum[None, :] <= p[:, None], axis=1), N_EXPERTS - 1).astype(jnp.int32)
    valid = p < cum[-1]
    onehot = e[:, None] == jnp.arange(N_EXPERTS, dtype=jnp.int32)[None, :]
    pick = lambda table: jnp.sum(jnp.where(onehot, table[None, :], 0), axis=1)
    blk = jnp.where(valid, pick(first) + p - pick(cum - per_expert), n_blocks - 1).astype(jnp.int32)
    lo = jnp.where(valid, pick(starts), 0).astype(jnp.int32)
    hi = jnp.where(valid, pick(ends), 0).astype(jnp.int32)
    return starts.astype(jnp.int32), blk, e, lo, hi


def _layer(x, norm_mix_w, w_in, lb_logits, hgrn_onorm_w, gdn_conv_w, gdn_a_log, gdn_dt_bias, gdn_onorm_w, w_out,
           norm_moe_w, router_group_w, router_expert_w, w_gate, w_up, w_down, final_w, *, tm, n_chunks):
    batch, seq, _ = x.shape
    t = batch * seq
    x2 = x.reshape(t, D_MODEL)
    n_hg = 4 * MIX_HALF
    n_gd = GDN_QKV + MIX_HALF
    wb16 = w_in.astype(BF16)
    wa = wb16[:, 0:n_hg]
    wb = wb16[:, n_hg:n_hg + n_gd]
    wc = jnp.pad(wb16[:, n_hg + n_gd:], ((0, 0), (0, LANES - 2 * HEADS)))
    pa, pb, pc = _inproj(x2, norm_mix_w.reshape(1, D_MODEL), wa, wb, wc, gdn_conv_w, 2 * tm, seq // (2 * tm))

    alog_row = jnp.pad(gdn_a_log.reshape(1, HEADS), ((0, 0), (0, LANES - HEADS)))
    dtb_row = jnp.pad(gdn_dt_bias.reshape(1, HEADS), ((0, 0), (0, LANES - HEADS)))
    o_hg, o_gd = _mixers(pa, pb, pc, lb_logits, hgrn_onorm_w.reshape(1, HEAD_DIM), alog_row, dtb_row,
                         gdn_onorm_w.reshape(1, HEAD_DIM), batch, seq, n_chunks)

    wr = jnp.pad(jnp.concatenate([router_group_w, router_expert_w], axis=1),
                 ((0, 0), (0, LANES - N_GROUPS - N_EXPERTS)))
    wr_hi = wr.astype(BF16)
    wr = jnp.concatenate([wr_hi, (wr - wr_hi.astype(F32)).astype(BF16)], axis=1)
    h, hn, ri, rg, cnt = _route(x2, o_hg, o_gd, w_out.astype(BF16), norm_moe_w.reshape(1, D_MODEL), wr, 4 * tm)

    counts = cnt[0, N_GROUPS:N_GROUPS + N_EXPERTS].astype(jnp.int32)
    starts, blk, exp, lo, hi = _pair_schedule(counts, t * TOP_K)
    onehot = ri[0:TOP_K, :, None] == jnp.arange(N_EXPERTS, dtype=jnp.int32)
    slot_kt = jnp.sum(jnp.where(onehot, starts, 0), axis=-1) + ri[TOP_K:2 * TOP_K]
    xs = _sc_scatter_rows(hn, slot_kt.reshape(TOP_K * t))
    ys = _experts(blk, exp, lo, hi, xs, w_gate, w_up, w_down)
    out = None
    for part in range(COMBINE_PARTS):
        tp = t // COMBINE_PARTS
        part_slots = slot_kt[:, part * tp:(part + 1) * tp].reshape(TOP_K * tp)
        y2 = _sc_gather_rows(ys, part_slots)
        out = _combine(h, rg, final_w.reshape(1, D_MODEL), y2, 4 * tm, part, COMBINE_PARTS, out)
    return out.reshape(batch, seq, D_MODEL)


def kernel(x, norm_mix_w, w_in, hgrn_lb_logits, hgrn_onorm_w, gdn_conv_w, gdn_a_log, gdn_dt_bias, gdn_onorm_w, w_out, norm_moe_w, router_group_w, router_expert_w, expert_w_gate, expert_w_up, expert_w_down, final_norm_w):
    return _layer(x, norm_mix_w[0], w_in[0], hgrn_lb_logits, hgrn_onorm_w[0], gdn_conv_w[0], gdn_a_log[0],
                  gdn_dt_bias[0], gdn_onorm_w[0], w_out[0], norm_moe_w[0], router_group_w[0], router_expert_w[0],
                  expert_w_gate[0], expert_w_up[0], expert_w_down[0], final_norm_w, tm=256, n_chunks=8)
```
